```python
import math
import jax, jax.numpy as jnp
from jax import lax
import numpy as np

D_MODEL = 1024
BATCH = 2
SEQ = 8192
DEPTH = 1

N_META = 16
D_MIX = D_MODEL
RET_HEADS = 4
RET_HEAD_DIM = 128
RET_WIDTH = RET_HEADS * RET_HEAD_DIM
SSM_WIDTH = D_MIX - RET_WIDTH
SSM_GROUP = 16
SSM_GROUPS = SSM_WIDTH // SSM_GROUP
SSM_STATE = 64
CHUNK = 128
D_FF = 2816
FFN_RES = 0.5
ROPE_BASE = 10000.0
EPS = 1e-6
IN_PROJ = 4 * RET_WIDTH + SSM_WIDTH

kernel_name = "hymba_retnet_s5_macaron_layer"


def rms_norm(x, w):
    xf = x.astype(jnp.float32)
    y = xf * lax.rsqrt(jnp.mean(xf * xf, axis=-1, keepdims=True) + EPS)
    return (y * w.astype(jnp.float32)).astype(x.dtype)


def swiglu(x, w_gate, w_up, w_down):
    return (jax.nn.silu(x @ w_gate) * (x @ w_up)) @ w_down


def rotary(x, pos):
    dh = x.shape[-1]
    freqs = 1.0 / (ROPE_BASE ** (jnp.arange(0, dh, 2, dtype=jnp.float32) / dh))
    ang = pos.astype(jnp.float32)[:, None] * freqs[None, :]
    cos = jnp.cos(ang)[None, :, None, :]
    sin = jnp.sin(ang)[None, :, None, :]
    xf = x.astype(jnp.float32)
    x1, x2 = xf[..., : dh // 2], xf[..., dh // 2:]
    out = jnp.concatenate([x1 * cos - x2 * sin, x1 * sin + x2 * cos], axis=-1)
    return out.astype(x.dtype)


def retention(q, k, v):
    bsz, L, H, dk = q.shape
    dv = v.shape[-1]
    log_g = jnp.log(1.0 - 2.0 ** (-5.0 - jnp.arange(H, dtype=jnp.float32)))

    def decay_mask(n):
        i = jnp.arange(n)
        diff = i[:, None] - i[None, :]
        return jnp.where(diff[None] >= 0,
                         jnp.exp(log_g[:, None, None] * jnp.maximum(diff, 0)[None].astype(jnp.float32)),
                         0.0)

    qm, km, vm = q[:, :N_META], k[:, :N_META], v[:, :N_META]
    s_m = jnp.einsum('bihd,bjhd->bhij', qm, km) * decay_mask(N_META)
    o_meta = jnp.einsum('bhij,bjhe->bihe', s_m, vm)
    w_m = jnp.exp(log_g[:, None] * (N_META - 1 - jnp.arange(N_META, dtype=jnp.float32))[None])
    state0 = jnp.einsum('bjhd,bjhe,hj->bhde', km, vm, w_m)

    n_chunks = (L - N_META) // CHUNK
    qc = q[:, N_META:].reshape(bsz, n_chunks, CHUNK, H, dk)
    kc = k[:, N_META:].reshape(bsz, n_chunks, CHUNK, H, dk)
    vc = v[:, N_META:].reshape(bsz, n_chunks, CHUNK, H, dv)
    s_c = jnp.einsum('bnihd,bnjhd->bnhij', qc, kc) * decay_mask(CHUNK)
    o_inner = jnp.einsum('bnhij,bnjhe->bnihe', s_c, vc)
    pos_c = jnp.arange(CHUNK, dtype=jnp.float32)
    w_k = jnp.exp(log_g[:, None] * (CHUNK - 1 - pos_c)[None])
    kv = jnp.einsum('bnjhd,bnjhe,hj->nbhde', kc, vc, w_k)
    g_chunk = jnp.exp(log_g * CHUNK)[None, :, None, None]
    state0 = state0.astype(kv.dtype)

    def step(S, kv_n):
        return (g_chunk * S + kv_n).astype(kv_n.dtype), S

    _, s_prev = lax.scan(step, state0, kv)
    w_q = jnp.exp(log_g[:, None] * (pos_c + 1.0)[None])
    o_cross = jnp.einsum('bnihd,nbhde,hi->bnihe', qc, s_prev, w_q)
    o_real = (o_inner + o_cross).reshape(bsz, L - N_META, H, dv)
    return jnp.concatenate([o_meta, o_real.astype(o_meta.dtype)], axis=1)


def head_group_norm(o, w):
    of = o.astype(jnp.float32)
    mu = jnp.mean(of, axis=-1, keepdims=True)
    var = jnp.mean(jnp.square(of - mu), axis=-1, keepdims=True)
    y = (of - mu) * lax.rsqrt(var + EPS)
    y = y.reshape(o.shape[0], o.shape[1], -1) * w.astype(jnp.float32)
    return y


def _linear_recurrence(e1, e2):
    a1, b1 = e1
    a2, b2 = e2
    return a1 * a2, a2 * b1 + b2


def s5_mixer(u, lam_re, lam_im, log_dt, b_re, b_im, c_re, c_im, d, glu_w, glu_b, norm_w):
    bsz, L, _ = u.shape
    uf = u.astype(jnp.float32).reshape(bsz, L, SSM_GROUPS, SSM_GROUP)
    lam = lax.complex(lam_re.astype(jnp.float32), lam_im.astype(jnp.float32))
    dt = jnp.exp(log_dt.astype(jnp.float32))[:, None]
    a_bar = jnp.exp(lam * dt)
    b = lax.complex(b_re.astype(jnp.float32), b_im.astype(jnp.float32))
    b_bar = ((a_bar - 1.0) / lam)[..., None] * b
    bu = jnp.einsum('gnp,blgp->blgn', b_bar, uf.astype(jnp.complex64))
    a = jnp.broadcast_to(a_bar[None, None], bu.shape)
    _, states = lax.associative_scan(_linear_recurrence, (a, bu), axis=1)
    c = lax.complex(c_re.astype(jnp.float32), c_im.astype(jnp.float32))
    y = jnp.real(jnp.einsum('gpn,blgn->blgp', c, states))
    y = y + d.astype(jnp.float32).reshape(SSM_GROUPS, SSM_GROUP) * uf
    y = jax.nn.gelu(y.reshape(bsz, L, SSM_WIDTH)).astype(u.dtype)
    y = y * jax.nn.sigmoid(y @ glu_w + glu_b)
    return rms_norm(y, norm_w)


def setup_inputs(seed: int = 0) -> dict:
    key = jax.random.key(seed)
    ks = jax.random.split(key, 32)
    f32 = jnp.float32
    nrm = lambda k, shape, scale: (jax.random.normal(k, shape, f32) * scale)
    gain = lambda k, shape: 1.0 + 0.01 * jax.random.normal(k, shape, f32)
    Ld = DEPTH
    n_idx = jnp.arange(SSM_STATE, dtype=f32)
    return {
        "x": nrm(ks[0], (BATCH, SEQ, D_MODEL), 1.0),
        "meta_tokens": nrm(ks[1], (N_META, D_MODEL), 1.0),
        "ffn1_norm_w": gain(ks[2], (Ld, D_MODEL)),
        "ffn1_w_gate": nrm(ks[3], (Ld, D_MODEL, D_FF), D_MODEL ** -0.5),
        "ffn1_w_up": nrm(ks[4], (Ld, D_MODEL, D_FF), D_MODEL ** -0.5),
        "ffn1_w_down": nrm(ks[5], (Ld, D_FF, D_MODEL), D_FF ** -0.5),
        "mix_norm_w": gain(ks[6], (Ld, D_MODEL)),
        "w_in": nrm(ks[7], (Ld, D_MODEL, IN_PROJ), D_MODEL ** -0.5),
        "ret_norm_w": gain(ks[8], (Ld, RET_WIDTH)),
        "ssm_lambda_re": -0.5 + 0.01 * jax.random.normal(ks[9], (Ld, SSM_GROUPS, SSM_STATE), f32),
        "ssm_lambda_im": jnp.pi * n_idx[None, None, :] + 0.01 * jax.random.normal(ks[10], (Ld, SSM_GROUPS, SSM_STATE), f32),
        "ssm_log_dt": jax.random.uniform(ks[11], (Ld, SSM_GROUPS), f32, minval=math.log(0.001), maxval=math.log(0.1)),
        "ssm_b_re": nrm(ks[12], (Ld, SSM_GROUPS, SSM_STATE, SSM_GROUP), (2.0 * SSM_GROUP) ** -0.5),
        "ssm_b_im": nrm(ks[13], (Ld, SSM_GROUPS, SSM_STATE, SSM_GROUP), (2.0 * SSM_GROUP) ** -0.5),
        "ssm_c_re": nrm(ks[14], (Ld, SSM_GROUPS, SSM_GROUP, SSM_STATE), (2.0 * SSM_STATE) ** -0.5),
        "ssm_c_im": nrm(ks[15], (Ld, SSM_GROUPS, SSM_GROUP, SSM_STATE), (2.0 * SSM_STATE) ** -0.5),
        "ssm_d": nrm(ks[16], (Ld, SSM_WIDTH), 1.0),
        "ssm_glu_w": nrm(ks[17], (Ld, SSM_WIDTH, SSM_WIDTH), SSM_WIDTH ** -0.5),
        "ssm_glu_b": nrm(ks[18], (Ld, SSM_WIDTH), 0.01),
        "ssm_norm_w": gain(ks[19], (Ld, SSM_WIDTH)),
        "w_out": nrm(ks[20], (Ld, D_MIX, D_MODEL), D_MIX ** -0.5),
        "ffn2_norm_w": gain(ks[21], (Ld, D_MODEL)),
        "ffn2_w_gate": nrm(ks[22], (Ld, D_MODEL, D_FF), D_MODEL ** -0.5),
        "ffn2_w_up": nrm(ks[23], (Ld, D_MODEL, D_FF), D_MODEL ** -0.5),
        "ffn2_w_down": nrm(ks[24], (Ld, D_FF, D_MODEL), D_FF ** -0.5),
        "final_norm_w": gain(ks[25], (D_MODEL,)),
    }


def reference(x, meta_tokens, ffn1_norm_w, ffn1_w_gate, ffn1_w_up, ffn1_w_down, mix_norm_w,
              w_in, ret_norm_w, ssm_lambda_re, ssm_lambda_im, ssm_log_dt, ssm_b_re, ssm_b_im,
              ssm_c_re, ssm_c_im, ssm_d, ssm_glu_w, ssm_glu_b, ssm_norm_w, w_out,
              ffn2_norm_w, ffn2_w_gate, ffn2_w_up, ffn2_w_down, final_norm_w):
    bsz = x.shape[0]
    meta = jnp.broadcast_to(meta_tokens.astype(x.dtype)[None], (bsz, N_META, D_MODEL))
    h = jnp.concatenate([meta, x], axis=1)
    L = h.shape[1]
    pos = jnp.arange(L)
    for l in range(DEPTH):
        h = h + FFN_RES * swiglu(rms_norm(h, ffn1_norm_w[l]), ffn1_w_gate[l], ffn1_w_up[l], ffn1_w_down[l])
        n = rms_norm(h, mix_norm_w[l])
        proj = n @ w_in[l]
        q = proj[..., 0:RET_WIDTH].reshape(bsz, L, RET_HEADS, RET_HEAD_DIM)
        k = proj[..., RET_WIDTH:2 * RET_WIDTH].reshape(bsz, L, RET_HEADS, RET_HEAD_DIM)
        v = proj[..., 2 * RET_WIDTH:3 * RET_WIDTH].reshape(bsz, L, RET_HEADS, RET_HEAD_DIM)
        g = proj[..., 3 * RET_WIDTH:4 * RET_WIDTH]
        u = proj[..., 4 * RET_WIDTH:]
        q = rotary(q, pos)
        k = rotary(k, pos) * (RET_HEAD_DIM ** -0.5)
        ret = head_group_norm(retention(q, k, v), ret_norm_w[l])
        ret = (jax.nn.silu(g.astype(jnp.float32)) * ret).astype(x.dtype)
        ssm = s5_mixer(u, ssm_lambda_re[l], ssm_lambda_im[l], ssm_log_dt[l], ssm_b_re[l], ssm_b_im[l],
                       ssm_c_re[l], ssm_c_im[l], ssm_d[l], ssm_glu_w[l], ssm_glu_b[l], ssm_norm_w[l])
        mixed = jnp.concatenate([ret, ssm.astype(x.dtype)], axis=-1) @ w_out[l]
        h = h + mixed
        h = h + FFN_RES * swiglu(rms_norm(h, ffn2_norm_w[l]), ffn2_w_gate[l], ffn2_w_up[l], ffn2_w_down[l])
    out = rms_norm(h, final_norm_w)
    return out[:, N_META:]
```

```python
import functools
import math

import jax
import jax.numpy as jnp
from jax import lax
from jax.experimental import pallas as pl
from jax.experimental.pallas import tpu as pltpu

D_MODEL = 1024
N_META = 16
RET_HEADS = 4
HEAD_DIM = 128
RET_WIDTH = RET_HEADS * HEAD_DIM
SSM_WIDTH = 512
SSM_GROUP = 16
SSM_GROUPS = SSM_WIDTH // SSM_GROUP
SSM_STATE = 64
CHUNK = 128
D_FF = 2816
FFN_RES = 0.5
ROPE_BASE = 10000.0
EPS = 1e-6
IN_PROJ = 4 * RET_WIDTH + SSM_WIDTH

LANES = 128
OCTETS = SSM_WIDTH // LANES
GROUPS_PER_OCTET = LANES // SSM_GROUP
SUB = N_META
OCT_STATE = GROUPS_PER_OCTET * SSM_STATE
FF_CHUNK = 256
VMEM_LIMIT = 56 * 1024 * 1024

F32 = jnp.float32
BF16 = jnp.bfloat16


def _rms(x, w):
    return x * lax.rsqrt(jnp.mean(x * x, axis=-1, keepdims=True) + EPS) * w


def _swiglu_half_step(h, nw_ref, wg_ref, wu_ref, wd_ref, acc_ref):
    n = _rms(h, nw_ref[...]).astype(BF16)
    for c in range(D_FF // FF_CHUNK):
        sl = slice(c * FF_CHUNK, (c + 1) * FF_CHUNK)
        g = jnp.dot(n, wg_ref[:, sl], preferred_element_type=F32)
        u = jnp.dot(n, wu_ref[:, sl], preferred_element_type=F32)
        a = (g * jax.nn.sigmoid(g) * u).astype(BF16)
        p = jnp.dot(a, wd_ref[sl, :], preferred_element_type=F32)
        if c == 0:
            acc_ref[...] = p
        else:
            acc_ref[...] += p
    return h + FFN_RES * acc_ref[...]


def _rope(x, cos, sin_signed):
    return x * cos + pltpu.roll(x, HEAD_DIM // 2, axis=1) * sin_signed


def _ffn_inproj_kernel(x_ref, cos_ref, sin_ref, n1w_ref, wg_ref, wu_ref, wd_ref, n2w_ref, win_ref,
                       h1_ref, q_ref, k_ref, v_ref, g_ref, u_ref, acc_ref):
    h1 = _swiglu_half_step(x_ref[...], n1w_ref, wg_ref, wu_ref, wd_ref, acc_ref)
    h1_ref[...] = h1
    n = _rms(h1, n2w_ref[...]).astype(BF16)
    cos = cos_ref[...]
    sin = sin_ref[...]
    k_scale = HEAD_DIM ** -0.5

    def proj(part):
        return jnp.dot(n, win_ref[:, part * RET_WIDTH:(part + 1) * RET_WIDTH],
                       preferred_element_type=F32)

    p = proj(0)
    for h in range(RET_HEADS):
        hs = slice(h * HEAD_DIM, (h + 1) * HEAD_DIM)
        q_ref[:, hs] = _rope(p[:, hs], cos, sin).astype(BF16)
    p = proj(1)
    for h in range(RET_HEADS):
        hs = slice(h * HEAD_DIM, (h + 1) * HEAD_DIM)
        k_ref[:, hs] = (_rope(p[:, hs], cos, sin) * k_scale).astype(BF16)
    v_ref[...] = proj(2).astype(BF16)
    g_ref[...] = proj(3)
    p = proj(4)
    for o in range(OCTETS):
        u_ref[o] = p[:, o * LANES:(o + 1) * LANES]


def _resident(shape):
    nd = len(shape)
    return pl.BlockSpec(shape, lambda *_: (0,) * nd, pipeline_mode=pl.Buffered(1))


def _ffn_inproj(x2, cos, sin, n1w, wg, wu, wd, n2w, w_in, tm):
    rows = x2.shape[0]
    pos_blocks = cos.shape[0] // tm
    row_blk = lambda i: (i, 0)
    pos_blk = lambda i: (i % pos_blocks, 0)
    out_shape = (
        jax.ShapeDtypeStruct((rows, D_MODEL), F32),
        jax.ShapeDtypeStruct((rows, RET_WIDTH), BF16),
        jax.ShapeDtypeStruct((rows, RET_WIDTH), BF16),
        jax.ShapeDtypeStruct((rows, RET_WIDTH), BF16),
        jax.ShapeDtypeStruct((rows, RET_WIDTH), F32),
        jax.ShapeDtypeStruct((OCTETS, rows, LANES), F32),
    )
    return pl.pallas_call(
        _ffn_inproj_kernel,
        grid=(rows // tm,),
        in_specs=[
            pl.BlockSpec((tm, D_MODEL), row_blk),
            pl.BlockSpec((tm, HEAD_DIM), pos_blk),
            pl.BlockSpec((tm, HEAD_DIM), pos_blk),
            _resident((1, D_MODEL)),
            _resident((D_MODEL, D_FF)),
            _resident((D_MODEL, D_FF)),
            _resident((D_FF, D_MODEL)),
            _resident((1, D_MODEL)),
            _resident((D_MODEL, IN_PROJ)),
        ],
        out_specs=(
            pl.BlockSpec((tm, D_MODEL), row_blk),
            pl.BlockSpec((tm, RET_WIDTH), row_blk),
            pl.BlockSpec((tm, RET_WIDTH), row_blk),
            pl.BlockSpec((tm, RET_WIDTH), row_blk),
            pl.BlockSpec((tm, RET_WIDTH), row_blk),
            pl.BlockSpec((OCTETS, tm, LANES), lambda i: (0, i, 0)),
        ),
        out_shape=out_shape,
        scratch_shapes=[pltpu.VMEM((tm, D_MODEL), F32)],
        compiler_params=pltpu.CompilerParams(
            dimension_semantics=("arbitrary",), vmem_limit_bytes=VMEM_LIMIT),
        name="ffn_inproj",
    )(x2, cos, sin, n1w, wg, wu, wd, n2w, w_in)


def _s5_weights_kernel(lre_ref, lim_ref, ldt_ref, btr_ref, bti_ref, cr_ref, ci_ref,
                       xr_ref, xi_ref, wor_ref, woi_ref, k_ref, ar_ref, ai_ref):
    lre = lre_ref[...]
    lim = lim_ref[...]
    dt = jnp.exp(ldt_ref[...])
    n_pow = SUB + 8
    ell = lax.broadcasted_iota(jnp.int32, (n_pow, SSM_STATE), 0).astype(F32)
    mag = jnp.exp(ell * (lre * dt))
    ang = ell * (lim * dt)
    pre = mag * jnp.cos(ang)
    pim = mag * jnp.sin(ang)
    a_re = pre[1:2]
    a_im = pim[1:2]
    den = lre * lre + lim * lim
    num_re = a_re - 1.0
    coef_re = (num_re * lre + a_im * lim) / den
    coef_im = (a_im * lre - num_re * lim) / den
    btr = btr_ref[...]
    bti = bti_ref[...]
    bbar_re = coef_re * btr - coef_im * bti
    bbar_im = coef_re * bti + coef_im * btr
    cr = cr_ref[...]
    ci = ci_ref[...]
    for l in range(SUB):
        rows = slice(l * SSM_GROUP, (l + 1) * SSM_GROUP)
        pr = pre[l:l + 1]
        pi = pim[l:l + 1]
        xr_ref[rows, :] = pr * bbar_re - pi * bbar_im
        xi_ref[rows, :] = pr * bbar_im + pi * bbar_re
        pr1 = pre[l + 1:l + 2]
        pi1 = pim[l + 1:l + 2]
        wor_ref[rows, :] = pr1 * cr - pi1 * ci
        woi_ref[rows, :] = -(pr1 * ci + pi1 * cr)
    nt = (((1,), (1,)), ((), ()))
    k_ref[...] = (lax.dot_general(xr_ref[...], cr, nt, precision=lax.Precision.HIGHEST,
                                  preferred_element_type=F32)
                  - lax.dot_general(xi_ref[...], ci, nt, precision=lax.Precision.HIGHEST,
                                    preferred_element_type=F32))
    ar_ref[...] = pre[SUB:SUB + 1]
    ai_ref[...] = pim[SUB:SUB + 1]


def _s5_weights(lam_re, lam_im, log_dt, b_re, b_im, c_re, c_im):
    g = SSM_GROUPS
    vec = lambda a: a.reshape(g, 1, SSM_STATE)
    bt = lambda a: jnp.swapaxes(a, 1, 2)
    grp = lambda r, c: pl.BlockSpec((None, r, c), lambda i: (i, 0, 0))
    rows = SUB * SSM_GROUP
    return pl.pallas_call(
        _s5_weights_kernel,
        grid=(g,),
        in_specs=[grp(1, SSM_STATE), grp(1, SSM_STATE), grp(1, 1),
                  grp(SSM_GROUP, SSM_STATE), grp(SSM_GROUP, SSM_STATE),
                  grp(SSM_GROUP, SSM_STATE), grp(SSM_GROUP, SSM_STATE)],
        out_specs=(grp(rows, SSM_STATE), grp(rows, SSM_STATE), grp(rows, SSM_STATE),
                   grp(rows, SSM_STATE), grp(rows, SSM_GROUP), grp(1, SSM_STATE), grp(1, SSM_STATE)),
        out_shape=(
            jax.ShapeDtypeStruct((g, rows, SSM_STATE), F32),
            jax.ShapeDtypeStruct((g, rows, SSM_STATE), F32),
            jax.ShapeDtypeStruct((g, rows, SSM_STATE), F32),
            jax.ShapeDtypeStruct((g, rows, SSM_STATE), F32),
            jax.ShapeDtypeStruct((g, rows, SSM_GROUP), F32),
            jax.ShapeDtypeStruct((g, 1, SSM_STATE), F32),
            jax.ShapeDtypeStruct((g, 1, SSM_STATE), F32),
        ),
        name="s5_weights",
    )(vec(lam_re), vec(lam_im), log_dt.reshape(g, 1, 1), bt(b_re), bt(b_im), c_re, c_im)


def _s5_pack_weights(xr, xi, wor, woi, kq, ar, ai):
    go = GROUPS_PER_OCTET
    def x_tiles(x):
        x = x.reshape(OCTETS, go, SUB, SSM_GROUP, SSM_STATE)[:, :, ::-1]
        x = jnp.transpose(x, (0, 2, 1, 3, 4)).reshape(OCTETS, SUB, LANES, SSM_STATE)
        return jnp.concatenate([x, x], axis=-1)
    xw = jnp.stack([x_tiles(xr), x_tiles(xi)], axis=2)
    def w_tiles(w):
        w = w.reshape(OCTETS, go, SUB, SSM_GROUP, SSM_STATE)
        w = jnp.transpose(w, (0, 2, 4, 1, 3)).reshape(OCTETS, SUB, SSM_STATE, LANES)
        return jnp.concatenate([w, w], axis=-2)
    wc = jnp.stack([w_tiles(wor), w_tiles(woi)], axis=1)
    k = kq.reshape(OCTETS, go, SUB, SSM_GROUP, SSM_GROUP)
    k = jnp.transpose(k, (0, 2, 1, 3, 4)).reshape(OCTETS, SUB, LANES, SSM_GROUP)
    kt = jnp.tile(k, (1, 1, 1, go))
    a16 = jnp.concatenate([ar.reshape(OCTETS, 1, OCT_STATE), ai.reshape(OCTETS, 1, OCT_STATE)], axis=-1)
    return xw, wc, kt, a16


def _retention_kernel(q_ref, k_ref, v_ref, g_ref, km_ref, vm_ref, mask_ref, wq_ref, wk_ref, wm_ref,
                      gc_ref, nw_ref, o_ref, state_ref, *, rows):
    tn = (((0,), (0,)), ((), ()))
    nt = (((1,), (1,)), ((), ()))

    @pl.when(pl.program_id(1) == 0)
    def _():
        for h in range(RET_HEADS):
            hs = slice(h * HEAD_DIM, (h + 1) * HEAD_DIM)
            kw = (km_ref[:, hs].astype(F32) * wm_ref[h]).astype(BF16)
            state_ref[h] = lax.dot_general(kw, vm_ref[:, hs], tn, preferred_element_type=F32)

    def chunk(i, carry):
        r0 = pl.multiple_of(i * CHUNK, CHUNK)
        rs = pl.ds(r0, CHUNK)
        for h in range(RET_HEADS):
            hs = slice(h * HEAD_DIM, (h + 1) * HEAD_DIM)
            qh = q_ref[rs, hs]
            kh = k_ref[rs, hs]
            vh = v_ref[rs, hs]
            st = state_ref[h]
            s = lax.dot_general(qh, kh, nt, preferred_element_type=F32) * mask_ref[h]
            o = (jnp.dot(s.astype(BF16), vh, preferred_element_type=F32)
                 + jnp.dot(qh, st.astype(BF16), preferred_element_type=F32) * wq_ref[h])
            kw = (kh.astype(F32) * wk_ref[h]).astype(BF16)
            state_ref[h] = gc_ref[h] * st + lax.dot_general(kw, vh, tn, preferred_element_type=F32)
            mu = jnp.mean(o, axis=-1, keepdims=True)
            d = o - mu
            var = jnp.mean(d * d, axis=-1, keepdims=True)
            y = d * lax.rsqrt(var + EPS) * nw_ref[:, hs]
            gate = g_ref[rs, hs]
            o_ref[rs, hs] = (gate * jax.nn.sigmoid(gate) * y).astype(BF16)
        return carry

    lax.fori_loop(0, rows // CHUNK, chunk, 0)


def _retention_tables():
    log_g = jnp.log(1.0 - 2.0 ** (-5.0 - jnp.arange(RET_HEADS, dtype=F32)))
    i = jnp.arange(CHUNK)
    diff = i[:, None] - i[None, :]
    mask = jnp.where(diff[None] >= 0,
                     jnp.exp(log_g[:, None, None] * jnp.maximum(diff, 0)[None].astype(F32)), 0.0)
    pos = jnp.arange(CHUNK, dtype=F32)
    full = lambda w: jnp.broadcast_to(w[:, :, None], w.shape + (HEAD_DIM,))
    w_q = full(jnp.exp(log_g[:, None] * (pos + 1.0)[None]))
    w_k = full(jnp.exp(log_g[:, None] * (CHUNK - 1 - pos)[None]))
    w_m = full(jnp.exp(log_g[:, None] * (N_META - 1 - jnp.arange(N_META, dtype=F32))[None]))
    g_c = jnp.broadcast_to(jnp.exp(log_g * CHUNK)[:, None, None], (RET_HEADS, HEAD_DIM, HEAD_DIM))
    return mask, w_q, w_k, w_m, g_c


def _retention(q, k, v, g, km, vm, ret_norm_w, bsz, seq, rows):
    mask, w_q, w_k, w_m, g_c = _retention_tables()
    steps = seq // rows
    blk = lambda b, c: (b * steps + c, 0)
    full3 = lambda a: pl.BlockSpec(a.shape, lambda b, c: (0, 0, 0))
    full2 = lambda a: pl.BlockSpec(a.shape, lambda b, c: (0, 0))
    return pl.pallas_call(
        functools.partial(_retention_kernel, rows=rows),
        grid=(bsz, steps),
        in_specs=[pl.BlockSpec((rows, RET_WIDTH), blk)] * 4
        + [full2(km), full2(vm), full3(mask), full3(w_q), full3(w_k), full3(w_m), full3(g_c),
           full2(ret_norm_w)],
        out_specs=pl.BlockSpec((rows, RET_WIDTH), blk),
        out_shape=jax.ShapeDtypeStruct((bsz * seq, RET_WIDTH), BF16),
        scratch_shapes=[pltpu.VMEM((RET_HEADS, HEAD_DIM, HEAD_DIM), F32)],
        compiler_params=pltpu.CompilerParams(
            dimension_semantics=("arbitrary", "arbitrary"), vmem_limit_bytes=VMEM_LIMIT),
        name="retention",
    )(q, k, v, g, km, vm, mask, w_q, w_k, w_m, g_c, ret_norm_w)


def _s5_kernel(u_ref, um_ref, kt_ref, xw_ref, wc_ref, a16_ref, d_ref, y_ref,
               toep_ref, win_ref, wout_ref, lhs_ref, lhsm_ref, st_ref, *, nsc):
    b = pl.program_id(1)
    r = pl.program_id(2)
    row = lax.broadcasted_iota(jnp.int32, (LANES, LANES), 0)
    col = lax.broadcasted_iota(jnp.int32, (LANES, LANES), 1)

    @pl.when((b == 0) & (r == 0))
    def _assemble():
        diag = (row // SSM_GROUP) == (col // SSM_GROUP)
        zeros = jnp.zeros((LANES, LANES), BF16)
        for lag in range(SUB):
            blk = jnp.where(diag, kt_ref[lag], 0.0).astype(BF16)
            for j in range(SUB - lag):
                i = j + lag
                toep_ref[j * LANES:(j + 1) * LANES, i * LANES:(i + 1) * LANES] = blk
        for j in range(SUB):
            for i in range(j):
                toep_ref[j * LANES:(j + 1) * LANES, i * LANES:(i + 1) * LANES] = zeros
        for t in range(GROUPS_PER_OCTET // 2):
            in_mask = (row // SSM_GROUP) == (2 * t + col // SSM_STATE)
            out_mask = (col // SSM_GROUP) == (2 * t + row // SSM_STATE)
            for part in range(2):
                c0 = part * OCT_STATE + t * LANES
                for j in range(SUB):
                    win_ref[j * LANES:(j + 1) * LANES, c0:c0 + LANES] = jnp.where(
                        in_mask, xw_ref[j, part], 0.0).astype(BF16)
                    wout_ref[c0:c0 + LANES, j * LANES:(j + 1) * LANES] = jnp.where(
                        out_mask, wc_ref[part, j], 0.0).astype(BF16)

    @pl.when(r == 0)
    def _meta_state():
        for j in range(SUB):
            lhsm_ref[:, j * LANES:(j + 1) * LANES] = um_ref[pl.ds(j, 8, stride=SUB), :].astype(BF16)
        st_ref[0:8, :] = jnp.dot(lhsm_ref[...], win_ref[...], preferred_element_type=F32)

    @pl.when(r > 0)
    def _carry_state():
        st_ref[0:8, :] = st_ref[nsc:nsc + 8, :]

    for j in range(SUB):
        lhs_ref[:, j * LANES:(j + 1) * LANES] = u_ref[pl.ds(j, nsc, stride=SUB), :].astype(BF16)
    lhs = lhs_ref[...]
    st_ref[8:8 + nsc, :] = jnp.dot(lhs, win_ref[...], preferred_element_type=F32)

    a_re = a16_ref[:, 0:OCT_STATE]
    a_im = a16_ref[:, OCT_STATE:2 * OCT_STATE]

    def step(c, carry):
        p_re, p_im = carry
        rs = pl.ds(8 + c, 1)
        n_re = a_re * p_re - a_im * p_im + st_ref[rs, 0:OCT_STATE]
        n_im = a_re * p_im + a_im * p_re + st_ref[rs, OCT_STATE:2 * OCT_STATE]
        st_ref[rs, 0:OCT_STATE] = n_re
        st_ref[rs, OCT_STATE:2 * OCT_STATE] = n_im
        return n_re, n_im

    lax.fori_loop(0, nsc, step, (st_ref[7:8, 0:OCT_STATE], st_ref[7:8, OCT_STATE:2 * OCT_STATE]))

    s_in = st_ref[pl.ds(7, nsc), :].astype(BF16)
    y = (jnp.dot(lhs, toep_ref[...], preferred_element_type=F32)
         + jnp.dot(s_in, wout_ref[...], preferred_element_type=F32))
    d = d_ref[...]
    for i in range(SUB):
        yi = y[:, i * LANES:(i + 1) * LANES] + d * u_ref[pl.ds(i, nsc, stride=SUB), :]
        y_ref[pl.ds(i, nsc, stride=SUB), :] = jax.nn.gelu(yi)


def _s5(u, um, kt, xw, wc, a16, d, bsz, seq, rows):
    nsc = rows // SUB
    steps = seq // rows
    kdim = SUB * LANES
    oct_blk = lambda shape: pl.BlockSpec((None,) + shape, lambda o, b, r: (o,) + (0,) * len(shape))
    tok_blk = pl.BlockSpec((None, rows, LANES), lambda o, b, r: (o, b * steps + r, 0))
    return pl.pallas_call(
        functools.partial(_s5_kernel, nsc=nsc),
        grid=(OCTETS, bsz, steps),
        in_specs=[tok_blk, oct_blk((LANES, LANES)), oct_blk((SUB, LANES, LANES)),
                  oct_blk((SUB, 2, LANES, LANES)), oct_blk((2, SUB, LANES, LANES)),
                  oct_blk((1, 2 * OCT_STATE)), oct_blk((1, LANES))],
        out_specs=tok_blk,
        out_shape=jax.ShapeDtypeStruct(u.shape, F32),
        scratch_shapes=[
            pltpu.VMEM((kdim, kdim), BF16),
            pltpu.VMEM((kdim, 2 * OCT_STATE), BF16),
            pltpu.VMEM((2 * OCT_STATE, kdim), BF16),
            pltpu.VMEM((nsc, kdim), BF16),
            pltpu.VMEM((8, kdim), BF16),
            pltpu.VMEM((nsc + 8, 2 * OCT_STATE), F32),
        ],
        compiler_params=pltpu.CompilerParams(
            dimension_semantics=("arbitrary", "arbitrary", "arbitrary"),
            vmem_limit_bytes=VMEM_LIMIT),
        name="s5_mixer",
    )(u, um, kt, xw, wc, a16, d)


def _mix_ffn_kernel(h1_ref, ret_ref, y_ref, gw_ref, gb_ref, snw_ref, wo_ref, n3w_ref,
                    wg_ref, wu_ref, wd_ref, fnw_ref, out_ref, acc_ref):
    y = jnp.concatenate([y_ref[o] for o in range(OCTETS)], axis=-1)
    z = y * jax.nn.sigmoid(jnp.dot(y.astype(BF16), gw_ref[...], preferred_element_type=F32)
                           + gb_ref[...])
    ssm = _rms(z, snw_ref[...]).astype(BF16)
    mixed = (jnp.dot(ret_ref[...], wo_ref[0:RET_WIDTH, :], preferred_element_type=F32)
             + jnp.dot(ssm, wo_ref[RET_WIDTH:, :], preferred_element_type=F32))
    h2 = h1_ref[...] + mixed
    h3 = _swiglu_half_step(h2, n3w_ref, wg_ref, wu_ref, wd_ref, acc_ref)
    out_ref[...] = _rms(h3, fnw_ref[...])


def _mix_ffn(h1, ret, y, glu_w, glu_b, ssm_norm_w, w_out, n3w, wg, wu, wd, fnw, tm):
    rows = h1.shape[0]
    row_blk = lambda i: (i, 0)
    return pl.pallas_call(
        _mix_ffn_kernel,
        grid=(rows // tm,),
        in_specs=[
            pl.BlockSpec((tm, D_MODEL), row_blk),
            pl.BlockSpec((tm, RET_WIDTH), row_blk),
            pl.BlockSpec((OCTETS, tm, LANES), lambda i: (0, i, 0)),
            _resident((SSM_WIDTH, SSM_WIDTH)),
            _resident((1, SSM_WIDTH)),
            _resident((1, SSM_WIDTH)),
            _resident((D_MODEL, D_MODEL)),
            _resident((1, D_MODEL)),
            _resident((D_MODEL, D_FF)),
            _resident((D_MODEL, D_FF)),
            _resident((D_FF, D_MODEL)),
            _resident((1, D_MODEL)),
        ],
        out_specs=pl.BlockSpec((tm, D_MODEL), row_blk),
        out_shape=jax.ShapeDtypeStruct((rows, D_MODEL), F32),
        scratch_shapes=[pltpu.VMEM((tm, D_MODEL), F32)],
        compiler_params=pltpu.CompilerParams(
            dimension_semantics=("arbitrary",), vmem_limit_bytes=VMEM_LIMIT),
        name="mix_ffn",
    )(h1, ret, y, glu_w, glu_b, ssm_norm_w, w_out, n3w, wg, wu, wd, fnw)


def _rope_tables(n_pos):
    freqs = 1.0 / (ROPE_BASE ** (jnp.arange(0, HEAD_DIM, 2, dtype=F32) / HEAD_DIM))
    ang = jnp.arange(n_pos).astype(F32)[:, None] * freqs[None, :]
    cos = jnp.cos(ang)
    sin = jnp.sin(ang)
    return jnp.concatenate([cos, cos], axis=-1), jnp.concatenate([-sin, sin], axis=-1)


def kernel(x, meta_tokens, ffn1_norm_w, ffn1_w_gate, ffn1_w_up, ffn1_w_down, mix_norm_w, w_in,
           ret_norm_w, ssm_lambda_re, ssm_lambda_im, ssm_log_dt, ssm_b_re, ssm_b_im, ssm_c_re,
           ssm_c_im, ssm_d, ssm_glu_w, ssm_glu_b, ssm_norm_w, w_out, ffn2_norm_w, ffn2_w_gate,
           ffn2_w_up, ffn2_w_down, final_norm_w):
    bsz, seq, _ = x.shape
    assert ffn1_norm_w.shape[0] == 1, "single layer only"
    tm = 512
    ret_rows = 1024
    s5_rows = 4096
    assert seq % tm == 0 and seq % ret_rows == 0 and seq % s5_rows == 0

    row = lambda a: a.reshape(1, -1)
    bf = lambda a: a.astype(BF16)
    l = 0
    cos, sin = _rope_tables(N_META + seq)
    ffn1 = (row(ffn1_norm_w[l]), bf(ffn1_w_gate[l]), bf(ffn1_w_up[l]), bf(ffn1_w_down[l]),
            row(mix_norm_w[l]), bf(w_in[l]))

    _, _, km, vm, _, um = _ffn_inproj(meta_tokens, cos[:N_META], sin[:N_META], *ffn1, tm=N_META)
    h1, q, k, v, g, u = _ffn_inproj(x.reshape(bsz * seq, D_MODEL), cos[N_META:], sin[N_META:],
                                    *ffn1, tm=tm)

    ret = _retention(q, k, v, g, km, vm, row(ret_norm_w[l]), bsz, seq, ret_rows)

    xr, xi, wor, woi, kq, ar, ai = _s5_weights(
        ssm_lambda_re[l], ssm_lambda_im[l], ssm_log_dt[l], ssm_b_re[l], ssm_b_im[l],
        ssm_c_re[l], ssm_c_im[l])
    xw, wc, kt, a16 = _s5_pack_weights(xr, xi, wor, woi, kq, ar, ai)
    um_pad = jnp.pad(um, ((0, 0), (LANES - N_META, 0), (0, 0)))
    y = _s5(u, um_pad, kt, xw, wc, a16, ssm_d[l].reshape(OCTETS, 1, LANES), bsz, seq, s5_rows)

    out = _mix_ffn(h1, ret, y, bf(ssm_glu_w[l]), row(ssm_glu_b[l]), row(ssm_norm_w[l]),
                   bf(w_out[l]), row(ffn2_norm_w[l]), bf(ffn2_w_gate[l]), bf(ffn2_w_up[l]),
                   bf(ffn2_w_down[l]), row(final_norm_w), tm=tm)
    return out.reshape(bsz, seq, D_MODEL)
```

```python
import functools
import math

import jax
import jax.numpy as jnp
import numpy as np
from jax import lax
from jax.experimental import pallas as pl
from jax.experimental.pallas import tpu as pltpu

D_MODEL = 1024
N_META = 16
RET_HEADS = 4
HEAD_DIM = 128
RET_WIDTH = RET_HEADS * HEAD_DIM
SSM_WIDTH = 512
SSM_GROUP = 16
SSM_GROUPS = SSM_WIDTH // SSM_GROUP
SSM_STATE = 64
RET_BLOCK = 256
D_FF = 2816
FFN_RES = 0.5
ROPE_BASE = 10000.0
EPS = 1e-6
IN_PROJ = 4 * RET_WIDTH + SSM_WIDTH

LANES = 128
MXU_DIM = 256
OCTETS = SSM_WIDTH // LANES
GROUPS_PER_OCTET = LANES // SSM_GROUP
SUB = N_META
OCT_STATE = GROUPS_PER_OCTET * SSM_STATE
FF_CHUNK = 256
VMEM_LIMIT = 56 * 1024 * 1024

F32 = jnp.float32
BF16 = jnp.bfloat16


def _rms(x, w):
    return x * lax.rsqrt(jnp.mean(x * x, axis=-1, keepdims=True) + EPS) * w


def _swiglu_half_step(h, nw_ref, wg_ref, wu_ref, wd_ref, acc_ref):
    n = _rms(h, nw_ref[...]).astype(BF16)
    for c in range(D_FF // FF_CHUNK):
        sl = slice(c * FF_CHUNK, (c + 1) * FF_CHUNK)
        g = jnp.dot(n, wg_ref[:, sl], preferred_element_type=F32)
        u = jnp.dot(n, wu_ref[:, sl], preferred_element_type=F32)
        a = (g * jax.nn.sigmoid(g) * u).astype(BF16)
        p = jnp.dot(a, wd_ref[sl, :], preferred_element_type=F32)
        if c == 0:
            acc_ref[...] = p
        else:
            acc_ref[...] += p
    return h + FFN_RES * acc_ref[...]


def _rope(x, cos, sin_signed):
    return x * cos + pltpu.roll(x, HEAD_DIM // 2, axis=1) * sin_signed


def _ffn_inproj_kernel(x_ref, cos_ref, sin_ref, n1w_ref, wg_ref, wu_ref, wd_ref, n2w_ref, win_ref,
                       h1_ref, q_ref, k_ref, v_ref, g_ref, u_ref, acc_ref):
    h1 = _swiglu_half_step(x_ref[...], n1w_ref, wg_ref, wu_ref, wd_ref, acc_ref)
    h1_ref[...] = h1
    n = _rms(h1, n2w_ref[...]).astype(BF16)
    cos = cos_ref[...]
    sin = sin_ref[...]
    k_scale = HEAD_DIM ** -0.5

    def proj(part):
        return jnp.dot(n, win_ref[:, part * RET_WIDTH:(part + 1) * RET_WIDTH],
                       preferred_element_type=F32)

    p = proj(0)
    for h in range(RET_HEADS):
        hs = slice(h * HEAD_DIM, (h + 1) * HEAD_DIM)
        q_ref[:, hs] = _rope(p[:, hs], cos, sin).astype(BF16)
    p = proj(1)
    for h in range(RET_HEADS):
        hs = slice(h * HEAD_DIM, (h + 1) * HEAD_DIM)
        k_ref[:, hs] = (_rope(p[:, hs], cos, sin) * k_scale).astype(BF16)
    v_ref[...] = proj(2).astype(BF16)
    g_ref[...] = proj(3)
    p = proj(4)
    for o in range(OCTETS):
        u_ref[o] = p[:, o * LANES:(o + 1) * LANES]


def _resident(shape):
    nd = len(shape)
    return pl.BlockSpec(shape, lambda *_: (0,) * nd, pipeline_mode=pl.Buffered(1))


def _ffn_inproj(x2, cos, sin, n1w, wg, wu, wd, n2w, w_in, tm):
    rows = x2.shape[0]
    pos_blocks = cos.shape[0] // tm
    row_blk = lambda i: (i, 0)
    pos_blk = lambda i: (i % pos_blocks, 0)
    out_shape = (
        jax.ShapeDtypeStruct((rows, D_MODEL), F32),
        jax.ShapeDtypeStruct((rows, RET_WIDTH), BF16),
        jax.ShapeDtypeStruct((rows, RET_WIDTH), BF16),
        jax.ShapeDtypeStruct((rows, RET_WIDTH), BF16),
        jax.ShapeDtypeStruct((rows, RET_WIDTH), F32),
        jax.ShapeDtypeStruct((OCTETS, rows, LANES), F32),
    )
    return pl.pallas_call(
        _ffn_inproj_kernel,
        grid=(rows // tm,),
        in_specs=[
            pl.BlockSpec((tm, D_MODEL), row_blk),
            pl.BlockSpec((tm, HEAD_DIM), pos_blk),
            pl.BlockSpec((tm, HEAD_DIM), pos_blk),
            _resident((1, D_MODEL)),
            _resident((D_MODEL, D_FF)),
            _resident((D_MODEL, D_FF)),
            _resident((D_FF, D_MODEL)),
            _resident((1, D_MODEL)),
            _resident((D_MODEL, IN_PROJ)),
        ],
        out_specs=(
            pl.BlockSpec((tm, D_MODEL), row_blk),
            pl.BlockSpec((tm, RET_WIDTH), row_blk),
            pl.BlockSpec((tm, RET_WIDTH), row_blk),
            pl.BlockSpec((tm, RET_WIDTH), row_blk),
            pl.BlockSpec((tm, RET_WIDTH), row_blk),
            pl.BlockSpec((OCTETS, tm, LANES), lambda i: (0, i, 0)),
        ),
        out_shape=out_shape,
        scratch_shapes=[pltpu.VMEM((tm, D_MODEL), F32)],
        compiler_params=pltpu.CompilerParams(
            dimension_semantics=("arbitrary",), vmem_limit_bytes=VMEM_LIMIT),
        name="ffn_inproj",
    )(x2, cos, sin, n1w, wg, wu, wd, n2w, w_in)


def _s5_weights_kernel(lre_ref, lim_ref, ldt_ref, btr_ref, bti_ref, cr_ref, ci_ref,
                       xr_ref, xi_ref, wor_ref, woi_ref, k_ref, ar_ref, ai_ref):
    lre = lre_ref[...]
    lim = lim_ref[...]
    dt = jnp.exp(ldt_ref[...])
    n_pow = SUB + 8
    ell = lax.broadcasted_iota(jnp.int32, (n_pow, SSM_STATE), 0).astype(F32)
    mag = jnp.exp(ell * (lre * dt))
    ang = ell * (lim * dt)
    pre = mag * jnp.cos(ang)
    pim = mag * jnp.sin(ang)
    a_re = pre[1:2]
    a_im = pim[1:2]
    den = lre * lre + lim * lim
    num_re = a_re - 1.0
    coef_re = (num_re * lre + a_im * lim) / den
    coef_im = (a_im * lre - num_re * lim) / den
    btr = btr_ref[...]
    bti = bti_ref[...]
    bbar_re = coef_re * btr - coef_im * bti
    bbar_im = coef_re * bti + coef_im * btr
    cr = cr_ref[...]
    ci = ci_ref[...]
    for l in range(SUB):
        rows = slice(l * SSM_GROUP, (l + 1) * SSM_GROUP)
        pr = pre[l:l + 1]
        pi = pim[l:l + 1]
        xr_ref[rows, :] = pr * bbar_re - pi * bbar_im
        xi_ref[rows, :] = pr * bbar_im + pi * bbar_re
        pr1 = pre[l + 1:l + 2]
        pi1 = pim[l + 1:l + 2]
        wor_ref[rows, :] = pr1 * cr - pi1 * ci
        woi_ref[rows, :] = -(pr1 * ci + pi1 * cr)
    nt = (((1,), (1,)), ((), ()))
    k_ref[...] = (lax.dot_general(xr_ref[...], cr, nt, precision=lax.Precision.HIGHEST,
                                  preferred_element_type=F32)
                  - lax.dot_general(xi_ref[...], ci, nt, precision=lax.Precision.HIGHEST,
                                    preferred_element_type=F32))
    ar_ref[...] = pre[SUB:SUB + 1]
    ai_ref[...] = pim[SUB:SUB + 1]


def _s5_weights(lam_re, lam_im, log_dt, b_re, b_im, c_re, c_im):
    g = SSM_GROUPS
    vec = lambda a: a.reshape(g, 1, SSM_STATE)
    bt = lambda a: jnp.swapaxes(a, 1, 2)
    grp = lambda r, c: pl.BlockSpec((None, r, c), lambda i: (i, 0, 0))
    rows = SUB * SSM_GROUP
    return pl.pallas_call(
        _s5_weights_kernel,
        grid=(g,),
        in_specs=[grp(1, SSM_STATE), grp(1, SSM_STATE), grp(1, 1),
                  grp(SSM_GROUP, SSM_STATE), grp(SSM_GROUP, SSM_STATE),
                  grp(SSM_GROUP, SSM_STATE), grp(SSM_GROUP, SSM_STATE)],
        out_specs=(grp(rows, SSM_STATE), grp(rows, SSM_STATE), grp(rows, SSM_STATE),
                   grp(rows, SSM_STATE), grp(rows, SSM_GROUP), grp(1, SSM_STATE), grp(1, SSM_STATE)),
        out_shape=(
            jax.ShapeDtypeStruct((g, rows, SSM_STATE), F32),
            jax.ShapeDtypeStruct((g, rows, SSM_STATE), F32),
            jax.ShapeDtypeStruct((g, rows, SSM_STATE), F32),
            jax.ShapeDtypeStruct((g, rows, SSM_STATE), F32),
            jax.ShapeDtypeStruct((g, rows, SSM_GROUP), F32),
            jax.ShapeDtypeStruct((g, 1, SSM_STATE), F32),
            jax.ShapeDtypeStruct((g, 1, SSM_STATE), F32),
        ),
        name="s5_weights",
    )(vec(lam_re), vec(lam_im), log_dt.reshape(g, 1, 1), bt(b_re), bt(b_im), c_re, c_im)


def _s5_pack_weights(xr, xi, wor, woi, kq, ar, ai):
    go = GROUPS_PER_OCTET
    def x_tiles(x):
        x = x.reshape(OCTETS, go, SUB, SSM_GROUP, SSM_STATE)[:, :, ::-1]
        x = jnp.transpose(x, (0, 2, 1, 3, 4)).reshape(OCTETS, SUB, LANES, SSM_STATE)
        return jnp.concatenate([x, x], axis=-1)
    xw = jnp.stack([x_tiles(xr), x_tiles(xi)], axis=2)
    def w_tiles(w):
        w = w.reshape(OCTETS, go, SUB, SSM_GROUP, SSM_STATE)
        w = jnp.transpose(w, (0, 2, 4, 1, 3)).reshape(OCTETS, SUB, SSM_STATE, LANES)
        return jnp.concatenate([w, w], axis=-2)
    wc = jnp.stack([w_tiles(wor), w_tiles(woi)], axis=1)
    k = kq.reshape(OCTETS, go, SUB, SSM_GROUP, SSM_GROUP)
    k = jnp.transpose(k, (0, 2, 1, 3, 4)).reshape(OCTETS, SUB, LANES, SSM_GROUP)
    kt = jnp.tile(k, (1, 1, 1, go))
    a16 = jnp.concatenate([ar.reshape(OCTETS, 1, OCT_STATE), ai.reshape(OCTETS, 1, OCT_STATE)], axis=-1)
    return xw, wc, kt, a16


def _retention_kernel(q_ref, k_ref, v_ref, g_ref, km_ref, vm_ref, mask_ref, wq_ref, wk_ref, wm_ref,
                      gc_ref, nw_ref, o_ref, state_ref, *, rows):
    tn = (((0,), (0,)), ((), ()))
    nt = (((1,), (1,)), ((), ()))

    @pl.when(pl.program_id(1) == 0)
    def _():
        for h in range(RET_HEADS):
            hs = slice(h * HEAD_DIM, (h + 1) * HEAD_DIM)
            kw = (km_ref[:, hs].astype(F32) * wm_ref[h]).astype(BF16)
            state_ref[h] = lax.dot_general(kw, vm_ref[:, hs], tn, preferred_element_type=F32)

    def chunk(i, carry):
        r0 = pl.multiple_of(i * RET_BLOCK, RET_BLOCK)
        rs = pl.ds(r0, RET_BLOCK)
        for h in range(RET_HEADS):
            hs = slice(h * HEAD_DIM, (h + 1) * HEAD_DIM)
            qh = q_ref[rs, hs]
            kh = k_ref[rs, hs]
            vh = v_ref[rs, hs]
            st = state_ref[h]
            s = lax.dot_general(qh, kh, nt, preferred_element_type=F32) * mask_ref[h]
            o = (jnp.dot(s.astype(BF16), vh, preferred_element_type=F32)
                 + jnp.dot(qh, st.astype(BF16), preferred_element_type=F32) * wq_ref[h])
            kw = (kh.astype(F32) * wk_ref[h]).astype(BF16)
            state_ref[h] = gc_ref[h] * st + lax.dot_general(kw, vh, tn, preferred_element_type=F32)
            mu = jnp.mean(o, axis=-1, keepdims=True)
            d = o - mu
            var = jnp.mean(d * d, axis=-1, keepdims=True)
            y = d * lax.rsqrt(var + EPS) * nw_ref[:, hs]
            gate = g_ref[rs, hs]
            o_ref[rs, hs] = (gate * jax.nn.sigmoid(gate) * y).astype(BF16)
        return carry

    lax.fori_loop(0, rows // RET_BLOCK, chunk, 0, unroll=2)


def _retention_tables():
    log_g = np.log(1.0 - 2.0 ** (-5.0 - np.arange(RET_HEADS, dtype=np.float64)))
    i = np.arange(RET_BLOCK)
    diff = i[:, None] - i[None, :]
    mask = np.where(diff[None] >= 0, np.exp(log_g[:, None, None] * np.maximum(diff, 0)[None]), 0.0)
    pos = np.arange(RET_BLOCK, dtype=np.float64)
    full = lambda w: np.broadcast_to(w[:, :, None], w.shape + (HEAD_DIM,))
    w_q = full(np.exp(log_g[:, None] * (pos + 1.0)[None]))
    w_k = full(np.exp(log_g[:, None] * (RET_BLOCK - 1 - pos)[None]))
    w_m = full(np.exp(log_g[:, None] * (N_META - 1 - np.arange(N_META, dtype=np.float64))[None]))
    g_c = np.broadcast_to(np.exp(log_g * RET_BLOCK)[:, None, None], (RET_HEADS, HEAD_DIM, HEAD_DIM))
    return tuple(jnp.asarray(t, dtype=F32) for t in (mask, w_q, w_k, w_m, g_c))


def _retention(q, k, v, g, km, vm, ret_norm_w, bsz, seq, rows):
    mask, w_q, w_k, w_m, g_c = _retention_tables()
    steps = seq // rows
    blk = lambda b, c: (b * steps + c, 0)
    full3 = lambda a: pl.BlockSpec(a.shape, lambda b, c: (0, 0, 0))
    full2 = lambda a: pl.BlockSpec(a.shape, lambda b, c: (0, 0))
    return pl.pallas_call(
        functools.partial(_retention_kernel, rows=rows),
        grid=(bsz, steps),
        in_specs=[pl.BlockSpec((rows, RET_WIDTH), blk)] * 4
        + [full2(km), full2(vm), full3(mask), full3(w_q), full3(w_k), full3(w_m), full3(g_c),
           full2(ret_norm_w)],
        out_specs=pl.BlockSpec((rows, RET_WIDTH), blk),
        out_shape=jax.ShapeDtypeStruct((bsz * seq, RET_WIDTH), BF16),
        scratch_shapes=[pltpu.VMEM((RET_HEADS, HEAD_DIM, HEAD_DIM), F32)],
        compiler_params=pltpu.CompilerParams(
            dimension_semantics=("arbitrary", "arbitrary"), vmem_limit_bytes=VMEM_LIMIT),
        name="retention",
    )(q, k, v, g, km, vm, mask, w_q, w_k, w_m, g_c, ret_norm_w)


def _s5_kernel(u_ref, um_ref, kt_ref, xw_ref, wc_ref, a16_ref, d_ref, y_ref,
               toep_ref, win_ref, wout_ref, lhs_ref, lhsm_ref, st_ref, *, nsc):
    b = pl.program_id(1)
    r = pl.program_id(2)
    row = lax.broadcasted_iota(jnp.int32, (LANES, LANES), 0)
    col = lax.broadcasted_iota(jnp.int32, (LANES, LANES), 1)

    @pl.when((b == 0) & (r == 0))
    def _assemble():
        diag = (row // SSM_GROUP) == (col // SSM_GROUP)
        zeros = jnp.zeros((LANES, LANES), BF16)
        for lag in range(SUB):
            blk = jnp.where(diag, kt_ref[lag], 0.0).astype(BF16)
            for j in range(SUB - lag):
                i = j + lag
                toep_ref[j * LANES:(j + 1) * LANES, i * LANES:(i + 1) * LANES] = blk
        for j in range(SUB):
            for i in range(j):
                if j * LANES // MXU_DIM == i * LANES // MXU_DIM:
                    toep_ref[j * LANES:(j + 1) * LANES, i * LANES:(i + 1) * LANES] = zeros
        for t in range(GROUPS_PER_OCTET // 2):
            in_mask = (row // SSM_GROUP) == (2 * t + col // SSM_STATE)
            out_mask = (col // SSM_GROUP) == (2 * t + row // SSM_STATE)
            for part in range(2):
                c0 = part * OCT_STATE + t * LANES
                for j in range(SUB):
                    win_ref[j * LANES:(j + 1) * LANES, c0:c0 + LANES] = jnp.where(
                        in_mask, xw_ref[j, part], 0.0).astype(BF16)
                    wout_ref[c0:c0 + LANES, j * LANES:(j + 1) * LANES] = jnp.where(
                        out_mask, wc_ref[part, j], 0.0).astype(BF16)

    @pl.when(r == 0)
    def _meta_state():
        for j in range(SUB):
            lhsm_ref[:, j * LANES:(j + 1) * LANES] = um_ref[pl.ds(j, 8, stride=SUB), :].astype(BF16)
        st_ref[0:8, :] = jnp.dot(lhsm_ref[...], win_ref[...], preferred_element_type=F32)

    @pl.when(r > 0)
    def _carry_state():
        st_ref[0:8, :] = st_ref[nsc:nsc + 8, :]

    for j in range(SUB):
        lhs_ref[:, j * LANES:(j + 1) * LANES] = u_ref[pl.ds(j, nsc, stride=SUB), :].astype(BF16)
    lhs = lhs_ref[...]
    st_ref[8:8 + nsc, :] = jnp.dot(lhs, win_ref[...], preferred_element_type=F32)

    a_re = a16_ref[:, 0:OCT_STATE]
    a_im = a16_ref[:, OCT_STATE:2 * OCT_STATE]

    def step(c, carry):
        p_re, p_im = carry
        rs = pl.ds(8 + c, 1)
        n_re = a_re * p_re - a_im * p_im + st_ref[rs, 0:OCT_STATE]
        n_im = a_re * p_im + a_im * p_re + st_ref[rs, OCT_STATE:2 * OCT_STATE]
        st_ref[rs, 0:OCT_STATE] = n_re
        st_ref[rs, OCT_STATE:2 * OCT_STATE] = n_im
        return n_re, n_im

    lax.fori_loop(0, nsc, step, (st_ref[7:8, 0:OCT_STATE], st_ref[7:8, OCT_STATE:2 * OCT_STATE]))

    s_in = st_ref[pl.ds(7, nsc), :].astype(BF16)
    d = d_ref[...]
    per_tile = MXU_DIM // LANES
    for i2 in range(SUB // per_tile):
        cols = slice(i2 * MXU_DIM, (i2 + 1) * MXU_DIM)
        kk = (i2 + 1) * MXU_DIM
        y = (jnp.dot(lhs_ref[:, 0:kk], toep_ref[0:kk, cols], preferred_element_type=F32)
             + jnp.dot(s_in, wout_ref[:, cols], preferred_element_type=F32))
        for ii in range(per_tile):
            i = i2 * per_tile + ii
            yi = y[:, ii * LANES:(ii + 1) * LANES] + d * u_ref[pl.ds(i, nsc, stride=SUB), :]
            y_ref[pl.ds(i, nsc, stride=SUB), :] = jax.nn.gelu(yi)


def _s5(u, um, kt, xw, wc, a16, d, bsz, seq, rows):
    nsc = rows // SUB
    steps = seq // rows
    kdim = SUB * LANES
    oct_blk = lambda shape: pl.BlockSpec((None,) + shape, lambda o, b, r: (o,) + (0,) * len(shape))
    tok_blk = pl.BlockSpec((None, rows, LANES), lambda o, b, r: (o, b * steps + r, 0))
    return pl.pallas_call(
        functools.partial(_s5_kernel, nsc=nsc),
        grid=(OCTETS, bsz, steps),
        in_specs=[tok_blk, oct_blk((LANES, LANES)), oct_blk((SUB, LANES, LANES)),
                  oct_blk((SUB, 2, LANES, LANES)), oct_blk((2, SUB, LANES, LANES)),
                  oct_blk((1, 2 * OCT_STATE)), oct_blk((1, LANES))],
        out_specs=tok_blk,
        out_shape=jax.ShapeDtypeStruct(u.shape, F32),
        scratch_shapes=[
            pltpu.VMEM((kdim, kdim), BF16),
            pltpu.VMEM((kdim, 2 * OCT_STATE), BF16),
            pltpu.VMEM((2 * OCT_STATE, kdim), BF16),
            pltpu.VMEM((nsc, kdim), BF16),
            pltpu.VMEM((8, kdim), BF16),
            pltpu.VMEM((nsc + 8, 2 * OCT_STATE), F32),
        ],
        compiler_params=pltpu.CompilerParams(
            dimension_semantics=("arbitrary", "arbitrary", "arbitrary"),
            vmem_limit_bytes=VMEM_LIMIT),
        name="s5_mixer",
    )(u, um, kt, xw, wc, a16, d)


def _mix_ffn_kernel(h1_ref, ret_ref, y_ref, gw_ref, gb_ref, snw_ref, wo_ref, n3w_ref,
                    wg_ref, wu_ref, wd_ref, fnw_ref, out_ref, acc_ref):
    y = jnp.concatenate([y_ref[o] for o in range(OCTETS)], axis=-1)
    z = y * jax.nn.sigmoid(jnp.dot(y.astype(BF16), gw_ref[...], preferred_element_type=F32)
                           + gb_ref[...])
    ssm = _rms(z, snw_ref[...]).astype(BF16)
    mixed = (jnp.dot(ret_ref[...], wo_ref[0:RET_WIDTH, :], preferred_element_type=F32)
             + jnp.dot(ssm, wo_ref[RET_WIDTH:, :], preferred_element_type=F32))
    h2 = h1_ref[...] + mixed
    h3 = _swiglu_half_step(h2, n3w_ref, wg_ref, wu_ref, wd_ref, acc_ref)
    out_ref[...] = _rms(h3, fnw_ref[...])


def _mix_ffn(h1, ret, y, glu_w, glu_b, ssm_norm_w, w_out, n3w, wg, wu, wd, fnw, tm):
    rows = h1.shape[0]
    row_blk = lambda i: (i, 0)
    return pl.pallas_call(
        _mix_ffn_kernel,
        grid=(rows // tm,),
        in_specs=[
            pl.BlockSpec((tm, D_MODEL), row_blk),
            pl.BlockSpec((tm, RET_WIDTH), row_blk),
            pl.BlockSpec((OCTETS, tm, LANES), lambda i: (0, i, 0)),
            _resident((SSM_WIDTH, SSM_WIDTH)),
            _resident((1, SSM_WIDTH)),
            _resident((1, SSM_WIDTH)),
            _resident((D_MODEL, D_MODEL)),
            _resident((1, D_MODEL)),
            _resident((D_MODEL, D_FF)),
            _resident((D_MODEL, D_FF)),
            _resident((D_FF, D_MODEL)),
            _resident((1, D_MODEL)),
        ],
        out_specs=pl.BlockSpec((tm, D_MODEL), row_blk),
        out_shape=jax.ShapeDtypeStruct((rows, D_MODEL), F32),
        scratch_shapes=[pltpu.VMEM((tm, D_MODEL), F32)],
        compiler_params=pltpu.CompilerParams(
            dimension_semantics=("arbitrary",), vmem_limit_bytes=VMEM_LIMIT),
        name="mix_ffn",
    )(h1, ret, y, glu_w, glu_b, ssm_norm_w, w_out, n3w, wg, wu, wd, fnw)


def _rope_tables(n_pos):
    freqs = 1.0 / (ROPE_BASE ** (np.arange(0, HEAD_DIM, 2, dtype=np.float64) / HEAD_DIM))
    ang = np.arange(n_pos, dtype=np.float64)[:, None] * freqs[None, :]
    cos = np.cos(ang)
    sin = np.sin(ang)
    cos = np.concatenate([cos, cos], axis=-1).astype(np.float32)
    sin = np.concatenate([-sin, sin], axis=-1).astype(np.float32)
    split = lambda t: (jnp.asarray(t[:N_META]), jnp.asarray(t[N_META:]))
    return split(cos), split(sin)


def kernel(x, meta_tokens, ffn1_norm_w, ffn1_w_gate, ffn1_w_up, ffn1_w_down, mix_norm_w, w_in,
           ret_norm_w, ssm_lambda_re, ssm_lambda_im, ssm_log_dt, ssm_b_re, ssm_b_im, ssm_c_re,
           ssm_c_im, ssm_d, ssm_glu_w, ssm_glu_b, ssm_norm_w, w_out, ffn2_norm_w, ffn2_w_gate,
           ffn2_w_up, ffn2_w_down, final_norm_w):
    bsz, seq, _ = x.shape
    assert ffn1_norm_w.shape[0] == 1, "single layer only"
    tm = 512
    ret_rows = 1024
    s5_rows = 4096
    assert seq % tm == 0 and seq % ret_rows == 0 and seq % s5_rows == 0

    row = lambda a: a.reshape(1, -1)
    bf = lambda a: a.astype(BF16)
    l = 0
    (cos_m, cos), (sin_m, sin) = _rope_tables(N_META + seq)
    ffn1 = (row(ffn1_norm_w[l]), bf(ffn1_w_gate[l]), bf(ffn1_w_up[l]), bf(ffn1_w_down[l]),
            row(mix_norm_w[l]), bf(w_in[l]))

    _, _, km, vm, _, um = _ffn_inproj(meta_tokens, cos_m, sin_m, *ffn1, tm=N_META)
    h1, q, k, v, g, u = _ffn_inproj(x.reshape(bsz * seq, D_MODEL), cos, sin,
                                    *ffn1, tm=tm)

    ret = _retention(q, k, v, g, km, vm, row(ret_norm_w[l]), bsz, seq, ret_rows)

    xr, xi, wor, woi, kq, ar, ai = _s5_weights(
        ssm_lambda_re[l], ssm_lambda_im[l], ssm_log_dt[l], ssm_b_re[l], ssm_b_im[l],
        ssm_c_re[l], ssm_c_im[l])
    xw, wc, kt, a16 = _s5_pack_weights(xr, xi, wor, woi, kq, ar, ai)
    um_pad = jnp.pad(um, ((0, 0), (LANES - N_META, 0), (0, 0)))
    y = _s5(u, um_pad, kt, xw, wc, a16, ssm_d[l].reshape(OCTETS, 1, LANES), bsz, seq, s5_rows)

    out = _mix_ffn(h1, ret, y, bf(ssm_glu_w[l]), row(ssm_glu_b[l]), row(ssm_norm_w[l]),
                   bf(w_out[l]), row(ffn2_norm_w[l]), bf(ffn2_w_gate[l]), bf(ffn2_w_up[l]),
                   bf(ffn2_w_down[l]), row(final_norm_w), tm=tm)
    return out.reshape(bsz, seq, D_MODEL)
```

```python
import functools
import math

import jax
import jax.numpy as jnp
import numpy as np
from jax import lax
from jax.experimental import pallas as pl
from jax.experimental.pallas import tpu as pltpu

D_MODEL = 1024
N_META = 16
RET_HEADS = 4
HEAD_DIM = 128
RET_WIDTH = RET_HEADS * HEAD_DIM
SSM_WIDTH = 512
SSM_GROUP = 16
SSM_GROUPS = SSM_WIDTH // SSM_GROUP
SSM_STATE = 64
RET_BLOCK = 256
D_FF = 2816
FFN_RES = 0.5
ROPE_BASE = 10000.0
EPS = 1e-6
IN_PROJ = 4 * RET_WIDTH + SSM_WIDTH

LANES = 128
MXU_DIM = 256
OCTETS = SSM_WIDTH // LANES
GROUPS_PER_OCTET = LANES // SSM_GROUP
SUB = N_META
OCT_STATE = GROUPS_PER_OCTET * SSM_STATE
FF_CHUNK = 256
VMEM_LIMIT = 56 * 1024 * 1024

F32 = jnp.float32
BF16 = jnp.bfloat16


def _rms(x, w):
    return x * lax.rsqrt(jnp.mean(x * x, axis=-1, keepdims=True) + EPS) * w


def _swiglu_half_step(h, nw_ref, wg_ref, wu_ref, wd_ref, acc_ref):
    n = _rms(h, nw_ref[...]).astype(BF16)
    for c in range(D_FF // FF_CHUNK):
        sl = slice(c * FF_CHUNK, (c + 1) * FF_CHUNK)
        g = jnp.dot(n, wg_ref[:, sl], preferred_element_type=F32)
        u = jnp.dot(n, wu_ref[:, sl], preferred_element_type=F32)
        a = (g * jax.nn.sigmoid(g) * u).astype(BF16)
        p = jnp.dot(a, wd_ref[sl, :], preferred_element_type=F32)
        if c == 0:
            acc_ref[...] = p
        else:
            acc_ref[...] += p
    return h + FFN_RES * acc_ref[...]


def _rope(x, cos, sin_signed):
    return x * cos + pltpu.roll(x, HEAD_DIM // 2, axis=1) * sin_signed


def _ffn_inproj_kernel(x_ref, cos_ref, sin_ref, n1w_ref, wg_ref, wu_ref, wd_ref, n2w_ref, win_ref,
                       h1_ref, q_ref, k_ref, v_ref, g_ref, u_ref, acc_ref):
    h1 = _swiglu_half_step(x_ref[...], n1w_ref, wg_ref, wu_ref, wd_ref, acc_ref)
    h1_ref[...] = h1
    n = _rms(h1, n2w_ref[...]).astype(BF16)
    cos = cos_ref[...]
    sin = sin_ref[...]
    k_scale = HEAD_DIM ** -0.5

    def proj(part):
        return jnp.dot(n, win_ref[:, part * RET_WIDTH:(part + 1) * RET_WIDTH],
                       preferred_element_type=F32)

    p = proj(0)
    for h in range(RET_HEADS):
        hs = slice(h * HEAD_DIM, (h + 1) * HEAD_DIM)
        q_ref[:, hs] = _rope(p[:, hs], cos, sin).astype(BF16)
    p = proj(1)
    for h in range(RET_HEADS):
        hs = slice(h * HEAD_DIM, (h + 1) * HEAD_DIM)
        k_ref[:, hs] = (_rope(p[:, hs], cos, sin) * k_scale).astype(BF16)
    v_ref[...] = proj(2).astype(BF16)
    g_ref[...] = proj(3)
    p = proj(4)
    for o in range(OCTETS):
        u_ref[o] = p[:, o * LANES:(o + 1) * LANES]


def _resident(shape):
    nd = len(shape)
    return pl.BlockSpec(shape, lambda *_: (0,) * nd, pipeline_mode=pl.Buffered(1))


def _ffn_inproj(x2, cos, sin, n1w, wg, wu, wd, n2w, w_in, tm):
    rows = x2.shape[0]
    pos_blocks = cos.shape[0] // tm
    row_blk = lambda i: (i, 0)
    pos_blk = lambda i: (i % pos_blocks, 0)
    out_shape = (
        jax.ShapeDtypeStruct((rows, D_MODEL), F32),
        jax.ShapeDtypeStruct((rows, RET_WIDTH), BF16),
        jax.ShapeDtypeStruct((rows, RET_WIDTH), BF16),
        jax.ShapeDtypeStruct((rows, RET_WIDTH), BF16),
        jax.ShapeDtypeStruct((rows, RET_WIDTH), F32),
        jax.ShapeDtypeStruct((OCTETS, rows, LANES), F32),
    )
    return pl.pallas_call(
        _ffn_inproj_kernel,
        grid=(rows // tm,),
        in_specs=[
            pl.BlockSpec((tm, D_MODEL), row_blk),
            pl.BlockSpec((tm, HEAD_DIM), pos_blk),
            pl.BlockSpec((tm, HEAD_DIM), pos_blk),
            _resident((1, D_MODEL)),
            _resident((D_MODEL, D_FF)),
            _resident((D_MODEL, D_FF)),
            _resident((D_FF, D_MODEL)),
            _resident((1, D_MODEL)),
            _resident((D_MODEL, IN_PROJ)),
        ],
        out_specs=(
            pl.BlockSpec((tm, D_MODEL), row_blk),
            pl.BlockSpec((tm, RET_WIDTH), row_blk),
            pl.BlockSpec((tm, RET_WIDTH), row_blk),
            pl.BlockSpec((tm, RET_WIDTH), row_blk),
            pl.BlockSpec((tm, RET_WIDTH), row_blk),
            pl.BlockSpec((OCTETS, tm, LANES), lambda i: (0, i, 0)),
        ),
        out_shape=out_shape,
        scratch_shapes=[pltpu.VMEM((tm, D_MODEL), F32)],
        compiler_params=pltpu.CompilerParams(
            dimension_semantics=("arbitrary",), vmem_limit_bytes=VMEM_LIMIT),
        name="ffn_inproj",
    )(x2, cos, sin, n1w, wg, wu, wd, n2w, w_in)


def _retention_kernel(q_ref, k_ref, v_ref, g_ref, km_ref, vm_ref, mask_ref, wq_ref, wk_ref, wm_ref,
                      gc_ref, nw_ref, o_ref, state_ref, *, rows):
    tn = (((0,), (0,)), ((), ()))
    nt = (((1,), (1,)), ((), ()))

    @pl.when(pl.program_id(1) == 0)
    def _():
        for h in range(RET_HEADS):
            hs = slice(h * HEAD_DIM, (h + 1) * HEAD_DIM)
            kw = (km_ref[:, hs].astype(F32) * wm_ref[h]).astype(BF16)
            state_ref[h] = lax.dot_general(kw, vm_ref[:, hs], tn, preferred_element_type=F32)

    def chunk(i, carry):
        r0 = pl.multiple_of(i * RET_BLOCK, RET_BLOCK)
        rs = pl.ds(r0, RET_BLOCK)
        for h in range(RET_HEADS):
            hs = slice(h * HEAD_DIM, (h + 1) * HEAD_DIM)
            qh = q_ref[rs, hs]
            kh = k_ref[rs, hs]
            vh = v_ref[rs, hs]
            st = state_ref[h]
            s = lax.dot_general(qh, kh, nt, preferred_element_type=F32) * mask_ref[h]
            o = (jnp.dot(s.astype(BF16), vh, preferred_element_type=F32)
                 + jnp.dot(qh, st.astype(BF16), preferred_element_type=F32) * wq_ref[h])
            kw = (kh.astype(F32) * wk_ref[h]).astype(BF16)
            state_ref[h] = gc_ref[h] * st + lax.dot_general(kw, vh, tn, preferred_element_type=F32)
            mu = jnp.mean(o, axis=-1, keepdims=True)
            d = o - mu
            var = jnp.mean(d * d, axis=-1, keepdims=True)
            y = d * lax.rsqrt(var + EPS) * nw_ref[:, hs]
            gate = g_ref[rs, hs]
            o_ref[rs, hs] = (gate * jax.nn.sigmoid(gate) * y).astype(BF16)
        return carry

    lax.fori_loop(0, rows // RET_BLOCK, chunk, 0, unroll=2)


def _retention_tables():
    log_g = np.log(1.0 - 2.0 ** (-5.0 - np.arange(RET_HEADS, dtype=np.float64)))
    i = np.arange(RET_BLOCK)
    diff = i[:, None] - i[None, :]
    mask = np.where(diff[None] >= 0, np.exp(log_g[:, None, None] * np.maximum(diff, 0)[None]), 0.0)
    pos = np.arange(RET_BLOCK, dtype=np.float64)
    full = lambda w: np.broadcast_to(w[:, :, None], w.shape + (HEAD_DIM,))
    w_q = full(np.exp(log_g[:, None] * (pos + 1.0)[None]))
    w_k = full(np.exp(log_g[:, None] * (RET_BLOCK - 1 - pos)[None]))
    w_m = full(np.exp(log_g[:, None] * (N_META - 1 - np.arange(N_META, dtype=np.float64))[None]))
    g_c = np.broadcast_to(np.exp(log_g * RET_BLOCK)[:, None, None], (RET_HEADS, HEAD_DIM, HEAD_DIM))
    return tuple(jnp.asarray(t, dtype=F32) for t in (mask, w_q, w_k, w_m, g_c))


def _retention(q, k, v, g, km, vm, ret_norm_w, bsz, seq, rows):
    mask, w_q, w_k, w_m, g_c = _retention_tables()
    steps = seq // rows
    blk = lambda b, c: (b * steps + c, 0)
    full3 = lambda a: pl.BlockSpec(a.shape, lambda b, c: (0, 0, 0))
    full2 = lambda a: pl.BlockSpec(a.shape, lambda b, c: (0, 0))
    return pl.pallas_call(
        functools.partial(_retention_kernel, rows=rows),
        grid=(bsz, steps),
        in_specs=[pl.BlockSpec((rows, RET_WIDTH), blk)] * 4
        + [full2(km), full2(vm), full3(mask), full3(w_q), full3(w_k), full3(w_m), full3(g_c),
           full2(ret_norm_w)],
        out_specs=pl.BlockSpec((rows, RET_WIDTH), blk),
        out_shape=jax.ShapeDtypeStruct((bsz * seq, RET_WIDTH), BF16),
        scratch_shapes=[pltpu.VMEM((RET_HEADS, HEAD_DIM, HEAD_DIM), F32)],
        compiler_params=pltpu.CompilerParams(
            dimension_semantics=("arbitrary", "arbitrary"), vmem_limit_bytes=VMEM_LIMIT),
        name="retention",
    )(q, k, v, g, km, vm, mask, w_q, w_k, w_m, g_c, ret_norm_w)


def _s5_assemble(lre_ref, lim_ref, ldt_ref, btr_ref, bti_ref, ctr_ref, cti_ref,
                 toep_ref, win_ref, wout_ref, pow_ref, xall_ref):
    lre = lre_ref[...]
    lim = lim_ref[...]
    dt = jnp.exp(ldt_ref[...])
    n_pow = pow_ref.shape[1]
    ell = lax.broadcasted_iota(jnp.int32, (n_pow, OCT_STATE), 0).astype(F32)
    mag = jnp.exp(ell * (lre * dt))
    ang = ell * (lim * dt)
    pow_ref[0] = mag * jnp.cos(ang)
    pow_ref[1] = mag * jnp.sin(ang)
    a_re = pow_ref[0, 1:2, :]
    a_im = pow_ref[1, 1:2, :]
    den = lre * lre + lim * lim
    num_re = a_re - 1.0
    coef_re = (num_re * lre + a_im * lim) / den
    coef_im = (a_im * lre - num_re * lim) / den
    btr = btr_ref[...]
    bti = bti_ref[...]
    bbar_re = coef_re * btr - coef_im * bti
    bbar_im = coef_re * bti + coef_im * btr
    ctr = ctr_ref[...]
    cti = cti_ref[...]

    row = lax.broadcasted_iota(jnp.int32, (LANES, OCT_STATE), 0)
    col = lax.broadcasted_iota(jnp.int32, (LANES, OCT_STATE), 1)
    same_group = (row // SSM_GROUP) == (col // SSM_STATE)

    def block_diag(x):
        return jnp.where(same_group, jnp.concatenate([x] * GROUPS_PER_OCTET, axis=0), 0.0)

    for l in range(SUB):
        pr = pow_ref[0, l:l + 1, :]
        pi = pow_ref[1, l:l + 1, :]
        x_re = pr * bbar_re - pi * bbar_im
        x_im = pr * bbar_im + pi * bbar_re
        rows = slice(l * SSM_GROUP, (l + 1) * SSM_GROUP)
        xall_ref[0, rows, :] = x_re
        xall_ref[1, rows, :] = x_im
        j = SUB - 1 - l
        win_ref[j * LANES:(j + 1) * LANES, 0:OCT_STATE] = block_diag(x_re).astype(BF16)
        win_ref[j * LANES:(j + 1) * LANES, OCT_STATE:2 * OCT_STATE] = block_diag(x_im).astype(BF16)
        pr1 = pow_ref[0, l + 1:l + 2, :]
        pi1 = pow_ref[1, l + 1:l + 2, :]
        w_re = pr1 * ctr - pi1 * cti
        w_im = pr1 * cti + pi1 * ctr
        wout_ref[0:OCT_STATE, l * LANES:(l + 1) * LANES] = block_diag(w_re).T.astype(BF16)
        wout_ref[OCT_STATE:2 * OCT_STATE, l * LANES:(l + 1) * LANES] = block_diag(-w_im).T.astype(BF16)

    k_all = (jnp.dot(xall_ref[0], block_diag(ctr).T, precision=lax.Precision.HIGHEST,
                     preferred_element_type=F32)
             - jnp.dot(xall_ref[1], block_diag(cti).T, precision=lax.Precision.HIGHEST,
                       preferred_element_type=F32))
    row = lax.broadcasted_iota(jnp.int32, (LANES, LANES), 0)
    col = lax.broadcasted_iota(jnp.int32, (LANES, LANES), 1)
    diag = (row // SSM_GROUP) == (col // SSM_GROUP)
    zeros = jnp.zeros((LANES, LANES), BF16)
    for lag in range(SUB):
        k_lag = k_all[lag * SSM_GROUP:(lag + 1) * SSM_GROUP, :]
        blk = jnp.where(diag, jnp.concatenate([k_lag] * GROUPS_PER_OCTET, axis=0), 0.0).astype(BF16)
        for j in range(SUB - lag):
            i = j + lag
            toep_ref[j * LANES:(j + 1) * LANES, i * LANES:(i + 1) * LANES] = blk
    for j in range(SUB):
        for i in range(j):
            if j * LANES // MXU_DIM == i * LANES // MXU_DIM:
                toep_ref[j * LANES:(j + 1) * LANES, i * LANES:(i + 1) * LANES] = zeros


def _s5_kernel(u_ref, um_ref, lre_ref, lim_ref, ldt_ref, btr_ref, bti_ref, ctr_ref, cti_ref, d_ref,
               y_ref, toep_ref, win_ref, wout_ref, pow_ref, xall_ref, lhs_ref, lhsm_ref, st_ref,
               *, nsc):
    b = pl.program_id(1)
    r = pl.program_id(2)

    @pl.when((b == 0) & (r == 0))
    def _():
        _s5_assemble(lre_ref, lim_ref, ldt_ref, btr_ref, bti_ref, ctr_ref, cti_ref,
                     toep_ref, win_ref, wout_ref, pow_ref, xall_ref)

    @pl.when(r == 0)
    def _meta_state():
        for j in range(SUB):
            lhsm_ref[:, j * LANES:(j + 1) * LANES] = um_ref[pl.ds(j, 8, stride=SUB), :].astype(BF16)
        st_ref[0:8, :] = jnp.dot(lhsm_ref[...], win_ref[...], preferred_element_type=F32)

    @pl.when(r > 0)
    def _carry_state():
        st_ref[0:8, :] = st_ref[nsc:nsc + 8, :]

    for j in range(SUB):
        lhs_ref[:, j * LANES:(j + 1) * LANES] = u_ref[pl.ds(j, nsc, stride=SUB), :].astype(BF16)
    st_ref[8:8 + nsc, :] = jnp.dot(lhs_ref[...], win_ref[...], preferred_element_type=F32)

    a_re = pow_ref[0, SUB:SUB + 1, :]
    a_im = pow_ref[1, SUB:SUB + 1, :]

    def step(c, carry):
        p_re, p_im = carry
        rs = pl.ds(8 + c, 1)
        n_re = a_re * p_re - a_im * p_im + st_ref[rs, 0:OCT_STATE]
        n_im = a_re * p_im + a_im * p_re + st_ref[rs, OCT_STATE:2 * OCT_STATE]
        st_ref[rs, 0:OCT_STATE] = n_re
        st_ref[rs, OCT_STATE:2 * OCT_STATE] = n_im
        return n_re, n_im

    lax.fori_loop(0, nsc, step, (st_ref[7:8, 0:OCT_STATE], st_ref[7:8, OCT_STATE:2 * OCT_STATE]))

    s_in = st_ref[pl.ds(7, nsc), :].astype(BF16)
    d = d_ref[...]
    per_tile = MXU_DIM // LANES
    for i2 in range(SUB // per_tile):
        cols = slice(i2 * MXU_DIM, (i2 + 1) * MXU_DIM)
        kk = (i2 + 1) * MXU_DIM
        y = (jnp.dot(lhs_ref[:, 0:kk], toep_ref[0:kk, cols], preferred_element_type=F32)
             + jnp.dot(s_in, wout_ref[:, cols], preferred_element_type=F32))
        for ii in range(per_tile):
            i = i2 * per_tile + ii
            yi = y[:, ii * LANES:(ii + 1) * LANES] + d * u_ref[pl.ds(i, nsc, stride=SUB), :]
            y_ref[pl.ds(i, nsc, stride=SUB), :] = jax.nn.gelu(yi)


def _s5_param_rows(lam_re, lam_im, log_dt, b_re, b_im, c_re, c_im, d):
    go = GROUPS_PER_OCTET
    lanes = lambda a: a.reshape(OCTETS, 1, OCT_STATE)
    ldt = jnp.broadcast_to(log_dt[:, None], (SSM_GROUPS, SSM_STATE))
    bt = lambda a: jnp.transpose(a.reshape(OCTETS, go, SSM_STATE, SSM_GROUP),
                                 (0, 3, 1, 2)).reshape(OCTETS, SSM_GROUP, OCT_STATE)
    ct = lambda a: jnp.transpose(a.reshape(OCTETS, go, SSM_GROUP, SSM_STATE),
                                 (0, 2, 1, 3)).reshape(OCTETS, SSM_GROUP, OCT_STATE)
    return (lanes(lam_re), lanes(lam_im), lanes(ldt), bt(b_re), bt(b_im), ct(c_re), ct(c_im),
            d.reshape(OCTETS, 1, LANES))


def _s5(u, um, params, bsz, seq, rows):
    nsc = rows // SUB
    steps = seq // rows
    kdim = SUB * LANES
    n_pow = SUB + 8
    oct_blk = lambda shape: pl.BlockSpec((None,) + shape, lambda o, b, r: (o,) + (0,) * len(shape))
    tok_blk = pl.BlockSpec((None, rows, LANES), lambda o, b, r: (o, b * steps + r, 0))
    vec = oct_blk((1, OCT_STATE))
    mat = oct_blk((SSM_GROUP, OCT_STATE))
    return pl.pallas_call(
        functools.partial(_s5_kernel, nsc=nsc),
        grid=(OCTETS, bsz, steps),
        in_specs=[tok_blk, oct_blk((LANES, LANES)), vec, vec, vec, mat, mat, mat, mat,
                  oct_blk((1, LANES))],
        out_specs=tok_blk,
        out_shape=jax.ShapeDtypeStruct(u.shape, F32),
        scratch_shapes=[
            pltpu.VMEM((kdim, kdim), BF16),
            pltpu.VMEM((kdim, 2 * OCT_STATE), BF16),
            pltpu.VMEM((2 * OCT_STATE, kdim), BF16),
            pltpu.VMEM((2, n_pow, OCT_STATE), F32),
            pltpu.VMEM((2, SUB * SSM_GROUP, OCT_STATE), F32),
            pltpu.VMEM((nsc, kdim), BF16),
            pltpu.VMEM((8, kdim), BF16),
            pltpu.VMEM((nsc + 8, 2 * OCT_STATE), F32),
        ],
        compiler_params=pltpu.CompilerParams(
            dimension_semantics=("arbitrary", "arbitrary", "arbitrary"),
            vmem_limit_bytes=VMEM_LIMIT),
        name="s5_mixer",
    )(u, um, *params)


def _mix_ffn_kernel(h1_ref, ret_ref, y_ref, gw_ref, gb_ref, snw_ref, wo_ref, n3w_ref,
                    wg_ref, wu_ref, wd_ref, fnw_ref, out_ref, acc_ref):
    y = jnp.concatenate([y_ref[o] for o in range(OCTETS)], axis=-1)
    z = y * jax.nn.sigmoid(jnp.dot(y.astype(BF16), gw_ref[...], preferred_element_type=F32)
                           + gb_ref[...])
    ssm = _rms(z, snw_ref[...]).astype(BF16)
    mixed = (jnp.dot(ret_ref[...], wo_ref[0:RET_WIDTH, :], preferred_element_type=F32)
             + jnp.dot(ssm, wo_ref[RET_WIDTH:, :], preferred_element_type=F32))
    h2 = h1_ref[...] + mixed
    h3 = _swiglu_half_step(h2, n3w_ref, wg_ref, wu_ref, wd_ref, acc_ref)
    out_ref[...] = _rms(h3, fnw_ref[...])


def _mix_ffn(h1, ret, y, glu_w, glu_b, ssm_norm_w, w_out, n3w, wg, wu, wd, fnw, tm):
    rows = h1.shape[0]
    row_blk = lambda i: (i, 0)
    return pl.pallas_call(
        _mix_ffn_kernel,
        grid=(rows // tm,),
        in_specs=[
            pl.BlockSpec((tm, D_MODEL), row_blk),
            pl.BlockSpec((tm, RET_WIDTH), row_blk),
            pl.BlockSpec((OCTETS, tm, LANES), lambda i: (0, i, 0)),
            _resident((SSM_WIDTH, SSM_WIDTH)),
            _resident((1, SSM_WIDTH)),
            _resident((1, SSM_WIDTH)),
            _resident((D_MODEL, D_MODEL)),
            _resident((1, D_MODEL)),
            _resident((D_MODEL, D_FF)),
            _resident((D_MODEL, D_FF)),
            _resident((D_FF, D_MODEL)),
            _resident((1, D_MODEL)),
        ],
        out_specs=pl.BlockSpec((tm, D_MODEL), row_blk),
        out_shape=jax.ShapeDtypeStruct((rows, D_MODEL), F32),
        scratch_shapes=[pltpu.VMEM((tm, D_MODEL), F32)],
        compiler_params=pltpu.CompilerParams(
            dimension_semantics=("arbitrary",), vmem_limit_bytes=VMEM_LIMIT),
        name="mix_ffn",
    )(h1, ret, y, glu_w, glu_b, ssm_norm_w, w_out, n3w, wg, wu, wd, fnw)


def _rope_tables(n_pos):
    freqs = 1.0 / (ROPE_BASE ** (np.arange(0, HEAD_DIM, 2, dtype=np.float64) / HEAD_DIM))
    ang = np.arange(n_pos, dtype=np.float64)[:, None] * freqs[None, :]
    cos = np.cos(ang)
    sin = np.sin(ang)
    cos = np.concatenate([cos, cos], axis=-1).astype(np.float32)
    sin = np.concatenate([-sin, sin], axis=-1).astype(np.float32)
    split = lambda t: (jnp.asarray(t[:N_META]), jnp.asarray(t[N_META:]))
    return split(cos), split(sin)


def kernel(x, meta_tokens, ffn1_norm_w, ffn1_w_gate, ffn1_w_up, ffn1_w_down, mix_norm_w, w_in,
           ret_norm_w, ssm_lambda_re, ssm_lambda_im, ssm_log_dt, ssm_b_re, ssm_b_im, ssm_c_re,
           ssm_c_im, ssm_d, ssm_glu_w, ssm_glu_b, ssm_norm_w, w_out, ffn2_norm_w, ffn2_w_gate,
           ffn2_w_up, ffn2_w_down, final_norm_w):
    bsz, seq, _ = x.shape
    assert ffn1_norm_w.shape[0] == 1, "single layer only"
    tm = 512
    ret_rows = 1024
    s5_rows = 4096
    assert seq % tm == 0 and seq % ret_rows == 0 and seq % s5_rows == 0

    row = lambda a: a.reshape(1, -1)
    bf = lambda a: a.astype(BF16)
    l = 0
    (cos_m, cos), (sin_m, sin) = _rope_tables(N_META + seq)
    ffn1 = (row(ffn1_norm_w[l]), bf(ffn1_w_gate[l]), bf(ffn1_w_up[l]), bf(ffn1_w_down[l]),
            row(mix_norm_w[l]), bf(w_in[l]))

    _, _, km, vm, _, um = _ffn_inproj(meta_tokens, cos_m, sin_m, *ffn1, tm=N_META)
    h1, q, k, v, g, u = _ffn_inproj(x.reshape(bsz * seq, D_MODEL), cos, sin,
                                    *ffn1, tm=tm)

    ret = _retention(q, k, v, g, km, vm, row(ret_norm_w[l]), bsz, seq, ret_rows)

    params = _s5_param_rows(ssm_lambda_re[l], ssm_lambda_im[l], ssm_log_dt[l], ssm_b_re[l],
                            ssm_b_im[l], ssm_c_re[l], ssm_c_im[l], ssm_d[l])
    um_pad = jnp.pad(um, ((0, 0), (LANES - N_META, 0), (0, 0)))
    y = _s5(u, um_pad, params, bsz, seq, s5_rows)

    out = _mix_ffn(h1, ret, y, bf(ssm_glu_w[l]), row(ssm_glu_b[l]), row(ssm_norm_w[l]),
                   bf(w_out[l]), row(ffn2_norm_w[l]), bf(ffn2_w_gate[l]), bf(ffn2_w_up[l]),
                   bf(ffn2_w_down[l]), row(final_norm_w), tm=tm)
    return out.reshape(bsz, seq, D_MODEL)
```

```python
import functools
import math

import jax
import jax.numpy as jnp
import numpy as np
from jax import lax
from jax.experimental import pallas as pl
from jax.experimental.pallas import tpu as pltpu

D_MODEL = 1024
N_META = 16
RET_HEADS = 4
HEAD_DIM = 128
RET_WIDTH = RET_HEADS * HEAD_DIM
SSM_WIDTH = 512
SSM_GROUP = 16
SSM_GROUPS = SSM_WIDTH // SSM_GROUP
SSM_STATE = 64
RET_BLOCK = 256
D_FF = 2816
FFN_RES = 0.5
ROPE_BASE = 10000.0
EPS = 1e-6
IN_PROJ = 4 * RET_WIDTH + SSM_WIDTH

LANES = 128
MXU_DIM = 256
OCTETS = SSM_WIDTH // LANES
GROUPS_PER_OCTET = LANES // SSM_GROUP
SUB = N_META
OCT_STATE = GROUPS_PER_OCTET * SSM_STATE
FF_CHUNK = 256
VMEM_LIMIT = 56 * 1024 * 1024

F32 = jnp.float32
BF16 = jnp.bfloat16


def _rms(x, w):
    return x * lax.rsqrt(jnp.mean(x * x, axis=-1, keepdims=True) + EPS) * w


def _swiglu_half_step(h, nw_ref, wg_ref, wu_ref, wd_ref, acc_ref):
    n = _rms(h, nw_ref[...]).astype(BF16)
    for c in range(D_FF // FF_CHUNK):
        sl = slice(c * FF_CHUNK, (c + 1) * FF_CHUNK)
        g = jnp.dot(n, wg_ref[:, sl], preferred_element_type=F32)
        u = jnp.dot(n, wu_ref[:, sl], preferred_element_type=F32)
        a = (g * jax.nn.sigmoid(g) * u).astype(BF16)
        p = jnp.dot(a, wd_ref[sl, :], preferred_element_type=F32)
        if c == 0:
            acc_ref[...] = p
        else:
            acc_ref[...] += p
    return h + FFN_RES * acc_ref[...]


def _rope(x, cos, sin_signed):
    return x * cos + pltpu.roll(x, HEAD_DIM // 2, axis=1) * sin_signed


def _ffn_inproj_kernel(x_ref, cos_ref, sin_ref, n1w_ref, wg_ref, wu_ref, wd_ref, n2w_ref, win_ref,
                       h1_ref, q_ref, k_ref, v_ref, g_ref, u_ref, acc_ref):
    h1 = _swiglu_half_step(x_ref[...], n1w_ref, wg_ref, wu_ref, wd_ref, acc_ref)
    h1_ref[...] = h1
    n = _rms(h1, n2w_ref[...]).astype(BF16)
    cos = cos_ref[...]
    sin = sin_ref[...]
    k_scale = HEAD_DIM ** -0.5

    def proj(part):
        return jnp.dot(n, win_ref[:, part * RET_WIDTH:(part + 1) * RET_WIDTH],
                       preferred_element_type=F32)

    p = proj(0)
    for h in range(RET_HEADS):
        hs = slice(h * HEAD_DIM, (h + 1) * HEAD_DIM)
        q_ref[:, hs] = _rope(p[:, hs], cos, sin).astype(BF16)
    p = proj(1)
    for h in range(RET_HEADS):
        hs = slice(h * HEAD_DIM, (h + 1) * HEAD_DIM)
        k_ref[:, hs] = (_rope(p[:, hs], cos, sin) * k_scale).astype(BF16)
    v_ref[...] = proj(2).astype(BF16)
    g_ref[...] = proj(3)
    p = proj(4)
    for o in range(OCTETS):
        u_ref[o] = p[:, o * LANES:(o + 1) * LANES]


def _resident(shape):
    nd = len(shape)
    return pl.BlockSpec(shape, lambda *_: (0,) * nd, pipeline_mode=pl.Buffered(1))


def _ffn_inproj(x2, cos, sin, n1w, wg, wu, wd, n2w, w_in, tm):
    rows = x2.shape[0]
    pos_blocks = cos.shape[0] // tm
    row_blk = lambda i: (i, 0)
    pos_blk = lambda i: (i % pos_blocks, 0)
    out_shape = (
        jax.ShapeDtypeStruct((rows, D_MODEL), F32),
        jax.ShapeDtypeStruct((rows, RET_WIDTH), BF16),
        jax.ShapeDtypeStruct((rows, RET_WIDTH), BF16),
        jax.ShapeDtypeStruct((rows, RET_WIDTH), BF16),
        jax.ShapeDtypeStruct((rows, RET_WIDTH), F32),
        jax.ShapeDtypeStruct((OCTETS, rows, LANES), F32),
    )
    return pl.pallas_call(
        _ffn_inproj_kernel,
        grid=(rows // tm,),
        in_specs=[
            pl.BlockSpec((tm, D_MODEL), row_blk),
            pl.BlockSpec((tm, HEAD_DIM), pos_blk),
            pl.BlockSpec((tm, HEAD_DIM), pos_blk),
            _resident((1, D_MODEL)),
            _resident((D_MODEL, D_FF)),
            _resident((D_MODEL, D_FF)),
            _resident((D_FF, D_MODEL)),
            _resident((1, D_MODEL)),
            _resident((D_MODEL, IN_PROJ)),
        ],
        out_specs=(
            pl.BlockSpec((tm, D_MODEL), row_blk),
            pl.BlockSpec((tm, RET_WIDTH), row_blk),
            pl.BlockSpec((tm, RET_WIDTH), row_blk),
            pl.BlockSpec((tm, RET_WIDTH), row_blk),
            pl.BlockSpec((tm, RET_WIDTH), row_blk),
            pl.BlockSpec((OCTETS, tm, LANES), lambda i: (0, i, 0)),
        ),
        out_shape=out_shape,
        scratch_shapes=[pltpu.VMEM((tm, D_MODEL), F32)],
        compiler_params=pltpu.CompilerParams(
            dimension_semantics=("arbitrary",), vmem_limit_bytes=VMEM_LIMIT),
        name="ffn_inproj",
    )(x2, cos, sin, n1w, wg, wu, wd, n2w, w_in)


def _retention_kernel(q_ref, k_ref, v_ref, g_ref, km_ref, vm_ref, mask_ref, wq_ref, wk_ref, wm_ref,
                      gc_ref, nw_ref, o_ref, state_ref, *, rows):
    tn = (((0,), (0,)), ((), ()))
    nt = (((1,), (1,)), ((), ()))

    @pl.when(pl.program_id(1) == 0)
    def _():
        for h in range(RET_HEADS):
            hs = slice(h * HEAD_DIM, (h + 1) * HEAD_DIM)
            kw = (km_ref[:, hs].astype(F32) * wm_ref[h]).astype(BF16)
            state_ref[h] = lax.dot_general(kw, vm_ref[:, hs], tn, preferred_element_type=F32)

    def chunk(i, carry):
        r0 = pl.multiple_of(i * RET_BLOCK, RET_BLOCK)
        rs = pl.ds(r0, RET_BLOCK)
        for h in range(RET_HEADS):
            hs = slice(h * HEAD_DIM, (h + 1) * HEAD_DIM)
            qh = q_ref[rs, hs]
            kh = k_ref[rs, hs]
            vh = v_ref[rs, hs]
            st = state_ref[h]
            s = lax.dot_general(qh, kh, nt, preferred_element_type=F32) * mask_ref[h]
            o = (jnp.dot(s.astype(BF16), vh, preferred_element_type=F32)
                 + jnp.dot(qh, st.astype(BF16), preferred_element_type=F32) * wq_ref[h])
            kw = (kh.astype(F32) * wk_ref[h]).astype(BF16)
            state_ref[h] = gc_ref[h] * st + lax.dot_general(kw, vh, tn, preferred_element_type=F32)
            mu = jnp.mean(o, axis=-1, keepdims=True)
            d = o - mu
            var = jnp.mean(d * d, axis=-1, keepdims=True)
            y = d * lax.rsqrt(var + EPS) * nw_ref[:, hs]
            gate = g_ref[rs, hs]
            o_ref[rs, hs] = (gate * jax.nn.sigmoid(gate) * y).astype(BF16)
        return carry

    lax.fori_loop(0, rows // RET_BLOCK, chunk, 0, unroll=2)


def _retention_tables():
    log_g = np.log(1.0 - 2.0 ** (-5.0 - np.arange(RET_HEADS, dtype=np.float64)))
    i = np.arange(RET_BLOCK)
    diff = i[:, None] - i[None, :]
    mask = np.where(diff[None] >= 0, np.exp(log_g[:, None, None] * np.maximum(diff, 0)[None]), 0.0)
    pos = np.arange(RET_BLOCK, dtype=np.float64)
    full = lambda w: np.broadcast_to(w[:, :, None], w.shape + (HEAD_DIM,))
    w_q = full(np.exp(log_g[:, None] * (pos + 1.0)[None]))
    w_k = full(np.exp(log_g[:, None] * (RET_BLOCK - 1 - pos)[None]))
    w_m = full(np.exp(log_g[:, None] * (N_META - 1 - np.arange(N_META, dtype=np.float64))[None]))
    g_c = np.broadcast_to(np.exp(log_g * RET_BLOCK)[:, None, None], (RET_HEADS, HEAD_DIM, HEAD_DIM))
    return tuple(jnp.asarray(t, dtype=F32) for t in (mask, w_q, w_k, w_m, g_c))


def _retention(q, k, v, g, km, vm, ret_norm_w, bsz, seq, rows):
    mask, w_q, w_k, w_m, g_c = _retention_tables()
    steps = seq // rows
    blk = lambda b, c: (b * steps + c, 0)
    full3 = lambda a: pl.BlockSpec(a.shape, lambda b, c: (0, 0, 0))
    full2 = lambda a: pl.BlockSpec(a.shape, lambda b, c: (0, 0))
    return pl.pallas_call(
        functools.partial(_retention_kernel, rows=rows),
        grid=(bsz, steps),
        in_specs=[pl.BlockSpec((rows, RET_WIDTH), blk)] * 4
        + [full2(km), full2(vm), full3(mask), full3(w_q), full3(w_k), full3(w_m), full3(g_c),
           full2(ret_norm_w)],
        out_specs=pl.BlockSpec((rows, RET_WIDTH), blk),
        out_shape=jax.ShapeDtypeStruct((bsz * seq, RET_WIDTH), BF16),
        scratch_shapes=[pltpu.VMEM((RET_HEADS, HEAD_DIM, HEAD_DIM), F32)],
        compiler_params=pltpu.CompilerParams(
            dimension_semantics=("arbitrary", "arbitrary"), vmem_limit_bytes=VMEM_LIMIT),
        name="retention",
    )(q, k, v, g, km, vm, mask, w_q, w_k, w_m, g_c, ret_norm_w)


def _s5_assemble(lre_ref, lim_ref, ldt_ref, btr_ref, bti_ref, ctr_ref, cti_ref,
                 toep_ref, win_ref, wout_ref, pow_ref, cpow_ref, xall_ref):
    lre = lre_ref[...]
    lim = lim_ref[...]
    dt = jnp.exp(ldt_ref[...])
    n_pow = pow_ref.shape[1]
    ell = lax.broadcasted_iota(jnp.int32, (n_pow, OCT_STATE), 0).astype(F32)
    mag = jnp.exp(ell * (lre * dt))
    ang = ell * (lim * dt)
    pow_ref[0] = mag * jnp.cos(ang)
    pow_ref[1] = mag * jnp.sin(ang)
    ell = ((lax.broadcasted_iota(jnp.int32, (8, OCT_STATE), 0) + 1) * SUB).astype(F32)
    mag = jnp.exp(ell * (lre * dt))
    ang = ell * (lim * dt)
    cpow_ref[0] = mag * jnp.cos(ang)
    cpow_ref[1] = mag * jnp.sin(ang)
    a_re = pow_ref[0, 1:2, :]
    a_im = pow_ref[1, 1:2, :]
    den = lre * lre + lim * lim
    num_re = a_re - 1.0
    coef_re = (num_re * lre + a_im * lim) / den
    coef_im = (a_im * lre - num_re * lim) / den
    btr = btr_ref[...]
    bti = bti_ref[...]
    bbar_re = coef_re * btr - coef_im * bti
    bbar_im = coef_re * bti + coef_im * btr
    ctr = ctr_ref[...]
    cti = cti_ref[...]

    row = lax.broadcasted_iota(jnp.int32, (LANES, OCT_STATE), 0)
    col = lax.broadcasted_iota(jnp.int32, (LANES, OCT_STATE), 1)
    same_group = (row // SSM_GROUP) == (col // SSM_STATE)

    def block_diag(x):
        return jnp.where(same_group, jnp.concatenate([x] * GROUPS_PER_OCTET, axis=0), 0.0)

    for l in range(SUB):
        pr = pow_ref[0, l:l + 1, :]
        pi = pow_ref[1, l:l + 1, :]
        x_re = pr * bbar_re - pi * bbar_im
        x_im = pr * bbar_im + pi * bbar_re
        rows = slice(l * SSM_GROUP, (l + 1) * SSM_GROUP)
        xall_ref[0, rows, :] = x_re
        xall_ref[1, rows, :] = x_im
        j = SUB - 1 - l
        win_ref[j * LANES:(j + 1) * LANES, 0:OCT_STATE] = block_diag(x_re).astype(BF16)
        win_ref[j * LANES:(j + 1) * LANES, OCT_STATE:2 * OCT_STATE] = block_diag(x_im).astype(BF16)
        pr1 = pow_ref[0, l + 1:l + 2, :]
        pi1 = pow_ref[1, l + 1:l + 2, :]
        w_re = pr1 * ctr - pi1 * cti
        w_im = pr1 * cti + pi1 * ctr
        wout_ref[0:OCT_STATE, l * LANES:(l + 1) * LANES] = block_diag(w_re).T.astype(BF16)
        wout_ref[OCT_STATE:2 * OCT_STATE, l * LANES:(l + 1) * LANES] = block_diag(-w_im).T.astype(BF16)

    k_all = (jnp.dot(xall_ref[0], block_diag(ctr).T, precision=lax.Precision.HIGHEST,
                     preferred_element_type=F32)
             - jnp.dot(xall_ref[1], block_diag(cti).T, precision=lax.Precision.HIGHEST,
                       preferred_element_type=F32))
    row = lax.broadcasted_iota(jnp.int32, (LANES, LANES), 0)
    col = lax.broadcasted_iota(jnp.int32, (LANES, LANES), 1)
    diag = (row // SSM_GROUP) == (col // SSM_GROUP)
    zeros = jnp.zeros((LANES, LANES), BF16)
    for lag in range(SUB):
        k_lag = k_all[lag * SSM_GROUP:(lag + 1) * SSM_GROUP, :]
        blk = jnp.where(diag, jnp.concatenate([k_lag] * GROUPS_PER_OCTET, axis=0), 0.0).astype(BF16)
        for j in range(SUB - lag):
            i = j + lag
            toep_ref[j * LANES:(j + 1) * LANES, i * LANES:(i + 1) * LANES] = blk
    for j in range(SUB):
        for i in range(j):
            if j * LANES // MXU_DIM == i * LANES // MXU_DIM:
                toep_ref[j * LANES:(j + 1) * LANES, i * LANES:(i + 1) * LANES] = zeros


def _s5_state_scan(st_ref, zb_ref, cpow_ref, nsc):
    tiles = nsc // 8
    re = slice(0, OCT_STATE)
    im = slice(OCT_STATE, 2 * OCT_STATE)
    rowmod = lax.broadcasted_iota(jnp.int32, (8, OCT_STATE), 0)
    src, dst = st_ref, zb_ref
    for s in (1, 2, 4):
        keep = rowmod >= s
        c_re = jnp.concatenate([jnp.where(keep, cpow_ref[0, s - 1:s, :], 0.0)] * tiles, axis=0)
        c_im = jnp.concatenate([jnp.where(keep, cpow_ref[1, s - 1:s, :], 0.0)] * tiles, axis=0)
        sh_re = src[pl.ds(8 - s, nsc), re]
        sh_im = src[pl.ds(8 - s, nsc), im]
        dst[8:8 + nsc, re] = src[8:8 + nsc, re] + c_re * sh_re - c_im * sh_im
        dst[8:8 + nsc, im] = src[8:8 + nsc, im] + c_re * sh_im + c_im * sh_re
        src, dst = dst, src
    ca_re = cpow_ref[0]
    ca_im = cpow_ref[1]
    c_re = st_ref[7:8, re]
    c_im = st_ref[7:8, im]
    for t in range(tiles):
        rows = slice(8 + 8 * t, 16 + 8 * t)
        cb_re = jnp.broadcast_to(c_re, (8, OCT_STATE))
        cb_im = jnp.broadcast_to(c_im, (8, OCT_STATE))
        p_re = src[rows, re] + ca_re * cb_re - ca_im * cb_im
        p_im = src[rows, im] + ca_re * cb_im + ca_im * cb_re
        st_ref[rows, re] = p_re
        st_ref[rows, im] = p_im
        c_re = p_re[7:8]
        c_im = p_im[7:8]


def _s5_kernel(u_ref, um_ref, lre_ref, lim_ref, ldt_ref, btr_ref, bti_ref, ctr_ref, cti_ref, d_ref,
               y_ref, toep_ref, win_ref, wout_ref, pow_ref, cpow_ref, xall_ref, lhs_ref, lhsm_ref,
               st_ref, zb_ref, *, nsc):
    b = pl.program_id(1)
    r = pl.program_id(2)

    @pl.when((b == 0) & (r == 0))
    def _():
        _s5_assemble(lre_ref, lim_ref, ldt_ref, btr_ref, bti_ref, ctr_ref, cti_ref,
                     toep_ref, win_ref, wout_ref, pow_ref, cpow_ref, xall_ref)
        zb_ref[0:8, :] = jnp.zeros((8, 2 * OCT_STATE), F32)

    @pl.when(r == 0)
    def _meta_state():
        for j in range(SUB):
            lhsm_ref[:, j * LANES:(j + 1) * LANES] = um_ref[pl.ds(j, 8, stride=SUB), :].astype(BF16)
        st_ref[0:8, :] = jnp.dot(lhsm_ref[...], win_ref[...], preferred_element_type=F32)

    @pl.when(r > 0)
    def _carry_state():
        st_ref[0:8, :] = st_ref[nsc:nsc + 8, :]

    for j in range(SUB):
        lhs_ref[:, j * LANES:(j + 1) * LANES] = u_ref[pl.ds(j, nsc, stride=SUB), :].astype(BF16)
    st_ref[8:8 + nsc, :] = jnp.dot(lhs_ref[...], win_ref[...], preferred_element_type=F32)

    _s5_state_scan(st_ref, zb_ref, cpow_ref, nsc)

    s_in = st_ref[pl.ds(7, nsc), :].astype(BF16)
    d = d_ref[...]
    per_tile = MXU_DIM // LANES
    for i2 in range(SUB // per_tile):
        cols = slice(i2 * MXU_DIM, (i2 + 1) * MXU_DIM)
        kk = (i2 + 1) * MXU_DIM
        y = (jnp.dot(lhs_ref[:, 0:kk], toep_ref[0:kk, cols], preferred_element_type=F32)
             + jnp.dot(s_in, wout_ref[:, cols], preferred_element_type=F32))
        for ii in range(per_tile):
            i = i2 * per_tile + ii
            yi = y[:, ii * LANES:(ii + 1) * LANES] + d * u_ref[pl.ds(i, nsc, stride=SUB), :]
            y_ref[pl.ds(i, nsc, stride=SUB), :] = jax.nn.gelu(yi)


def _s5_param_rows(lam_re, lam_im, log_dt, b_re, b_im, c_re, c_im, d):
    go = GROUPS_PER_OCTET
    lanes = lambda a: a.reshape(OCTETS, 1, OCT_STATE)
    ldt = jnp.broadcast_to(log_dt[:, None], (SSM_GROUPS, SSM_STATE))
    bt = lambda a: jnp.transpose(a.reshape(OCTETS, go, SSM_STATE, SSM_GROUP),
                                 (0, 3, 1, 2)).reshape(OCTETS, SSM_GROUP, OCT_STATE)
    ct = lambda a: jnp.transpose(a.reshape(OCTETS, go, SSM_GROUP, SSM_STATE),
                                 (0, 2, 1, 3)).reshape(OCTETS, SSM_GROUP, OCT_STATE)
    return (lanes(lam_re), lanes(lam_im), lanes(ldt), bt(b_re), bt(b_im), ct(c_re), ct(c_im),
            d.reshape(OCTETS, 1, LANES))


def _s5(u, um, params, bsz, seq, rows):
    nsc = rows // SUB
    steps = seq // rows
    kdim = SUB * LANES
    n_pow = SUB + 8
    oct_blk = lambda shape: pl.BlockSpec((None,) + shape, lambda o, b, r: (o,) + (0,) * len(shape))
    tok_blk = pl.BlockSpec((None, rows, LANES), lambda o, b, r: (o, b * steps + r, 0))
    vec = oct_blk((1, OCT_STATE))
    mat = oct_blk((SSM_GROUP, OCT_STATE))
    return pl.pallas_call(
        functools.partial(_s5_kernel, nsc=nsc),
        grid=(OCTETS, bsz, steps),
        in_specs=[tok_blk, oct_blk((LANES, LANES)), vec, vec, vec, mat, mat, mat, mat,
                  oct_blk((1, LANES))],
        out_specs=tok_blk,
        out_shape=jax.ShapeDtypeStruct(u.shape, F32),
        scratch_shapes=[
            pltpu.VMEM((kdim, kdim), BF16),
            pltpu.VMEM((kdim, 2 * OCT_STATE), BF16),
            pltpu.VMEM((2 * OCT_STATE, kdim), BF16),
            pltpu.VMEM((2, n_pow, OCT_STATE), F32),
            pltpu.VMEM((2, 8, OCT_STATE), F32),
            pltpu.VMEM((2, SUB * SSM_GROUP, OCT_STATE), F32),
            pltpu.VMEM((nsc, kdim), BF16),
            pltpu.VMEM((8, kdim), BF16),
            pltpu.VMEM((nsc + 8, 2 * OCT_STATE), F32),
            pltpu.VMEM((nsc + 8, 2 * OCT_STATE), F32),
        ],
        compiler_params=pltpu.CompilerParams(
            dimension_semantics=("arbitrary", "arbitrary", "arbitrary"),
            vmem_limit_bytes=VMEM_LIMIT),
        name="s5_mixer",
    )(u, um, *params)


def _mix_ffn_kernel(h1_ref, ret_ref, y_ref, gw_ref, gb_ref, snw_ref, wo_ref, n3w_ref,
                    wg_ref, wu_ref, wd_ref, fnw_ref, out_ref, acc_ref):
    y = jnp.concatenate([y_ref[o] for o in range(OCTETS)], axis=-1)
    z = y * jax.nn.sigmoid(jnp.dot(y.astype(BF16), gw_ref[...], preferred_element_type=F32)
                           + gb_ref[...])
    ssm = _rms(z, snw_ref[...]).astype(BF16)
    mixed = (jnp.dot(ret_ref[...], wo_ref[0:RET_WIDTH, :], preferred_element_type=F32)
             + jnp.dot(ssm, wo_ref[RET_WIDTH:, :], preferred_element_type=F32))
    h2 = h1_ref[...] + mixed
    h3 = _swiglu_half_step(h2, n3w_ref, wg_ref, wu_ref, wd_ref, acc_ref)
    out_ref[...] = _rms(h3, fnw_ref[...])


def _mix_ffn(h1, ret, y, glu_w, glu_b, ssm_norm_w, w_out, n3w, wg, wu, wd, fnw, tm):
    rows = h1.shape[0]
    row_blk = lambda i: (i, 0)
    return pl.pallas_call(
        _mix_ffn_kernel,
        grid=(rows // tm,),
        in_specs=[
            pl.BlockSpec((tm, D_MODEL), row_blk),
            pl.BlockSpec((tm, RET_WIDTH), row_blk),
            pl.BlockSpec((OCTETS, tm, LANES), lambda i: (0, i, 0)),
            _resident((SSM_WIDTH, SSM_WIDTH)),
            _resident((1, SSM_WIDTH)),
            _resident((1, SSM_WIDTH)),
            _resident((D_MODEL, D_MODEL)),
            _resident((1, D_MODEL)),
            _resident((D_MODEL, D_FF)),
            _resident((D_MODEL, D_FF)),
            _resident((D_FF, D_MODEL)),
            _resident((1, D_MODEL)),
        ],
        out_specs=pl.BlockSpec((tm, D_MODEL), row_blk),
        out_shape=jax.ShapeDtypeStruct((rows, D_MODEL), F32),
        scratch_shapes=[pltpu.VMEM((tm, D_MODEL), F32)],
        compiler_params=pltpu.CompilerParams(
            dimension_semantics=("arbitrary",), vmem_limit_bytes=VMEM_LIMIT),
        name="mix_ffn",
    )(h1, ret, y, glu_w, glu_b, ssm_norm_w, w_out, n3w, wg, wu, wd, fnw)


def _rope_tables(n_pos):
    freqs = 1.0 / (ROPE_BASE ** (np.arange(0, HEAD_DIM, 2, dtype=np.float64) / HEAD_DIM))
    ang = np.arange(n_pos, dtype=np.float64)[:, None] * freqs[None, :]
    cos = np.cos(ang)
    sin = np.sin(ang)
    cos = np.concatenate([cos, cos], axis=-1).astype(np.float32)
    sin = np.concatenate([-sin, sin], axis=-1).astype(np.float32)
    split = lambda t: (jnp.asarray(t[:N_META]), jnp.asarray(t[N_META:]))
    return split(cos), split(sin)


def kernel(x, meta_tokens, ffn1_norm_w, ffn1_w_gate, ffn1_w_up, ffn1_w_down, mix_norm_w, w_in,
           ret_norm_w, ssm_lambda_re, ssm_lambda_im, ssm_log_dt, ssm_b_re, ssm_b_im, ssm_c_re,
           ssm_c_im, ssm_d, ssm_glu_w, ssm_glu_b, ssm_norm_w, w_out, ffn2_norm_w, ffn2_w_gate,
           ffn2_w_up, ffn2_w_down, final_norm_w):
    bsz, seq, _ = x.shape
    assert ffn1_norm_w.shape[0] == 1, "single layer only"
    tm = 512
    ret_rows = 1024
    s5_rows = 4096
    assert seq % tm == 0 and seq % ret_rows == 0 and seq % s5_rows == 0

    row = lambda a: a.reshape(1, -1)
    bf = lambda a: a.astype(BF16)
    l = 0
    (cos_m, cos), (sin_m, sin) = _rope_tables(N_META + seq)
    ffn1 = (row(ffn1_norm_w[l]), bf(ffn1_w_gate[l]), bf(ffn1_w_up[l]), bf(ffn1_w_down[l]),
            row(mix_norm_w[l]), bf(w_in[l]))

    _, _, km, vm, _, um = _ffn_inproj(meta_tokens, cos_m, sin_m, *ffn1, tm=N_META)
    h1, q, k, v, g, u = _ffn_inproj(x.reshape(bsz * seq, D_MODEL), cos, sin,
                                    *ffn1, tm=tm)

    ret = _retention(q, k, v, g, km, vm, row(ret_norm_w[l]), bsz, seq, ret_rows)

    params = _s5_param_rows(ssm_lambda_re[l], ssm_lambda_im[l], ssm_log_dt[l], ssm_b_re[l],
                            ssm_b_im[l], ssm_c_re[l], ssm_c_im[l], ssm_d[l])
    um_pad = jnp.pad(um, ((0, 0), (LANES - N_META, 0), (0, 0)))
    y = _s5(u, um_pad, params, bsz, seq, s5_rows)

    out = _mix_ffn(h1, ret, y, bf(ssm_glu_w[l]), row(ssm_glu_b[l]), row(ssm_norm_w[l]),
                   bf(w_out[l]), row(ffn2_norm_w[l]), bf(ffn2_w_gate[l]), bf(ffn2_w_up[l]),
                   bf(ffn2_w_down[l]), row(final_norm_w), tm=tm)
    return out.reshape(bsz, seq, D_MODEL)
```

```python
import functools
import math

import jax
import jax.numpy as jnp
import numpy as np
from jax import lax
from jax.experimental import pallas as pl
from jax.experimental.pallas import tpu as pltpu

D_MODEL = 1024
N_META = 16
RET_HEADS = 4
HEAD_DIM = 128
RET_WIDTH = RET_HEADS * HEAD_DIM
SSM_WIDTH = 512
SSM_GROUP = 16
SSM_GROUPS = SSM_WIDTH // SSM_GROUP
SSM_STATE = 64
RET_BLOCK = 256
D_FF = 2816
FFN_RES = 0.5
ROPE_BASE = 10000.0
EPS = 1e-6
IN_PROJ = 4 * RET_WIDTH + SSM_WIDTH

LANES = 128
MXU_DIM = 256
OCTETS = SSM_WIDTH // LANES
GROUPS_PER_OCTET = LANES // SSM_GROUP
SUB = N_META
OCT_STATE = GROUPS_PER_OCTET * SSM_STATE
FF_CHUNK = 256
WEIGHT_CHUNKS = 8
VMEM_LIMIT = 56 * 1024 * 1024

F32 = jnp.float32
BF16 = jnp.bfloat16


def _rms(x, w):
    return x * lax.rsqrt(jnp.mean(x * x, axis=-1, keepdims=True) + EPS) * w


def _swiglu_half_step(h, nw_ref, wg_ref, wu_ref, wd_ref, acc_ref):
    n = _rms(h, nw_ref[...]).astype(BF16)
    for c in range(D_FF // FF_CHUNK):
        sl = slice(c * FF_CHUNK, (c + 1) * FF_CHUNK)
        g = jnp.dot(n, wg_ref[:, sl], preferred_element_type=F32)
        u = jnp.dot(n, wu_ref[:, sl], preferred_element_type=F32)
        a = (g * jax.nn.sigmoid(g) * u).astype(BF16)
        p = jnp.dot(a, wd_ref[sl, :], preferred_element_type=F32)
        if c == 0:
            acc_ref[...] = p
        else:
            acc_ref[...] += p
    return h + FFN_RES * acc_ref[...]


def _rope(x, cos, sin_signed):
    return x * cos + pltpu.roll(x, HEAD_DIM // 2, axis=1) * sin_signed


def _cast_weight_chunks(step, pairs):
    for src_ref, dst_ref in pairs:
        rows = src_ref.shape[0]
        r0 = pl.multiple_of(step * rows, rows)
        dst_ref[pl.ds(r0, rows), :] = src_ref[...].astype(BF16)


def _weight_chunk_spec(shape):
    rows = shape[0] // WEIGHT_CHUNKS
    assert rows * WEIGHT_CHUNKS == shape[0] and rows % 16 == 0, shape
    return pl.BlockSpec((rows, shape[1]), lambda i: (jnp.minimum(i, WEIGHT_CHUNKS - 1), 0))


def _tile_index(i):
    return jnp.maximum(i - WEIGHT_CHUNKS, 0)


def _ffn_inproj_kernel(x_ref, cos_ref, sin_ref, xm_ref, cosm_ref, sinm_ref, n1w_ref, wg32_ref,
                       wu32_ref, wd32_ref, n2w_ref, win32_ref,
                       h1_ref, q_ref, k_ref, v_ref, g_ref, u_ref, km_ref, vm_ref, um_ref,
                       wg_ref, wu_ref, wd_ref, win_ref, acc_ref, accm_ref):
    step = pl.program_id(0)
    k_scale = HEAD_DIM ** -0.5

    def mix_norm(h1):
        return _rms(h1, n2w_ref[...]).astype(BF16)

    def in_proj(n, part):
        return jnp.dot(n, win_ref[:, part * RET_WIDTH:(part + 1) * RET_WIDTH],
                       preferred_element_type=F32)

    def rope_heads(p, cos, sin, scale, out_ref):
        for h in range(RET_HEADS):
            hs = slice(h * HEAD_DIM, (h + 1) * HEAD_DIM)
            y = _rope(p[:, hs], cos, sin)
            out_ref[:, hs] = (y if scale is None else y * scale).astype(BF16)

    def split_octets(p, out_ref):
        for o in range(OCTETS):
            out_ref[o] = p[:, o * LANES:(o + 1) * LANES]

    @pl.when(step < WEIGHT_CHUNKS)
    def _():
        _cast_weight_chunks(step, ((wg32_ref, wg_ref), (wu32_ref, wu_ref), (wd32_ref, wd_ref),
                                   (win32_ref, win_ref)))

    @pl.when(step == WEIGHT_CHUNKS)
    def _():
        h1 = _swiglu_half_step(xm_ref[...], n1w_ref, wg_ref, wu_ref, wd_ref, accm_ref)
        n = mix_norm(h1)
        rope_heads(in_proj(n, 1), cosm_ref[...], sinm_ref[...], k_scale, km_ref)
        vm_ref[...] = in_proj(n, 2).astype(BF16)
        split_octets(in_proj(n, 4), um_ref)

    @pl.when(step >= WEIGHT_CHUNKS)
    def _():
        h1 = _swiglu_half_step(x_ref[...], n1w_ref, wg_ref, wu_ref, wd_ref, acc_ref)
        h1_ref[...] = h1
        n = mix_norm(h1)
        cos = cos_ref[...]
        sin = sin_ref[...]
        rope_heads(in_proj(n, 0), cos, sin, None, q_ref)
        rope_heads(in_proj(n, 1), cos, sin, k_scale, k_ref)
        v_ref[...] = in_proj(n, 2).astype(BF16)
        g_ref[...] = in_proj(n, 3)
        split_octets(in_proj(n, 4), u_ref)


def _resident(shape):
    nd = len(shape)
    return pl.BlockSpec(shape, lambda *_: (0,) * nd, pipeline_mode=pl.Buffered(1))


def _ffn_inproj(x2, xm, cos, sin, cos_m, sin_m, n1w, wg, wu, wd, n2w, w_in, tm):
    rows = x2.shape[0]
    pos_blocks = cos.shape[0] // tm
    row_blk = lambda i: (_tile_index(i), 0)
    pos_blk = lambda i: (_tile_index(i) % pos_blocks, 0)
    out_shape = (
        jax.ShapeDtypeStruct((rows, D_MODEL), F32),
        jax.ShapeDtypeStruct((rows, RET_WIDTH), BF16),
        jax.ShapeDtypeStruct((rows, RET_WIDTH), BF16),
        jax.ShapeDtypeStruct((rows, RET_WIDTH), BF16),
        jax.ShapeDtypeStruct((rows, RET_WIDTH), F32),
        jax.ShapeDtypeStruct((OCTETS, rows, LANES), F32),
        jax.ShapeDtypeStruct((N_META, RET_WIDTH), BF16),
        jax.ShapeDtypeStruct((N_META, RET_WIDTH), BF16),
        jax.ShapeDtypeStruct((OCTETS, N_META, LANES), F32),
    )
    return pl.pallas_call(
        _ffn_inproj_kernel,
        grid=(WEIGHT_CHUNKS + rows // tm,),
        in_specs=[
            pl.BlockSpec((tm, D_MODEL), row_blk),
            pl.BlockSpec((tm, HEAD_DIM), pos_blk),
            pl.BlockSpec((tm, HEAD_DIM), pos_blk),
            _resident((N_META, D_MODEL)),
            _resident((N_META, HEAD_DIM)),
            _resident((N_META, HEAD_DIM)),
            _resident((1, D_MODEL)),
            _weight_chunk_spec((D_MODEL, D_FF)),
            _weight_chunk_spec((D_MODEL, D_FF)),
            _weight_chunk_spec((D_FF, D_MODEL)),
            _resident((1, D_MODEL)),
            _weight_chunk_spec((D_MODEL, IN_PROJ)),
        ],
        out_specs=(
            pl.BlockSpec((tm, D_MODEL), row_blk),
            pl.BlockSpec((tm, RET_WIDTH), row_blk),
            pl.BlockSpec((tm, RET_WIDTH), row_blk),
            pl.BlockSpec((tm, RET_WIDTH), row_blk),
            pl.BlockSpec((tm, RET_WIDTH), row_blk),
            pl.BlockSpec((OCTETS, tm, LANES), lambda i: (0, _tile_index(i), 0)),
            pl.BlockSpec((N_META, RET_WIDTH), lambda i: (0, 0)),
            pl.BlockSpec((N_META, RET_WIDTH), lambda i: (0, 0)),
            pl.BlockSpec((OCTETS, N_META, LANES), lambda i: (0, 0, 0)),
        ),
        out_shape=out_shape,
        scratch_shapes=[
            pltpu.VMEM((D_MODEL, D_FF), BF16),
            pltpu.VMEM((D_MODEL, D_FF), BF16),
            pltpu.VMEM((D_FF, D_MODEL), BF16),
            pltpu.VMEM((D_MODEL, IN_PROJ), BF16),
            pltpu.VMEM((tm, D_MODEL), F32),
            pltpu.VMEM((N_META, D_MODEL), F32),
        ],
        compiler_params=pltpu.CompilerParams(
            dimension_semantics=("arbitrary",), vmem_limit_bytes=VMEM_LIMIT),
        name="ffn_inproj",
    )(x2, cos, sin, xm, cos_m, sin_m, n1w, wg, wu, wd, n2w, w_in)


def _retention_kernel(q_ref, k_ref, v_ref, g_ref, km_ref, vm_ref, mask_ref, wq_ref, wk_ref, wm_ref,
                      gc_ref, nw_ref, o_ref, state_ref, *, rows):
    tn = (((0,), (0,)), ((), ()))
    nt = (((1,), (1,)), ((), ()))

    @pl.when(pl.program_id(1) == 0)
    def _():
        for h in range(RET_HEADS):
            hs = slice(h * HEAD_DIM, (h + 1) * HEAD_DIM)
            kw = (km_ref[:, hs].astype(F32) * wm_ref[h]).astype(BF16)
            state_ref[h] = lax.dot_general(kw, vm_ref[:, hs], tn, preferred_element_type=F32)

    def chunk(i, carry):
        r0 = pl.multiple_of(i * RET_BLOCK, RET_BLOCK)
        rs = pl.ds(r0, RET_BLOCK)
        for h in range(RET_HEADS):
            hs = slice(h * HEAD_DIM, (h + 1) * HEAD_DIM)
            qh = q_ref[rs, hs]
            kh = k_ref[rs, hs]
            vh = v_ref[rs, hs]
            st = state_ref[h]
            s = lax.dot_general(qh, kh, nt, preferred_element_type=F32) * mask_ref[h]
            o = (jnp.dot(s.astype(BF16), vh, preferred_element_type=F32)
                 + jnp.dot(qh, st.astype(BF16), preferred_element_type=F32) * wq_ref[h])
            kw = (kh.astype(F32) * wk_ref[h]).astype(BF16)
            state_ref[h] = gc_ref[h] * st + lax.dot_general(kw, vh, tn, preferred_element_type=F32)
            mu = jnp.mean(o, axis=-1, keepdims=True)
            d = o - mu
            var = jnp.mean(d * d, axis=-1, keepdims=True)
            y = d * lax.rsqrt(var + EPS) * nw_ref[:, hs]
            gate = g_ref[rs, hs]
            o_ref[rs, hs] = (gate * jax.nn.sigmoid(gate) * y).astype(BF16)
        return carry

    lax.fori_loop(0, rows // RET_BLOCK, chunk, 0, unroll=2)


def _retention_tables():
    log_g = np.log(1.0 - 2.0 ** (-5.0 - np.arange(RET_HEADS, dtype=np.float64)))
    i = np.arange(RET_BLOCK)
    diff = i[:, None] - i[None, :]
    mask = np.where(diff[None] >= 0, np.exp(log_g[:, None, None] * np.maximum(diff, 0)[None]), 0.0)
    pos = np.arange(RET_BLOCK, dtype=np.float64)
    full = lambda w: np.broadcast_to(w[:, :, None], w.shape + (HEAD_DIM,))
    w_q = full(np.exp(log_g[:, None] * (pos + 1.0)[None]))
    w_k = full(np.exp(log_g[:, None] * (RET_BLOCK - 1 - pos)[None]))
    w_m = full(np.exp(log_g[:, None] * (N_META - 1 - np.arange(N_META, dtype=np.float64))[None]))
    g_c = np.broadcast_to(np.exp(log_g * RET_BLOCK)[:, None, None], (RET_HEADS, HEAD_DIM, HEAD_DIM))
    return tuple(jnp.asarray(t, dtype=F32) for t in (mask, w_q, w_k, w_m, g_c))


def _retention(q, k, v, g, km, vm, ret_norm_w, bsz, seq, rows):
    mask, w_q, w_k, w_m, g_c = _retention_tables()
    steps = seq // rows
    blk = lambda b, c: (b * steps + c, 0)
    full3 = lambda a: pl.BlockSpec(a.shape, lambda b, c: (0, 0, 0))
    full2 = lambda a: pl.BlockSpec(a.shape, lambda b, c: (0, 0))
    return pl.pallas_call(
        functools.partial(_retention_kernel, rows=rows),
        grid=(bsz, steps),
        in_specs=[pl.BlockSpec((rows, RET_WIDTH), blk)] * 4
        + [full2(km), full2(vm), full3(mask), full3(w_q), full3(w_k), full3(w_m), full3(g_c),
           full2(ret_norm_w)],
        out_specs=pl.BlockSpec((rows, RET_WIDTH), blk),
        out_shape=jax.ShapeDtypeStruct((bsz * seq, RET_WIDTH), BF16),
        scratch_shapes=[pltpu.VMEM((RET_HEADS, HEAD_DIM, HEAD_DIM), F32)],
        compiler_params=pltpu.CompilerParams(
            dimension_semantics=("arbitrary", "arbitrary"), vmem_limit_bytes=VMEM_LIMIT),
        name="retention",
    )(q, k, v, g, km, vm, mask, w_q, w_k, w_m, g_c, ret_norm_w)


def _s5_assemble(lre_ref, lim_ref, ldt_ref, btr_ref, bti_ref, ctr_ref, cti_ref,
                 toep_ref, win_ref, wout_ref, pow_ref, cpow_ref, xall_ref):
    lre = lre_ref[...]
    lim = lim_ref[...]
    dt = jnp.exp(ldt_ref[...])
    n_pow = pow_ref.shape[1]
    ell = lax.broadcasted_iota(jnp.int32, (n_pow, OCT_STATE), 0).astype(F32)
    mag = jnp.exp(ell * (lre * dt))
    ang = ell * (lim * dt)
    pow_ref[0] = mag * jnp.cos(ang)
    pow_ref[1] = mag * jnp.sin(ang)
    ell = ((lax.broadcasted_iota(jnp.int32, (8, OCT_STATE), 0) + 1) * SUB).astype(F32)
    mag = jnp.exp(ell * (lre * dt))
    ang = ell * (lim * dt)
    cpow_ref[0] = mag * jnp.cos(ang)
    cpow_ref[1] = mag * jnp.sin(ang)
    a_re = pow_ref[0, 1:2, :]
    a_im = pow_ref[1, 1:2, :]
    den = lre * lre + lim * lim
    num_re = a_re - 1.0
    coef_re = (num_re * lre + a_im * lim) / den
    coef_im = (a_im * lre - num_re * lim) / den
    btr = btr_ref[...]
    bti = bti_ref[...]
    bbar_re = coef_re * btr - coef_im * bti
    bbar_im = coef_re * bti + coef_im * btr
    ctr = ctr_ref[...]
    cti = cti_ref[...]

    row = lax.broadcasted_iota(jnp.int32, (LANES, OCT_STATE), 0)
    col = lax.broadcasted_iota(jnp.int32, (LANES, OCT_STATE), 1)
    same_group = (row // SSM_GROUP) == (col // SSM_STATE)

    def block_diag(x):
        return jnp.where(same_group, jnp.concatenate([x] * GROUPS_PER_OCTET, axis=0), 0.0)

    for l in range(SUB):
        pr = pow_ref[0, l:l + 1, :]
        pi = pow_ref[1, l:l + 1, :]
        x_re = pr * bbar_re - pi * bbar_im
        x_im = pr * bbar_im + pi * bbar_re
        rows = slice(l * SSM_GROUP, (l + 1) * SSM_GROUP)
        xall_ref[0, rows, :] = x_re
        xall_ref[1, rows, :] = x_im
        j = SUB - 1 - l
        win_ref[j * LANES:(j + 1) * LANES, 0:OCT_STATE] = block_diag(x_re).astype(BF16)
        win_ref[j * LANES:(j + 1) * LANES, OCT_STATE:2 * OCT_STATE] = block_diag(x_im).astype(BF16)
        pr1 = pow_ref[0, l + 1:l + 2, :]
        pi1 = pow_ref[1, l + 1:l + 2, :]
        w_re = pr1 * ctr - pi1 * cti
        w_im = pr1 * cti + pi1 * ctr
        wout_ref[0:OCT_STATE, l * LANES:(l + 1) * LANES] = block_diag(w_re).T.astype(BF16)
        wout_ref[OCT_STATE:2 * OCT_STATE, l * LANES:(l + 1) * LANES] = block_diag(-w_im).T.astype(BF16)

    k_all = (jnp.dot(xall_ref[0], block_diag(ctr).T, precision=lax.Precision.HIGHEST,
                     preferred_element_type=F32)
             - jnp.dot(xall_ref[1], block_diag(cti).T, precision=lax.Precision.HIGHEST,
                       preferred_element_type=F32))
    row = lax.broadcasted_iota(jnp.int32, (LANES, LANES), 0)
    col = lax.broadcasted_iota(jnp.int32, (LANES, LANES), 1)
    diag = (row // SSM_GROUP) == (col // SSM_GROUP)
    zeros = jnp.zeros((LANES, LANES), BF16)
    for lag in range(SUB):
        k_lag = k_all[lag * SSM_GROUP:(lag + 1) * SSM_GROUP, :]
        blk = jnp.where(diag, jnp.concatenate([k_lag] * GROUPS_PER_OCTET, axis=0), 0.0).astype(BF16)
        for j in range(SUB - lag):
            i = j + lag
            toep_ref[j * LANES:(j + 1) * LANES, i * LANES:(i + 1) * LANES] = blk
    for j in range(SUB):
        for i in range(j):
            if j * LANES // MXU_DIM == i * LANES // MXU_DIM:
                toep_ref[j * LANES:(j + 1) * LANES, i * LANES:(i + 1) * LANES] = zeros


def _s5_state_scan(st_ref, zb_ref, cpow_ref, nsc):
    tiles = nsc // 8
    re = slice(0, OCT_STATE)
    im = slice(OCT_STATE, 2 * OCT_STATE)
    rowmod = lax.broadcasted_iota(jnp.int32, (8, OCT_STATE), 0)
    src, dst = st_ref, zb_ref
    for s in (1, 2, 4):
        keep = rowmod >= s
        c_re = jnp.concatenate([jnp.where(keep, cpow_ref[0, s - 1:s, :], 0.0)] * tiles, axis=0)
        c_im = jnp.concatenate([jnp.where(keep, cpow_ref[1, s - 1:s, :], 0.0)] * tiles, axis=0)
        sh_re = src[pl.ds(8 - s, nsc), re]
        sh_im = src[pl.ds(8 - s, nsc), im]
        dst[8:8 + nsc, re] = src[8:8 + nsc, re] + c_re * sh_re - c_im * sh_im
        dst[8:8 + nsc, im] = src[8:8 + nsc, im] + c_re * sh_im + c_im * sh_re
        src, dst = dst, src
    ca_re = cpow_ref[0]
    ca_im = cpow_ref[1]
    c_re = st_ref[7:8, re]
    c_im = st_ref[7:8, im]
    for t in range(tiles):
        rows = slice(8 + 8 * t, 16 + 8 * t)
        cb_re = jnp.broadcast_to(c_re, (8, OCT_STATE))
        cb_im = jnp.broadcast_to(c_im, (8, OCT_STATE))
        p_re = src[rows, re] + ca_re * cb_re - ca_im * cb_im
        p_im = src[rows, im] + ca_re * cb_im + ca_im * cb_re
        st_ref[rows, re] = p_re
        st_ref[rows, im] = p_im
        c_re = p_re[7:8]
        c_im = p_im[7:8]


def _s5_kernel(u_ref, um_ref, lre_ref, lim_ref, ldt_ref, btr_ref, bti_ref, ctr_ref, cti_ref, d_ref,
               y_ref, toep_ref, win_ref, wout_ref, pow_ref, cpow_ref, xall_ref, lhs_ref, lhsm_ref,
               st_ref, zb_ref, *, nsc):
    b = pl.program_id(1)
    r = pl.program_id(2)

    @pl.when((b == 0) & (r == 0))
    def _():
        _s5_assemble(lre_ref, lim_ref, ldt_ref, btr_ref, bti_ref, ctr_ref, cti_ref,
                     toep_ref, win_ref, wout_ref, pow_ref, cpow_ref, xall_ref)
        zb_ref[0:8, :] = jnp.zeros((8, 2 * OCT_STATE), F32)

    @pl.when(r == 0)
    def _meta_state():
        lhsm_ref[...] = jnp.zeros(lhsm_ref.shape, F32)
        for j in range(SUB):
            lhsm_ref[7:8, j * LANES:(j + 1) * LANES] = um_ref[j:j + 1, :]
        st_ref[0:8, :] = jnp.dot(lhsm_ref[...].astype(BF16), win_ref[...],
                                 preferred_element_type=F32)

    @pl.when(r > 0)
    def _carry_state():
        st_ref[0:8, :] = st_ref[nsc:nsc + 8, :]

    for j in range(SUB):
        lhs_ref[:, j * LANES:(j + 1) * LANES] = u_ref[pl.ds(j, nsc, stride=SUB), :].astype(BF16)
    st_ref[8:8 + nsc, :] = jnp.dot(lhs_ref[...], win_ref[...], preferred_element_type=F32)

    _s5_state_scan(st_ref, zb_ref, cpow_ref, nsc)

    s_in = st_ref[pl.ds(7, nsc), :].astype(BF16)
    d = d_ref[...]
    per_tile = MXU_DIM // LANES
    for i2 in range(SUB // per_tile):
        cols = slice(i2 * MXU_DIM, (i2 + 1) * MXU_DIM)
        kk = (i2 + 1) * MXU_DIM
        y = (jnp.dot(lhs_ref[:, 0:kk], toep_ref[0:kk, cols], preferred_element_type=F32)
             + jnp.dot(s_in, wout_ref[:, cols], preferred_element_type=F32))
        for ii in range(per_tile):
            i = i2 * per_tile + ii
            yi = y[:, ii * LANES:(ii + 1) * LANES] + d * u_ref[pl.ds(i, nsc, stride=SUB), :]
            y_ref[pl.ds(i, nsc, stride=SUB), :] = jax.nn.gelu(yi)


def _s5_param_rows(lam_re, lam_im, log_dt, b_re, b_im, c_re, c_im, d):
    go = GROUPS_PER_OCTET
    lanes = lambda a: a.reshape(OCTETS, 1, OCT_STATE)
    ldt = jnp.broadcast_to(log_dt[:, None], (SSM_GROUPS, SSM_STATE))
    bt = lambda a: jnp.transpose(a.reshape(OCTETS, go, SSM_STATE, SSM_GROUP),
                                 (0, 3, 1, 2)).reshape(OCTETS, SSM_GROUP, OCT_STATE)
    ct = lambda a: jnp.transpose(a.reshape(OCTETS, go, SSM_GROUP, SSM_STATE),
                                 (0, 2, 1, 3)).reshape(OCTETS, SSM_GROUP, OCT_STATE)
    return (lanes(lam_re), lanes(lam_im), lanes(ldt), bt(b_re), bt(b_im), ct(c_re), ct(c_im),
            d.reshape(OCTETS, 1, LANES))


def _s5(u, um, params, bsz, seq, rows):
    nsc = rows // SUB
    steps = seq // rows
    kdim = SUB * LANES
    n_pow = SUB + 8
    oct_blk = lambda shape: pl.BlockSpec((None,) + shape, lambda o, b, r: (o,) + (0,) * len(shape))
    tok_blk = pl.BlockSpec((None, rows, LANES), lambda o, b, r: (o, b * steps + r, 0))
    vec = oct_blk((1, OCT_STATE))
    mat = oct_blk((SSM_GROUP, OCT_STATE))
    return pl.pallas_call(
        functools.partial(_s5_kernel, nsc=nsc),
        grid=(OCTETS, bsz, steps),
        in_specs=[tok_blk, oct_blk((N_META, LANES)), vec, vec, vec, mat, mat, mat, mat,
                  oct_blk((1, LANES))],
        out_specs=tok_blk,
        out_shape=jax.ShapeDtypeStruct(u.shape, F32),
        scratch_shapes=[
            pltpu.VMEM((kdim, kdim), BF16),
            pltpu.VMEM((kdim, 2 * OCT_STATE), BF16),
            pltpu.VMEM((2 * OCT_STATE, kdim), BF16),
            pltpu.VMEM((2, n_pow, OCT_STATE), F32),
            pltpu.VMEM((2, 8, OCT_STATE), F32),
            pltpu.VMEM((2, SUB * SSM_GROUP, OCT_STATE), F32),
            pltpu.VMEM((nsc, kdim), BF16),
            pltpu.VMEM((8, kdim), F32),
            pltpu.VMEM((nsc + 8, 2 * OCT_STATE), F32),
            pltpu.VMEM((nsc + 8, 2 * OCT_STATE), F32),
        ],
        compiler_params=pltpu.CompilerParams(
            dimension_semantics=("arbitrary", "arbitrary", "arbitrary"),
            vmem_limit_bytes=VMEM_LIMIT),
        name="s5_mixer",
    )(u, um, *params)


def _mix_ffn_kernel(h1_ref, ret_ref, y_ref, gw32_ref, gb_ref, snw_ref, wo32_ref, n3w_ref,
                    wg32_ref, wu32_ref, wd32_ref, fnw_ref, out_ref,
                    gw_ref, wo_ref, wg_ref, wu_ref, wd_ref, acc_ref):
    step = pl.program_id(0)

    @pl.when(step < WEIGHT_CHUNKS)
    def _():
        _cast_weight_chunks(step, ((gw32_ref, gw_ref), (wo32_ref, wo_ref), (wg32_ref, wg_ref),
                                   (wu32_ref, wu_ref), (wd32_ref, wd_ref)))

    @pl.when(step >= WEIGHT_CHUNKS)
    def _():
        y = jnp.concatenate([y_ref[o] for o in range(OCTETS)], axis=-1)
        z = y * jax.nn.sigmoid(jnp.dot(y.astype(BF16), gw_ref[...], preferred_element_type=F32)
                               + gb_ref[...])
        ssm = _rms(z, snw_ref[...]).astype(BF16)
        mixed = (jnp.dot(ret_ref[...], wo_ref[0:RET_WIDTH, :], preferred_element_type=F32)
                 + jnp.dot(ssm, wo_ref[RET_WIDTH:, :], preferred_element_type=F32))
        h2 = h1_ref[...] + mixed
        h3 = _swiglu_half_step(h2, n3w_ref, wg_ref, wu_ref, wd_ref, acc_ref)
        out_ref[...] = _rms(h3, fnw_ref[...])


def _mix_ffn(h1, ret, y, glu_w, glu_b, ssm_norm_w, w_out, n3w, wg, wu, wd, fnw, tm):
    rows = h1.shape[0]
    row_blk = lambda i: (_tile_index(i), 0)
    return pl.pallas_call(
        _mix_ffn_kernel,
        grid=(WEIGHT_CHUNKS + rows // tm,),
        in_specs=[
            pl.BlockSpec((tm, D_MODEL), row_blk),
            pl.BlockSpec((tm, RET_WIDTH), row_blk),
            pl.BlockSpec((OCTETS, tm, LANES), lambda i: (0, _tile_index(i), 0)),
            _weight_chunk_spec((SSM_WIDTH, SSM_WIDTH)),
            _resident((1, SSM_WIDTH)),
            _resident((1, SSM_WIDTH)),
            _weight_chunk_spec((D_MODEL, D_MODEL)),
            _resident((1, D_MODEL)),
            _weight_chunk_spec((D_MODEL, D_FF)),
            _weight_chunk_spec((D_MODEL, D_FF)),
            _weight_chunk_spec((D_FF, D_MODEL)),
            _resident((1, D_MODEL)),
        ],
        out_specs=pl.BlockSpec((tm, D_MODEL), row_blk),
        out_shape=jax.ShapeDtypeStruct((rows, D_MODEL), F32),
        scratch_shapes=[
            pltpu.VMEM((SSM_WIDTH, SSM_WIDTH), BF16),
            pltpu.VMEM((D_MODEL, D_MODEL), BF16),
            pltpu.VMEM((D_MODEL, D_FF), BF16),
            pltpu.VMEM((D_MODEL, D_FF), BF16),
            pltpu.VMEM((D_FF, D_MODEL), BF16),
            pltpu.VMEM((tm, D_MODEL), F32),
        ],
        compiler_params=pltpu.CompilerParams(
            dimension_semantics=("arbitrary",), vmem_limit_bytes=VMEM_LIMIT),
        name="mix_ffn",
    )(h1, ret, y, glu_w, glu_b, ssm_norm_w, w_out, n3w, wg, wu, wd, fnw)


def _rope_tables(n_pos):
    freqs = 1.0 / (ROPE_BASE ** (np.arange(0, HEAD_DIM, 2, dtype=np.float64) / HEAD_DIM))
    ang = np.arange(n_pos, dtype=np.float64)[:, None] * freqs[None, :]
    cos = np.cos(ang)
    sin = np.sin(ang)
    cos = np.concatenate([cos, cos], axis=-1).astype(np.float32)
    sin = np.concatenate([-sin, sin], axis=-1).astype(np.float32)
    split = lambda t: (jnp.asarray(t[:N_META]), jnp.asarray(t[N_META:]))
    return split(cos), split(sin)


def kernel(x, meta_tokens, ffn1_norm_w, ffn1_w_gate, ffn1_w_up, ffn1_w_down, mix_norm_w, w_in,
           ret_norm_w, ssm_lambda_re, ssm_lambda_im, ssm_log_dt, ssm_b_re, ssm_b_im, ssm_c_re,
           ssm_c_im, ssm_d, ssm_glu_w, ssm_glu_b, ssm_norm_w, w_out, ffn2_norm_w, ffn2_w_gate,
           ffn2_w_up, ffn2_w_down, final_norm_w):
    bsz, seq, _ = x.shape
    assert ffn1_norm_w.shape[0] == 1, "single layer only"
    tm = 512
    ret_rows = 1024
    s5_rows = 4096
    assert seq % tm == 0 and seq % ret_rows == 0 and seq % s5_rows == 0

    row = lambda a: a.reshape(1, -1)
    l = 0
    (cos_m, cos), (sin_m, sin) = _rope_tables(N_META + seq)

    h1, q, k, v, g, u, km, vm, um = _ffn_inproj(
        x.reshape(bsz * seq, D_MODEL), meta_tokens, cos, sin, cos_m, sin_m,
        row(ffn1_norm_w[l]), ffn1_w_gate[l], ffn1_w_up[l], ffn1_w_down[l],
        row(mix_norm_w[l]), w_in[l], tm=tm)

    ret = _retention(q, k, v, g, km, vm, row(ret_norm_w[l]), bsz, seq, ret_rows)

    params = _s5_param_rows(ssm_lambda_re[l], ssm_lambda_im[l], ssm_log_dt[l], ssm_b_re[l],
                            ssm_b_im[l], ssm_c_re[l], ssm_c_im[l], ssm_d[l])
    y = _s5(u, um, params, bsz, seq, s5_rows)

    out = _mix_ffn(h1, ret, y, ssm_glu_w[l], row(ssm_glu_b[l]), row(ssm_norm_w[l]),
                   w_out[l], row(ffn2_norm_w[l]), ffn2_w_gate[l], ffn2_w_up[l],
                   ffn2_w_down[l], row(final_norm_w), tm=tm)
    return out.reshape(bsz, seq, D_MODEL)
```

```python
import functools
import math

import jax
import jax.numpy as jnp
import numpy as np
from jax import lax
from jax.experimental import pallas as pl
from jax.experimental.pallas import tpu as pltpu

D_MODEL = 1024
N_META = 16
RET_HEADS = 4
HEAD_DIM = 128
RET_WIDTH = RET_HEADS * HEAD_DIM
SSM_WIDTH = 512
SSM_GROUP = 16
SSM_GROUPS = SSM_WIDTH // SSM_GROUP
SSM_STATE = 64
RET_BLOCK = 256
D_FF = 2816
FFN_RES = 0.5
ROPE_BASE = 10000.0
EPS = 1e-6
IN_PROJ = 4 * RET_WIDTH + SSM_WIDTH

LANES = 128
MXU_DIM = 256
OCTETS = SSM_WIDTH // LANES
GROUPS_PER_OCTET = LANES // SSM_GROUP
SUB = N_META
OCT_STATE = GROUPS_PER_OCTET * SSM_STATE
GROUP_LANES = SUB * SSM_GROUP
PAIRS = SSM_GROUPS // 2
PAIR_STATE = 2 * SSM_STATE
FF_CHUNK = 256
WEIGHT_CHUNKS = 8
VMEM_LIMIT = 56 * 1024 * 1024

F32 = jnp.float32
BF16 = jnp.bfloat16


def _rms(x, w):
    return x * lax.rsqrt(jnp.mean(x * x, axis=-1, keepdims=True) + EPS) * w


def _swiglu_half_step(h, nw_ref, wg_ref, wu_ref, wd_ref, acc_ref):
    n = _rms(h, nw_ref[...]).astype(BF16)
    for c in range(D_FF // FF_CHUNK):
        sl = slice(c * FF_CHUNK, (c + 1) * FF_CHUNK)
        g = jnp.dot(n, wg_ref[:, sl], preferred_element_type=F32)
        u = jnp.dot(n, wu_ref[:, sl], preferred_element_type=F32)
        a = (g * jax.nn.sigmoid(g) * u).astype(BF16)
        p = jnp.dot(a, wd_ref[sl, :], preferred_element_type=F32)
        if c == 0:
            acc_ref[...] = p
        else:
            acc_ref[...] += p
    return h + FFN_RES * acc_ref[...]


def _rope(x, cos, sin_signed):
    return x * cos + pltpu.roll(x, HEAD_DIM // 2, axis=1) * sin_signed


def _lane_block(shape, width):
    return lax.broadcasted_iota(jnp.int32, shape, 1) // width


def _to_group_dense(tok_ref, out_ref, nsub, row0=0):
    per_half = LANES // SSM_GROUP
    blk = _lane_block((nsub, LANES), SSM_GROUP)
    for o in range(OCTETS):
        slabs = [tok_ref[o, pl.ds(j, nsub, stride=SUB), :] for j in range(SUB)]
        for g8 in range(GROUPS_PER_OCTET):
            for hh in range(GROUP_LANES // LANES):
                acc = jnp.zeros((nsub, LANES), F32)
                for jj in range(per_half):
                    shift = ((jj - g8) * SSM_GROUP) % LANES
                    slab = slabs[hh * per_half + jj]
                    rolled = slab if shift == 0 else pltpu.roll(slab, shift, axis=1)
                    acc = jnp.where(blk == jj, rolled, acc)
                out_ref[o * GROUPS_PER_OCTET + g8, row0:row0 + nsub,
                        hh * LANES:(hh + 1) * LANES] = acc


def _to_token_major(gd_ref, tok_ref, nsub):
    per_half = LANES // SSM_GROUP
    blk = _lane_block((nsub, LANES), SSM_GROUP)
    for o in range(OCTETS):
        for j in range(SUB):
            hh, jj = divmod(j, per_half)
            acc = jnp.zeros((nsub, LANES), F32)
            for g8 in range(GROUPS_PER_OCTET):
                src = gd_ref[o * GROUPS_PER_OCTET + g8, :, hh * LANES:(hh + 1) * LANES]
                shift = ((g8 - jj) * SSM_GROUP) % LANES
                rolled = src if shift == 0 else pltpu.roll(src, shift, axis=1)
                acc = jnp.where(blk == g8, rolled, acc)
            tok_ref[o, pl.ds(j, nsub, stride=SUB), :] = acc


def _cast_weight_chunks(step, pairs):
    for src_ref, dst_ref in pairs:
        rows = src_ref.shape[0]
        r0 = pl.multiple_of(step * rows, rows)
        dst_ref[pl.ds(r0, rows), :] = src_ref[...].astype(BF16)


def _weight_chunk_spec(shape):
    rows = shape[0] // WEIGHT_CHUNKS
    assert rows * WEIGHT_CHUNKS == shape[0] and rows % 16 == 0, shape
    return pl.BlockSpec((rows, shape[1]), lambda i: (jnp.minimum(i, WEIGHT_CHUNKS - 1), 0))


def _tile_index(i):
    return jnp.maximum(i - WEIGHT_CHUNKS, 0)


def _ffn_inproj_kernel(x_ref, cos_ref, sin_ref, xm_ref, cosm_ref, sinm_ref, n1w_ref, wg32_ref,
                       wu32_ref, wd32_ref, n2w_ref, win32_ref,
                       h1_ref, q_ref, k_ref, v_ref, g_ref, u_ref, km_ref, vm_ref, um_ref,
                       wg_ref, wu_ref, wd_ref, win_ref, acc_ref, accm_ref, utok_ref, utokm_ref):
    step = pl.program_id(0)
    k_scale = HEAD_DIM ** -0.5

    def mix_norm(h1):
        return _rms(h1, n2w_ref[...]).astype(BF16)

    def in_proj(n, part):
        return jnp.dot(n, win_ref[:, part * RET_WIDTH:(part + 1) * RET_WIDTH],
                       preferred_element_type=F32)

    def rope_heads(p, cos, sin, scale, out_ref):
        for h in range(RET_HEADS):
            hs = slice(h * HEAD_DIM, (h + 1) * HEAD_DIM)
            y = _rope(p[:, hs], cos, sin)
            out_ref[:, hs] = (y if scale is None else y * scale).astype(BF16)

    def split_octets(p, tok_ref):
        for o in range(OCTETS):
            tok_ref[o] = p[:, o * LANES:(o + 1) * LANES]

    @pl.when(step < WEIGHT_CHUNKS)
    def _():
        _cast_weight_chunks(step, ((wg32_ref, wg_ref), (wu32_ref, wu_ref), (wd32_ref, wd_ref),
                                   (win32_ref, win_ref)))

    @pl.when(step == WEIGHT_CHUNKS)
    def _():
        h1 = _swiglu_half_step(xm_ref[...], n1w_ref, wg_ref, wu_ref, wd_ref, accm_ref)
        n = mix_norm(h1)
        rope_heads(in_proj(n, 1), cosm_ref[...], sinm_ref[...], k_scale, km_ref)
        vm_ref[...] = in_proj(n, 2).astype(BF16)
        split_octets(in_proj(n, 4), utokm_ref)
        um_ref[...] = jnp.zeros(um_ref.shape, F32)
        _to_group_dense(utokm_ref, um_ref, 1, row0=7)

    @pl.when(step >= WEIGHT_CHUNKS)
    def _():
        h1 = _swiglu_half_step(x_ref[...], n1w_ref, wg_ref, wu_ref, wd_ref, acc_ref)
        h1_ref[...] = h1
        n = mix_norm(h1)
        split_octets(in_proj(n, 4), utok_ref)
        _to_group_dense(utok_ref, u_ref, u_ref.shape[1])
        cos = cos_ref[...]
        sin = sin_ref[...]
        rope_heads(in_proj(n, 0), cos, sin, None, q_ref)
        rope_heads(in_proj(n, 1), cos, sin, k_scale, k_ref)
        v_ref[...] = in_proj(n, 2).astype(BF16)
        g_ref[...] = in_proj(n, 3)


def _resident(shape):
    nd = len(shape)
    return pl.BlockSpec(shape, lambda *_: (0,) * nd, pipeline_mode=pl.Buffered(1))


def _ffn_inproj(x2, xm, cos, sin, cos_m, sin_m, n1w, wg, wu, wd, n2w, w_in, tm):
    rows = x2.shape[0]
    pos_blocks = cos.shape[0] // tm
    row_blk = lambda i: (_tile_index(i), 0)
    pos_blk = lambda i: (_tile_index(i) % pos_blocks, 0)
    out_shape = (
        jax.ShapeDtypeStruct((rows, D_MODEL), F32),
        jax.ShapeDtypeStruct((rows, RET_WIDTH), BF16),
        jax.ShapeDtypeStruct((rows, RET_WIDTH), BF16),
        jax.ShapeDtypeStruct((rows, RET_WIDTH), BF16),
        jax.ShapeDtypeStruct((rows, RET_WIDTH), F32),
        jax.ShapeDtypeStruct((SSM_GROUPS, rows // SUB, GROUP_LANES), F32),
        jax.ShapeDtypeStruct((N_META, RET_WIDTH), BF16),
        jax.ShapeDtypeStruct((N_META, RET_WIDTH), BF16),
        jax.ShapeDtypeStruct((SSM_GROUPS, 8, GROUP_LANES), F32),
    )
    return pl.pallas_call(
        _ffn_inproj_kernel,
        grid=(WEIGHT_CHUNKS + rows // tm,),
        in_specs=[
            pl.BlockSpec((tm, D_MODEL), row_blk),
            pl.BlockSpec((tm, HEAD_DIM), pos_blk),
            pl.BlockSpec((tm, HEAD_DIM), pos_blk),
            _resident((N_META, D_MODEL)),
            _resident((N_META, HEAD_DIM)),
            _resident((N_META, HEAD_DIM)),
            _resident((1, D_MODEL)),
            _weight_chunk_spec((D_MODEL, D_FF)),
            _weight_chunk_spec((D_MODEL, D_FF)),
            _weight_chunk_spec((D_FF, D_MODEL)),
            _resident((1, D_MODEL)),
            _weight_chunk_spec((D_MODEL, IN_PROJ)),
        ],
        out_specs=(
            pl.BlockSpec((tm, D_MODEL), row_blk),
            pl.BlockSpec((tm, RET_WIDTH), row_blk),
            pl.BlockSpec((tm, RET_WIDTH), row_blk),
            pl.BlockSpec((tm, RET_WIDTH), row_blk),
            pl.BlockSpec((tm, RET_WIDTH), row_blk),
            pl.BlockSpec((SSM_GROUPS, tm // SUB, GROUP_LANES), lambda i: (0, _tile_index(i), 0)),
            pl.BlockSpec((N_META, RET_WIDTH), lambda i: (0, 0)),
            pl.BlockSpec((N_META, RET_WIDTH), lambda i: (0, 0)),
            pl.BlockSpec((SSM_GROUPS, 8, GROUP_LANES), lambda i: (0, 0, 0)),
        ),
        out_shape=out_shape,
        scratch_shapes=[
            pltpu.VMEM((D_MODEL, D_FF), BF16),
            pltpu.VMEM((D_MODEL, D_FF), BF16),
            pltpu.VMEM((D_FF, D_MODEL), BF16),
            pltpu.VMEM((D_MODEL, IN_PROJ), BF16),
            pltpu.VMEM((tm, D_MODEL), F32),
            pltpu.VMEM((N_META, D_MODEL), F32),
            pltpu.VMEM((OCTETS, tm, LANES), F32),
            pltpu.VMEM((OCTETS, N_META, LANES), F32),
        ],
        compiler_params=pltpu.CompilerParams(
            dimension_semantics=("arbitrary",), vmem_limit_bytes=VMEM_LIMIT),
        name="ffn_inproj",
    )(x2, cos, sin, xm, cos_m, sin_m, n1w, wg, wu, wd, n2w, w_in)


def _retention_kernel(q_ref, k_ref, v_ref, g_ref, km_ref, vm_ref, mask_ref, wq_ref, wk_ref, wm_ref,
                      gc_ref, nw_ref, o_ref, state_ref, *, rows):
    tn = (((0,), (0,)), ((), ()))
    nt = (((1,), (1,)), ((), ()))

    @pl.when(pl.program_id(1) == 0)
    def _():
        for h in range(RET_HEADS):
            hs = slice(h * HEAD_DIM, (h + 1) * HEAD_DIM)
            kw = (km_ref[:, hs].astype(F32) * wm_ref[h]).astype(BF16)
            state_ref[h] = lax.dot_general(kw, vm_ref[:, hs], tn, preferred_element_type=F32)

    def chunk(i, carry):
        r0 = pl.multiple_of(i * RET_BLOCK, RET_BLOCK)
        rs = pl.ds(r0, RET_BLOCK)
        for h in range(RET_HEADS):
            hs = slice(h * HEAD_DIM, (h + 1) * HEAD_DIM)
            qh = q_ref[rs, hs]
            kh = k_ref[rs, hs]
            vh = v_ref[rs, hs]
            st = state_ref[h]
            s = lax.dot_general(qh, kh, nt, preferred_element_type=F32) * mask_ref[h]
            o = (jnp.dot(s.astype(BF16), vh, preferred_element_type=F32)
                 + jnp.dot(qh, st.astype(BF16), preferred_element_type=F32) * wq_ref[h])
            kw = (kh.astype(F32) * wk_ref[h]).astype(BF16)
            state_ref[h] = gc_ref[h] * st + lax.dot_general(kw, vh, tn, preferred_element_type=F32)
            mu = jnp.mean(o, axis=-1, keepdims=True)
            d = o - mu
            var = jnp.mean(d * d, axis=-1, keepdims=True)
            y = d * lax.rsqrt(var + EPS) * nw_ref[:, hs]
            gate = g_ref[rs, hs]
            o_ref[rs, hs] = (gate * jax.nn.sigmoid(gate) * y).astype(BF16)
        return carry

    lax.fori_loop(0, rows // RET_BLOCK, chunk, 0, unroll=2)


def _retention_tables():
    log_g = np.log(1.0 - 2.0 ** (-5.0 - np.arange(RET_HEADS, dtype=np.float64)))
    i = np.arange(RET_BLOCK)
    diff = i[:, None] - i[None, :]
    mask = np.where(diff[None] >= 0, np.exp(log_g[:, None, None] * np.maximum(diff, 0)[None]), 0.0)
    pos = np.arange(RET_BLOCK, dtype=np.float64)
    full = lambda w: np.broadcast_to(w[:, :, None], w.shape + (HEAD_DIM,))
    w_q = full(np.exp(log_g[:, None] * (pos + 1.0)[None]))
    w_k = full(np.exp(log_g[:, None] * (RET_BLOCK - 1 - pos)[None]))
    w_m = full(np.exp(log_g[:, None] * (N_META - 1 - np.arange(N_META, dtype=np.float64))[None]))
    g_c = np.broadcast_to(np.exp(log_g * RET_BLOCK)[:, None, None], (RET_HEADS, HEAD_DIM, HEAD_DIM))
    return tuple(jnp.asarray(t, dtype=F32) for t in (mask, w_q, w_k, w_m, g_c))


def _retention(q, k, v, g, km, vm, ret_norm_w, bsz, seq, rows):
    mask, w_q, w_k, w_m, g_c = _retention_tables()
    steps = seq // rows
    blk = lambda b, c: (b * steps + c, 0)
    full3 = lambda a: pl.BlockSpec(a.shape, lambda b, c: (0, 0, 0))
    full2 = lambda a: pl.BlockSpec(a.shape, lambda b, c: (0, 0))
    return pl.pallas_call(
        functools.partial(_retention_kernel, rows=rows),
        grid=(bsz, steps),
        in_specs=[pl.BlockSpec((rows, RET_WIDTH), blk)] * 4
        + [full2(km), full2(vm), full3(mask), full3(w_q), full3(w_k), full3(w_m), full3(g_c),
           full2(ret_norm_w)],
        out_specs=pl.BlockSpec((rows, RET_WIDTH), blk),
        out_shape=jax.ShapeDtypeStruct((bsz * seq, RET_WIDTH), BF16),
        scratch_shapes=[pltpu.VMEM((RET_HEADS, HEAD_DIM, HEAD_DIM), F32)],
        compiler_params=pltpu.CompilerParams(
            dimension_semantics=("arbitrary", "arbitrary"), vmem_limit_bytes=VMEM_LIMIT),
        name="retention",
    )(q, k, v, g, km, vm, mask, w_q, w_k, w_m, g_c, ret_norm_w)


def _shift_lanes_zero_fill(halves, shift, lane):
    h0, h1 = halves
    whole, s = divmod(shift, LANES)
    r0 = h0 if s == 0 else pltpu.roll(h0, s, axis=1)
    zero = jnp.zeros_like(h0)
    if whole == 1:
        return zero, jnp.where(lane >= s, r0, zero)
    r1 = h1 if s == 0 else pltpu.roll(h1, s, axis=1)
    return jnp.where(lane >= s, r0, zero), jnp.where(lane >= s, r1, r0)


def _s5_assemble_octet(o, lre_ref, lim_ref, ldt_ref, btr_ref, bti_ref, ctr_ref, cti_ref,
                       toep_ref, win_ref, wout_ref, cpw_ref, pow_ref, xall_ref, wt_ref):
    lre = lre_ref[o]
    lim = lim_ref[o]
    dt = jnp.exp(ldt_ref[o])
    n_pow = pow_ref.shape[1]
    ell = lax.broadcasted_iota(jnp.int32, (n_pow, OCT_STATE), 0).astype(F32)
    mag = jnp.exp(ell * (lre * dt))
    ang = ell * (lim * dt)
    pow_ref[0] = mag * jnp.cos(ang)
    pow_ref[1] = mag * jnp.sin(ang)
    ell = ((lax.broadcasted_iota(jnp.int32, (8, OCT_STATE), 0) + 1) * SUB).astype(F32)
    mag = jnp.exp(ell * (lre * dt))
    ang = ell * (lim * dt)
    cp_re = mag * jnp.cos(ang)
    cp_im = mag * jnp.sin(ang)
    a_re = pow_ref[0, 1:2, :]
    a_im = pow_ref[1, 1:2, :]
    den = lre * lre + lim * lim
    num_re = a_re - 1.0
    coef_re = (num_re * lre + a_im * lim) / den
    coef_im = (a_im * lre - num_re * lim) / den
    btr = btr_ref[o]
    bti = bti_ref[o]
    bbar_re = coef_re * btr - coef_im * bti
    bbar_im = coef_re * bti + coef_im * btr
    ctr = ctr_ref[o]
    cti = cti_ref[o]

    pairs_per_octet = GROUPS_PER_OCTET // 2
    pair_lane_group = _lane_block((SSM_GROUP, PAIR_STATE), SSM_STATE)
    for l in range(SUB):
        pr = pow_ref[0, l:l + 1, :]
        pi = pow_ref[1, l:l + 1, :]
        x_re = pr * bbar_re - pi * bbar_im
        x_im = pr * bbar_im + pi * bbar_re
        rows = slice(l * SSM_GROUP, (l + 1) * SSM_GROUP)
        xall_ref[0, rows, :] = x_re
        xall_ref[1, rows, :] = x_im
        pr1 = pow_ref[0, l + 1:l + 2, :]
        pi1 = pow_ref[1, l + 1:l + 2, :]
        w_re = pr1 * ctr - pi1 * cti
        w_im = pr1 * cti + pi1 * ctr
        j = SUB - 1 - l
        for pp in range(pairs_per_octet):
            sl = slice(pp * PAIR_STATE, (pp + 1) * PAIR_STATE)
            pair = o * pairs_per_octet + pp
            for gi in range(2):
                keep = pair_lane_group == gi
                in_rows = slice(gi * GROUP_LANES + j * SSM_GROUP, gi * GROUP_LANES + (j + 1) * SSM_GROUP)
                out_rows = slice(gi * GROUP_LANES + l * SSM_GROUP, gi * GROUP_LANES + (l + 1) * SSM_GROUP)
                for part, x, w in ((0, x_re, w_re), (1, x_im, -w_im)):
                    cols = slice(part * PAIR_STATE, (part + 1) * PAIR_STATE)
                    win_ref[pair, in_rows, cols] = jnp.where(keep, x[:, sl], 0.0).astype(BF16)
                    wt_ref[pp, out_rows, cols] = jnp.where(keep, w[:, sl], 0.0)
    for pp in range(pairs_per_octet):
        pair = o * pairs_per_octet + pp
        wout_ref[pair] = wt_ref[pp].T.astype(BF16)
        sl = slice(pp * PAIR_STATE, (pp + 1) * PAIR_STATE)
        cpw_ref[pair, 0] = cp_re[:, sl]
        cpw_ref[pair, 1] = cp_im[:, sl]

    row = lax.broadcasted_iota(jnp.int32, (LANES, OCT_STATE), 0)
    col = lax.broadcasted_iota(jnp.int32, (LANES, OCT_STATE), 1)
    same_group = (row // SSM_GROUP) == (col // SSM_STATE)

    def block_diag_t(x):
        return jnp.where(same_group, jnp.concatenate([x] * GROUPS_PER_OCTET, axis=0), 0.0).T

    k_all = (jnp.dot(xall_ref[0], block_diag_t(ctr), precision=lax.Precision.HIGHEST,
                     preferred_element_type=F32)
             - jnp.dot(xall_ref[1], block_diag_t(cti), precision=lax.Precision.HIGHEST,
                       preferred_element_type=F32))
    per_half = LANES // SSM_GROUP
    blk = _lane_block((SSM_GROUP, LANES), SSM_GROUP)
    lane = lax.broadcasted_iota(jnp.int32, (SSM_GROUP, LANES), 1)
    for g8 in range(GROUPS_PER_OCTET):
        strip = []
        for hh in range(GROUP_LANES // LANES):
            acc = jnp.zeros((SSM_GROUP, LANES), F32)
            for ll in range(per_half):
                lag = hh * per_half + ll
                k_lag = k_all[lag * SSM_GROUP:(lag + 1) * SSM_GROUP, :]
                shift = ((ll - g8) * SSM_GROUP) % LANES
                rolled = k_lag if shift == 0 else pltpu.roll(k_lag, shift, axis=1)
                acc = jnp.where(blk == ll, rolled, acc)
            strip.append(acc)
        for j in range(SUB):
            h0, h1 = _shift_lanes_zero_fill(strip, j * SSM_GROUP, lane)
            rows = slice(j * SSM_GROUP, (j + 1) * SSM_GROUP)
            toep_ref[o * GROUPS_PER_OCTET + g8, rows, 0:LANES] = h0.astype(BF16)
            toep_ref[o * GROUPS_PER_OCTET + g8, rows, LANES:2 * LANES] = h1.astype(BF16)


def _s5_state_scan(st_ref, zb_ref, cpow_ref, nsc):
    tiles = nsc // 8
    width = st_ref.shape[1] // 2
    re = slice(0, width)
    im = slice(width, 2 * width)
    rowmod = lax.broadcasted_iota(jnp.int32, (8, width), 0)
    src, dst = st_ref, zb_ref
    for s in (1, 2, 4):
        keep = rowmod >= s
        c_re = jnp.concatenate([jnp.where(keep, cpow_ref[0, s - 1:s, :], 0.0)] * tiles, axis=0)
        c_im = jnp.concatenate([jnp.where(keep, cpow_ref[1, s - 1:s, :], 0.0)] * tiles, axis=0)
        sh_re = src[pl.ds(8 - s, nsc), re]
        sh_im = src[pl.ds(8 - s, nsc), im]
        dst[8:8 + nsc, re] = src[8:8 + nsc, re] + c_re * sh_re - c_im * sh_im
        dst[8:8 + nsc, im] = src[8:8 + nsc, im] + c_re * sh_im + c_im * sh_re
        src, dst = dst, src
    ca_re = cpow_ref[0]
    ca_im = cpow_ref[1]
    c_re = st_ref[7:8, re]
    c_im = st_ref[7:8, im]
    for t in range(tiles):
        rows = slice(8 + 8 * t, 16 + 8 * t)
        cb_re = jnp.broadcast_to(c_re, (8, width))
        cb_im = jnp.broadcast_to(c_im, (8, width))
        p_re = src[rows, re] + ca_re * cb_re - ca_im * cb_im
        p_im = src[rows, im] + ca_re * cb_im + ca_im * cb_re
        st_ref[rows, re] = p_re
        st_ref[rows, im] = p_im
        c_re = p_re[7:8]
        c_im = p_im[7:8]


def _s5_kernel(u_ref, um_ref, lre_ref, lim_ref, ldt_ref, btr_ref, bti_ref, ctr_ref, cti_ref, d_ref,
               y_ref, toep_ref, win_ref, wout_ref, cpw_ref, pow_ref, xall_ref, wt_ref, lhs_ref,
               st_ref, zb_ref, carry_ref, *, nsc):
    first_call_step = (pl.program_id(0) == 0) & (pl.program_id(1) == 0)
    r = pl.program_id(1)

    @pl.when(first_call_step)
    def _():
        zb_ref[0:8, :] = jnp.zeros((8, zb_ref.shape[1]), F32)

        def octet(o, carry):
            _s5_assemble_octet(o, lre_ref, lim_ref, ldt_ref, btr_ref, bti_ref, ctr_ref, cti_ref,
                               toep_ref, win_ref, wout_ref, cpw_ref, pow_ref, xall_ref, wt_ref)
            return carry
        lax.fori_loop(0, OCTETS, octet, 0)

    def group_pair(pair, carry):
        g0 = 2 * pair
        lo = slice(0, GROUP_LANES)
        hi = slice(GROUP_LANES, 2 * GROUP_LANES)
        lhs_ref[:, lo] = u_ref[g0].astype(BF16)
        lhs_ref[:, hi] = u_ref[g0 + 1].astype(BF16)

        @pl.when(r == 0)
        def _():
            meta = jnp.concatenate([um_ref[g0], um_ref[g0 + 1]], axis=1).astype(BF16)
            st_ref[0:8, :] = jnp.dot(meta, win_ref[pair], preferred_element_type=F32)

        @pl.when(r > 0)
        def _():
            st_ref[0:8, :] = carry_ref[pair]

        st_ref[8:8 + nsc, :] = jnp.dot(lhs_ref[...], win_ref[pair], preferred_element_type=F32)
        _s5_state_scan(st_ref, zb_ref, cpw_ref.at[pair], nsc)
        carry_ref[pair] = st_ref[nsc:nsc + 8, :]

        s_in = st_ref[pl.ds(7, nsc), :].astype(BF16)
        y_state = jnp.dot(s_in, wout_ref[pair], preferred_element_type=F32)
        d = d_ref[pair]
        for gi, cols in ((0, lo), (1, hi)):
            y = (jnp.dot(lhs_ref[:, cols], toep_ref[g0 + gi], preferred_element_type=F32)
                 + y_state[:, cols] + d[:, cols] * u_ref[g0 + gi])
            y_ref[g0 + gi] = jax.nn.gelu(y)
        return carry

    lax.fori_loop(0, PAIRS, group_pair, 0)


def _s5_param_rows(lam_re, lam_im, log_dt, b_re, b_im, c_re, c_im, d):
    go = GROUPS_PER_OCTET
    lanes = lambda a: a.reshape(OCTETS, 1, OCT_STATE)
    ldt = jnp.broadcast_to(log_dt[:, None], (SSM_GROUPS, SSM_STATE))
    bt = lambda a: jnp.transpose(a.reshape(OCTETS, go, SSM_STATE, SSM_GROUP),
                                 (0, 3, 1, 2)).reshape(OCTETS, SSM_GROUP, OCT_STATE)
    ct = lambda a: jnp.transpose(a.reshape(OCTETS, go, SSM_GROUP, SSM_STATE),
                                 (0, 2, 1, 3)).reshape(OCTETS, SSM_GROUP, OCT_STATE)
    d_pairs = jnp.broadcast_to(d.reshape(PAIRS, 2, 1, SSM_GROUP),
                               (PAIRS, 2, SUB, SSM_GROUP)).reshape(PAIRS, 1, 2 * GROUP_LANES)
    return (lanes(lam_re), lanes(lam_im), lanes(ldt), bt(b_re), bt(b_im), ct(c_re), ct(c_im), d_pairs)


def _s5(u, um, params, bsz, seq, rows):
    nsc = rows // SUB
    steps = seq // rows
    n_pow = SUB + 8
    full = lambda a: pl.BlockSpec(a.shape, lambda b, r: (0,) * a.ndim)
    tok_blk = pl.BlockSpec((SSM_GROUPS, nsc, GROUP_LANES), lambda b, r: (0, b * steps + r, 0))
    return pl.pallas_call(
        functools.partial(_s5_kernel, nsc=nsc),
        grid=(bsz, steps),
        in_specs=[tok_blk, full(um)] + [full(p) for p in params],
        out_specs=tok_blk,
        out_shape=jax.ShapeDtypeStruct(u.shape, F32),
        scratch_shapes=[
            pltpu.VMEM((SSM_GROUPS, GROUP_LANES, GROUP_LANES), BF16),
            pltpu.VMEM((PAIRS, 2 * GROUP_LANES, 2 * PAIR_STATE), BF16),
            pltpu.VMEM((PAIRS, 2 * PAIR_STATE, 2 * GROUP_LANES), BF16),
            pltpu.VMEM((PAIRS, 2, 8, PAIR_STATE), F32),
            pltpu.VMEM((2, n_pow, OCT_STATE), F32),
            pltpu.VMEM((2, SUB * SSM_GROUP, OCT_STATE), F32),
            pltpu.VMEM((GROUPS_PER_OCTET // 2, 2 * GROUP_LANES, 2 * PAIR_STATE), F32),
            pltpu.VMEM((nsc, 2 * GROUP_LANES), BF16),
            pltpu.VMEM((nsc + 8, 2 * PAIR_STATE), F32),
            pltpu.VMEM((nsc + 8, 2 * PAIR_STATE), F32),
            pltpu.VMEM((PAIRS, 8, 2 * PAIR_STATE), F32),
        ],
        compiler_params=pltpu.CompilerParams(
            dimension_semantics=("arbitrary", "arbitrary"), vmem_limit_bytes=VMEM_LIMIT),
        name="s5_mixer",
    )(u, um, *params)


def _mix_ffn_kernel(h1_ref, ret_ref, ynext_ref, gw32_ref, gb_ref, snw_ref, wo32_ref, n3w_ref,
                    wg32_ref, wu32_ref, wd32_ref, fnw_ref, out_ref,
                    gw_ref, wo_ref, wg_ref, wu_ref, wd_ref, acc_ref, ytok_ref):
    step = pl.program_id(0)
    nsub = ynext_ref.shape[1]

    @pl.when(step < WEIGHT_CHUNKS)
    def _():
        _cast_weight_chunks(step, ((gw32_ref, gw_ref), (wo32_ref, wo_ref), (wg32_ref, wg_ref),
                                   (wu32_ref, wu_ref), (wd32_ref, wd_ref)))

    @pl.when(step == WEIGHT_CHUNKS - 1)
    def _():
        _to_token_major(ynext_ref, ytok_ref.at[0], nsub)

    @pl.when(step >= WEIGHT_CHUNKS)
    def _():
        slot = lax.rem(step - WEIGHT_CHUNKS, 2)
        cur = ytok_ref.at[slot]
        y = jnp.concatenate([cur[o] for o in range(OCTETS)], axis=-1)
        _to_token_major(ynext_ref, ytok_ref.at[1 - slot], nsub)
        z = y * jax.nn.sigmoid(jnp.dot(y.astype(BF16), gw_ref[...], preferred_element_type=F32)
                               + gb_ref[...])
        ssm = _rms(z, snw_ref[...]).astype(BF16)
        mixed = (jnp.dot(ret_ref[...], wo_ref[0:RET_WIDTH, :], preferred_element_type=F32)
                 + jnp.dot(ssm, wo_ref[RET_WIDTH:, :], preferred_element_type=F32))
        h2 = h1_ref[...] + mixed
        h3 = _swiglu_half_step(h2, n3w_ref, wg_ref, wu_ref, wd_ref, acc_ref)
        out_ref[...] = _rms(h3, fnw_ref[...])


def _mix_ffn(h1, ret, y, glu_w, glu_b, ssm_norm_w, w_out, n3w, wg, wu, wd, fnw, tm):
    rows = h1.shape[0]
    tiles = rows // tm
    row_blk = lambda i: (_tile_index(i), 0)
    next_blk = lambda i: (0, jnp.minimum(_tile_index(i + 1), tiles - 1), 0)
    return pl.pallas_call(
        _mix_ffn_kernel,
        grid=(WEIGHT_CHUNKS + tiles,),
        in_specs=[
            pl.BlockSpec((tm, D_MODEL), row_blk),
            pl.BlockSpec((tm, RET_WIDTH), row_blk),
            pl.BlockSpec((SSM_GROUPS, tm // SUB, GROUP_LANES), next_blk),
            _weight_chunk_spec((SSM_WIDTH, SSM_WIDTH)),
            _resident((1, SSM_WIDTH)),
            _resident((1, SSM_WIDTH)),
            _weight_chunk_spec((D_MODEL, D_MODEL)),
            _resident((1, D_MODEL)),
            _weight_chunk_spec((D_MODEL, D_FF)),
            _weight_chunk_spec((D_MODEL, D_FF)),
            _weight_chunk_spec((D_FF, D_MODEL)),
            _resident((1, D_MODEL)),
        ],
        out_specs=pl.BlockSpec((tm, D_MODEL), row_blk),
        out_shape=jax.ShapeDtypeStruct((rows, D_MODEL), F32),
        scratch_shapes=[
            pltpu.VMEM((SSM_WIDTH, SSM_WIDTH), BF16),
            pltpu.VMEM((D_MODEL, D_MODEL), BF16),
            pltpu.VMEM((D_MODEL, D_FF), BF16),
            pltpu.VMEM((D_MODEL, D_FF), BF16),
            pltpu.VMEM((D_FF, D_MODEL), BF16),
            pltpu.VMEM((tm, D_MODEL), F32),
            pltpu.VMEM((2, OCTETS, tm, LANES), F32),
        ],
        compiler_params=pltpu.CompilerParams(
            dimension_semantics=("arbitrary",), vmem_limit_bytes=VMEM_LIMIT),
        name="mix_ffn",
    )(h1, ret, y, glu_w, glu_b, ssm_norm_w, w_out, n3w, wg, wu, wd, fnw)


def _rope_tables(n_pos):
    freqs = 1.0 / (ROPE_BASE ** (np.arange(0, HEAD_DIM, 2, dtype=np.float64) / HEAD_DIM))
    ang = np.arange(n_pos, dtype=np.float64)[:, None] * freqs[None, :]
    cos = np.cos(ang)
    sin = np.sin(ang)
    cos = np.concatenate([cos, cos], axis=-1).astype(np.float32)
    sin = np.concatenate([-sin, sin], axis=-1).astype(np.float32)
    split = lambda t: (jnp.asarray(t[:N_META]), jnp.asarray(t[N_META:]))
    return split(cos), split(sin)


def kernel(x, meta_tokens, ffn1_norm_w, ffn1_w_gate, ffn1_w_up, ffn1_w_down, mix_norm_w, w_in,
           ret_norm_w, ssm_lambda_re, ssm_lambda_im, ssm_log_dt, ssm_b_re, ssm_b_im, ssm_c_re,
           ssm_c_im, ssm_d, ssm_glu_w, ssm_glu_b, ssm_norm_w, w_out, ffn2_norm_w, ffn2_w_gate,
           ffn2_w_up, ffn2_w_down, final_norm_w):
    bsz, seq, _ = x.shape
    assert ffn1_norm_w.shape[0] == 1, "single layer only"
    tm = 512
    ret_rows = 1024
    s5_rows = 4096
    assert seq % tm == 0 and seq % ret_rows == 0 and seq % s5_rows == 0

    row = lambda a: a.reshape(1, -1)
    l = 0
    (cos_m, cos), (sin_m, sin) = _rope_tables(N_META + seq)

    h1, q, k, v, g, u, km, vm, um = _ffn_inproj(
        x.reshape(bsz * seq, D_MODEL), meta_tokens, cos, sin, cos_m, sin_m,
        row(ffn1_norm_w[l]), ffn1_w_gate[l], ffn1_w_up[l], ffn1_w_down[l],
        row(mix_norm_w[l]), w_in[l], tm=tm)

    ret = _retention(q, k, v, g, km, vm, row(ret_norm_w[l]), bsz, seq, ret_rows)

    params = _s5_param_rows(ssm_lambda_re[l], ssm_lambda_im[l], ssm_log_dt[l], ssm_b_re[l],
                            ssm_b_im[l], ssm_c_re[l], ssm_c_im[l], ssm_d[l])
    y = _s5(u, um, params, bsz, seq, s5_rows)

    out = _mix_ffn(h1, ret, y, ssm_glu_w[l], row(ssm_glu_b[l]), row(ssm_norm_w[l]),
                   w_out[l], row(ffn2_norm_w[l]), ffn2_w_gate[l], ffn2_w_up[l],
                   ffn2_w_down[l], row(final_norm_w), tm=tm)
    return out.reshape(bsz, seq, D_MODEL)
```

```python
import functools
import math

import jax
import jax.numpy as jnp
import numpy as np
from jax import lax
from jax.experimental import pallas as pl
from jax.experimental.pallas import tpu as pltpu

D_MODEL = 1024
N_META = 16
RET_HEADS = 4
HEAD_DIM = 128
RET_WIDTH = RET_HEADS * HEAD_DIM
SSM_WIDTH = 512
SSM_GROUP = 16
SSM_GROUPS = SSM_WIDTH // SSM_GROUP
SSM_STATE = 64
RET_BLOCK = 256
D_FF = 2816
FFN_RES = 0.5
ROPE_BASE = 10000.0
EPS = 1e-6
IN_PROJ = 4 * RET_WIDTH + SSM_WIDTH

LANES = 128
MXU_DIM = 256
OCTETS = SSM_WIDTH // LANES
GROUPS_PER_OCTET = LANES // SSM_GROUP
SUB = N_META
OCT_STATE = GROUPS_PER_OCTET * SSM_STATE
GROUP_LANES = SUB * SSM_GROUP
PAIRS = SSM_GROUPS // 2
PAIR_STATE = 2 * SSM_STATE
S5_SLOTS = 2
FF_CHUNK = 256
WEIGHT_CHUNKS = 8
VMEM_LIMIT = 56 * 1024 * 1024

F32 = jnp.float32
BF16 = jnp.bfloat16


def _rms(x, w):
    return x * lax.rsqrt(jnp.mean(x * x, axis=-1, keepdims=True) + EPS) * w


def _swiglu_half_step(h, nw_ref, wg_ref, wu_ref, wd_ref, acc_ref):
    n = _rms(h, nw_ref[...]).astype(BF16)
    for c in range(D_FF // FF_CHUNK):
        sl = slice(c * FF_CHUNK, (c + 1) * FF_CHUNK)
        g = jnp.dot(n, wg_ref[:, sl], preferred_element_type=F32)
        u = jnp.dot(n, wu_ref[:, sl], preferred_element_type=F32)
        a = (g * jax.nn.sigmoid(g) * u).astype(BF16)
        p = jnp.dot(a, wd_ref[sl, :], preferred_element_type=F32)
        if c == 0:
            acc_ref[...] = p
        else:
            acc_ref[...] += p
    return h + FFN_RES * acc_ref[...]


def _rope(x, cos, sin_signed):
    return x * cos + pltpu.roll(x, HEAD_DIM // 2, axis=1) * sin_signed


def _lane_block(shape, width):
    return lax.broadcasted_iota(jnp.int32, shape, 1) // width


def _transpose_lane_blocks(v):
    n = LANES // SSM_GROUP
    assert len(v) == n == GROUPS_PER_OCTET
    blk = _lane_block(v[0].shape, SSM_GROUP)
    d = n // 2
    while d:
        low = (blk & d) == 0
        nxt = list(v)
        for a in range(n):
            if a & d == 0:
                nxt[a] = jnp.where(low, v[a], pltpu.roll(v[a + d], d * SSM_GROUP, axis=1))
                nxt[a + d] = jnp.where(low, pltpu.roll(v[a], LANES - d * SSM_GROUP, axis=1), v[a + d])
        v = nxt
        d //= 2
    return v


def _to_group_dense(tok_ref, out_ref, nsub, row0=0):
    per_half = LANES // SSM_GROUP
    for o in range(OCTETS):
        for hh in range(GROUP_LANES // LANES):
            slabs = [tok_ref[o, pl.ds(hh * per_half + jj, nsub, stride=SUB), :]
                     for jj in range(per_half)]
            for g8, rows in enumerate(_transpose_lane_blocks(slabs)):
                out_ref[o * GROUPS_PER_OCTET + g8, row0:row0 + nsub,
                        hh * LANES:(hh + 1) * LANES] = rows


def _to_token_major(gd_ref, tok_ref, nsub):
    per_half = LANES // SSM_GROUP
    for o in range(OCTETS):
        for hh in range(GROUP_LANES // LANES):
            rows = [gd_ref[o * GROUPS_PER_OCTET + g8, :, hh * LANES:(hh + 1) * LANES]
                    for g8 in range(GROUPS_PER_OCTET)]
            for jj, slab in enumerate(_transpose_lane_blocks(rows)):
                tok_ref[o, pl.ds(hh * per_half + jj, nsub, stride=SUB), :] = slab


def _cast_weight_chunks(step, pairs):
    for src_ref, dst_ref in pairs:
        rows = src_ref.shape[0]
        r0 = pl.multiple_of(step * rows, rows)
        dst_ref[pl.ds(r0, rows), :] = src_ref[...].astype(BF16)


def _weight_chunk_spec(shape):
    rows = shape[0] // WEIGHT_CHUNKS
    assert rows * WEIGHT_CHUNKS == shape[0] and rows % 16 == 0, shape
    return pl.BlockSpec((rows, shape[1]), lambda i: (jnp.minimum(i, WEIGHT_CHUNKS - 1), 0))


def _tile_index(i):
    return jnp.maximum(i - WEIGHT_CHUNKS, 0)


def _ffn_inproj_kernel(x_ref, cos_ref, sin_ref, xm_ref, cosm_ref, sinm_ref, n1w_ref, wg32_ref,
                       wu32_ref, wd32_ref, n2w_ref, win32_ref,
                       h1_ref, q_ref, k_ref, v_ref, g_ref, u_ref, km_ref, vm_ref, um_ref,
                       wg_ref, wu_ref, wd_ref, win_ref, acc_ref, accm_ref, utok_ref, utokm_ref):
    step = pl.program_id(0)
    k_scale = HEAD_DIM ** -0.5

    def mix_norm(h1):
        return _rms(h1, n2w_ref[...]).astype(BF16)

    def in_proj(n, part):
        return jnp.dot(n, win_ref[:, part * RET_WIDTH:(part + 1) * RET_WIDTH],
                       preferred_element_type=F32)

    def rope_heads(p, cos, sin, scale, out_ref):
        for h in range(RET_HEADS):
            hs = slice(h * HEAD_DIM, (h + 1) * HEAD_DIM)
            y = _rope(p[:, hs], cos, sin)
            out_ref[:, hs] = (y if scale is None else y * scale).astype(BF16)

    def split_octets(p, tok_ref):
        for o in range(OCTETS):
            tok_ref[o] = p[:, o * LANES:(o + 1) * LANES]

    @pl.when(step < WEIGHT_CHUNKS)
    def _():
        _cast_weight_chunks(step, ((wg32_ref, wg_ref), (wu32_ref, wu_ref), (wd32_ref, wd_ref),
                                   (win32_ref, win_ref)))

    @pl.when(step == WEIGHT_CHUNKS)
    def _():
        h1 = _swiglu_half_step(xm_ref[...], n1w_ref, wg_ref, wu_ref, wd_ref, accm_ref)
        n = mix_norm(h1)
        rope_heads(in_proj(n, 1), cosm_ref[...], sinm_ref[...], k_scale, km_ref)
        vm_ref[...] = in_proj(n, 2).astype(BF16)
        split_octets(in_proj(n, 4), utokm_ref)
        um_ref[...] = jnp.zeros(um_ref.shape, F32)
        _to_group_dense(utokm_ref, um_ref, 1, row0=7)

    @pl.when(step >= WEIGHT_CHUNKS)
    def _():
        h1 = _swiglu_half_step(x_ref[...], n1w_ref, wg_ref, wu_ref, wd_ref, acc_ref)
        h1_ref[...] = h1
        n = mix_norm(h1)
        split_octets(in_proj(n, 4), utok_ref)
        _to_group_dense(utok_ref, u_ref, u_ref.shape[1])
        cos = cos_ref[...]
        sin = sin_ref[...]
        rope_heads(in_proj(n, 0), cos, sin, None, q_ref)
        rope_heads(in_proj(n, 1), cos, sin, k_scale, k_ref)
        v_ref[...] = in_proj(n, 2).astype(BF16)
        g_ref[...] = in_proj(n, 3)


def _resident(shape):
    nd = len(shape)
    return pl.BlockSpec(shape, lambda *_: (0,) * nd, pipeline_mode=pl.Buffered(1))


def _ffn_inproj(x2, xm, cos, sin, cos_m, sin_m, n1w, wg, wu, wd, n2w, w_in, tm):
    rows = x2.shape[0]
    pos_blocks = cos.shape[0] // tm
    row_blk = lambda i: (_tile_index(i), 0)
    pos_blk = lambda i: (_tile_index(i) % pos_blocks, 0)
    out_shape = (
        jax.ShapeDtypeStruct((rows, D_MODEL), F32),
        jax.ShapeDtypeStruct((rows, RET_WIDTH), BF16),
        jax.ShapeDtypeStruct((rows, RET_WIDTH), BF16),
        jax.ShapeDtypeStruct((rows, RET_WIDTH), BF16),
        jax.ShapeDtypeStruct((rows, RET_WIDTH), F32),
        jax.ShapeDtypeStruct((SSM_GROUPS, rows // SUB, GROUP_LANES), F32),
        jax.ShapeDtypeStruct((N_META, RET_WIDTH), BF16),
        jax.ShapeDtypeStruct((N_META, RET_WIDTH), BF16),
        jax.ShapeDtypeStruct((SSM_GROUPS, 8, GROUP_LANES), F32),
    )
    return pl.pallas_call(
        _ffn_inproj_kernel,
        grid=(WEIGHT_CHUNKS + rows // tm,),
        in_specs=[
            pl.BlockSpec((tm, D_MODEL), row_blk),
            pl.BlockSpec((tm, HEAD_DIM), pos_blk),
            pl.BlockSpec((tm, HEAD_DIM), pos_blk),
            _resident((N_META, D_MODEL)),
            _resident((N_META, HEAD_DIM)),
            _resident((N_META, HEAD_DIM)),
            _resident((1, D_MODEL)),
            _weight_chunk_spec((D_MODEL, D_FF)),
            _weight_chunk_spec((D_MODEL, D_FF)),
            _weight_chunk_spec((D_FF, D_MODEL)),
            _resident((1, D_MODEL)),
            _weight_chunk_spec((D_MODEL, IN_PROJ)),
        ],
        out_specs=(
            pl.BlockSpec((tm, D_MODEL), row_blk),
            pl.BlockSpec((tm, RET_WIDTH), row_blk),
            pl.BlockSpec((tm, RET_WIDTH), row_blk),
            pl.BlockSpec((tm, RET_WIDTH), row_blk),
            pl.BlockSpec((tm, RET_WIDTH), row_blk),
            pl.BlockSpec((SSM_GROUPS, tm // SUB, GROUP_LANES), lambda i: (0, _tile_index(i), 0)),
            pl.BlockSpec((N_META, RET_WIDTH), lambda i: (0, 0)),
            pl.BlockSpec((N_META, RET_WIDTH), lambda i: (0, 0)),
            pl.BlockSpec((SSM_GROUPS, 8, GROUP_LANES), lambda i: (0, 0, 0)),
        ),
        out_shape=out_shape,
        scratch_shapes=[
            pltpu.VMEM((D_MODEL, D_FF), BF16),
            pltpu.VMEM((D_MODEL, D_FF), BF16),
            pltpu.VMEM((D_FF, D_MODEL), BF16),
            pltpu.VMEM((D_MODEL, IN_PROJ), BF16),
            pltpu.VMEM((tm, D_MODEL), F32),
            pltpu.VMEM((N_META, D_MODEL), F32),
            pltpu.VMEM((OCTETS, tm, LANES), F32),
            pltpu.VMEM((OCTETS, N_META, LANES), F32),
        ],
        compiler_params=pltpu.CompilerParams(
            dimension_semantics=("arbitrary",), vmem_limit_bytes=VMEM_LIMIT),
        name="ffn_inproj",
    )(x2, cos, sin, xm, cos_m, sin_m, n1w, wg, wu, wd, n2w, w_in)


def _retention_kernel(q_ref, k_ref, v_ref, g_ref, km_ref, vm_ref, mask_ref, wq_ref, wk_ref, wm_ref,
                      gc_ref, nw_ref, o_ref, state_ref, *, rows):
    tn = (((0,), (0,)), ((), ()))
    nt = (((1,), (1,)), ((), ()))

    @pl.when(pl.program_id(1) == 0)
    def _():
        for h in range(RET_HEADS):
            hs = slice(h * HEAD_DIM, (h + 1) * HEAD_DIM)
            kw = (km_ref[:, hs].astype(F32) * wm_ref[h]).astype(BF16)
            state_ref[h] = lax.dot_general(kw, vm_ref[:, hs], tn, preferred_element_type=F32)

    def chunk(i, carry):
        r0 = pl.multiple_of(i * RET_BLOCK, RET_BLOCK)
        rs = pl.ds(r0, RET_BLOCK)
        for h in range(RET_HEADS):
            hs = slice(h * HEAD_DIM, (h + 1) * HEAD_DIM)
            qh = q_ref[rs, hs]
            kh = k_ref[rs, hs]
            vh = v_ref[rs, hs]
            st = state_ref[h]
            s = lax.dot_general(qh, kh, nt, preferred_element_type=F32) * mask_ref[h]
            o = (jnp.dot(s.astype(BF16), vh, preferred_element_type=F32)
                 + jnp.dot(qh, st.astype(BF16), preferred_element_type=F32) * wq_ref[h])
            kw = (kh.astype(F32) * wk_ref[h]).astype(BF16)
            state_ref[h] = gc_ref[h] * st + lax.dot_general(kw, vh, tn, preferred_element_type=F32)
            mu = jnp.mean(o, axis=-1, keepdims=True)
            d = o - mu
            var = jnp.mean(d * d, axis=-1, keepdims=True)
            y = d * lax.rsqrt(var + EPS) * nw_ref[:, hs]
            gate = g_ref[rs, hs]
            o_ref[rs, hs] = (gate * jax.nn.sigmoid(gate) * y).astype(BF16)
        return carry

    lax.fori_loop(0, rows // RET_BLOCK, chunk, 0, unroll=2)


def _retention_tables():
    log_g = np.log(1.0 - 2.0 ** (-5.0 - np.arange(RET_HEADS, dtype=np.float64)))
    i = np.arange(RET_BLOCK)
    diff = i[:, None] - i[None, :]
    mask = np.where(diff[None] >= 0, np.exp(log_g[:, None, None] * np.maximum(diff, 0)[None]), 0.0)
    pos = np.arange(RET_BLOCK, dtype=np.float64)
    full = lambda w: np.broadcast_to(w[:, :, None], w.shape + (HEAD_DIM,))
    w_q = full(np.exp(log_g[:, None] * (pos + 1.0)[None]))
    w_k = full(np.exp(log_g[:, None] * (RET_BLOCK - 1 - pos)[None]))
    w_m = full(np.exp(log_g[:, None] * (N_META - 1 - np.arange(N_META, dtype=np.float64))[None]))
    g_c = np.broadcast_to(np.exp(log_g * RET_BLOCK)[:, None, None], (RET_HEADS, HEAD_DIM, HEAD_DIM))
    return tuple(jnp.asarray(t, dtype=F32) for t in (mask, w_q, w_k, w_m, g_c))


def _retention(q, k, v, g, km, vm, ret_norm_w, bsz, seq, rows):
    mask, w_q, w_k, w_m, g_c = _retention_tables()
    steps = seq // rows
    blk = lambda b, c: (b * steps + c, 0)
    full3 = lambda a: pl.BlockSpec(a.shape, lambda b, c: (0, 0, 0))
    full2 = lambda a: pl.BlockSpec(a.shape, lambda b, c: (0, 0))
    return pl.pallas_call(
        functools.partial(_retention_kernel, rows=rows),
        grid=(bsz, steps),
        in_specs=[pl.BlockSpec((rows, RET_WIDTH), blk)] * 4
        + [full2(km), full2(vm), full3(mask), full3(w_q), full3(w_k), full3(w_m), full3(g_c),
           full2(ret_norm_w)],
        out_specs=pl.BlockSpec((rows, RET_WIDTH), blk),
        out_shape=jax.ShapeDtypeStruct((bsz * seq, RET_WIDTH), BF16),
        scratch_shapes=[pltpu.VMEM((RET_HEADS, HEAD_DIM, HEAD_DIM), F32)],
        compiler_params=pltpu.CompilerParams(
            dimension_semantics=("arbitrary", "arbitrary"), vmem_limit_bytes=VMEM_LIMIT),
        name="retention",
    )(q, k, v, g, km, vm, mask, w_q, w_k, w_m, g_c, ret_norm_w)


def _shift_lanes_zero_fill(halves, shift, lane):
    h0, h1 = halves
    whole, s = divmod(shift, LANES)
    r0 = h0 if s == 0 else pltpu.roll(h0, s, axis=1)
    zero = jnp.zeros_like(h0)
    if whole == 1:
        return zero, jnp.where(lane >= s, r0, zero)
    r1 = h1 if s == 0 else pltpu.roll(h1, s, axis=1)
    return jnp.where(lane >= s, r0, zero), jnp.where(lane >= s, r1, r0)


def _s5_assemble_octet(o, lre_ref, lim_ref, ldt_ref, btr_ref, bti_ref, ctr_ref, cti_ref,
                       toep_ref, win_ref, wout_ref, cpw_ref, pow_ref, xall_ref, wt_ref):
    lre = lre_ref[o]
    lim = lim_ref[o]
    dt = jnp.exp(ldt_ref[o])
    n_pow = pow_ref.shape[1]
    ell = lax.broadcasted_iota(jnp.int32, (n_pow, OCT_STATE), 0).astype(F32)
    mag = jnp.exp(ell * (lre * dt))
    ang = ell * (lim * dt)
    pow_ref[0] = mag * jnp.cos(ang)
    pow_ref[1] = mag * jnp.sin(ang)
    ell = ((lax.broadcasted_iota(jnp.int32, (8, OCT_STATE), 0) + 1) * SUB).astype(F32)
    mag = jnp.exp(ell * (lre * dt))
    ang = ell * (lim * dt)
    cp_re = mag * jnp.cos(ang)
    cp_im = mag * jnp.sin(ang)
    a_re = pow_ref[0, 1:2, :]
    a_im = pow_ref[1, 1:2, :]
    den = lre * lre + lim * lim
    num_re = a_re - 1.0
    coef_re = (num_re * lre + a_im * lim) / den
    coef_im = (a_im * lre - num_re * lim) / den
    btr = btr_ref[o]
    bti = bti_ref[o]
    bbar_re = coef_re * btr - coef_im * bti
    bbar_im = coef_re * bti + coef_im * btr
    ctr = ctr_ref[o]
    cti = cti_ref[o]

    pairs_per_octet = GROUPS_PER_OCTET // 2
    pair_lane_group = _lane_block((SSM_GROUP, PAIR_STATE), SSM_STATE)
    for l in range(SUB):
        pr = pow_ref[0, l:l + 1, :]
        pi = pow_ref[1, l:l + 1, :]
        x_re = pr * bbar_re - pi * bbar_im
        x_im = pr * bbar_im + pi * bbar_re
        rows = slice(l * SSM_GROUP, (l + 1) * SSM_GROUP)
        xall_ref[0, rows, :] = x_re
        xall_ref[1, rows, :] = x_im
        pr1 = pow_ref[0, l + 1:l + 2, :]
        pi1 = pow_ref[1, l + 1:l + 2, :]
        w_re = pr1 * ctr - pi1 * cti
        w_im = pr1 * cti + pi1 * ctr
        j = SUB - 1 - l
        for pp in range(pairs_per_octet):
            sl = slice(pp * PAIR_STATE, (pp + 1) * PAIR_STATE)
            pair = o * pairs_per_octet + pp
            for gi in range(2):
                keep = pair_lane_group == gi
                in_rows = slice(gi * GROUP_LANES + j * SSM_GROUP, gi * GROUP_LANES + (j + 1) * SSM_GROUP)
                out_rows = slice(gi * GROUP_LANES + l * SSM_GROUP, gi * GROUP_LANES + (l + 1) * SSM_GROUP)
                for part, x, w in ((0, x_re, w_re), (1, x_im, -w_im)):
                    cols = slice(part * PAIR_STATE, (part + 1) * PAIR_STATE)
                    win_ref[pair, in_rows, cols] = jnp.where(keep, x[:, sl], 0.0).astype(BF16)
                    wt_ref[pp, out_rows, cols] = jnp.where(keep, w[:, sl], 0.0)
    for pp in range(pairs_per_octet):
        pair = o * pairs_per_octet + pp
        wout_ref[pair] = wt_ref[pp].T.astype(BF16)
        sl = slice(pp * PAIR_STATE, (pp + 1) * PAIR_STATE)
        cpw_ref[pair, 0] = cp_re[:, sl]
        cpw_ref[pair, 1] = cp_im[:, sl]

    row = lax.broadcasted_iota(jnp.int32, (LANES, OCT_STATE), 0)
    col = lax.broadcasted_iota(jnp.int32, (LANES, OCT_STATE), 1)
    same_group = (row // SSM_GROUP) == (col // SSM_STATE)

    def block_diag_t(x):
        return jnp.where(same_group, jnp.concatenate([x] * GROUPS_PER_OCTET, axis=0), 0.0).T

    k_all = (jnp.dot(xall_ref[0], block_diag_t(ctr), precision=lax.Precision.HIGHEST,
                     preferred_element_type=F32)
             - jnp.dot(xall_ref[1], block_diag_t(cti), precision=lax.Precision.HIGHEST,
                       preferred_element_type=F32))
    per_half = LANES // SSM_GROUP
    blk = _lane_block((SSM_GROUP, LANES), SSM_GROUP)
    lane = lax.broadcasted_iota(jnp.int32, (SSM_GROUP, LANES), 1)
    for g8 in range(GROUPS_PER_OCTET):
        strip = []
        for hh in range(GROUP_LANES // LANES):
            acc = jnp.zeros((SSM_GROUP, LANES), F32)
            for ll in range(per_half):
                lag = hh * per_half + ll
                k_lag = k_all[lag * SSM_GROUP:(lag + 1) * SSM_GROUP, :]
                shift = ((ll - g8) * SSM_GROUP) % LANES
                rolled = k_lag if shift == 0 else pltpu.roll(k_lag, shift, axis=1)
                acc = jnp.where(blk == ll, rolled, acc)
            strip.append(acc)
        for j in range(SUB):
            h0, h1 = _shift_lanes_zero_fill(strip, j * SSM_GROUP, lane)
            rows = slice(j * SSM_GROUP, (j + 1) * SSM_GROUP)
            toep_ref[o * GROUPS_PER_OCTET + g8, rows, 0:LANES] = h0.astype(BF16)
            toep_ref[o * GROUPS_PER_OCTET + g8, rows, LANES:2 * LANES] = h1.astype(BF16)


def _s5_state_scan(st_ref, zb_ref, cpow_ref, nsc):
    tiles = nsc // 8
    width = st_ref.shape[1] // 2
    re = slice(0, width)
    im = slice(width, 2 * width)
    rowmod = lax.broadcasted_iota(jnp.int32, (8, width), 0)
    src, dst = st_ref, zb_ref
    for s in (1, 2, 4):
        keep = rowmod >= s
        c_re = jnp.concatenate([jnp.where(keep, cpow_ref[0, s - 1:s, :], 0.0)] * tiles, axis=0)
        c_im = jnp.concatenate([jnp.where(keep, cpow_ref[1, s - 1:s, :], 0.0)] * tiles, axis=0)
        sh_re = src[pl.ds(8 - s, nsc), re]
        sh_im = src[pl.ds(8 - s, nsc), im]
        dst[8:8 + nsc, re] = src[8:8 + nsc, re] + c_re * sh_re - c_im * sh_im
        dst[8:8 + nsc, im] = src[8:8 + nsc, im] + c_re * sh_im + c_im * sh_re
        src, dst = dst, src
    ca_re = cpow_ref[0]
    ca_im = cpow_ref[1]
    c_re = st_ref[7:8, re]
    c_im = st_ref[7:8, im]
    for t in range(tiles):
        rows = slice(8 + 8 * t, 16 + 8 * t)
        cb_re = jnp.broadcast_to(c_re, (8, width))
        cb_im = jnp.broadcast_to(c_im, (8, width))
        p_re = src[rows, re] + ca_re * cb_re - ca_im * cb_im
        p_im = src[rows, im] + ca_re * cb_im + ca_im * cb_re
        st_ref[rows, re] = p_re
        st_ref[rows, im] = p_im
        c_re = p_re[7:8]
        c_im = p_im[7:8]


def _s5_kernel(u_ref, um_ref, lre_ref, lim_ref, ldt_ref, btr_ref, bti_ref, ctr_ref, cti_ref, d_ref,
               y_ref, toep_ref, win_ref, wout_ref, cpw_ref, pow_ref, xall_ref, wt_ref, lhs_ref,
               st_ref, zb_ref, carry_ref, *, nsc):
    first_call_step = (pl.program_id(0) == 0) & (pl.program_id(1) == 0)
    r = pl.program_id(1)

    @pl.when(first_call_step)
    def _():
        zb_ref[:, 0:8, :] = jnp.zeros((zb_ref.shape[0], 8, zb_ref.shape[2]), F32)

        def octet(o, carry):
            _s5_assemble_octet(o, lre_ref, lim_ref, ldt_ref, btr_ref, bti_ref, ctr_ref, cti_ref,
                               toep_ref, win_ref, wout_ref, cpw_ref, pow_ref, xall_ref, wt_ref)
            return carry
        lax.fori_loop(0, OCTETS, octet, 0)

    @pl.when(r == 0)
    def _():
        def meta_state(pair, carry):
            meta = jnp.concatenate([um_ref[2 * pair], um_ref[2 * pair + 1]], axis=1).astype(BF16)
            carry_ref[pair] = jnp.dot(meta, win_ref[pair], preferred_element_type=F32)
            return carry
        lax.fori_loop(0, PAIRS, meta_state, 0)

    lo = slice(0, GROUP_LANES)
    hi = slice(GROUP_LANES, 2 * GROUP_LANES)
    slots = st_ref.shape[0]

    def pair_block(k, carry):
        pairs = [slots * k + s for s in range(slots)]
        for s, pair in enumerate(pairs):
            lhs_ref[s, :, lo] = u_ref[2 * pair].astype(BF16)
            lhs_ref[s, :, hi] = u_ref[2 * pair + 1].astype(BF16)
            st_ref[s, 0:8, :] = carry_ref[pair]
            st_ref[s, 8:8 + nsc, :] = jnp.dot(lhs_ref[s], win_ref[pair], preferred_element_type=F32)
        for s, pair in enumerate(pairs):
            _s5_state_scan(st_ref.at[s], zb_ref.at[s], cpw_ref.at[pair], nsc)
            carry_ref[pair] = st_ref[s, nsc:nsc + 8, :]
        for s, pair in enumerate(pairs):
            s_in = st_ref[s, pl.ds(7, nsc), :].astype(BF16)
            y_state = jnp.dot(s_in, wout_ref[pair], preferred_element_type=F32)
            d = d_ref[pair]
            for gi, cols in ((0, lo), (1, hi)):
                g = 2 * pair + gi
                y = (jnp.dot(lhs_ref[s, :, cols], toep_ref[g], preferred_element_type=F32)
                     + y_state[:, cols] + d[:, cols] * u_ref[g])
                y_ref[g] = jax.nn.gelu(y)
        return carry

    lax.fori_loop(0, PAIRS // slots, pair_block, 0)


def _s5_param_rows(lam_re, lam_im, log_dt, b_re, b_im, c_re, c_im, d):
    go = GROUPS_PER_OCTET
    lanes = lambda a: a.reshape(OCTETS, 1, OCT_STATE)
    ldt = jnp.broadcast_to(log_dt[:, None], (SSM_GROUPS, SSM_STATE))
    bt = lambda a: jnp.transpose(a.reshape(OCTETS, go, SSM_STATE, SSM_GROUP),
                                 (0, 3, 1, 2)).reshape(OCTETS, SSM_GROUP, OCT_STATE)
    ct = lambda a: jnp.transpose(a.reshape(OCTETS, go, SSM_GROUP, SSM_STATE),
                                 (0, 2, 1, 3)).reshape(OCTETS, SSM_GROUP, OCT_STATE)
    d_pairs = jnp.broadcast_to(d.reshape(PAIRS, 2, 1, SSM_GROUP),
                               (PAIRS, 2, SUB, SSM_GROUP)).reshape(PAIRS, 1, 2 * GROUP_LANES)
    return (lanes(lam_re), lanes(lam_im), lanes(ldt), bt(b_re), bt(b_im), ct(c_re), ct(c_im), d_pairs)


def _s5(u, um, params, bsz, seq, rows):
    nsc = rows // SUB
    steps = seq // rows
    n_pow = SUB + 8
    full = lambda a: pl.BlockSpec(a.shape, lambda b, r: (0,) * a.ndim)
    tok_blk = pl.BlockSpec((SSM_GROUPS, nsc, GROUP_LANES), lambda b, r: (0, b * steps + r, 0))
    return pl.pallas_call(
        functools.partial(_s5_kernel, nsc=nsc),
        grid=(bsz, steps),
        in_specs=[tok_blk, full(um)] + [full(p) for p in params],
        out_specs=tok_blk,
        out_shape=jax.ShapeDtypeStruct(u.shape, F32),
        scratch_shapes=[
            pltpu.VMEM((SSM_GROUPS, GROUP_LANES, GROUP_LANES), BF16),
            pltpu.VMEM((PAIRS, 2 * GROUP_LANES, 2 * PAIR_STATE), BF16),
            pltpu.VMEM((PAIRS, 2 * PAIR_STATE, 2 * GROUP_LANES), BF16),
            pltpu.VMEM((PAIRS, 2, 8, PAIR_STATE), F32),
            pltpu.VMEM((2, n_pow, OCT_STATE), F32),
            pltpu.VMEM((2, SUB * SSM_GROUP, OCT_STATE), F32),
            pltpu.VMEM((GROUPS_PER_OCTET // 2, 2 * GROUP_LANES, 2 * PAIR_STATE), F32),
            pltpu.VMEM((S5_SLOTS, nsc, 2 * GROUP_LANES), BF16),
            pltpu.VMEM((S5_SLOTS, nsc + 8, 2 * PAIR_STATE), F32),
            pltpu.VMEM((S5_SLOTS, nsc + 8, 2 * PAIR_STATE), F32),
            pltpu.VMEM((PAIRS, 8, 2 * PAIR_STATE), F32),
        ],
        compiler_params=pltpu.CompilerParams(
            dimension_semantics=("arbitrary", "arbitrary"), vmem_limit_bytes=VMEM_LIMIT),
        name="s5_mixer",
    )(u, um, *params)


def _mix_ffn_kernel(h1_ref, ret_ref, ynext_ref, gw32_ref, gb_ref, snw_ref, wo32_ref, n3w_ref,
                    wg32_ref, wu32_ref, wd32_ref, fnw_ref, out_ref,
                    gw_ref, wo_ref, wg_ref, wu_ref, wd_ref, acc_ref, ytok_ref):
    step = pl.program_id(0)
    nsub = ynext_ref.shape[1]

    @pl.when(step < WEIGHT_CHUNKS)
    def _():
        _cast_weight_chunks(step, ((gw32_ref, gw_ref), (wo32_ref, wo_ref), (wg32_ref, wg_ref),
                                   (wu32_ref, wu_ref), (wd32_ref, wd_ref)))

    @pl.when(step == WEIGHT_CHUNKS - 1)
    def _():
        _to_token_major(ynext_ref, ytok_ref.at[0], nsub)

    @pl.when(step >= WEIGHT_CHUNKS)
    def _():
        slot = lax.rem(step - WEIGHT_CHUNKS, 2)
        cur = ytok_ref.at[slot]
        y = jnp.concatenate([cur[o] for o in range(OCTETS)], axis=-1)
        _to_token_major(ynext_ref, ytok_ref.at[1 - slot], nsub)
        z = y * jax.nn.sigmoid(jnp.dot(y.astype(BF16), gw_ref[...], preferred_element_type=F32)
                               + gb_ref[...])
        ssm = _rms(z, snw_ref[...]).astype(BF16)
        mixed = (jnp.dot(ret_ref[...], wo_ref[0:RET_WIDTH, :], preferred_element_type=F32)
                 + jnp.dot(ssm, wo_ref[RET_WIDTH:, :], preferred_element_type=F32))
        h2 = h1_ref[...] + mixed
        h3 = _swiglu_half_step(h2, n3w_ref, wg_ref, wu_ref, wd_ref, acc_ref)
        out_ref[...] = _rms(h3, fnw_ref[...])


def _mix_ffn(h1, ret, y, glu_w, glu_b, ssm_norm_w, w_out, n3w, wg, wu, wd, fnw, tm):
    rows = h1.shape[0]
    tiles = rows // tm
    row_blk = lambda i: (_tile_index(i), 0)
    next_blk = lambda i: (0, jnp.minimum(_tile_index(i + 1), tiles - 1), 0)
    return pl.pallas_call(
        _mix_ffn_kernel,
        grid=(WEIGHT_CHUNKS + tiles,),
        in_specs=[
            pl.BlockSpec((tm, D_MODEL), row_blk),
            pl.BlockSpec((tm, RET_WIDTH), row_blk),
            pl.BlockSpec((SSM_GROUPS, tm // SUB, GROUP_LANES), next_blk),
            _weight_chunk_spec((SSM_WIDTH, SSM_WIDTH)),
            _resident((1, SSM_WIDTH)),
            _resident((1, SSM_WIDTH)),
            _weight_chunk_spec((D_MODEL, D_MODEL)),
            _resident((1, D_MODEL)),
            _weight_chunk_spec((D_MODEL, D_FF)),
            _weight_chunk_spec((D_MODEL, D_FF)),
            _weight_chunk_spec((D_FF, D_MODEL)),
            _resident((1, D_MODEL)),
        ],
        out_specs=pl.BlockSpec((tm, D_MODEL), row_blk),
        out_shape=jax.ShapeDtypeStruct((rows, D_MODEL), F32),
        scratch_shapes=[
            pltpu.VMEM((SSM_WIDTH, SSM_WIDTH), BF16),
            pltpu.VMEM((D_MODEL, D_MODEL), BF16),
            pltpu.VMEM((D_MODEL, D_FF), BF16),
            pltpu.VMEM((D_MODEL, D_FF), BF16),
            pltpu.VMEM((D_FF, D_MODEL), BF16),
            pltpu.VMEM((tm, D_MODEL), F32),
            pltpu.VMEM((2, OCTETS, tm, LANES), F32),
        ],
        compiler_params=pltpu.CompilerParams(
            dimension_semantics=("arbitrary",), vmem_limit_bytes=VMEM_LIMIT),
        name="mix_ffn",
    )(h1, ret, y, glu_w, glu_b, ssm_norm_w, w_out, n3w, wg, wu, wd, fnw)


def _rope_tables(n_pos):
    freqs = 1.0 / (ROPE_BASE ** (np.arange(0, HEAD_DIM, 2, dtype=np.float64) / HEAD_DIM))
    ang = np.arange(n_pos, dtype=np.float64)[:, None] * freqs[None, :]
    cos = np.cos(ang)
    sin = np.sin(ang)
    cos = np.concatenate([cos, cos], axis=-1).astype(np.float32)
    sin = np.concatenate([-sin, sin], axis=-1).astype(np.float32)
    split = lambda t: (jnp.asarray(t[:N_META]), jnp.asarray(t[N_META:]))
    return split(cos), split(sin)


def kernel(x, meta_tokens, ffn1_norm_w, ffn1_w_gate, ffn1_w_up, ffn1_w_down, mix_norm_w, w_in,
           ret_norm_w, ssm_lambda_re, ssm_lambda_im, ssm_log_dt, ssm_b_re, ssm_b_im, ssm_c_re,
           ssm_c_im, ssm_d, ssm_glu_w, ssm_glu_b, ssm_norm_w, w_out, ffn2_norm_w, ffn2_w_gate,
           ffn2_w_up, ffn2_w_down, final_norm_w):
    bsz, seq, _ = x.shape
    assert ffn1_norm_w.shape[0] == 1, "single layer only"
    tm = 512
    ret_rows = 1024
    s5_rows = 4096
    assert seq % tm == 0 and seq % ret_rows == 0 and seq % s5_rows == 0

    row = lambda a: a.reshape(1, -1)
    l = 0
    (cos_m, cos), (sin_m, sin) = _rope_tables(N_META + seq)

    h1, q, k, v, g, u, km, vm, um = _ffn_inproj(
        x.reshape(bsz * seq, D_MODEL), meta_tokens, cos, sin, cos_m, sin_m,
        row(ffn1_norm_w[l]), ffn1_w_gate[l], ffn1_w_up[l], ffn1_w_down[l],
        row(mix_norm_w[l]), w_in[l], tm=tm)

    ret = _retention(q, k, v, g, km, vm, row(ret_norm_w[l]), bsz, seq, ret_rows)

    params = _s5_param_rows(ssm_lambda_re[l], ssm_lambda_im[l], ssm_log_dt[l], ssm_b_re[l],
                            ssm_b_im[l], ssm_c_re[l], ssm_c_im[l], ssm_d[l])
    y = _s5(u, um, params, bsz, seq, s5_rows)

    out = _mix_ffn(h1, ret, y, ssm_glu_w[l], row(ssm_glu_b[l]), row(ssm_norm_w[l]),
                   w_out[l], row(ffn2_norm_w[l]), ffn2_w_gate[l], ffn2_w_up[l],
                   ffn2_w_down[l], row(final_norm_w), tm=tm)
    return out.reshape(bsz, seq, D_MODEL)
```

```python
import functools
import math

import jax
import jax.numpy as jnp
import numpy as np
from jax import lax
from jax.experimental import pallas as pl
from jax.experimental.pallas import tpu as pltpu

D_MODEL = 1024
N_META = 16
RET_HEADS = 4
HEAD_DIM = 128
RET_WIDTH = RET_HEADS * HEAD_DIM
SSM_WIDTH = 512
SSM_GROUP = 16
SSM_GROUPS = SSM_WIDTH // SSM_GROUP
SSM_STATE = 64
RET_BLOCK = 256
D_FF = 2816
FFN_RES = 0.5
ROPE_BASE = 10000.0
EPS = 1e-6
IN_PROJ = 4 * RET_WIDTH + SSM_WIDTH

LANES = 128
MXU_DIM = 256
OCTETS = SSM_WIDTH // LANES
GROUPS_PER_OCTET = LANES // SSM_GROUP
SUB = N_META
OCT_STATE = GROUPS_PER_OCTET * SSM_STATE
GROUP_LANES = SUB * SSM_GROUP
PAIRS = SSM_GROUPS // 2
PAIR_STATE = 2 * SSM_STATE
S5_SLOTS = 2
TOK_PITCH = 24
FF_CHUNK = 256
WEIGHT_CHUNKS = 8
VMEM_LIMIT = 56 * 1024 * 1024

F32 = jnp.float32
BF16 = jnp.bfloat16


def _rms(x, w):
    return x * lax.rsqrt(jnp.mean(x * x, axis=-1, keepdims=True) + EPS) * w


def _swiglu_half_step(h, nw_ref, wg_ref, wu_ref, wd_ref, acc_ref):
    n = _rms(h, nw_ref[...]).astype(BF16)
    for c in range(D_FF // FF_CHUNK):
        sl = slice(c * FF_CHUNK, (c + 1) * FF_CHUNK)
        g = jnp.dot(n, wg_ref[:, sl], preferred_element_type=F32)
        u = jnp.dot(n, wu_ref[:, sl], preferred_element_type=F32)
        a = (g * jax.nn.sigmoid(g) * u).astype(BF16)
        p = jnp.dot(a, wd_ref[sl, :], preferred_element_type=F32)
        if c == 0:
            acc_ref[...] = p
        else:
            acc_ref[...] += p
    return h + FFN_RES * acc_ref[...]


def _rope(x, cos, sin_signed):
    return x * cos + pltpu.roll(x, HEAD_DIM // 2, axis=1) * sin_signed


def _lane_block(shape, width):
    return lax.broadcasted_iota(jnp.int32, shape, 1) // width


def _transpose_lane_blocks(v):
    n = LANES // SSM_GROUP
    assert len(v) == n == GROUPS_PER_OCTET
    blk = _lane_block(v[0].shape, SSM_GROUP)
    d = n // 2
    while d:
        low = (blk & d) == 0
        nxt = list(v)
        for a in range(n):
            if a & d == 0:
                nxt[a] = jnp.where(low, v[a], pltpu.roll(v[a + d], d * SSM_GROUP, axis=1))
                nxt[a + d] = jnp.where(low, pltpu.roll(v[a], LANES - d * SSM_GROUP, axis=1), v[a + d])
        v = nxt
        d //= 2
    return v


def _to_group_dense(tok_ref, out_ref, nsub, row0=0):
    per_half = LANES // SSM_GROUP
    for o in range(OCTETS):
        for hh in range(GROUP_LANES // LANES):
            slabs = [tok_ref[o, pl.ds(hh * per_half + jj, nsub, stride=SUB), :]
                     for jj in range(per_half)]
            for g8, rows in enumerate(_transpose_lane_blocks(slabs)):
                out_ref[o * GROUPS_PER_OCTET + g8, row0:row0 + nsub,
                        hh * LANES:(hh + 1) * LANES] = rows


def _to_token_major(gd_ref, tok_ref, nsub, fn):
    per_half = LANES // SSM_GROUP
    for o in range(OCTETS):
        for hh in range(GROUP_LANES // LANES):
            rows = [gd_ref[o * GROUPS_PER_OCTET + g8, :, hh * LANES:(hh + 1) * LANES]
                    for g8 in range(GROUPS_PER_OCTET)]
            for jj, slab in enumerate(_transpose_lane_blocks(rows)):
                tok_ref[o, pl.ds(hh * per_half + jj, nsub, stride=TOK_PITCH), :] = fn(slab)


def _token_major_rows(tok_ref, nsub):
    return jnp.concatenate(
        [jnp.concatenate([tok_ref[o, c * TOK_PITCH:c * TOK_PITCH + SUB, :] for c in range(nsub)],
                         axis=0) for o in range(OCTETS)], axis=-1)


def _cast_weight_chunks(step, pairs):
    for src_ref, dst_ref in pairs:
        rows = src_ref.shape[0]
        r0 = pl.multiple_of(step * rows, rows)
        dst_ref[pl.ds(r0, rows), :] = src_ref[...].astype(BF16)


def _weight_chunk_spec(shape):
    rows = shape[0] // WEIGHT_CHUNKS
    assert rows * WEIGHT_CHUNKS == shape[0] and rows % 16 == 0, shape
    return pl.BlockSpec((rows, shape[1]), lambda i: (jnp.minimum(i, WEIGHT_CHUNKS - 1), 0))


def _tile_index(i):
    return jnp.maximum(i - WEIGHT_CHUNKS, 0)


def _ffn_inproj_kernel(x_ref, cos_ref, sin_ref, xm_ref, cosm_ref, sinm_ref, n1w_ref, wg32_ref,
                       wu32_ref, wd32_ref, n2w_ref, win32_ref, rnw_ref,
                       h1_ref, q_ref, k_ref, v_ref, g_ref, u_ref, km_ref, vm_ref, um_ref,
                       wg_ref, wu_ref, wd_ref, win_ref, acc_ref, accm_ref, utok_ref, utokm_ref):
    step = pl.program_id(0)
    k_scale = HEAD_DIM ** -0.5

    def mix_norm(h1):
        return _rms(h1, n2w_ref[...]).astype(BF16)

    def in_proj(n, part):
        return jnp.dot(n, win_ref[:, part * RET_WIDTH:(part + 1) * RET_WIDTH],
                       preferred_element_type=F32)

    def rope_heads(p, cos, sin, scale, out_ref):
        for h in range(RET_HEADS):
            hs = slice(h * HEAD_DIM, (h + 1) * HEAD_DIM)
            y = _rope(p[:, hs], cos, sin)
            out_ref[:, hs] = (y if scale is None else y * scale).astype(BF16)

    def split_octets(p, tok_ref):
        for o in range(OCTETS):
            tok_ref[o] = p[:, o * LANES:(o + 1) * LANES]

    @pl.when(step < WEIGHT_CHUNKS)
    def _():
        _cast_weight_chunks(step, ((wg32_ref, wg_ref), (wu32_ref, wu_ref), (wd32_ref, wd_ref),
                                   (win32_ref, win_ref)))

    @pl.when(step == WEIGHT_CHUNKS)
    def _():
        h1 = _swiglu_half_step(xm_ref[...], n1w_ref, wg_ref, wu_ref, wd_ref, accm_ref)
        n = mix_norm(h1)
        rope_heads(in_proj(n, 1), cosm_ref[...], sinm_ref[...], k_scale, km_ref)
        vm_ref[...] = in_proj(n, 2).astype(BF16)
        split_octets(in_proj(n, 4), utokm_ref)
        um_ref[...] = jnp.zeros(um_ref.shape, F32)
        _to_group_dense(utokm_ref, um_ref, 1, row0=7)

    @pl.when(step >= WEIGHT_CHUNKS)
    def _():
        h1 = _swiglu_half_step(x_ref[...], n1w_ref, wg_ref, wu_ref, wd_ref, acc_ref)
        h1_ref[...] = h1
        n = mix_norm(h1)
        split_octets(in_proj(n, 4), utok_ref)
        _to_group_dense(utok_ref, u_ref, u_ref.shape[1])
        g = in_proj(n, 3)
        g_ref[...] = g * jax.nn.sigmoid(g) * rnw_ref[...]
        cos = cos_ref[...]
        sin = sin_ref[...]
        rope_heads(in_proj(n, 0), cos, sin, None, q_ref)
        rope_heads(in_proj(n, 1), cos, sin, k_scale, k_ref)
        v_ref[...] = in_proj(n, 2).astype(BF16)


def _resident(shape):
    nd = len(shape)
    return pl.BlockSpec(shape, lambda *_: (0,) * nd, pipeline_mode=pl.Buffered(1))


def _ffn_inproj(x2, xm, cos, sin, cos_m, sin_m, n1w, wg, wu, wd, n2w, w_in, ret_norm_w, tm):
    rows = x2.shape[0]
    pos_blocks = cos.shape[0] // tm
    row_blk = lambda i: (_tile_index(i), 0)
    pos_blk = lambda i: (_tile_index(i) % pos_blocks, 0)
    out_shape = (
        jax.ShapeDtypeStruct((rows, D_MODEL), F32),
        jax.ShapeDtypeStruct((rows, RET_WIDTH), BF16),
        jax.ShapeDtypeStruct((rows, RET_WIDTH), BF16),
        jax.ShapeDtypeStruct((rows, RET_WIDTH), BF16),
        jax.ShapeDtypeStruct((rows, RET_WIDTH), F32),
        jax.ShapeDtypeStruct((SSM_GROUPS, rows // SUB, GROUP_LANES), F32),
        jax.ShapeDtypeStruct((N_META, RET_WIDTH), BF16),
        jax.ShapeDtypeStruct((N_META, RET_WIDTH), BF16),
        jax.ShapeDtypeStruct((SSM_GROUPS, 8, GROUP_LANES), F32),
    )
    return pl.pallas_call(
        _ffn_inproj_kernel,
        grid=(WEIGHT_CHUNKS + rows // tm,),
        in_specs=[
            pl.BlockSpec((tm, D_MODEL), row_blk),
            pl.BlockSpec((tm, HEAD_DIM), pos_blk),
            pl.BlockSpec((tm, HEAD_DIM), pos_blk),
            _resident((N_META, D_MODEL)),
            _resident((N_META, HEAD_DIM)),
            _resident((N_META, HEAD_DIM)),
            _resident((1, D_MODEL)),
            _weight_chunk_spec((D_MODEL, D_FF)),
            _weight_chunk_spec((D_MODEL, D_FF)),
            _weight_chunk_spec((D_FF, D_MODEL)),
            _resident((1, D_MODEL)),
            _weight_chunk_spec((D_MODEL, IN_PROJ)),
            _resident((1, RET_WIDTH)),
        ],
        out_specs=(
            pl.BlockSpec((tm, D_MODEL), row_blk),
            pl.BlockSpec((tm, RET_WIDTH), row_blk),
            pl.BlockSpec((tm, RET_WIDTH), row_blk),
            pl.BlockSpec((tm, RET_WIDTH), row_blk),
            pl.BlockSpec((tm, RET_WIDTH), row_blk),
            pl.BlockSpec((SSM_GROUPS, tm // SUB, GROUP_LANES), lambda i: (0, _tile_index(i), 0)),
            pl.BlockSpec((N_META, RET_WIDTH), lambda i: (0, 0)),
            pl.BlockSpec((N_META, RET_WIDTH), lambda i: (0, 0)),
            pl.BlockSpec((SSM_GROUPS, 8, GROUP_LANES), lambda i: (0, 0, 0)),
        ),
        out_shape=out_shape,
        scratch_shapes=[
            pltpu.VMEM((D_MODEL, D_FF), BF16),
            pltpu.VMEM((D_MODEL, D_FF), BF16),
            pltpu.VMEM((D_FF, D_MODEL), BF16),
            pltpu.VMEM((D_MODEL, IN_PROJ), BF16),
            pltpu.VMEM((tm, D_MODEL), F32),
            pltpu.VMEM((N_META, D_MODEL), F32),
            pltpu.VMEM((OCTETS, tm, LANES), F32),
            pltpu.VMEM((OCTETS, N_META, LANES), F32),
        ],
        compiler_params=pltpu.CompilerParams(
            dimension_semantics=("arbitrary",), vmem_limit_bytes=VMEM_LIMIT),
        name="ffn_inproj",
    )(x2, cos, sin, xm, cos_m, sin_m, n1w, wg, wu, wd, n2w, w_in, ret_norm_w)


def _retention_kernel(q_ref, k_ref, v_ref, g_ref, km_ref, vm_ref, mask_ref, wq_ref, wk_ref, wm_ref,
                      gc_ref, o_ref, state_ref, *, rows):
    tn = (((0,), (0,)), ((), ()))
    nt = (((1,), (1,)), ((), ()))

    @pl.when(pl.program_id(1) == 0)
    def _():
        for h in range(RET_HEADS):
            hs = slice(h * HEAD_DIM, (h + 1) * HEAD_DIM)
            kw = (km_ref[:, hs].astype(F32) * wm_ref[h]).astype(BF16)
            state_ref[h] = lax.dot_general(kw, vm_ref[:, hs], tn, preferred_element_type=F32)

    def chunk(i, carry):
        r0 = pl.multiple_of(i * RET_BLOCK, RET_BLOCK)
        rs = pl.ds(r0, RET_BLOCK)
        for h in range(RET_HEADS):
            hs = slice(h * HEAD_DIM, (h + 1) * HEAD_DIM)
            qh = q_ref[rs, hs]
            kh = k_ref[rs, hs]
            vh = v_ref[rs, hs]
            st = state_ref[h]
            s = lax.dot_general(qh, kh, nt, preferred_element_type=F32) * mask_ref[h]
            o = (jnp.dot(s.astype(BF16), vh, preferred_element_type=F32)
                 + jnp.dot(qh, st.astype(BF16), preferred_element_type=F32) * wq_ref[h])
            kw = (kh.astype(F32) * wk_ref[h]).astype(BF16)
            state_ref[h] = gc_ref[h] * st + lax.dot_general(kw, vh, tn, preferred_element_type=F32)
            mu = jnp.mean(o, axis=-1, keepdims=True)
            d = o - mu
            var = jnp.mean(d * d, axis=-1, keepdims=True)
            o_ref[rs, hs] = (g_ref[rs, hs] * (d * lax.rsqrt(var + EPS))).astype(BF16)
        return carry

    lax.fori_loop(0, rows // RET_BLOCK, chunk, 0, unroll=2)


def _retention_tables():
    log_g = np.log(1.0 - 2.0 ** (-5.0 - np.arange(RET_HEADS, dtype=np.float64)))
    i = np.arange(RET_BLOCK)
    diff = i[:, None] - i[None, :]
    mask = np.where(diff[None] >= 0, np.exp(log_g[:, None, None] * np.maximum(diff, 0)[None]), 0.0)
    pos = np.arange(RET_BLOCK, dtype=np.float64)
    full = lambda w: np.broadcast_to(w[:, :, None], w.shape + (HEAD_DIM,))
    w_q = full(np.exp(log_g[:, None] * (pos + 1.0)[None]))
    w_k = full(np.exp(log_g[:, None] * (RET_BLOCK - 1 - pos)[None]))
    w_m = full(np.exp(log_g[:, None] * (N_META - 1 - np.arange(N_META, dtype=np.float64))[None]))
    g_c = np.broadcast_to(np.exp(log_g * RET_BLOCK)[:, None, None], (RET_HEADS, HEAD_DIM, HEAD_DIM))
    return tuple(jnp.asarray(t, dtype=F32) for t in (mask, w_q, w_k, w_m, g_c))


def _retention(q, k, v, g, km, vm, bsz, seq, rows):
    mask, w_q, w_k, w_m, g_c = _retention_tables()
    steps = seq // rows
    blk = lambda b, c: (b * steps + c, 0)
    full3 = lambda a: pl.BlockSpec(a.shape, lambda b, c: (0, 0, 0))
    full2 = lambda a: pl.BlockSpec(a.shape, lambda b, c: (0, 0))
    return pl.pallas_call(
        functools.partial(_retention_kernel, rows=rows),
        grid=(bsz, steps),
        in_specs=[pl.BlockSpec((rows, RET_WIDTH), blk)] * 4
        + [full2(km), full2(vm), full3(mask), full3(w_q), full3(w_k), full3(w_m), full3(g_c)],
        out_specs=pl.BlockSpec((rows, RET_WIDTH), blk),
        out_shape=jax.ShapeDtypeStruct((bsz * seq, RET_WIDTH), BF16),
        scratch_shapes=[pltpu.VMEM((RET_HEADS, HEAD_DIM, HEAD_DIM), F32)],
        compiler_params=pltpu.CompilerParams(
            dimension_semantics=("arbitrary", "arbitrary"), vmem_limit_bytes=VMEM_LIMIT),
        name="retention",
    )(q, k, v, g, km, vm, mask, w_q, w_k, w_m, g_c)


def _shift_lanes_zero_fill(halves, shift, lane):
    h0, h1 = halves
    whole, s = divmod(shift, LANES)
    r0 = h0 if s == 0 else pltpu.roll(h0, s, axis=1)
    zero = jnp.zeros_like(h0)
    if whole == 1:
        return zero, jnp.where(lane >= s, r0, zero)
    r1 = h1 if s == 0 else pltpu.roll(h1, s, axis=1)
    return jnp.where(lane >= s, r0, zero), jnp.where(lane >= s, r1, r0)


def _s5_assemble_octet(o, lre_ref, lim_ref, ldt_ref, btr_ref, bti_ref, ctr_ref, cti_ref,
                       toep_ref, win_ref, wout_ref, cpw_ref, pow_ref, xall_ref, wt_ref):
    lre = lre_ref[o]
    lim = lim_ref[o]
    dt = jnp.exp(ldt_ref[o])
    n_pow = pow_ref.shape[1]
    ell = lax.broadcasted_iota(jnp.int32, (n_pow, OCT_STATE), 0).astype(F32)
    mag = jnp.exp(ell * (lre * dt))
    ang = ell * (lim * dt)
    pow_ref[0] = mag * jnp.cos(ang)
    pow_ref[1] = mag * jnp.sin(ang)
    ell = ((lax.broadcasted_iota(jnp.int32, (8, OCT_STATE), 0) + 1) * SUB).astype(F32)
    mag = jnp.exp(ell * (lre * dt))
    ang = ell * (lim * dt)
    cp_re = mag * jnp.cos(ang)
    cp_im = mag * jnp.sin(ang)
    a_re = pow_ref[0, 1:2, :]
    a_im = pow_ref[1, 1:2, :]
    den = lre * lre + lim * lim
    num_re = a_re - 1.0
    coef_re = (num_re * lre + a_im * lim) / den
    coef_im = (a_im * lre - num_re * lim) / den
    btr = btr_ref[o]
    bti = bti_ref[o]
    bbar_re = coef_re * btr - coef_im * bti
    bbar_im = coef_re * bti + coef_im * btr
    ctr = ctr_ref[o]
    cti = cti_ref[o]

    pairs_per_octet = GROUPS_PER_OCTET // 2
    pair_lane_group = _lane_block((SSM_GROUP, PAIR_STATE), SSM_STATE)
    for l in range(SUB):
        pr = pow_ref[0, l:l + 1, :]
        pi = pow_ref[1, l:l + 1, :]
        x_re = pr * bbar_re - pi * bbar_im
        x_im = pr * bbar_im + pi * bbar_re
        rows = slice(l * SSM_GROUP, (l + 1) * SSM_GROUP)
        xall_ref[0, rows, :] = x_re
        xall_ref[1, rows, :] = x_im
        pr1 = pow_ref[0, l + 1:l + 2, :]
        pi1 = pow_ref[1, l + 1:l + 2, :]
        w_re = pr1 * ctr - pi1 * cti
        w_im = pr1 * cti + pi1 * ctr
        j = SUB - 1 - l
        for pp in range(pairs_per_octet):
            sl = slice(pp * PAIR_STATE, (pp + 1) * PAIR_STATE)
            pair = o * pairs_per_octet + pp
            for gi in range(2):
                keep = pair_lane_group == gi
                in_rows = slice(gi * GROUP_LANES + j * SSM_GROUP, gi * GROUP_LANES + (j + 1) * SSM_GROUP)
                out_rows = slice(gi * GROUP_LANES + l * SSM_GROUP, gi * GROUP_LANES + (l + 1) * SSM_GROUP)
                for part, x, w in ((0, x_re, w_re), (1, x_im, -w_im)):
                    cols = slice(part * PAIR_STATE, (part + 1) * PAIR_STATE)
                    win_ref[pair, in_rows, cols] = jnp.where(keep, x[:, sl], 0.0).astype(BF16)
                    wt_ref[pp, out_rows, cols] = jnp.where(keep, w[:, sl], 0.0)
    for pp in range(pairs_per_octet):
        pair = o * pairs_per_octet + pp
        wout_ref[pair] = wt_ref[pp].T.astype(BF16)
        sl = slice(pp * PAIR_STATE, (pp + 1) * PAIR_STATE)
        cpw_ref[pair, 0] = cp_re[:, sl]
        cpw_ref[pair, 1] = cp_im[:, sl]

    row = lax.broadcasted_iota(jnp.int32, (LANES, OCT_STATE), 0)
    col = lax.broadcasted_iota(jnp.int32, (LANES, OCT_STATE), 1)
    same_group = (row // SSM_GROUP) == (col // SSM_STATE)

    def block_diag_t(x):
        return jnp.where(same_group, jnp.concatenate([x] * GROUPS_PER_OCTET, axis=0), 0.0).T

    k_all = (jnp.dot(xall_ref[0], block_diag_t(ctr), precision=lax.Precision.HIGHEST,
                     preferred_element_type=F32)
             - jnp.dot(xall_ref[1], block_diag_t(cti), precision=lax.Precision.HIGHEST,
                       preferred_element_type=F32))
    per_half = LANES // SSM_GROUP
    blk = _lane_block((SSM_GROUP, LANES), SSM_GROUP)
    lane = lax.broadcasted_iota(jnp.int32, (SSM_GROUP, LANES), 1)
    for g8 in range(GROUPS_PER_OCTET):
        strip = []
        for hh in range(GROUP_LANES // LANES):
            acc = jnp.zeros((SSM_GROUP, LANES), F32)
            for ll in range(per_half):
                lag = hh * per_half + ll
                k_lag = k_all[lag * SSM_GROUP:(lag + 1) * SSM_GROUP, :]
                shift = ((ll - g8) * SSM_GROUP) % LANES
                rolled = k_lag if shift == 0 else pltpu.roll(k_lag, shift, axis=1)
                acc = jnp.where(blk == ll, rolled, acc)
            strip.append(acc)
        for j in range(SUB):
            h0, h1 = _shift_lanes_zero_fill(strip, j * SSM_GROUP, lane)
            rows = slice(j * SSM_GROUP, (j + 1) * SSM_GROUP)
            toep_ref[o * GROUPS_PER_OCTET + g8, rows, 0:LANES] = h0.astype(BF16)
            toep_ref[o * GROUPS_PER_OCTET + g8, rows, LANES:2 * LANES] = h1.astype(BF16)


def _s5_state_scan(st_ref, zb_ref, cpow_ref, nsc):
    tiles = nsc // 8
    width = st_ref.shape[1] // 2
    re = slice(0, width)
    im = slice(width, 2 * width)
    rowmod = lax.broadcasted_iota(jnp.int32, (8, width), 0)
    src, dst = st_ref, zb_ref
    for s in (1, 2, 4):
        keep = rowmod >= s
        c_re = jnp.concatenate([jnp.where(keep, cpow_ref[0, s - 1:s, :], 0.0)] * tiles, axis=0)
        c_im = jnp.concatenate([jnp.where(keep, cpow_ref[1, s - 1:s, :], 0.0)] * tiles, axis=0)
        sh_re = src[pl.ds(8 - s, nsc), re]
        sh_im = src[pl.ds(8 - s, nsc), im]
        dst[8:8 + nsc, re] = src[8:8 + nsc, re] + c_re * sh_re - c_im * sh_im
        dst[8:8 + nsc, im] = src[8:8 + nsc, im] + c_re * sh_im + c_im * sh_re
        src, dst = dst, src
    ca_re = cpow_ref[0]
    ca_im = cpow_ref[1]
    c_re = st_ref[7:8, re]
    c_im = st_ref[7:8, im]
    for t in range(tiles):
        rows = slice(8 + 8 * t, 16 + 8 * t)
        cb_re = jnp.broadcast_to(c_re, (8, width))
        cb_im = jnp.broadcast_to(c_im, (8, width))
        p_re = src[rows, re] + ca_re * cb_re - ca_im * cb_im
        p_im = src[rows, im] + ca_re * cb_im + ca_im * cb_re
        st_ref[rows, re] = p_re
        st_ref[rows, im] = p_im
        c_re = p_re[7:8]
        c_im = p_im[7:8]


def _s5_kernel(u_ref, um_ref, lre_ref, lim_ref, ldt_ref, btr_ref, bti_ref, ctr_ref, cti_ref, d_ref,
               y_ref, toep_ref, win_ref, wout_ref, cpw_ref, pow_ref, xall_ref, wt_ref, lhs_ref,
               st_ref, zb_ref, carry_ref, *, nsc):
    first_call_step = (pl.program_id(0) == 0) & (pl.program_id(1) == 0)
    r = pl.program_id(1)

    @pl.when(first_call_step)
    def _():
        zb_ref[:, 0:8, :] = jnp.zeros((zb_ref.shape[0], 8, zb_ref.shape[2]), F32)

        def octet(o, carry):
            _s5_assemble_octet(o, lre_ref, lim_ref, ldt_ref, btr_ref, bti_ref, ctr_ref, cti_ref,
                               toep_ref, win_ref, wout_ref, cpw_ref, pow_ref, xall_ref, wt_ref)
            return carry
        lax.fori_loop(0, OCTETS, octet, 0)

    @pl.when(r == 0)
    def _():
        def meta_state(pair, carry):
            meta = jnp.concatenate([um_ref[2 * pair], um_ref[2 * pair + 1]], axis=1).astype(BF16)
            carry_ref[pair] = jnp.dot(meta, win_ref[pair], preferred_element_type=F32)
            return carry
        lax.fori_loop(0, PAIRS, meta_state, 0)

    lo = slice(0, GROUP_LANES)
    hi = slice(GROUP_LANES, 2 * GROUP_LANES)
    slots = st_ref.shape[0]

    def pair_block(k, carry):
        pairs = [slots * k + s for s in range(slots)]
        for s, pair in enumerate(pairs):
            lhs_ref[s, :, lo] = u_ref[2 * pair].astype(BF16)
            lhs_ref[s, :, hi] = u_ref[2 * pair + 1].astype(BF16)
            st_ref[s, 0:8, :] = carry_ref[pair]
            st_ref[s, 8:8 + nsc, :] = jnp.dot(lhs_ref[s], win_ref[pair], preferred_element_type=F32)
        for s, pair in enumerate(pairs):
            _s5_state_scan(st_ref.at[s], zb_ref.at[s], cpw_ref.at[pair], nsc)
            carry_ref[pair] = st_ref[s, nsc:nsc + 8, :]
        for s, pair in enumerate(pairs):
            s_in = st_ref[s, pl.ds(7, nsc), :].astype(BF16)
            y_state = jnp.dot(s_in, wout_ref[pair], preferred_element_type=F32)
            d = d_ref[pair]
            for gi, cols in ((0, lo), (1, hi)):
                g = 2 * pair + gi
                y = (jnp.dot(lhs_ref[s, :, cols], toep_ref[g], preferred_element_type=F32)
                     + y_state[:, cols] + d[:, cols] * u_ref[g])
                y_ref[g] = y
        return carry

    lax.fori_loop(0, PAIRS // slots, pair_block, 0)


def _s5_param_rows(lam_re, lam_im, log_dt, b_re, b_im, c_re, c_im, d):
    go = GROUPS_PER_OCTET
    lanes = lambda a: a.reshape(OCTETS, 1, OCT_STATE)
    ldt = jnp.broadcast_to(log_dt[:, None], (SSM_GROUPS, SSM_STATE))
    bt = lambda a: jnp.transpose(a.reshape(OCTETS, go, SSM_STATE, SSM_GROUP),
                                 (0, 3, 1, 2)).reshape(OCTETS, SSM_GROUP, OCT_STATE)
    ct = lambda a: jnp.transpose(a.reshape(OCTETS, go, SSM_GROUP, SSM_STATE),
                                 (0, 2, 1, 3)).reshape(OCTETS, SSM_GROUP, OCT_STATE)
    d_pairs = jnp.broadcast_to(d.reshape(PAIRS, 2, 1, SSM_GROUP),
                               (PAIRS, 2, SUB, SSM_GROUP)).reshape(PAIRS, 1, 2 * GROUP_LANES)
    return (lanes(lam_re), lanes(lam_im), lanes(ldt), bt(b_re), bt(b_im), ct(c_re), ct(c_im), d_pairs)


def _s5(u, um, params, bsz, seq, rows):
    nsc = rows // SUB
    steps = seq // rows
    n_pow = SUB + 8
    full = lambda a: pl.BlockSpec(a.shape, lambda b, r: (0,) * a.ndim)
    tok_blk = pl.BlockSpec((SSM_GROUPS, nsc, GROUP_LANES), lambda b, r: (0, b * steps + r, 0))
    return pl.pallas_call(
        functools.partial(_s5_kernel, nsc=nsc),
        grid=(bsz, steps),
        in_specs=[tok_blk, full(um)] + [full(p) for p in params],
        out_specs=tok_blk,
        out_shape=jax.ShapeDtypeStruct(u.shape, F32),
        scratch_shapes=[
            pltpu.VMEM((SSM_GROUPS, GROUP_LANES, GROUP_LANES), BF16),
            pltpu.VMEM((PAIRS, 2 * GROUP_LANES, 2 * PAIR_STATE), BF16),
            pltpu.VMEM((PAIRS, 2 * PAIR_STATE, 2 * GROUP_LANES), BF16),
            pltpu.VMEM((PAIRS, 2, 8, PAIR_STATE), F32),
            pltpu.VMEM((2, n_pow, OCT_STATE), F32),
            pltpu.VMEM((2, SUB * SSM_GROUP, OCT_STATE), F32),
            pltpu.VMEM((GROUPS_PER_OCTET // 2, 2 * GROUP_LANES, 2 * PAIR_STATE), F32),
            pltpu.VMEM((S5_SLOTS, nsc, 2 * GROUP_LANES), BF16),
            pltpu.VMEM((S5_SLOTS, nsc + 8, 2 * PAIR_STATE), F32),
            pltpu.VMEM((S5_SLOTS, nsc + 8, 2 * PAIR_STATE), F32),
            pltpu.VMEM((PAIRS, 8, 2 * PAIR_STATE), F32),
        ],
        compiler_params=pltpu.CompilerParams(
            dimension_semantics=("arbitrary", "arbitrary"), vmem_limit_bytes=VMEM_LIMIT),
        name="s5_mixer",
    )(u, um, *params)


def _mix_ffn_kernel(h1_ref, ret_ref, ynext_ref, gw32_ref, gb_ref, snw_ref, wo32_ref, n3w_ref,
                    wg32_ref, wu32_ref, wd32_ref, fnw_ref, out_ref,
                    gw_ref, wo_ref, wg_ref, wu_ref, wd_ref, acc_ref, ytok_ref):
    step = pl.program_id(0)
    nsub = ynext_ref.shape[1]

    @pl.when(step < WEIGHT_CHUNKS)
    def _():
        _cast_weight_chunks(step, ((gw32_ref, gw_ref), (wo32_ref, wo_ref), (wg32_ref, wg_ref),
                                   (wu32_ref, wu_ref), (wd32_ref, wd_ref)))

    @pl.when(step == WEIGHT_CHUNKS - 1)
    def _():
        _to_token_major(ynext_ref, ytok_ref.at[0], nsub, jax.nn.gelu)

    @pl.when(step >= WEIGHT_CHUNKS)
    def _():
        slot = lax.rem(step - WEIGHT_CHUNKS, 2)
        y = _token_major_rows(ytok_ref.at[slot], nsub)
        _to_token_major(ynext_ref, ytok_ref.at[1 - slot], nsub, jax.nn.gelu)
        z = y * jax.nn.sigmoid(jnp.dot(y.astype(BF16), gw_ref[...], preferred_element_type=F32)
                               + gb_ref[...])
        ssm = _rms(z, snw_ref[...]).astype(BF16)
        mixed = (jnp.dot(ret_ref[...], wo_ref[0:RET_WIDTH, :], preferred_element_type=F32)
                 + jnp.dot(ssm, wo_ref[RET_WIDTH:, :], preferred_element_type=F32))
        h2 = h1_ref[...] + mixed
        h3 = _swiglu_half_step(h2, n3w_ref, wg_ref, wu_ref, wd_ref, acc_ref)
        out_ref[...] = _rms(h3, fnw_ref[...])


def _mix_ffn(h1, ret, y, glu_w, glu_b, ssm_norm_w, w_out, n3w, wg, wu, wd, fnw, tm):
    rows = h1.shape[0]
    tiles = rows // tm
    row_blk = lambda i: (_tile_index(i), 0)
    next_blk = lambda i: (0, jnp.minimum(_tile_index(i + 1), tiles - 1), 0)
    return pl.pallas_call(
        _mix_ffn_kernel,
        grid=(WEIGHT_CHUNKS + tiles,),
        in_specs=[
            pl.BlockSpec((tm, D_MODEL), row_blk),
            pl.BlockSpec((tm, RET_WIDTH), row_blk),
            pl.BlockSpec((SSM_GROUPS, tm // SUB, GROUP_LANES), next_blk),
            _weight_chunk_spec((SSM_WIDTH, SSM_WIDTH)),
            _resident((1, SSM_WIDTH)),
            _resident((1, SSM_WIDTH)),
            _weight_chunk_spec((D_MODEL, D_MODEL)),
            _resident((1, D_MODEL)),
            _weight_chunk_spec((D_MODEL, D_FF)),
            _weight_chunk_spec((D_MODEL, D_FF)),
            _weight_chunk_spec((D_FF, D_MODEL)),
            _resident((1, D_MODEL)),
        ],
        out_specs=pl.BlockSpec((tm, D_MODEL), row_blk),
        out_shape=jax.ShapeDtypeStruct((rows, D_MODEL), F32),
        scratch_shapes=[
            pltpu.VMEM((SSM_WIDTH, SSM_WIDTH), BF16),
            pltpu.VMEM((D_MODEL, D_MODEL), BF16),
            pltpu.VMEM((D_MODEL, D_FF), BF16),
            pltpu.VMEM((D_MODEL, D_FF), BF16),
            pltpu.VMEM((D_FF, D_MODEL), BF16),
            pltpu.VMEM((tm, D_MODEL), F32),
            pltpu.VMEM((2, OCTETS, tm // SUB * TOK_PITCH, LANES), F32),
        ],
        compiler_params=pltpu.CompilerParams(
            dimension_semantics=("arbitrary",), vmem_limit_bytes=VMEM_LIMIT),
        name="mix_ffn",
    )(h1, ret, y, glu_w, glu_b, ssm_norm_w, w_out, n3w, wg, wu, wd, fnw)


def _rope_tables(n_pos):
    freqs = 1.0 / (ROPE_BASE ** (np.arange(0, HEAD_DIM, 2, dtype=np.float64) / HEAD_DIM))
    ang = np.arange(n_pos, dtype=np.float64)[:, None] * freqs[None, :]
    cos = np.cos(ang)
    sin = np.sin(ang)
    cos = np.concatenate([cos, cos], axis=-1).astype(np.float32)
    sin = np.concatenate([-sin, sin], axis=-1).astype(np.float32)
    split = lambda t: (jnp.asarray(t[:N_META]), jnp.asarray(t[N_META:]))
    return split(cos), split(sin)


def kernel(x, meta_tokens, ffn1_norm_w, ffn1_w_gate, ffn1_w_up, ffn1_w_down, mix_norm_w, w_in,
           ret_norm_w, ssm_lambda_re, ssm_lambda_im, ssm_log_dt, ssm_b_re, ssm_b_im, ssm_c_re,
           ssm_c_im, ssm_d, ssm_glu_w, ssm_glu_b, ssm_norm_w, w_out, ffn2_norm_w, ffn2_w_gate,
           ffn2_w_up, ffn2_w_down, final_norm_w):
    bsz, seq, _ = x.shape
    assert ffn1_norm_w.shape[0] == 1, "single layer only"
    tm = 512
    ret_rows = 2048
    s5_rows = 4096
    assert seq % tm == 0 and seq % ret_rows == 0 and seq % s5_rows == 0

    row = lambda a: a.reshape(1, -1)
    l = 0
    (cos_m, cos), (sin_m, sin) = _rope_tables(N_META + seq)

    h1, q, k, v, g, u, km, vm, um = _ffn_inproj(
        x.reshape(bsz * seq, D_MODEL), meta_tokens, cos, sin, cos_m, sin_m,
        row(ffn1_norm_w[l]), ffn1_w_gate[l], ffn1_w_up[l], ffn1_w_down[l],
        row(mix_norm_w[l]), w_in[l], row(ret_norm_w[l]), tm=tm)

    ret = _retention(q, k, v, g, km, vm, bsz, seq, ret_rows)

    params = _s5_param_rows(ssm_lambda_re[l], ssm_lambda_im[l], ssm_log_dt[l], ssm_b_re[l],
                            ssm_b_im[l], ssm_c_re[l], ssm_c_im[l], ssm_d[l])
    y = _s5(u, um, params, bsz, seq, s5_rows)

    out = _mix_ffn(h1, ret, y, ssm_glu_w[l], row(ssm_glu_b[l]), row(ssm_norm_w[l]),
                   w_out[l], row(ffn2_norm_w[l]), ffn2_w_gate[l], ffn2_w_up[l],
                   ffn2_w_down[l], row(final_norm_w), tm=tm)
    return out.reshape(bsz, seq, D_MODEL)
```

```python
import functools
import math

import jax
import jax.numpy as jnp
import numpy as np
from jax import lax
from jax.experimental import pallas as pl
from jax.experimental.pallas import tpu as pltpu

D_MODEL = 1024
N_META = 16
RET_HEADS = 4
HEAD_DIM = 128
RET_WIDTH = RET_HEADS * HEAD_DIM
SSM_WIDTH = 512
SSM_GROUP = 16
SSM_GROUPS = SSM_WIDTH // SSM_GROUP
SSM_STATE = 64
RET_BLOCK = 256
D_FF = 2816
FFN_RES = 0.5
ROPE_BASE = 10000.0
EPS = 1e-6
IN_PROJ = 4 * RET_WIDTH + SSM_WIDTH

LANES = 128
MXU_DIM = 256
OCTETS = SSM_WIDTH // LANES
GROUPS_PER_OCTET = LANES // SSM_GROUP
SUB = N_META
OCT_STATE = GROUPS_PER_OCTET * SSM_STATE
GROUP_LANES = SUB * SSM_GROUP
PAIRS = SSM_GROUPS // 2
PAIR_STATE = 2 * SSM_STATE
S5_SLOTS = 4
TOK_PITCH = 24
FF_CHUNK = 256
WEIGHT_CHUNKS = 8
VMEM_LIMIT = 56 * 1024 * 1024

F32 = jnp.float32
BF16 = jnp.bfloat16


def _rms(x, w):
    return x * lax.rsqrt(jnp.mean(x * x, axis=-1, keepdims=True) + EPS) * w


def _swiglu_half_step(h, nw_ref, wg_ref, wu_ref, wd_ref, acc_ref):
    n = _rms(h, nw_ref[...]).astype(BF16)
    for c in range(D_FF // FF_CHUNK):
        sl = slice(c * FF_CHUNK, (c + 1) * FF_CHUNK)
        g = jnp.dot(n, wg_ref[:, sl], preferred_element_type=F32)
        u = jnp.dot(n, wu_ref[:, sl], preferred_element_type=F32)
        a = (g * jax.nn.sigmoid(g) * u).astype(BF16)
        p = jnp.dot(a, wd_ref[sl, :], preferred_element_type=F32)
        if c == 0:
            acc_ref[...] = p
        else:
            acc_ref[...] += p
    return h + FFN_RES * acc_ref[...]


def _rope(x, cos, sin_signed):
    return x * cos + pltpu.roll(x, HEAD_DIM // 2, axis=1) * sin_signed


def _lane_block(shape, width):
    return lax.broadcasted_iota(jnp.int32, shape, 1) // width


def _transpose_lane_blocks(v):
    n = LANES // SSM_GROUP
    assert len(v) == n == GROUPS_PER_OCTET
    blk = _lane_block(v[0].shape, SSM_GROUP)
    d = n // 2
    while d:
        low = (blk & d) == 0
        nxt = list(v)
        for a in range(n):
            if a & d == 0:
                nxt[a] = jnp.where(low, v[a], pltpu.roll(v[a + d], d * SSM_GROUP, axis=1))
                nxt[a + d] = jnp.where(low, pltpu.roll(v[a], LANES - d * SSM_GROUP, axis=1), v[a + d])
        v = nxt
        d //= 2
    return v


def _to_group_dense(tok_ref, out_ref, nsub, row0=0):
    per_half = LANES // SSM_GROUP
    for o in range(OCTETS):
        for hh in range(GROUP_LANES // LANES):
            slabs = [tok_ref[o, pl.ds(hh * per_half + jj, nsub, stride=SUB), :]
                     for jj in range(per_half)]
            for g8, rows in enumerate(_transpose_lane_blocks(slabs)):
                out_ref[o * GROUPS_PER_OCTET + g8, row0:row0 + nsub,
                        hh * LANES:(hh + 1) * LANES] = rows


def _to_token_major(gd_ref, tok_ref, nsub, fn):
    per_half = LANES // SSM_GROUP
    for o in range(OCTETS):
        for hh in range(GROUP_LANES // LANES):
            rows = [gd_ref[o * GROUPS_PER_OCTET + g8, :, hh * LANES:(hh + 1) * LANES]
                    for g8 in range(GROUPS_PER_OCTET)]
            for jj, slab in enumerate(_transpose_lane_blocks(rows)):
                tok_ref[o, pl.ds(hh * per_half + jj, nsub, stride=TOK_PITCH), :] = fn(slab)


def _token_major_rows(tok_ref, nsub):
    return jnp.concatenate(
        [jnp.concatenate([tok_ref[o, c * TOK_PITCH:c * TOK_PITCH + SUB, :] for c in range(nsub)],
                         axis=0) for o in range(OCTETS)], axis=-1)


def _cast_weight_chunks(step, pairs):
    for src_ref, dst_ref in pairs:
        rows = src_ref.shape[0]
        r0 = pl.multiple_of(step * rows, rows)
        dst_ref[pl.ds(r0, rows), :] = src_ref[...].astype(BF16)


def _weight_chunk_spec(shape):
    rows = shape[0] // WEIGHT_CHUNKS
    assert rows * WEIGHT_CHUNKS == shape[0] and rows % 16 == 0, shape
    return pl.BlockSpec((rows, shape[1]), lambda i: (jnp.minimum(i, WEIGHT_CHUNKS - 1), 0))


def _tile_index(i):
    return jnp.maximum(i - WEIGHT_CHUNKS, 0)


def _ffn_inproj_kernel(x_ref, cos_ref, sin_ref, xm_ref, cosm_ref, sinm_ref, n1w_ref, wg32_ref,
                       wu32_ref, wd32_ref, n2w_ref, win32_ref,
                       h1_ref, q_ref, k_ref, v_ref, g_ref, u_ref, km_ref, vm_ref, um_ref,
                       wg_ref, wu_ref, wd_ref, win_ref, acc_ref, accm_ref, utok_ref, utokm_ref):
    step = pl.program_id(0)
    k_scale = HEAD_DIM ** -0.5

    def mix_norm(h1):
        return _rms(h1, n2w_ref[...]).astype(BF16)

    def in_proj(n, part):
        return jnp.dot(n, win_ref[:, part * RET_WIDTH:(part + 1) * RET_WIDTH],
                       preferred_element_type=F32)

    def rope_heads(p, cos, sin, scale, out_ref):
        for h in range(RET_HEADS):
            hs = slice(h * HEAD_DIM, (h + 1) * HEAD_DIM)
            y = _rope(p[:, hs], cos, sin)
            out_ref[:, hs] = (y if scale is None else y * scale).astype(BF16)

    def split_octets(p, tok_ref):
        for o in range(OCTETS):
            tok_ref[o] = p[:, o * LANES:(o + 1) * LANES]

    @pl.when(step < WEIGHT_CHUNKS)
    def _():
        _cast_weight_chunks(step, ((wg32_ref, wg_ref), (wu32_ref, wu_ref), (wd32_ref, wd_ref),
                                   (win32_ref, win_ref)))

    @pl.when(step == WEIGHT_CHUNKS)
    def _():
        h1 = _swiglu_half_step(xm_ref[...], n1w_ref, wg_ref, wu_ref, wd_ref, accm_ref)
        n = mix_norm(h1)
        rope_heads(in_proj(n, 1), cosm_ref[...], sinm_ref[...], k_scale, km_ref)
        vm_ref[...] = in_proj(n, 2).astype(BF16)
        split_octets(in_proj(n, 4), utokm_ref)
        um_ref[...] = jnp.zeros(um_ref.shape, F32)
        _to_group_dense(utokm_ref, um_ref, 1, row0=7)

    @pl.when(step >= WEIGHT_CHUNKS)
    def _():
        h1 = _swiglu_half_step(x_ref[...], n1w_ref, wg_ref, wu_ref, wd_ref, acc_ref)
        h1_ref[...] = h1
        n = mix_norm(h1)
        split_octets(in_proj(n, 4), utok_ref)
        _to_group_dense(utok_ref, u_ref, u_ref.shape[1])
        cos = cos_ref[...]
        sin = sin_ref[...]
        rope_heads(in_proj(n, 0), cos, sin, None, q_ref)
        rope_heads(in_proj(n, 1), cos, sin, k_scale, k_ref)
        v_ref[...] = in_proj(n, 2).astype(BF16)
        g_ref[...] = in_proj(n, 3)


def _resident(shape):
    nd = len(shape)
    return pl.BlockSpec(shape, lambda *_: (0,) * nd, pipeline_mode=pl.Buffered(1))


def _ffn_inproj(x2, xm, cos, sin, cos_m, sin_m, n1w, wg, wu, wd, n2w, w_in, tm):
    rows = x2.shape[0]
    pos_blocks = cos.shape[0] // tm
    row_blk = lambda i: (_tile_index(i), 0)
    pos_blk = lambda i: (_tile_index(i) % pos_blocks, 0)
    out_shape = (
        jax.ShapeDtypeStruct((rows, D_MODEL), F32),
        jax.ShapeDtypeStruct((rows, RET_WIDTH), BF16),
        jax.ShapeDtypeStruct((rows, RET_WIDTH), BF16),
        jax.ShapeDtypeStruct((rows, RET_WIDTH), BF16),
        jax.ShapeDtypeStruct((rows, RET_WIDTH), F32),
        jax.ShapeDtypeStruct((SSM_GROUPS, rows // SUB, GROUP_LANES), F32),
        jax.ShapeDtypeStruct((N_META, RET_WIDTH), BF16),
        jax.ShapeDtypeStruct((N_META, RET_WIDTH), BF16),
        jax.ShapeDtypeStruct((SSM_GROUPS, 8, GROUP_LANES), F32),
    )
    return pl.pallas_call(
        _ffn_inproj_kernel,
        grid=(WEIGHT_CHUNKS + rows // tm,),
        in_specs=[
            pl.BlockSpec((tm, D_MODEL), row_blk),
            pl.BlockSpec((tm, HEAD_DIM), pos_blk),
            pl.BlockSpec((tm, HEAD_DIM), pos_blk),
            _resident((N_META, D_MODEL)),
            _resident((N_META, HEAD_DIM)),
            _resident((N_META, HEAD_DIM)),
            _resident((1, D_MODEL)),
            _weight_chunk_spec((D_MODEL, D_FF)),
            _weight_chunk_spec((D_MODEL, D_FF)),
            _weight_chunk_spec((D_FF, D_MODEL)),
            _resident((1, D_MODEL)),
            _weight_chunk_spec((D_MODEL, IN_PROJ)),
        ],
        out_specs=(
            pl.BlockSpec((tm, D_MODEL), row_blk),
            pl.BlockSpec((tm, RET_WIDTH), row_blk),
            pl.BlockSpec((tm, RET_WIDTH), row_blk),
            pl.BlockSpec((tm, RET_WIDTH), row_blk),
            pl.BlockSpec((tm, RET_WIDTH), row_blk),
            pl.BlockSpec((SSM_GROUPS, tm // SUB, GROUP_LANES), lambda i: (0, _tile_index(i), 0)),
            pl.BlockSpec((N_META, RET_WIDTH), lambda i: (0, 0)),
            pl.BlockSpec((N_META, RET_WIDTH), lambda i: (0, 0)),
            pl.BlockSpec((SSM_GROUPS, 8, GROUP_LANES), lambda i: (0, 0, 0)),
        ),
        out_shape=out_shape,
        scratch_shapes=[
            pltpu.VMEM((D_MODEL, D_FF), BF16),
            pltpu.VMEM((D_MODEL, D_FF), BF16),
            pltpu.VMEM((D_FF, D_MODEL), BF16),
            pltpu.VMEM((D_MODEL, IN_PROJ), BF16),
            pltpu.VMEM((tm, D_MODEL), F32),
            pltpu.VMEM((N_META, D_MODEL), F32),
            pltpu.VMEM((OCTETS, tm, LANES), F32),
            pltpu.VMEM((OCTETS, N_META, LANES), F32),
        ],
        compiler_params=pltpu.CompilerParams(
            dimension_semantics=("arbitrary",), vmem_limit_bytes=VMEM_LIMIT),
        name="ffn_inproj",
    )(x2, cos, sin, xm, cos_m, sin_m, n1w, wg, wu, wd, n2w, w_in)


def _retention_kernel(q_ref, k_ref, v_ref, g_ref, km_ref, vm_ref, mask_ref, wq_ref, wk_ref, wm_ref,
                      gc_ref, nw_ref, o_ref, state_ref, *, rows):
    tn = (((0,), (0,)), ((), ()))
    nt = (((1,), (1,)), ((), ()))

    @pl.when(pl.program_id(1) == 0)
    def _():
        for h in range(RET_HEADS):
            hs = slice(h * HEAD_DIM, (h + 1) * HEAD_DIM)
            kw = (km_ref[:, hs].astype(F32) * wm_ref[h]).astype(BF16)
            state_ref[h] = lax.dot_general(kw, vm_ref[:, hs], tn, preferred_element_type=F32)

    def chunk(i, carry):
        r0 = pl.multiple_of(i * RET_BLOCK, RET_BLOCK)
        rs = pl.ds(r0, RET_BLOCK)
        for h in range(RET_HEADS):
            hs = slice(h * HEAD_DIM, (h + 1) * HEAD_DIM)
            qh = q_ref[rs, hs]
            kh = k_ref[rs, hs]
            vh = v_ref[rs, hs]
            st = state_ref[h]
            s = lax.dot_general(qh, kh, nt, preferred_element_type=F32) * mask_ref[h]
            o = (jnp.dot(s.astype(BF16), vh, preferred_element_type=F32)
                 + jnp.dot(qh, st.astype(BF16), preferred_element_type=F32) * wq_ref[h])
            kw = (kh.astype(F32) * wk_ref[h]).astype(BF16)
            state_ref[h] = gc_ref[h] * st + lax.dot_general(kw, vh, tn, preferred_element_type=F32)
            mu = jnp.mean(o, axis=-1, keepdims=True)
            d = o - mu
            var = jnp.mean(d * d, axis=-1, keepdims=True)
            y = d * lax.rsqrt(var + EPS) * nw_ref[:, hs]
            gate = g_ref[rs, hs]
            o_ref[rs, hs] = (gate * jax.nn.sigmoid(gate) * y).astype(BF16)
        return carry

    lax.fori_loop(0, rows // RET_BLOCK, chunk, 0, unroll=4)


def _retention_tables():
    log_g = np.log(1.0 - 2.0 ** (-5.0 - np.arange(RET_HEADS, dtype=np.float64)))
    i = np.arange(RET_BLOCK)
    diff = i[:, None] - i[None, :]
    mask = np.where(diff[None] >= 0, np.exp(log_g[:, None, None] * np.maximum(diff, 0)[None]), 0.0)
    pos = np.arange(RET_BLOCK, dtype=np.float64)
    full = lambda w: np.broadcast_to(w[:, :, None], w.shape + (HEAD_DIM,))
    w_q = full(np.exp(log_g[:, None] * (pos + 1.0)[None]))
    w_k = full(np.exp(log_g[:, None] * (RET_BLOCK - 1 - pos)[None]))
    w_m = full(np.exp(log_g[:, None] * (N_META - 1 - np.arange(N_META, dtype=np.float64))[None]))
    g_c = np.broadcast_to(np.exp(log_g * RET_BLOCK)[:, None, None], (RET_HEADS, HEAD_DIM, HEAD_DIM))
    return tuple(jnp.asarray(t, dtype=F32) for t in (mask, w_q, w_k, w_m, g_c))


def _retention(q, k, v, g, km, vm, ret_norm_w, bsz, seq, rows):
    mask, w_q, w_k, w_m, g_c = _retention_tables()
    steps = seq // rows
    blk = lambda b, c: (b * steps + c, 0)
    full3 = lambda a: pl.BlockSpec(a.shape, lambda b, c: (0, 0, 0))
    full2 = lambda a: pl.BlockSpec(a.shape, lambda b, c: (0, 0))
    return pl.pallas_call(
        functools.partial(_retention_kernel, rows=rows),
        grid=(bsz, steps),
        in_specs=[pl.BlockSpec((rows, RET_WIDTH), blk)] * 4
        + [full2(km), full2(vm), full3(mask), full3(w_q), full3(w_k), full3(w_m), full3(g_c),
           full2(ret_norm_w)],
        out_specs=pl.BlockSpec((rows, RET_WIDTH), blk),
        out_shape=jax.ShapeDtypeStruct((bsz * seq, RET_WIDTH), BF16),
        scratch_shapes=[pltpu.VMEM((RET_HEADS, HEAD_DIM, HEAD_DIM), F32)],
        compiler_params=pltpu.CompilerParams(
            dimension_semantics=("arbitrary", "arbitrary"), vmem_limit_bytes=VMEM_LIMIT),
        name="retention",
    )(q, k, v, g, km, vm, mask, w_q, w_k, w_m, g_c, ret_norm_w)


def _shift_lanes_zero_fill(halves, shift, lane):
    h0, h1 = halves
    whole, s = divmod(shift, LANES)
    r0 = h0 if s == 0 else pltpu.roll(h0, s, axis=1)
    zero = jnp.zeros_like(h0)
    if whole == 1:
        return zero, jnp.where(lane >= s, r0, zero)
    r1 = h1 if s == 0 else pltpu.roll(h1, s, axis=1)
    return jnp.where(lane >= s, r0, zero), jnp.where(lane >= s, r1, r0)


def _s5_assemble_octet(o, lre_ref, lim_ref, ldt_ref, btr_ref, bti_ref, ctr_ref, cti_ref,
                       toep_ref, win_ref, wout_ref, cpw_ref, pow_ref, xall_ref, wt_ref):
    lre = lre_ref[o]
    lim = lim_ref[o]
    dt = jnp.exp(ldt_ref[o])
    n_pow = pow_ref.shape[1]
    ell = lax.broadcasted_iota(jnp.int32, (n_pow, OCT_STATE), 0).astype(F32)
    mag = jnp.exp(ell * (lre * dt))
    ang = ell * (lim * dt)
    pow_ref[0] = mag * jnp.cos(ang)
    pow_ref[1] = mag * jnp.sin(ang)
    ell = ((lax.broadcasted_iota(jnp.int32, (8, OCT_STATE), 0) + 1) * SUB).astype(F32)
    mag = jnp.exp(ell * (lre * dt))
    ang = ell * (lim * dt)
    cp_re = mag * jnp.cos(ang)
    cp_im = mag * jnp.sin(ang)
    a_re = pow_ref[0, 1:2, :]
    a_im = pow_ref[1, 1:2, :]
    den = lre * lre + lim * lim
    num_re = a_re - 1.0
    coef_re = (num_re * lre + a_im * lim) / den
    coef_im = (a_im * lre - num_re * lim) / den
    btr = btr_ref[o]
    bti = bti_ref[o]
    bbar_re = coef_re * btr - coef_im * bti
    bbar_im = coef_re * bti + coef_im * btr
    ctr = ctr_ref[o]
    cti = cti_ref[o]

    pairs_per_octet = GROUPS_PER_OCTET // 2
    pair_lane_group = _lane_block((SSM_GROUP, PAIR_STATE), SSM_STATE)
    for l in range(SUB):
        pr = pow_ref[0, l:l + 1, :]
        pi = pow_ref[1, l:l + 1, :]
        x_re = pr * bbar_re - pi * bbar_im
        x_im = pr * bbar_im + pi * bbar_re
        rows = slice(l * SSM_GROUP, (l + 1) * SSM_GROUP)
        xall_ref[0, rows, :] = x_re
        xall_ref[1, rows, :] = x_im
        pr1 = pow_ref[0, l + 1:l + 2, :]
        pi1 = pow_ref[1, l + 1:l + 2, :]
        w_re = pr1 * ctr - pi1 * cti
        w_im = pr1 * cti + pi1 * ctr
        j = SUB - 1 - l
        for pp in range(pairs_per_octet):
            sl = slice(pp * PAIR_STATE, (pp + 1) * PAIR_STATE)
            pair = o * pairs_per_octet + pp
            for gi in range(2):
                keep = pair_lane_group == gi
                in_rows = slice(gi * GROUP_LANES + j * SSM_GROUP, gi * GROUP_LANES + (j + 1) * SSM_GROUP)
                out_rows = slice(gi * GROUP_LANES + l * SSM_GROUP, gi * GROUP_LANES + (l + 1) * SSM_GROUP)
                for part, x, w in ((0, x_re, w_re), (1, x_im, -w_im)):
                    cols = slice(part * PAIR_STATE, (part + 1) * PAIR_STATE)
                    win_ref[pair, in_rows, cols] = jnp.where(keep, x[:, sl], 0.0).astype(BF16)
                    wt_ref[pp, out_rows, cols] = jnp.where(keep, w[:, sl], 0.0)
    for pp in range(pairs_per_octet):
        pair = o * pairs_per_octet + pp
        wout_ref[pair] = wt_ref[pp].T.astype(BF16)
        sl = slice(pp * PAIR_STATE, (pp + 1) * PAIR_STATE)
        cpw_ref[pair, 0] = cp_re[:, sl]
        cpw_ref[pair, 1] = cp_im[:, sl]

    row = lax.broadcasted_iota(jnp.int32, (LANES, OCT_STATE), 0)
    col = lax.broadcasted_iota(jnp.int32, (LANES, OCT_STATE), 1)
    same_group = (row // SSM_GROUP) == (col // SSM_STATE)

    def block_diag_t(x):
        return jnp.where(same_group, jnp.concatenate([x] * GROUPS_PER_OCTET, axis=0), 0.0).T

    k_all = (jnp.dot(xall_ref[0], block_diag_t(ctr), precision=lax.Precision.HIGHEST,
                     preferred_element_type=F32)
             - jnp.dot(xall_ref[1], block_diag_t(cti), precision=lax.Precision.HIGHEST,
                       preferred_element_type=F32))
    per_half = LANES // SSM_GROUP
    blk = _lane_block((SSM_GROUP, LANES), SSM_GROUP)
    lane = lax.broadcasted_iota(jnp.int32, (SSM_GROUP, LANES), 1)
    for g8 in range(GROUPS_PER_OCTET):
        strip = []
        for hh in range(GROUP_LANES // LANES):
            acc = jnp.zeros((SSM_GROUP, LANES), F32)
            for ll in range(per_half):
                lag = hh * per_half + ll
                k_lag = k_all[lag * SSM_GROUP:(lag + 1) * SSM_GROUP, :]
                shift = ((ll - g8) * SSM_GROUP) % LANES
                rolled = k_lag if shift == 0 else pltpu.roll(k_lag, shift, axis=1)
                acc = jnp.where(blk == ll, rolled, acc)
            strip.append(acc)
        for j in range(SUB):
            h0, h1 = _shift_lanes_zero_fill(strip, j * SSM_GROUP, lane)
            rows = slice(j * SSM_GROUP, (j + 1) * SSM_GROUP)
            toep_ref[o * GROUPS_PER_OCTET + g8, rows, 0:LANES] = h0.astype(BF16)
            toep_ref[o * GROUPS_PER_OCTET + g8, rows, LANES:2 * LANES] = h1.astype(BF16)


def _s5_state_scan(st_ref, zb_ref, cpow_ref, nsc):
    tiles = nsc // 8
    width = st_ref.shape[1] // 2
    re = slice(0, width)
    im = slice(width, 2 * width)
    rowmod = lax.broadcasted_iota(jnp.int32, (8, width), 0)
    src, dst = st_ref, zb_ref
    for s in (1, 2, 4):
        keep = rowmod >= s
        c_re = jnp.concatenate([jnp.where(keep, cpow_ref[0, s - 1:s, :], 0.0)] * tiles, axis=0)
        c_im = jnp.concatenate([jnp.where(keep, cpow_ref[1, s - 1:s, :], 0.0)] * tiles, axis=0)
        sh_re = src[pl.ds(8 - s, nsc), re]
        sh_im = src[pl.ds(8 - s, nsc), im]
        dst[8:8 + nsc, re] = src[8:8 + nsc, re] + c_re * sh_re - c_im * sh_im
        dst[8:8 + nsc, im] = src[8:8 + nsc, im] + c_re * sh_im + c_im * sh_re
        src, dst = dst, src
    ca_re = cpow_ref[0]
    ca_im = cpow_ref[1]
    c_re = st_ref[7:8, re]
    c_im = st_ref[7:8, im]
    for t in range(tiles):
        rows = slice(8 + 8 * t, 16 + 8 * t)
        cb_re = jnp.broadcast_to(c_re, (8, width))
        cb_im = jnp.broadcast_to(c_im, (8, width))
        p_re = src[rows, re] + ca_re * cb_re - ca_im * cb_im
        p_im = src[rows, im] + ca_re * cb_im + ca_im * cb_re
        st_ref[rows, re] = p_re
        st_ref[rows, im] = p_im
        c_re = p_re[7:8]
        c_im = p_im[7:8]


def _s5_kernel(u_ref, um_ref, lre_ref, lim_ref, ldt_ref, btr_ref, bti_ref, ctr_ref, cti_ref, d_ref,
               y_ref, toep_ref, win_ref, wout_ref, cpw_ref, pow_ref, xall_ref, wt_ref, lhs_ref,
               st_ref, zb_ref, carry_ref, *, nsc):
    first_call_step = (pl.program_id(0) == 0) & (pl.program_id(1) == 0)
    r = pl.program_id(1)

    @pl.when(first_call_step)
    def _():
        zb_ref[:, 0:8, :] = jnp.zeros((zb_ref.shape[0], 8, zb_ref.shape[2]), F32)

        def octet(o, carry):
            _s5_assemble_octet(o, lre_ref, lim_ref, ldt_ref, btr_ref, bti_ref, ctr_ref, cti_ref,
                               toep_ref, win_ref, wout_ref, cpw_ref, pow_ref, xall_ref, wt_ref)
            return carry
        lax.fori_loop(0, OCTETS, octet, 0)

    @pl.when(r == 0)
    def _():
        def meta_state(pair, carry):
            meta = jnp.concatenate([um_ref[2 * pair], um_ref[2 * pair + 1]], axis=1).astype(BF16)
            carry_ref[pair] = jnp.dot(meta, win_ref[pair], preferred_element_type=F32)
            return carry
        lax.fori_loop(0, PAIRS, meta_state, 0)

    lo = slice(0, GROUP_LANES)
    hi = slice(GROUP_LANES, 2 * GROUP_LANES)
    slots = st_ref.shape[0]

    def pair_block(k, carry):
        pairs = [slots * k + s for s in range(slots)]
        for s, pair in enumerate(pairs):
            lhs_ref[s, :, lo] = u_ref[2 * pair].astype(BF16)
            lhs_ref[s, :, hi] = u_ref[2 * pair + 1].astype(BF16)
            st_ref[s, 0:8, :] = carry_ref[pair]
            st_ref[s, 8:8 + nsc, :] = jnp.dot(lhs_ref[s], win_ref[pair], preferred_element_type=F32)
        for s, pair in enumerate(pairs):
            _s5_state_scan(st_ref.at[s], zb_ref.at[s], cpw_ref.at[pair], nsc)
            carry_ref[pair] = st_ref[s, nsc:nsc + 8, :]
        for s, pair in enumerate(pairs):
            s_in = st_ref[s, pl.ds(7, nsc), :].astype(BF16)
            y_state = jnp.dot(s_in, wout_ref[pair], preferred_element_type=F32)
            d = d_ref[pair]
            for gi, cols in ((0, lo), (1, hi)):
                g = 2 * pair + gi
                y = (jnp.dot(lhs_ref[s, :, cols], toep_ref[g], preferred_element_type=F32)
                     + y_state[:, cols] + d[:, cols] * u_ref[g])
                y_ref[g] = y
        return carry

    lax.fori_loop(0, PAIRS // slots, pair_block, 0)


def _s5_param_rows(lam_re, lam_im, log_dt, b_re, b_im, c_re, c_im, d):
    go = GROUPS_PER_OCTET
    lanes = lambda a: a.reshape(OCTETS, 1, OCT_STATE)
    ldt = jnp.broadcast_to(log_dt[:, None], (SSM_GROUPS, SSM_STATE))
    bt = lambda a: jnp.transpose(a.reshape(OCTETS, go, SSM_STATE, SSM_GROUP),
                                 (0, 3, 1, 2)).reshape(OCTETS, SSM_GROUP, OCT_STATE)
    ct = lambda a: jnp.transpose(a.reshape(OCTETS, go, SSM_GROUP, SSM_STATE),
                                 (0, 2, 1, 3)).reshape(OCTETS, SSM_GROUP, OCT_STATE)
    d_pairs = jnp.broadcast_to(d.reshape(PAIRS, 2, 1, SSM_GROUP),
                               (PAIRS, 2, SUB, SSM_GROUP)).reshape(PAIRS, 1, 2 * GROUP_LANES)
    return (lanes(lam_re), lanes(lam_im), lanes(ldt), bt(b_re), bt(b_im), ct(c_re), ct(c_im), d_pairs)


def _s5(u, um, params, bsz, seq, rows):
    nsc = rows // SUB
    steps = seq // rows
    n_pow = SUB + 8
    full = lambda a: pl.BlockSpec(a.shape, lambda b, r: (0,) * a.ndim)
    tok_blk = pl.BlockSpec((SSM_GROUPS, nsc, GROUP_LANES), lambda b, r: (0, b * steps + r, 0))
    return pl.pallas_call(
        functools.partial(_s5_kernel, nsc=nsc),
        grid=(bsz, steps),
        in_specs=[tok_blk, full(um)] + [full(p) for p in params],
        out_specs=tok_blk,
        out_shape=jax.ShapeDtypeStruct(u.shape, F32),
        scratch_shapes=[
            pltpu.VMEM((SSM_GROUPS, GROUP_LANES, GROUP_LANES), BF16),
            pltpu.VMEM((PAIRS, 2 * GROUP_LANES, 2 * PAIR_STATE), BF16),
            pltpu.VMEM((PAIRS, 2 * PAIR_STATE, 2 * GROUP_LANES), BF16),
            pltpu.VMEM((PAIRS, 2, 8, PAIR_STATE), F32),
            pltpu.VMEM((2, n_pow, OCT_STATE), F32),
            pltpu.VMEM((2, SUB * SSM_GROUP, OCT_STATE), F32),
            pltpu.VMEM((GROUPS_PER_OCTET // 2, 2 * GROUP_LANES, 2 * PAIR_STATE), F32),
            pltpu.VMEM((S5_SLOTS, nsc, 2 * GROUP_LANES), BF16),
            pltpu.VMEM((S5_SLOTS, nsc + 8, 2 * PAIR_STATE), F32),
            pltpu.VMEM((S5_SLOTS, nsc + 8, 2 * PAIR_STATE), F32),
            pltpu.VMEM((PAIRS, 8, 2 * PAIR_STATE), F32),
        ],
        compiler_params=pltpu.CompilerParams(
            dimension_semantics=("arbitrary", "arbitrary"), vmem_limit_bytes=VMEM_LIMIT),
        name="s5_mixer",
    )(u, um, *params)


def _mix_ffn_kernel(h1_ref, ret_ref, ynext_ref, gw32_ref, gb_ref, snw_ref, wo32_ref, n3w_ref,
                    wg32_ref, wu32_ref, wd32_ref, fnw_ref, out_ref,
                    gw_ref, wo_ref, wg_ref, wu_ref, wd_ref, acc_ref, ytok_ref):
    step = pl.program_id(0)
    nsub = ynext_ref.shape[1]

    @pl.when(step < WEIGHT_CHUNKS)
    def _():
        _cast_weight_chunks(step, ((gw32_ref, gw_ref), (wo32_ref, wo_ref), (wg32_ref, wg_ref),
                                   (wu32_ref, wu_ref), (wd32_ref, wd_ref)))

    @pl.when(step == WEIGHT_CHUNKS - 1)
    def _():
        _to_token_major(ynext_ref, ytok_ref.at[0], nsub, jax.nn.gelu)

    @pl.when(step >= WEIGHT_CHUNKS)
    def _():
        slot = lax.rem(step - WEIGHT_CHUNKS, 2)
        y = _token_major_rows(ytok_ref.at[slot], nsub)
        _to_token_major(ynext_ref, ytok_ref.at[1 - slot], nsub, jax.nn.gelu)
        z = y * jax.nn.sigmoid(jnp.dot(y.astype(BF16), gw_ref[...], preferred_element_type=F32)
                               + gb_ref[...])
        ssm = _rms(z, snw_ref[...]).astype(BF16)
        mixed = (jnp.dot(ret_ref[...], wo_ref[0:RET_WIDTH, :], preferred_element_type=F32)
                 + jnp.dot(ssm, wo_ref[RET_WIDTH:, :], preferred_element_type=F32))
        h2 = h1_ref[...] + mixed
        h3 = _swiglu_half_step(h2, n3w_ref, wg_ref, wu_ref, wd_ref, acc_ref)
        out_ref[...] = _rms(h3, fnw_ref[...])


def _mix_ffn(h1, ret, y, glu_w, glu_b, ssm_norm_w, w_out, n3w, wg, wu, wd, fnw, tm):
    rows = h1.shape[0]
    tiles = rows // tm
    row_blk = lambda i: (_tile_index(i), 0)
    next_blk = lambda i: (0, jnp.minimum(_tile_index(i + 1), tiles - 1), 0)
    return pl.pallas_call(
        _mix_ffn_kernel,
        grid=(WEIGHT_CHUNKS + tiles,),
        in_specs=[
            pl.BlockSpec((tm, D_MODEL), row_blk),
            pl.BlockSpec((tm, RET_WIDTH), row_blk),
            pl.BlockSpec((SSM_GROUPS, tm // SUB, GROUP_LANES), next_blk),
            _weight_chunk_spec((SSM_WIDTH, SSM_WIDTH)),
            _resident((1, SSM_WIDTH)),
            _resident((1, SSM_WIDTH)),
            _weight_chunk_spec((D_MODEL, D_MODEL)),
            _resident((1, D_MODEL)),
            _weight_chunk_spec((D_MODEL, D_FF)),
            _weight_chunk_spec((D_MODEL, D_FF)),
            _weight_chunk_spec((D_FF, D_MODEL)),
            _resident((1, D_MODEL)),
        ],
        out_specs=pl.BlockSpec((tm, D_MODEL), row_blk),
        out_shape=jax.ShapeDtypeStruct((rows, D_MODEL), F32),
        scratch_shapes=[
            pltpu.VMEM((SSM_WIDTH, SSM_WIDTH), BF16),
            pltpu.VMEM((D_MODEL, D_MODEL), BF16),
            pltpu.VMEM((D_MODEL, D_FF), BF16),
            pltpu.VMEM((D_MODEL, D_FF), BF16),
            pltpu.VMEM((D_FF, D_MODEL), BF16),
            pltpu.VMEM((tm, D_MODEL), F32),
            pltpu.VMEM((2, OCTETS, tm // SUB * TOK_PITCH, LANES), F32),
        ],
        compiler_params=pltpu.CompilerParams(
            dimension_semantics=("arbitrary",), vmem_limit_bytes=VMEM_LIMIT),
        name="mix_ffn",
    )(h1, ret, y, glu_w, glu_b, ssm_norm_w, w_out, n3w, wg, wu, wd, fnw)


def _rope_tables(n_pos):
    freqs = 1.0 / (ROPE_BASE ** (np.arange(0, HEAD_DIM, 2, dtype=np.float64) / HEAD_DIM))
    ang = np.arange(n_pos, dtype=np.float64)[:, None] * freqs[None, :]
    cos = np.cos(ang)
    sin = np.sin(ang)
    cos = np.concatenate([cos, cos], axis=-1).astype(np.float32)
    sin = np.concatenate([-sin, sin], axis=-1).astype(np.float32)
    split = lambda t: (jnp.asarray(t[:N_META]), jnp.asarray(t[N_META:]))
    return split(cos), split(sin)


def kernel(x, meta_tokens, ffn1_norm_w, ffn1_w_gate, ffn1_w_up, ffn1_w_down, mix_norm_w, w_in,
           ret_norm_w, ssm_lambda_re, ssm_lambda_im, ssm_log_dt, ssm_b_re, ssm_b_im, ssm_c_re,
           ssm_c_im, ssm_d, ssm_glu_w, ssm_glu_b, ssm_norm_w, w_out, ffn2_norm_w, ffn2_w_gate,
           ffn2_w_up, ffn2_w_down, final_norm_w):
    bsz, seq, _ = x.shape
    assert ffn1_norm_w.shape[0] == 1, "single layer only"
    tm = 512
    ret_rows = 2048
    s5_rows = 4096
    assert seq % tm == 0 and seq % ret_rows == 0 and seq % s5_rows == 0

    row = lambda a: a.reshape(1, -1)
    l = 0
    (cos_m, cos), (sin_m, sin) = _rope_tables(N_META + seq)

    h1, q, k, v, g, u, km, vm, um = _ffn_inproj(
        x.reshape(bsz * seq, D_MODEL), meta_tokens, cos, sin, cos_m, sin_m,
        row(ffn1_norm_w[l]), ffn1_w_gate[l], ffn1_w_up[l], ffn1_w_down[l],
        row(mix_norm_w[l]), w_in[l], tm=tm)

    ret = _retention(q, k, v, g, km, vm, row(ret_norm_w[l]), bsz, seq, ret_rows)

    params = _s5_param_rows(ssm_lambda_re[l], ssm_lambda_im[l], ssm_log_dt[l], ssm_b_re[l],
                            ssm_b_im[l], ssm_c_re[l], ssm_c_im[l], ssm_d[l])
    y = _s5(u, um, params, bsz, seq, s5_rows)

    out = _mix_ffn(h1, ret, y, ssm_glu_w[l], row(ssm_glu_b[l]), row(ssm_norm_w[l]),
                   w_out[l], row(ffn2_norm_w[l]), ffn2_w_gate[l], ffn2_w_up[l],
                   ffn2_w_down[l], row(final_norm_w), tm=tm)
    return out.reshape(bsz, seq, D_MODEL)
```

```python
import functools
import math

import jax
import jax.numpy as jnp
import numpy as np
from jax import lax
from jax.experimental import pallas as pl
from jax.experimental.pallas import tpu as pltpu

D_MODEL = 1024
N_META = 16
RET_HEADS = 4
HEAD_DIM = 128
RET_WIDTH = RET_HEADS * HEAD_DIM
SSM_WIDTH = 512
SSM_GROUP = 16
SSM_GROUPS = SSM_WIDTH // SSM_GROUP
SSM_STATE = 64
RET_BLOCK = 256
D_FF = 2816
FFN_RES = 0.5
ROPE_BASE = 10000.0
EPS = 1e-6
IN_PROJ = 4 * RET_WIDTH + SSM_WIDTH

LANES = 128
MXU_DIM = 256
OCTETS = SSM_WIDTH // LANES
GROUPS_PER_OCTET = LANES // SSM_GROUP
SUB = N_META
OCT_STATE = GROUPS_PER_OCTET * SSM_STATE
GROUP_LANES = SUB * SSM_GROUP
PAIRS = SSM_GROUPS // 2
PAIR_STATE = 2 * SSM_STATE
S5_SLOTS = 4
TOK_PITCH = 24
FF_CHUNK = 256
WEIGHT_CHUNKS = 8
VMEM_LIMIT = 56 * 1024 * 1024

F32 = jnp.float32
BF16 = jnp.bfloat16


def _rms(x, w):
    return x * lax.rsqrt(jnp.mean(x * x, axis=-1, keepdims=True) + EPS) * w


def _swiglu_half_step(h, nw_ref, wg_ref, wu_ref, wd_ref, acc_ref):
    n = _rms(h, nw_ref[...]).astype(BF16)
    for lo in range(0, D_FF, FF_CHUNK):
        sl = slice(lo, min(lo + FF_CHUNK, D_FF))
        g = jnp.dot(n, wg_ref[:, sl], preferred_element_type=F32)
        u = jnp.dot(n, wu_ref[:, sl], preferred_element_type=F32)
        acc_ref[:, sl] = (g * jax.nn.sigmoid(g) * u).astype(BF16)
    return h + FFN_RES * jnp.dot(acc_ref[...], wd_ref[...], preferred_element_type=F32)


def _rope(x, cos, sin_signed):
    return x * cos + pltpu.roll(x, HEAD_DIM // 2, axis=1) * sin_signed


def _lane_block(shape, width):
    return lax.broadcasted_iota(jnp.int32, shape, 1) // width


def _transpose_lane_blocks(v):
    n = LANES // SSM_GROUP
    assert len(v) == n == GROUPS_PER_OCTET
    blk = _lane_block(v[0].shape, SSM_GROUP)
    d = n // 2
    while d:
        low = (blk & d) == 0
        nxt = list(v)
        for a in range(n):
            if a & d == 0:
                nxt[a] = jnp.where(low, v[a], pltpu.roll(v[a + d], d * SSM_GROUP, axis=1))
                nxt[a + d] = jnp.where(low, pltpu.roll(v[a], LANES - d * SSM_GROUP, axis=1), v[a + d])
        v = nxt
        d //= 2
    return v


def _to_group_dense(tok_ref, out_ref, nsub, row0=0):
    per_half = LANES // SSM_GROUP
    for o in range(OCTETS):
        for hh in range(GROUP_LANES // LANES):
            slabs = [tok_ref[o, pl.ds(hh * per_half + jj, nsub, stride=SUB), :]
                     for jj in range(per_half)]
            for g8, rows in enumerate(_transpose_lane_blocks(slabs)):
                out_ref[o * GROUPS_PER_OCTET + g8, row0:row0 + nsub,
                        hh * LANES:(hh + 1) * LANES] = rows


def _to_token_major(gd_ref, tok_ref, nsub, fn):
    per_half = LANES // SSM_GROUP
    for o in range(OCTETS):
        for hh in range(GROUP_LANES // LANES):
            rows = [gd_ref[o * GROUPS_PER_OCTET + g8, :, hh * LANES:(hh + 1) * LANES]
                    for g8 in range(GROUPS_PER_OCTET)]
            for jj, slab in enumerate(_transpose_lane_blocks(rows)):
                tok_ref[o, pl.ds(hh * per_half + jj, nsub, stride=TOK_PITCH), :] = fn(slab)


def _token_major_rows(tok_ref, nsub):
    return jnp.concatenate(
        [jnp.concatenate([tok_ref[o, c * TOK_PITCH:c * TOK_PITCH + SUB, :] for c in range(nsub)],
                         axis=0) for o in range(OCTETS)], axis=-1)


def _cast_weight_chunks(step, pairs):
    for src_ref, dst_ref in pairs:
        rows = src_ref.shape[0]
        r0 = pl.multiple_of(step * rows, rows)
        dst_ref[pl.ds(r0, rows), :] = src_ref[...].astype(BF16)


def _weight_chunk_spec(shape):
    rows = shape[0] // WEIGHT_CHUNKS
    assert rows * WEIGHT_CHUNKS == shape[0] and rows % 16 == 0, shape
    return pl.BlockSpec((rows, shape[1]), lambda i: (jnp.minimum(i, WEIGHT_CHUNKS - 1), 0))


def _tile_index(i):
    return jnp.maximum(i - WEIGHT_CHUNKS, 0)


def _ffn_inproj_kernel(x_ref, cos_ref, sin_ref, xm_ref, cosm_ref, sinm_ref, n1w_ref, wg32_ref,
                       wu32_ref, wd32_ref, n2w_ref, win32_ref,
                       h1_ref, q_ref, k_ref, v_ref, g_ref, u_ref, km_ref, vm_ref, um_ref,
                       wg_ref, wu_ref, wd_ref, win_ref, acc_ref, accm_ref, utok_ref, utokm_ref):
    step = pl.program_id(0)
    k_scale = HEAD_DIM ** -0.5

    def mix_norm(h1):
        return _rms(h1, n2w_ref[...]).astype(BF16)

    def in_proj(n, part):
        return jnp.dot(n, win_ref[:, part * RET_WIDTH:(part + 1) * RET_WIDTH],
                       preferred_element_type=F32)

    def rope_heads(p, cos, sin, scale, out_ref):
        for h in range(RET_HEADS):
            hs = slice(h * HEAD_DIM, (h + 1) * HEAD_DIM)
            y = _rope(p[:, hs], cos, sin)
            out_ref[:, hs] = (y if scale is None else y * scale).astype(BF16)

    def split_octets(p, tok_ref):
        for o in range(OCTETS):
            tok_ref[o] = p[:, o * LANES:(o + 1) * LANES]

    @pl.when(step < WEIGHT_CHUNKS)
    def _():
        _cast_weight_chunks(step, ((wg32_ref, wg_ref), (wu32_ref, wu_ref), (wd32_ref, wd_ref),
                                   (win32_ref, win_ref)))

    @pl.when(step == WEIGHT_CHUNKS)
    def _():
        h1 = _swiglu_half_step(xm_ref[...], n1w_ref, wg_ref, wu_ref, wd_ref, accm_ref)
        n = mix_norm(h1)
        rope_heads(in_proj(n, 1), cosm_ref[...], sinm_ref[...], k_scale, km_ref)
        vm_ref[...] = in_proj(n, 2).astype(BF16)
        split_octets(in_proj(n, 4), utokm_ref)
        um_ref[...] = jnp.zeros(um_ref.shape, F32)
        _to_group_dense(utokm_ref, um_ref, 1, row0=7)

    @pl.when(step >= WEIGHT_CHUNKS)
    def _():
        h1 = _swiglu_half_step(x_ref[...], n1w_ref, wg_ref, wu_ref, wd_ref, acc_ref)
        h1_ref[...] = h1
        n = mix_norm(h1)
        split_octets(in_proj(n, 4), utok_ref)
        _to_group_dense(utok_ref, u_ref, u_ref.shape[1])
        cos = cos_ref[...]
        sin = sin_ref[...]
        rope_heads(in_proj(n, 0), cos, sin, None, q_ref)
        rope_heads(in_proj(n, 1), cos, sin, k_scale, k_ref)
        v_ref[...] = in_proj(n, 2).astype(BF16)
        g_ref[...] = in_proj(n, 3)


def _resident(shape):
    nd = len(shape)
    return pl.BlockSpec(shape, lambda *_: (0,) * nd, pipeline_mode=pl.Buffered(1))


def _ffn_inproj(x2, xm, cos, sin, cos_m, sin_m, n1w, wg, wu, wd, n2w, w_in, tm):
    rows = x2.shape[0]
    pos_blocks = cos.shape[0] // tm
    row_blk = lambda i: (_tile_index(i), 0)
    pos_blk = lambda i: (_tile_index(i) % pos_blocks, 0)
    out_shape = (
        jax.ShapeDtypeStruct((rows, D_MODEL), F32),
        jax.ShapeDtypeStruct((rows, RET_WIDTH), BF16),
        jax.ShapeDtypeStruct((rows, RET_WIDTH), BF16),
        jax.ShapeDtypeStruct((rows, RET_WIDTH), BF16),
        jax.ShapeDtypeStruct((rows, RET_WIDTH), F32),
        jax.ShapeDtypeStruct((SSM_GROUPS, rows // SUB, GROUP_LANES), F32),
        jax.ShapeDtypeStruct((N_META, RET_WIDTH), BF16),
        jax.ShapeDtypeStruct((N_META, RET_WIDTH), BF16),
        jax.ShapeDtypeStruct((SSM_GROUPS, 8, GROUP_LANES), F32),
    )
    return pl.pallas_call(
        _ffn_inproj_kernel,
        grid=(WEIGHT_CHUNKS + rows // tm,),
        in_specs=[
            pl.BlockSpec((tm, D_MODEL), row_blk),
            pl.BlockSpec((tm, HEAD_DIM), pos_blk),
            pl.BlockSpec((tm, HEAD_DIM), pos_blk),
            _resident((N_META, D_MODEL)),
            _resident((N_META, HEAD_DIM)),
            _resident((N_META, HEAD_DIM)),
            _resident((1, D_MODEL)),
            _weight_chunk_spec((D_MODEL, D_FF)),
            _weight_chunk_spec((D_MODEL, D_FF)),
            _weight_chunk_spec((D_FF, D_MODEL)),
            _resident((1, D_MODEL)),
            _weight_chunk_spec((D_MODEL, IN_PROJ)),
        ],
        out_specs=(
            pl.BlockSpec((tm, D_MODEL), row_blk),
            pl.BlockSpec((tm, RET_WIDTH), row_blk),
            pl.BlockSpec((tm, RET_WIDTH), row_blk),
            pl.BlockSpec((tm, RET_WIDTH), row_blk),
            pl.BlockSpec((tm, RET_WIDTH), row_blk),
            pl.BlockSpec((SSM_GROUPS, tm // SUB, GROUP_LANES), lambda i: (0, _tile_index(i), 0)),
            pl.BlockSpec((N_META, RET_WIDTH), lambda i: (0, 0)),
            pl.BlockSpec((N_META, RET_WIDTH), lambda i: (0, 0)),
            pl.BlockSpec((SSM_GROUPS, 8, GROUP_LANES), lambda i: (0, 0, 0)),
        ),
        out_shape=out_shape,
        scratch_shapes=[
            pltpu.VMEM((D_MODEL, D_FF), BF16),
            pltpu.VMEM((D_MODEL, D_FF), BF16),
            pltpu.VMEM((D_FF, D_MODEL), BF16),
            pltpu.VMEM((D_MODEL, IN_PROJ), BF16),
            pltpu.VMEM((tm, D_FF), BF16),
            pltpu.VMEM((N_META, D_FF), BF16),
            pltpu.VMEM((OCTETS, tm, LANES), F32),
            pltpu.VMEM((OCTETS, N_META, LANES), F32),
        ],
        compiler_params=pltpu.CompilerParams(
            dimension_semantics=("arbitrary",), vmem_limit_bytes=VMEM_LIMIT),
        name="ffn_inproj",
    )(x2, cos, sin, xm, cos_m, sin_m, n1w, wg, wu, wd, n2w, w_in)


def _retention_kernel(q_ref, k_ref, v_ref, g_ref, km_ref, vm_ref, mask_ref, wq_ref, wk_ref, wm_ref,
                      gc_ref, nw_ref, o_ref, state_ref, *, rows):
    tn = (((0,), (0,)), ((), ()))
    nt = (((1,), (1,)), ((), ()))

    @pl.when(pl.program_id(1) == 0)
    def _():
        for h in range(RET_HEADS):
            hs = slice(h * HEAD_DIM, (h + 1) * HEAD_DIM)
            kw = (km_ref[:, hs].astype(F32) * wm_ref[h]).astype(BF16)
            state_ref[h] = lax.dot_general(kw, vm_ref[:, hs], tn, preferred_element_type=F32)

    def chunk(i, carry):
        r0 = pl.multiple_of(i * RET_BLOCK, RET_BLOCK)
        rs = pl.ds(r0, RET_BLOCK)
        for h in range(RET_HEADS):
            hs = slice(h * HEAD_DIM, (h + 1) * HEAD_DIM)
            qh = q_ref[rs, hs]
            kh = k_ref[rs, hs]
            vh = v_ref[rs, hs]
            st = state_ref[h]
            s = lax.dot_general(qh, kh, nt, preferred_element_type=F32) * mask_ref[h]
            o = (jnp.dot(s.astype(BF16), vh, preferred_element_type=F32)
                 + jnp.dot(qh, st.astype(BF16), preferred_element_type=F32) * wq_ref[h])
            kw = (kh.astype(F32) * wk_ref[h]).astype(BF16)
            state_ref[h] = gc_ref[h] * st + lax.dot_general(kw, vh, tn, preferred_element_type=F32)
            mu = jnp.mean(o, axis=-1, keepdims=True)
            d = o - mu
            var = jnp.mean(d * d, axis=-1, keepdims=True)
            y = d * lax.rsqrt(var + EPS) * nw_ref[:, hs]
            gate = g_ref[rs, hs]
            o_ref[rs, hs] = (gate * jax.nn.sigmoid(gate) * y).astype(BF16)
        return carry

    lax.fori_loop(0, rows // RET_BLOCK, chunk, 0, unroll=4)


def _retention_tables():
    log_g = np.log(1.0 - 2.0 ** (-5.0 - np.arange(RET_HEADS, dtype=np.float64)))
    i = np.arange(RET_BLOCK)
    diff = i[:, None] - i[None, :]
    mask = np.where(diff[None] >= 0, np.exp(log_g[:, None, None] * np.maximum(diff, 0)[None]), 0.0)
    pos = np.arange(RET_BLOCK, dtype=np.float64)
    full = lambda w: np.broadcast_to(w[:, :, None], w.shape + (HEAD_DIM,))
    w_q = full(np.exp(log_g[:, None] * (pos + 1.0)[None]))
    w_k = full(np.exp(log_g[:, None] * (RET_BLOCK - 1 - pos)[None]))
    w_m = full(np.exp(log_g[:, None] * (N_META - 1 - np.arange(N_META, dtype=np.float64))[None]))
    g_c = np.broadcast_to(np.exp(log_g * RET_BLOCK)[:, None, None], (RET_HEADS, HEAD_DIM, HEAD_DIM))
    return tuple(jnp.asarray(t, dtype=F32) for t in (mask, w_q, w_k, w_m, g_c))


def _retention(q, k, v, g, km, vm, ret_norm_w, bsz, seq, rows):
    mask, w_q, w_k, w_m, g_c = _retention_tables()
    steps = seq // rows
    blk = lambda b, c: (b * steps + c, 0)
    full3 = lambda a: pl.BlockSpec(a.shape, lambda b, c: (0, 0, 0))
    full2 = lambda a: pl.BlockSpec(a.shape, lambda b, c: (0, 0))
    return pl.pallas_call(
        functools.partial(_retention_kernel, rows=rows),
        grid=(bsz, steps),
        in_specs=[pl.BlockSpec((rows, RET_WIDTH), blk)] * 4
        + [full2(km), full2(vm), full3(mask), full3(w_q), full3(w_k), full3(w_m), full3(g_c),
           full2(ret_norm_w)],
        out_specs=pl.BlockSpec((rows, RET_WIDTH), blk),
        out_shape=jax.ShapeDtypeStruct((bsz * seq, RET_WIDTH), BF16),
        scratch_shapes=[pltpu.VMEM((RET_HEADS, HEAD_DIM, HEAD_DIM), F32)],
        compiler_params=pltpu.CompilerParams(
            dimension_semantics=("arbitrary", "arbitrary"), vmem_limit_bytes=VMEM_LIMIT),
        name="retention",
    )(q, k, v, g, km, vm, mask, w_q, w_k, w_m, g_c, ret_norm_w)


def _shift_lanes_zero_fill(halves, shift, lane):
    h0, h1 = halves
    whole, s = divmod(shift, LANES)
    r0 = h0 if s == 0 else pltpu.roll(h0, s, axis=1)
    zero = jnp.zeros_like(h0)
    if whole == 1:
        return zero, jnp.where(lane >= s, r0, zero)
    r1 = h1 if s == 0 else pltpu.roll(h1, s, axis=1)
    return jnp.where(lane >= s, r0, zero), jnp.where(lane >= s, r1, r0)


def _s5_assemble_octet(o, lre_ref, lim_ref, ldt_ref, btr_ref, bti_ref, ctr_ref, cti_ref,
                       toep_ref, win_ref, wout_ref, cpw_ref, pow_ref, xall_ref, wt_ref):
    lre = lre_ref[o]
    lim = lim_ref[o]
    dt = jnp.exp(ldt_ref[o])
    n_pow = pow_ref.shape[1]
    ell = lax.broadcasted_iota(jnp.int32, (n_pow, OCT_STATE), 0).astype(F32)
    mag = jnp.exp(ell * (lre * dt))
    ang = ell * (lim * dt)
    pow_ref[0] = mag * jnp.cos(ang)
    pow_ref[1] = mag * jnp.sin(ang)
    ell = ((lax.broadcasted_iota(jnp.int32, (8, OCT_STATE), 0) + 1) * SUB).astype(F32)
    mag = jnp.exp(ell * (lre * dt))
    ang = ell * (lim * dt)
    cp_re = mag * jnp.cos(ang)
    cp_im = mag * jnp.sin(ang)
    a_re = pow_ref[0, 1:2, :]
    a_im = pow_ref[1, 1:2, :]
    den = lre * lre + lim * lim
    num_re = a_re - 1.0
    coef_re = (num_re * lre + a_im * lim) / den
    coef_im = (a_im * lre - num_re * lim) / den
    btr = btr_ref[o]
    bti = bti_ref[o]
    bbar_re = coef_re * btr - coef_im * bti
    bbar_im = coef_re * bti + coef_im * btr
    ctr = ctr_ref[o]
    cti = cti_ref[o]

    pairs_per_octet = GROUPS_PER_OCTET // 2
    pair_lane_group = _lane_block((SSM_GROUP, PAIR_STATE), SSM_STATE)
    for l in range(SUB):
        pr = pow_ref[0, l:l + 1, :]
        pi = pow_ref[1, l:l + 1, :]
        x_re = pr * bbar_re - pi * bbar_im
        x_im = pr * bbar_im + pi * bbar_re
        rows = slice(l * SSM_GROUP, (l + 1) * SSM_GROUP)
        xall_ref[0, rows, :] = x_re
        xall_ref[1, rows, :] = x_im
        pr1 = pow_ref[0, l + 1:l + 2, :]
        pi1 = pow_ref[1, l + 1:l + 2, :]
        w_re = pr1 * ctr - pi1 * cti
        w_im = pr1 * cti + pi1 * ctr
        j = SUB - 1 - l
        for pp in range(pairs_per_octet):
            sl = slice(pp * PAIR_STATE, (pp + 1) * PAIR_STATE)
            pair = o * pairs_per_octet + pp
            for gi in range(2):
                keep = pair_lane_group == gi
                in_rows = slice(gi * GROUP_LANES + j * SSM_GROUP, gi * GROUP_LANES + (j + 1) * SSM_GROUP)
                out_rows = slice(gi * GROUP_LANES + l * SSM_GROUP, gi * GROUP_LANES + (l + 1) * SSM_GROUP)
                for part, x, w in ((0, x_re, w_re), (1, x_im, -w_im)):
                    cols = slice(part * PAIR_STATE, (part + 1) * PAIR_STATE)
                    win_ref[pair, in_rows, cols] = jnp.where(keep, x[:, sl], 0.0).astype(BF16)
                    wt_ref[pp, out_rows, cols] = jnp.where(keep, w[:, sl], 0.0)
    for pp in range(pairs_per_octet):
        pair = o * pairs_per_octet + pp
        wout_ref[pair] = wt_ref[pp].T.astype(BF16)
        sl = slice(pp * PAIR_STATE, (pp + 1) * PAIR_STATE)
        cpw_ref[pair, 0] = cp_re[:, sl]
        cpw_ref[pair, 1] = cp_im[:, sl]

    row = lax.broadcasted_iota(jnp.int32, (LANES, OCT_STATE), 0)
    col = lax.broadcasted_iota(jnp.int32, (LANES, OCT_STATE), 1)
    same_group = (row // SSM_GROUP) == (col // SSM_STATE)

    def block_diag_t(x):
        return jnp.where(same_group, jnp.concatenate([x] * GROUPS_PER_OCTET, axis=0), 0.0).T

    k_all = (jnp.dot(xall_ref[0], block_diag_t(ctr), precision=lax.Precision.HIGHEST,
                     preferred_element_type=F32)
             - jnp.dot(xall_ref[1], block_diag_t(cti), precision=lax.Precision.HIGHEST,
                       preferred_element_type=F32))
    per_half = LANES // SSM_GROUP
    blk = _lane_block((SSM_GROUP, LANES), SSM_GROUP)
    lane = lax.broadcasted_iota(jnp.int32, (SSM_GROUP, LANES), 1)
    for g8 in range(GROUPS_PER_OCTET):
        strip = []
        for hh in range(GROUP_LANES // LANES):
            acc = jnp.zeros((SSM_GROUP, LANES), F32)
            for ll in range(per_half):
                lag = hh * per_half + ll
                k_lag = k_all[lag * SSM_GROUP:(lag + 1) * SSM_GROUP, :]
                shift = ((ll - g8) * SSM_GROUP) % LANES
                rolled = k_lag if shift == 0 else pltpu.roll(k_lag, shift, axis=1)
                acc = jnp.where(blk == ll, rolled, acc)
            strip.append(acc)
        for j in range(SUB):
            h0, h1 = _shift_lanes_zero_fill(strip, j * SSM_GROUP, lane)
            rows = slice(j * SSM_GROUP, (j + 1) * SSM_GROUP)
            toep_ref[o * GROUPS_PER_OCTET + g8, rows, 0:LANES] = h0.astype(BF16)
            toep_ref[o * GROUPS_PER_OCTET + g8, rows, LANES:2 * LANES] = h1.astype(BF16)


def _s5_state_scan(st_ref, zb_ref, cpow_ref, nsc):
    tiles = nsc // 8
    width = st_ref.shape[1] // 2
    re = slice(0, width)
    im = slice(width, 2 * width)
    rowmod = lax.broadcasted_iota(jnp.int32, (8, width), 0)
    src, dst = st_ref, zb_ref
    for s in (1, 2, 4):
        keep = rowmod >= s
        c_re = jnp.concatenate([jnp.where(keep, cpow_ref[0, s - 1:s, :], 0.0)] * tiles, axis=0)
        c_im = jnp.concatenate([jnp.where(keep, cpow_ref[1, s - 1:s, :], 0.0)] * tiles, axis=0)
        sh_re = src[pl.ds(8 - s, nsc), re]
        sh_im = src[pl.ds(8 - s, nsc), im]
        dst[8:8 + nsc, re] = src[8:8 + nsc, re] + c_re * sh_re - c_im * sh_im
        dst[8:8 + nsc, im] = src[8:8 + nsc, im] + c_re * sh_im + c_im * sh_re
        src, dst = dst, src
    ca_re = cpow_ref[0]
    ca_im = cpow_ref[1]
    c_re = st_ref[7:8, re]
    c_im = st_ref[7:8, im]
    for t in range(tiles):
        rows = slice(8 + 8 * t, 16 + 8 * t)
        cb_re = jnp.broadcast_to(c_re, (8, width))
        cb_im = jnp.broadcast_to(c_im, (8, width))
        p_re = src[rows, re] + ca_re * cb_re - ca_im * cb_im
        p_im = src[rows, im] + ca_re * cb_im + ca_im * cb_re
        st_ref[rows, re] = p_re
        st_ref[rows, im] = p_im
        c_re = p_re[7:8]
        c_im = p_im[7:8]


def _s5_kernel(u_ref, um_ref, lre_ref, lim_ref, ldt_ref, btr_ref, bti_ref, ctr_ref, cti_ref, d_ref,
               y_ref, toep_ref, win_ref, wout_ref, cpw_ref, pow_ref, xall_ref, wt_ref, lhs_ref,
               st_ref, zb_ref, carry_ref, *, nsc):
    first_call_step = (pl.program_id(0) == 0) & (pl.program_id(1) == 0)
    r = pl.program_id(1)

    @pl.when(first_call_step)
    def _():
        zb_ref[:, 0:8, :] = jnp.zeros((zb_ref.shape[0], 8, zb_ref.shape[2]), F32)

        def octet(o, carry):
            _s5_assemble_octet(o, lre_ref, lim_ref, ldt_ref, btr_ref, bti_ref, ctr_ref, cti_ref,
                               toep_ref, win_ref, wout_ref, cpw_ref, pow_ref, xall_ref, wt_ref)
            return carry
        lax.fori_loop(0, OCTETS, octet, 0)

    @pl.when(r == 0)
    def _():
        def meta_state(pair, carry):
            meta = jnp.concatenate([um_ref[2 * pair], um_ref[2 * pair + 1]], axis=1).astype(BF16)
            carry_ref[pair] = jnp.dot(meta, win_ref[pair], preferred_element_type=F32)
            return carry
        lax.fori_loop(0, PAIRS, meta_state, 0)

    lo = slice(0, GROUP_LANES)
    hi = slice(GROUP_LANES, 2 * GROUP_LANES)
    slots = st_ref.shape[0]

    def pair_block(k, carry):
        pairs = [slots * k + s for s in range(slots)]
        for s, pair in enumerate(pairs):
            lhs_ref[s, :, lo] = u_ref[2 * pair].astype(BF16)
            lhs_ref[s, :, hi] = u_ref[2 * pair + 1].astype(BF16)
            st_ref[s, 0:8, :] = carry_ref[pair]
            st_ref[s, 8:8 + nsc, :] = jnp.dot(lhs_ref[s], win_ref[pair], preferred_element_type=F32)
        for s, pair in enumerate(pairs):
            _s5_state_scan(st_ref.at[s], zb_ref.at[s], cpw_ref.at[pair], nsc)
            carry_ref[pair] = st_ref[s, nsc:nsc + 8, :]
        for s, pair in enumerate(pairs):
            s_in = st_ref[s, pl.ds(7, nsc), :].astype(BF16)
            y_state = jnp.dot(s_in, wout_ref[pair], preferred_element_type=F32)
            d = d_ref[pair]
            for gi, cols in ((0, lo), (1, hi)):
                g = 2 * pair + gi
                y = (jnp.dot(lhs_ref[s, :, cols], toep_ref[g], preferred_element_type=F32)
                     + y_state[:, cols] + d[:, cols] * u_ref[g])
                y_ref[g] = y
        return carry

    lax.fori_loop(0, PAIRS // slots, pair_block, 0)


def _s5_param_rows(lam_re, lam_im, log_dt, b_re, b_im, c_re, c_im, d):
    go = GROUPS_PER_OCTET
    lanes = lambda a: a.reshape(OCTETS, 1, OCT_STATE)
    ldt = jnp.broadcast_to(log_dt[:, None], (SSM_GROUPS, SSM_STATE))
    bt = lambda a: jnp.transpose(a.reshape(OCTETS, go, SSM_STATE, SSM_GROUP),
                                 (0, 3, 1, 2)).reshape(OCTETS, SSM_GROUP, OCT_STATE)
    ct = lambda a: jnp.transpose(a.reshape(OCTETS, go, SSM_GROUP, SSM_STATE),
                                 (0, 2, 1, 3)).reshape(OCTETS, SSM_GROUP, OCT_STATE)
    d_pairs = jnp.broadcast_to(d.reshape(PAIRS, 2, 1, SSM_GROUP),
                               (PAIRS, 2, SUB, SSM_GROUP)).reshape(PAIRS, 1, 2 * GROUP_LANES)
    return (lanes(lam_re), lanes(lam_im), lanes(ldt), bt(b_re), bt(b_im), ct(c_re), ct(c_im), d_pairs)


def _s5(u, um, params, bsz, seq, rows):
    nsc = rows // SUB
    steps = seq // rows
    n_pow = SUB + 8
    full = lambda a: pl.BlockSpec(a.shape, lambda b, r: (0,) * a.ndim)
    tok_blk = pl.BlockSpec((SSM_GROUPS, nsc, GROUP_LANES), lambda b, r: (0, b * steps + r, 0))
    return pl.pallas_call(
        functools.partial(_s5_kernel, nsc=nsc),
        grid=(bsz, steps),
        in_specs=[tok_blk, full(um)] + [full(p) for p in params],
        out_specs=tok_blk,
        out_shape=jax.ShapeDtypeStruct(u.shape, F32),
        scratch_shapes=[
            pltpu.VMEM((SSM_GROUPS, GROUP_LANES, GROUP_LANES), BF16),
            pltpu.VMEM((PAIRS, 2 * GROUP_LANES, 2 * PAIR_STATE), BF16),
            pltpu.VMEM((PAIRS, 2 * PAIR_STATE, 2 * GROUP_LANES), BF16),
            pltpu.VMEM((PAIRS, 2, 8, PAIR_STATE), F32),
            pltpu.VMEM((2, n_pow, OCT_STATE), F32),
            pltpu.VMEM((2, SUB * SSM_GROUP, OCT_STATE), F32),
            pltpu.VMEM((GROUPS_PER_OCTET // 2, 2 * GROUP_LANES, 2 * PAIR_STATE), F32),
            pltpu.VMEM((S5_SLOTS, nsc, 2 * GROUP_LANES), BF16),
            pltpu.VMEM((S5_SLOTS, nsc + 8, 2 * PAIR_STATE), F32),
            pltpu.VMEM((S5_SLOTS, nsc + 8, 2 * PAIR_STATE), F32),
            pltpu.VMEM((PAIRS, 8, 2 * PAIR_STATE), F32),
        ],
        compiler_params=pltpu.CompilerParams(
            dimension_semantics=("arbitrary", "arbitrary"), vmem_limit_bytes=VMEM_LIMIT),
        name="s5_mixer",
    )(u, um, *params)


def _mix_ffn_kernel(h1_ref, ret_ref, ynext_ref, gw32_ref, gb_ref, snw_ref, wo32_ref, n3w_ref,
                    wg32_ref, wu32_ref, wd32_ref, fnw_ref, out_ref,
                    gw_ref, wo_ref, wg_ref, wu_ref, wd_ref, acc_ref, ytok_ref):
    step = pl.program_id(0)
    nsub = ynext_ref.shape[1]

    @pl.when(step < WEIGHT_CHUNKS)
    def _():
        _cast_weight_chunks(step, ((gw32_ref, gw_ref), (wo32_ref, wo_ref), (wg32_ref, wg_ref),
                                   (wu32_ref, wu_ref), (wd32_ref, wd_ref)))

    @pl.when(step == WEIGHT_CHUNKS - 1)
    def _():
        _to_token_major(ynext_ref, ytok_ref.at[0], nsub, jax.nn.gelu)

    @pl.when(step >= WEIGHT_CHUNKS)
    def _():
        slot = lax.rem(step - WEIGHT_CHUNKS, 2)
        y = _token_major_rows(ytok_ref.at[slot], nsub)
        _to_token_major(ynext_ref, ytok_ref.at[1 - slot], nsub, jax.nn.gelu)
        z = y * jax.nn.sigmoid(jnp.dot(y.astype(BF16), gw_ref[...], preferred_element_type=F32)
                               + gb_ref[...])
        ssm = _rms(z, snw_ref[...]).astype(BF16)
        mixed = jnp.dot(jnp.concatenate([ret_ref[...], ssm], axis=-1), wo_ref[...],
                        preferred_element_type=F32)
        h2 = h1_ref[...] + mixed
        h3 = _swiglu_half_step(h2, n3w_ref, wg_ref, wu_ref, wd_ref, acc_ref)
        out_ref[...] = _rms(h3, fnw_ref[...])


def _mix_ffn(h1, ret, y, glu_w, glu_b, ssm_norm_w, w_out, n3w, wg, wu, wd, fnw, tm):
    rows = h1.shape[0]
    tiles = rows // tm
    row_blk = lambda i: (_tile_index(i), 0)
    next_blk = lambda i: (0, jnp.minimum(_tile_index(i + 1), tiles - 1), 0)
    return pl.pallas_call(
        _mix_ffn_kernel,
        grid=(WEIGHT_CHUNKS + tiles,),
        in_specs=[
            pl.BlockSpec((tm, D_MODEL), row_blk),
            pl.BlockSpec((tm, RET_WIDTH), row_blk),
            pl.BlockSpec((SSM_GROUPS, tm // SUB, GROUP_LANES), next_blk),
            _weight_chunk_spec((SSM_WIDTH, SSM_WIDTH)),
            _resident((1, SSM_WIDTH)),
            _resident((1, SSM_WIDTH)),
            _weight_chunk_spec((D_MODEL, D_MODEL)),
            _resident((1, D_MODEL)),
            _weight_chunk_spec((D_MODEL, D_FF)),
            _weight_chunk_spec((D_MODEL, D_FF)),
            _weight_chunk_spec((D_FF, D_MODEL)),
            _resident((1, D_MODEL)),
        ],
        out_specs=pl.BlockSpec((tm, D_MODEL), row_blk),
        out_shape=jax.ShapeDtypeStruct((rows, D_MODEL), F32),
        scratch_shapes=[
            pltpu.VMEM((SSM_WIDTH, SSM_WIDTH), BF16),
            pltpu.VMEM((D_MODEL, D_MODEL), BF16),
            pltpu.VMEM((D_MODEL, D_FF), BF16),
            pltpu.VMEM((D_MODEL, D_FF), BF16),
            pltpu.VMEM((D_FF, D_MODEL), BF16),
            pltpu.VMEM((tm, D_FF), BF16),
            pltpu.VMEM((2, OCTETS, tm // SUB * TOK_PITCH, LANES), F32),
        ],
        compiler_params=pltpu.CompilerParams(
            dimension_semantics=("arbitrary",), vmem_limit_bytes=VMEM_LIMIT),
        name="mix_ffn",
    )(h1, ret, y, glu_w, glu_b, ssm_norm_w, w_out, n3w, wg, wu, wd, fnw)


def _rope_tables(n_pos):
    freqs = 1.0 / (ROPE_BASE ** (np.arange(0, HEAD_DIM, 2, dtype=np.float64) / HEAD_DIM))
    ang = np.arange(n_pos, dtype=np.float64)[:, None] * freqs[None, :]
    cos = np.cos(ang)
    sin = np.sin(ang)
    cos = np.concatenate([cos, cos], axis=-1).astype(np.float32)
    sin = np.concatenate([-sin, sin], axis=-1).astype(np.float32)
    split = lambda t: (jnp.asarray(t[:N_META]), jnp.asarray(t[N_META:]))
    return split(cos), split(sin)


def kernel(x, meta_tokens, ffn1_norm_w, ffn1_w_gate, ffn1_w_up, ffn1_w_down, mix_norm_w, w_in,
           ret_norm_w, ssm_lambda_re, ssm_lambda_im, ssm_log_dt, ssm_b_re, ssm_b_im, ssm_c_re,
           ssm_c_im, ssm_d, ssm_glu_w, ssm_glu_b, ssm_norm_w, w_out, ffn2_norm_w, ffn2_w_gate,
           ffn2_w_up, ffn2_w_down, final_norm_w):
    bsz, seq, _ = x.shape
    assert ffn1_norm_w.shape[0] == 1, "single layer only"
    tm = 512
    ret_rows = 2048
    s5_rows = 4096
    assert seq % tm == 0 and seq % ret_rows == 0 and seq % s5_rows == 0

    row = lambda a: a.reshape(1, -1)
    l = 0
    (cos_m, cos), (sin_m, sin) = _rope_tables(N_META + seq)

    h1, q, k, v, g, u, km, vm, um = _ffn_inproj(
        x.reshape(bsz * seq, D_MODEL), meta_tokens, cos, sin, cos_m, sin_m,
        row(ffn1_norm_w[l]), ffn1_w_gate[l], ffn1_w_up[l], ffn1_w_down[l],
        row(mix_norm_w[l]), w_in[l], tm=tm)

    ret = _retention(q, k, v, g, km, vm, row(ret_norm_w[l]), bsz, seq, ret_rows)

    params = _s5_param_rows(ssm_lambda_re[l], ssm_lambda_im[l], ssm_log_dt[l], ssm_b_re[l],
                            ssm_b_im[l], ssm_c_re[l], ssm_c_im[l], ssm_d[l])
    y = _s5(u, um, params, bsz, seq, s5_rows)

    out = _mix_ffn(h1, ret, y, ssm_glu_w[l], row(ssm_glu_b[l]), row(ssm_norm_w[l]),
                   w_out[l], row(ffn2_norm_w[l]), ffn2_w_gate[l], ffn2_w_up[l],
                   ffn2_w_down[l], row(final_norm_w), tm=tm)
    return out.reshape(bsz, seq, D_MODEL)
```

```python
import functools
import math

import jax
import jax.numpy as jnp
import numpy as np
from jax import lax
from jax.experimental import pallas as pl
from jax.experimental.pallas import tpu as pltpu

D_MODEL = 1024
N_META = 16
RET_HEADS = 4
HEAD_DIM = 128
RET_WIDTH = RET_HEADS * HEAD_DIM
SSM_WIDTH = 512
SSM_GROUP = 16
SSM_GROUPS = SSM_WIDTH // SSM_GROUP
SSM_STATE = 64
RET_BLOCK = 256
D_FF = 2816
FFN_RES = 0.5
ROPE_BASE = 10000.0
EPS = 1e-6
IN_PROJ = 4 * RET_WIDTH + SSM_WIDTH

LANES = 128
MXU_DIM = 256
OCTETS = SSM_WIDTH // LANES
GROUPS_PER_OCTET = LANES // SSM_GROUP
SUB = N_META
OCT_STATE = GROUPS_PER_OCTET * SSM_STATE
GROUP_LANES = SUB * SSM_GROUP
PAIRS = SSM_GROUPS // 2
PAIR_STATE = 2 * SSM_STATE
S5_SLOTS = 4
TOK_PITCH = 24
FF_CHUNK = 256
WEIGHT_CHUNKS = 8
VMEM_LIMIT = 56 * 1024 * 1024

F32 = jnp.float32
BF16 = jnp.bfloat16


def _rms(x, w):
    return x * lax.rsqrt(jnp.mean(x * x, axis=-1, keepdims=True) + EPS) * w


def _swiglu_half_step(h, n, wg_ref, wu_ref, wd_ref, acc_ref):
    for lo in range(0, D_FF, FF_CHUNK):
        sl = slice(lo, min(lo + FF_CHUNK, D_FF))
        g = jnp.dot(n, wg_ref[:, sl], preferred_element_type=F32)
        u = jnp.dot(n, wu_ref[:, sl], preferred_element_type=F32)
        acc_ref[:, sl] = (g * jax.nn.sigmoid(g) * u).astype(BF16)
    return h + FFN_RES * jnp.dot(acc_ref[...], wd_ref[...], preferred_element_type=F32)


def _rope(x, cos, sin_signed):
    return x * cos + pltpu.roll(x, HEAD_DIM // 2, axis=1) * sin_signed


def _lane_block(shape, width):
    return lax.broadcasted_iota(jnp.int32, shape, 1) // width


def _transpose_lane_blocks(v):
    n = LANES // SSM_GROUP
    assert len(v) == n == GROUPS_PER_OCTET
    blk = _lane_block(v[0].shape, SSM_GROUP)
    d = n // 2
    while d:
        low = (blk & d) == 0
        nxt = list(v)
        for a in range(n):
            if a & d == 0:
                nxt[a] = jnp.where(low, v[a], pltpu.roll(v[a + d], d * SSM_GROUP, axis=1))
                nxt[a + d] = jnp.where(low, pltpu.roll(v[a], LANES - d * SSM_GROUP, axis=1), v[a + d])
        v = nxt
        d //= 2
    return v


def _to_group_dense(tok_ref, out_ref, nsub, row0=0):
    per_half = LANES // SSM_GROUP
    for o in range(OCTETS):
        for hh in range(GROUP_LANES // LANES):
            slabs = [tok_ref[o, pl.ds(hh * per_half + jj, nsub, stride=SUB), :]
                     for jj in range(per_half)]
            for g8, rows in enumerate(_transpose_lane_blocks(slabs)):
                out_ref[o * GROUPS_PER_OCTET + g8, row0:row0 + nsub,
                        hh * LANES:(hh + 1) * LANES] = rows


def _to_token_major(gd_ref, tok_ref, nsub, fn):
    per_half = LANES // SSM_GROUP
    for o in range(OCTETS):
        for hh in range(GROUP_LANES // LANES):
            rows = [gd_ref[o * GROUPS_PER_OCTET + g8, :, hh * LANES:(hh + 1) * LANES]
                    for g8 in range(GROUPS_PER_OCTET)]
            for jj, slab in enumerate(_transpose_lane_blocks(rows)):
                tok_ref[o, pl.ds(hh * per_half + jj, nsub, stride=TOK_PITCH), :] = fn(slab)


def _token_major_rows(tok_ref, nsub):
    return jnp.concatenate(
        [jnp.concatenate([tok_ref[o, c * TOK_PITCH:c * TOK_PITCH + SUB, :] for c in range(nsub)],
                         axis=0) for o in range(OCTETS)], axis=-1)


def _cast_weight_chunks(step, pairs):
    for src_ref, dst_ref in pairs:
        rows = src_ref.shape[0]
        r0 = pl.multiple_of(step * rows, rows)
        dst_ref[pl.ds(r0, rows), :] = src_ref[...].astype(BF16)


def _weight_chunk_spec(shape):
    rows = shape[0] // WEIGHT_CHUNKS
    assert rows * WEIGHT_CHUNKS == shape[0] and rows % 16 == 0, shape
    return pl.BlockSpec((rows, shape[1]), lambda i: (jnp.minimum(i, WEIGHT_CHUNKS - 1), 0))


def _tile_index(i):
    return jnp.maximum(i - WEIGHT_CHUNKS, 0)


def _ffn_inproj_kernel(x_ref, cos_ref, sin_ref, xm_ref, cosm_ref, sinm_ref, n1w_ref, wg32_ref,
                       wu32_ref, wd32_ref, n2w_ref, win32_ref,
                       h1_ref, q_ref, k_ref, v_ref, g_ref, u_ref, km_ref, vm_ref, um_ref,
                       wg_ref, wu_ref, wd_ref, win_ref, acc_ref, accm_ref, utok_ref, utokm_ref):
    step = pl.program_id(0)
    k_scale = HEAD_DIM ** -0.5

    def ffn_norm(h):
        return _rms(h, n1w_ref[...]).astype(BF16)

    def mix_norm(h1):
        return _rms(h1, n2w_ref[...]).astype(BF16)

    def in_proj(n, part):
        return jnp.dot(n, win_ref[:, part * RET_WIDTH:(part + 1) * RET_WIDTH],
                       preferred_element_type=F32)

    def rope_heads(p, cos, sin, scale, out_ref):
        for h in range(RET_HEADS):
            hs = slice(h * HEAD_DIM, (h + 1) * HEAD_DIM)
            y = _rope(p[:, hs], cos, sin)
            out_ref[:, hs] = (y if scale is None else y * scale).astype(BF16)

    def split_octets(p, tok_ref):
        for o in range(OCTETS):
            tok_ref[o] = p[:, o * LANES:(o + 1) * LANES]

    @pl.when(step < WEIGHT_CHUNKS)
    def _():
        _cast_weight_chunks(step, ((wg32_ref, wg_ref), (wu32_ref, wu_ref), (wd32_ref, wd_ref),
                                   (win32_ref, win_ref)))

    @pl.when(step == WEIGHT_CHUNKS)
    def _():
        xm = xm_ref[...]
        h1 = _swiglu_half_step(xm, ffn_norm(xm), wg_ref, wu_ref, wd_ref, accm_ref)
        n = mix_norm(h1)
        rope_heads(in_proj(n, 1), cosm_ref[...], sinm_ref[...], k_scale, km_ref)
        vm_ref[...] = in_proj(n, 2).astype(BF16)
        split_octets(in_proj(n, 4), utokm_ref)
        um_ref[...] = jnp.zeros(um_ref.shape, F32)
        _to_group_dense(utokm_ref, um_ref, 1, row0=7)

    @pl.when(step >= WEIGHT_CHUNKS)
    def _():
        x = x_ref[...]
        h1 = _swiglu_half_step(x, ffn_norm(x), wg_ref, wu_ref, wd_ref, acc_ref)
        h1_ref[...] = h1
        n = mix_norm(h1)
        split_octets(in_proj(n, 4), utok_ref)
        _to_group_dense(utok_ref, u_ref, u_ref.shape[1])
        cos = cos_ref[...]
        sin = sin_ref[...]
        rope_heads(in_proj(n, 0), cos, sin, None, q_ref)
        rope_heads(in_proj(n, 1), cos, sin, k_scale, k_ref)
        v_ref[...] = in_proj(n, 2).astype(BF16)
        g_ref[...] = in_proj(n, 3)


def _resident(shape):
    nd = len(shape)
    return pl.BlockSpec(shape, lambda *_: (0,) * nd, pipeline_mode=pl.Buffered(1))


def _ffn_inproj(x2, xm, cos, sin, cos_m, sin_m, n1w, wg, wu, wd, n2w, w_in, tm):
    rows = x2.shape[0]
    pos_blocks = cos.shape[0] // tm
    row_blk = lambda i: (_tile_index(i), 0)
    pos_blk = lambda i: (_tile_index(i) % pos_blocks, 0)
    out_shape = (
        jax.ShapeDtypeStruct((rows, D_MODEL), F32),
        jax.ShapeDtypeStruct((rows, RET_WIDTH), BF16),
        jax.ShapeDtypeStruct((rows, RET_WIDTH), BF16),
        jax.ShapeDtypeStruct((rows, RET_WIDTH), BF16),
        jax.ShapeDtypeStruct((rows, RET_WIDTH), F32),
        jax.ShapeDtypeStruct((SSM_GROUPS, rows // SUB, GROUP_LANES), F32),
        jax.ShapeDtypeStruct((N_META, RET_WIDTH), BF16),
        jax.ShapeDtypeStruct((N_META, RET_WIDTH), BF16),
        jax.ShapeDtypeStruct((SSM_GROUPS, 8, GROUP_LANES), F32),
    )
    return pl.pallas_call(
        _ffn_inproj_kernel,
        grid=(WEIGHT_CHUNKS + rows // tm,),
        in_specs=[
            pl.BlockSpec((tm, D_MODEL), row_blk),
            pl.BlockSpec((tm, HEAD_DIM), pos_blk),
            pl.BlockSpec((tm, HEAD_DIM), pos_blk),
            _resident((N_META, D_MODEL)),
            _resident((N_META, HEAD_DIM)),
            _resident((N_META, HEAD_DIM)),
            _resident((1, D_MODEL)),
            _weight_chunk_spec((D_MODEL, D_FF)),
            _weight_chunk_spec((D_MODEL, D_FF)),
            _weight_chunk_spec((D_FF, D_MODEL)),
            _resident((1, D_MODEL)),
            _weight_chunk_spec((D_MODEL, IN_PROJ)),
        ],
        out_specs=(
            pl.BlockSpec((tm, D_MODEL), row_blk),
            pl.BlockSpec((tm, RET_WIDTH), row_blk),
            pl.BlockSpec((tm, RET_WIDTH), row_blk),
            pl.BlockSpec((tm, RET_WIDTH), row_blk),
            pl.BlockSpec((tm, RET_WIDTH), row_blk),
            pl.BlockSpec((SSM_GROUPS, tm // SUB, GROUP_LANES), lambda i: (0, _tile_index(i), 0)),
            pl.BlockSpec((N_META, RET_WIDTH), lambda i: (0, 0)),
            pl.BlockSpec((N_META, RET_WIDTH), lambda i: (0, 0)),
            pl.BlockSpec((SSM_GROUPS, 8, GROUP_LANES), lambda i: (0, 0, 0)),
        ),
        out_shape=out_shape,
        scratch_shapes=[
            pltpu.VMEM((D_MODEL, D_FF), BF16),
            pltpu.VMEM((D_MODEL, D_FF), BF16),
            pltpu.VMEM((D_FF, D_MODEL), BF16),
            pltpu.VMEM((D_MODEL, IN_PROJ), BF16),
            pltpu.VMEM((tm, D_FF), BF16),
            pltpu.VMEM((N_META, D_FF), BF16),
            pltpu.VMEM((OCTETS, tm, LANES), F32),
            pltpu.VMEM((OCTETS, N_META, LANES), F32),
        ],
        compiler_params=pltpu.CompilerParams(
            dimension_semantics=("arbitrary",), vmem_limit_bytes=VMEM_LIMIT),
        name="ffn_inproj",
    )(x2, cos, sin, xm, cos_m, sin_m, n1w, wg, wu, wd, n2w, w_in)


def _retention_kernel(q_ref, k_ref, v_ref, g_ref, km_ref, vm_ref, mask_ref, wq_ref, wk_ref, wm_ref,
                      gc_ref, nw_ref, o_ref, state_ref, *, rows):
    tn = (((0,), (0,)), ((), ()))
    nt = (((1,), (1,)), ((), ()))

    @pl.when(pl.program_id(1) == 0)
    def _():
        for h in range(RET_HEADS):
            hs = slice(h * HEAD_DIM, (h + 1) * HEAD_DIM)
            kw = (km_ref[:, hs].astype(F32) * wm_ref[h]).astype(BF16)
            state_ref[h] = lax.dot_general(kw, vm_ref[:, hs], tn, preferred_element_type=F32)

    def chunk(i, carry):
        r0 = pl.multiple_of(i * RET_BLOCK, RET_BLOCK)
        rs = pl.ds(r0, RET_BLOCK)
        for h in range(RET_HEADS):
            hs = slice(h * HEAD_DIM, (h + 1) * HEAD_DIM)
            qh = q_ref[rs, hs]
            kh = k_ref[rs, hs]
            vh = v_ref[rs, hs]
            st = state_ref[h]
            s = lax.dot_general(qh, kh, nt, preferred_element_type=F32) * mask_ref[h]
            o = (jnp.dot(s.astype(BF16), vh, preferred_element_type=F32)
                 + jnp.dot(qh, st.astype(BF16), preferred_element_type=F32) * wq_ref[h])
            kw = (kh.astype(F32) * wk_ref[h]).astype(BF16)
            state_ref[h] = gc_ref[h] * st + lax.dot_general(kw, vh, tn, preferred_element_type=F32)
            mu = jnp.mean(o, axis=-1, keepdims=True)
            d = o - mu
            var = jnp.mean(d * d, axis=-1, keepdims=True)
            y = d * lax.rsqrt(var + EPS) * nw_ref[:, hs]
            gate = g_ref[rs, hs]
            o_ref[rs, hs] = (gate * jax.nn.sigmoid(gate) * y).astype(BF16)
        return carry

    lax.fori_loop(0, rows // RET_BLOCK, chunk, 0, unroll=4)


def _retention_tables():
    log_g = np.log(1.0 - 2.0 ** (-5.0 - np.arange(RET_HEADS, dtype=np.float64)))
    i = np.arange(RET_BLOCK)
    diff = i[:, None] - i[None, :]
    mask = np.where(diff[None] >= 0, np.exp(log_g[:, None, None] * np.maximum(diff, 0)[None]), 0.0)
    pos = np.arange(RET_BLOCK, dtype=np.float64)
    full = lambda w: np.broadcast_to(w[:, :, None], w.shape + (HEAD_DIM,))
    w_q = full(np.exp(log_g[:, None] * (pos + 1.0)[None]))
    w_k = full(np.exp(log_g[:, None] * (RET_BLOCK - 1 - pos)[None]))
    w_m = full(np.exp(log_g[:, None] * (N_META - 1 - np.arange(N_META, dtype=np.float64))[None]))
    g_c = np.broadcast_to(np.exp(log_g * RET_BLOCK)[:, None, None], (RET_HEADS, HEAD_DIM, HEAD_DIM))
    return tuple(jnp.asarray(t, dtype=F32) for t in (mask, w_q, w_k, w_m, g_c))


def _retention(q, k, v, g, km, vm, ret_norm_w, bsz, seq, rows):
    mask, w_q, w_k, w_m, g_c = _retention_tables()
    steps = seq // rows
    blk = lambda b, c: (b * steps + c, 0)
    full3 = lambda a: pl.BlockSpec(a.shape, lambda b, c: (0, 0, 0))
    full2 = lambda a: pl.BlockSpec(a.shape, lambda b, c: (0, 0))
    return pl.pallas_call(
        functools.partial(_retention_kernel, rows=rows),
        grid=(bsz, steps),
        in_specs=[pl.BlockSpec((rows, RET_WIDTH), blk)] * 4
        + [full2(km), full2(vm), full3(mask), full3(w_q), full3(w_k), full3(w_m), full3(g_c),
           full2(ret_norm_w)],
        out_specs=pl.BlockSpec((rows, RET_WIDTH), blk),
        out_shape=jax.ShapeDtypeStruct((bsz * seq, RET_WIDTH), BF16),
        scratch_shapes=[pltpu.VMEM((RET_HEADS, HEAD_DIM, HEAD_DIM), F32)],
        compiler_params=pltpu.CompilerParams(
            dimension_semantics=("arbitrary", "arbitrary"), vmem_limit_bytes=VMEM_LIMIT),
        name="retention",
    )(q, k, v, g, km, vm, mask, w_q, w_k, w_m, g_c, ret_norm_w)


def _shift_lanes_zero_fill(halves, shift, lane):
    h0, h1 = halves
    whole, s = divmod(shift, LANES)
    r0 = h0 if s == 0 else pltpu.roll(h0, s, axis=1)
    zero = jnp.zeros_like(h0)
    if whole == 1:
        return zero, jnp.where(lane >= s, r0, zero)
    r1 = h1 if s == 0 else pltpu.roll(h1, s, axis=1)
    return jnp.where(lane >= s, r0, zero), jnp.where(lane >= s, r1, r0)


def _s5_assemble_octet(o, vec_ref, mat_ref,
                       toep_ref, win_ref, wout_ref, cpw_ref, pow_ref, xall_ref, wt_ref):
    vec = vec_ref[o]
    lre = vec[0:1]
    lim = vec[1:2]
    dt = jnp.exp(vec[2:3])
    n_pow = pow_ref.shape[1]
    ell = lax.broadcasted_iota(jnp.int32, (n_pow, OCT_STATE), 0).astype(F32)
    mag = jnp.exp(ell * (lre * dt))
    ang = ell * (lim * dt)
    pow_ref[0] = mag * jnp.cos(ang)
    pow_ref[1] = mag * jnp.sin(ang)
    ell = ((lax.broadcasted_iota(jnp.int32, (8, OCT_STATE), 0) + 1) * SUB).astype(F32)
    mag = jnp.exp(ell * (lre * dt))
    ang = ell * (lim * dt)
    cp_re = mag * jnp.cos(ang)
    cp_im = mag * jnp.sin(ang)
    a_re = pow_ref[0, 1:2, :]
    a_im = pow_ref[1, 1:2, :]
    den = lre * lre + lim * lim
    num_re = a_re - 1.0
    coef_re = (num_re * lre + a_im * lim) / den
    coef_im = (a_im * lre - num_re * lim) / den
    btr = mat_ref[o, 0]
    bti = mat_ref[o, 1]
    bbar_re = coef_re * btr - coef_im * bti
    bbar_im = coef_re * bti + coef_im * btr
    ctr = mat_ref[o, 2]
    cti = mat_ref[o, 3]

    pairs_per_octet = GROUPS_PER_OCTET // 2
    pair_lane_group = _lane_block((SSM_GROUP, PAIR_STATE), SSM_STATE)
    for l in range(SUB):
        pr = pow_ref[0, l:l + 1, :]
        pi = pow_ref[1, l:l + 1, :]
        x_re = pr * bbar_re - pi * bbar_im
        x_im = pr * bbar_im + pi * bbar_re
        rows = slice(l * SSM_GROUP, (l + 1) * SSM_GROUP)
        xall_ref[0, rows, :] = x_re
        xall_ref[1, rows, :] = x_im
        pr1 = pow_ref[0, l + 1:l + 2, :]
        pi1 = pow_ref[1, l + 1:l + 2, :]
        w_re = pr1 * ctr - pi1 * cti
        w_im = pr1 * cti + pi1 * ctr
        j = SUB - 1 - l
        for pp in range(pairs_per_octet):
            sl = slice(pp * PAIR_STATE, (pp + 1) * PAIR_STATE)
            pair = o * pairs_per_octet + pp
            for gi in range(2):
                keep = pair_lane_group == gi
                in_rows = slice(gi * GROUP_LANES + j * SSM_GROUP, gi * GROUP_LANES + (j + 1) * SSM_GROUP)
                out_rows = slice(gi * GROUP_LANES + l * SSM_GROUP, gi * GROUP_LANES + (l + 1) * SSM_GROUP)
                for part, x, w in ((0, x_re, w_re), (1, x_im, -w_im)):
                    cols = slice(part * PAIR_STATE, (part + 1) * PAIR_STATE)
                    win_ref[pair, in_rows, cols] = jnp.where(keep, x[:, sl], 0.0).astype(BF16)
                    wt_ref[pp, out_rows, cols] = jnp.where(keep, w[:, sl], 0.0)
    for pp in range(pairs_per_octet):
        pair = o * pairs_per_octet + pp
        wout_ref[pair] = wt_ref[pp].T.astype(BF16)
        sl = slice(pp * PAIR_STATE, (pp + 1) * PAIR_STATE)
        cpw_ref[pair, 0] = cp_re[:, sl]
        cpw_ref[pair, 1] = cp_im[:, sl]

    row = lax.broadcasted_iota(jnp.int32, (LANES, OCT_STATE), 0)
    col = lax.broadcasted_iota(jnp.int32, (LANES, OCT_STATE), 1)
    same_group = (row // SSM_GROUP) == (col // SSM_STATE)

    def block_diag_t(x):
        return jnp.where(same_group, jnp.concatenate([x] * GROUPS_PER_OCTET, axis=0), 0.0).T

    k_all = (jnp.dot(xall_ref[0], block_diag_t(ctr), precision=lax.Precision.HIGHEST,
                     preferred_element_type=F32)
             - jnp.dot(xall_ref[1], block_diag_t(cti), precision=lax.Precision.HIGHEST,
                       preferred_element_type=F32))
    per_half = LANES // SSM_GROUP
    blk = _lane_block((SSM_GROUP, LANES), SSM_GROUP)
    lane = lax.broadcasted_iota(jnp.int32, (SSM_GROUP, LANES), 1)
    for g8 in range(GROUPS_PER_OCTET):
        strip = []
        for hh in range(GROUP_LANES // LANES):
            acc = jnp.zeros((SSM_GROUP, LANES), F32)
            for ll in range(per_half):
                lag = hh * per_half + ll
                k_lag = k_all[lag * SSM_GROUP:(lag + 1) * SSM_GROUP, :]
                shift = ((ll - g8) * SSM_GROUP) % LANES
                rolled = k_lag if shift == 0 else pltpu.roll(k_lag, shift, axis=1)
                acc = jnp.where(blk == ll, rolled, acc)
            strip.append(acc)
        for j in range(SUB):
            h0, h1 = _shift_lanes_zero_fill(strip, j * SSM_GROUP, lane)
            rows = slice(j * SSM_GROUP, (j + 1) * SSM_GROUP)
            toep_ref[o * GROUPS_PER_OCTET + g8, rows, 0:LANES] = h0.astype(BF16)
            toep_ref[o * GROUPS_PER_OCTET + g8, rows, LANES:2 * LANES] = h1.astype(BF16)


def _s5_state_scan(st_ref, zb_ref, cpow_ref, nsc):
    tiles = nsc // 8
    width = st_ref.shape[1] // 2
    re = slice(0, width)
    im = slice(width, 2 * width)
    rowmod = lax.broadcasted_iota(jnp.int32, (8, width), 0)
    src, dst = st_ref, zb_ref
    for s in (1, 2, 4):
        keep = rowmod >= s
        c_re = jnp.concatenate([jnp.where(keep, cpow_ref[0, s - 1:s, :], 0.0)] * tiles, axis=0)
        c_im = jnp.concatenate([jnp.where(keep, cpow_ref[1, s - 1:s, :], 0.0)] * tiles, axis=0)
        sh_re = src[pl.ds(8 - s, nsc), re]
        sh_im = src[pl.ds(8 - s, nsc), im]
        dst[8:8 + nsc, re] = src[8:8 + nsc, re] + c_re * sh_re - c_im * sh_im
        dst[8:8 + nsc, im] = src[8:8 + nsc, im] + c_re * sh_im + c_im * sh_re
        src, dst = dst, src
    ca_re = cpow_ref[0]
    ca_im = cpow_ref[1]
    c_re = st_ref[7:8, re]
    c_im = st_ref[7:8, im]
    for t in range(tiles):
        rows = slice(8 + 8 * t, 16 + 8 * t)
        cb_re = jnp.broadcast_to(c_re, (8, width))
        cb_im = jnp.broadcast_to(c_im, (8, width))
        p_re = src[rows, re] + ca_re * cb_re - ca_im * cb_im
        p_im = src[rows, im] + ca_re * cb_im + ca_im * cb_re
        st_ref[rows, re] = p_re
        st_ref[rows, im] = p_im
        c_re = p_re[7:8]
        c_im = p_im[7:8]


def _s5_kernel(u_ref, um_ref, vec_ref, mat_ref, d_ref,
               y_ref, toep_ref, win_ref, wout_ref, cpw_ref, pow_ref, xall_ref, wt_ref, lhs_ref,
               st_ref, zb_ref, carry_ref, *, nsc):
    first_call_step = (pl.program_id(0) == 0) & (pl.program_id(1) == 0)
    r = pl.program_id(1)

    @pl.when(first_call_step)
    def _():
        zb_ref[:, 0:8, :] = jnp.zeros((zb_ref.shape[0], 8, zb_ref.shape[2]), F32)

        def octet(o, carry):
            _s5_assemble_octet(o, vec_ref, mat_ref,
                               toep_ref, win_ref, wout_ref, cpw_ref, pow_ref, xall_ref, wt_ref)
            return carry
        lax.fori_loop(0, OCTETS, octet, 0)

    @pl.when(r == 0)
    def _():
        def meta_state(pair, carry):
            meta = jnp.concatenate([um_ref[2 * pair], um_ref[2 * pair + 1]], axis=1).astype(BF16)
            carry_ref[pair] = jnp.dot(meta, win_ref[pair], preferred_element_type=F32)
            return carry
        lax.fori_loop(0, PAIRS, meta_state, 0)

    lo = slice(0, GROUP_LANES)
    hi = slice(GROUP_LANES, 2 * GROUP_LANES)
    slots = st_ref.shape[0]

    def pair_block(k, carry):
        pairs = [slots * k + s for s in range(slots)]
        for s, pair in enumerate(pairs):
            lhs_ref[s, :, lo] = u_ref[2 * pair].astype(BF16)
            lhs_ref[s, :, hi] = u_ref[2 * pair + 1].astype(BF16)
            st_ref[s, 0:8, :] = carry_ref[pair]
            st_ref[s, 8:8 + nsc, :] = jnp.dot(lhs_ref[s], win_ref[pair], preferred_element_type=F32)
        for s, pair in enumerate(pairs):
            _s5_state_scan(st_ref.at[s], zb_ref.at[s], cpw_ref.at[pair], nsc)
            carry_ref[pair] = st_ref[s, nsc:nsc + 8, :]
        for s, pair in enumerate(pairs):
            s_in = st_ref[s, pl.ds(7, nsc), :].astype(BF16)
            y_state = jnp.dot(s_in, wout_ref[pair], preferred_element_type=F32)
            d = d_ref[pair]
            for gi, cols in ((0, lo), (1, hi)):
                g = 2 * pair + gi
                y = (jnp.dot(lhs_ref[s, :, cols], toep_ref[g], preferred_element_type=F32)
                     + y_state[:, cols] + d[:, cols] * u_ref[g])
                y_ref[g] = y
        return carry

    lax.fori_loop(0, PAIRS // slots, pair_block, 0)


def _s5_param_rows(lam_re, lam_im, log_dt, b_re, b_im, c_re, c_im, d):
    go = GROUPS_PER_OCTET
    lanes = lambda a: a.reshape(OCTETS, OCT_STATE)
    ldt = jnp.broadcast_to(log_dt[:, None], (SSM_GROUPS, SSM_STATE))
    bt = lambda a: jnp.transpose(a.reshape(OCTETS, go, SSM_STATE, SSM_GROUP),
                                 (0, 3, 1, 2)).reshape(OCTETS, SSM_GROUP, OCT_STATE)
    ct = lambda a: jnp.transpose(a.reshape(OCTETS, go, SSM_GROUP, SSM_STATE),
                                 (0, 2, 1, 3)).reshape(OCTETS, SSM_GROUP, OCT_STATE)
    vec = jnp.stack([lanes(lam_re), lanes(lam_im), lanes(ldt)], axis=1)
    mat = jnp.stack([bt(b_re), bt(b_im), ct(c_re), ct(c_im)], axis=1)
    d_pairs = jnp.broadcast_to(d.reshape(PAIRS, 2, 1, SSM_GROUP),
                               (PAIRS, 2, SUB, SSM_GROUP)).reshape(PAIRS, 1, 2 * GROUP_LANES)
    return vec, mat, d_pairs


def _s5(u, um, params, bsz, seq, rows):
    nsc = rows // SUB
    steps = seq // rows
    n_pow = SUB + 8
    full = lambda a: pl.BlockSpec(a.shape, lambda b, r: (0,) * a.ndim)
    tok_blk = pl.BlockSpec((SSM_GROUPS, nsc, GROUP_LANES), lambda b, r: (0, b * steps + r, 0))
    return pl.pallas_call(
        functools.partial(_s5_kernel, nsc=nsc),
        grid=(bsz, steps),
        in_specs=[tok_blk, full(um)] + [full(p) for p in params],
        out_specs=tok_blk,
        out_shape=jax.ShapeDtypeStruct(u.shape, F32),
        scratch_shapes=[
            pltpu.VMEM((SSM_GROUPS, GROUP_LANES, GROUP_LANES), BF16),
            pltpu.VMEM((PAIRS, 2 * GROUP_LANES, 2 * PAIR_STATE), BF16),
            pltpu.VMEM((PAIRS, 2 * PAIR_STATE, 2 * GROUP_LANES), BF16),
            pltpu.VMEM((PAIRS, 2, 8, PAIR_STATE), F32),
            pltpu.VMEM((2, n_pow, OCT_STATE), F32),
            pltpu.VMEM((2, SUB * SSM_GROUP, OCT_STATE), F32),
            pltpu.VMEM((GROUPS_PER_OCTET // 2, 2 * GROUP_LANES, 2 * PAIR_STATE), F32),
            pltpu.VMEM((S5_SLOTS, nsc, 2 * GROUP_LANES), BF16),
            pltpu.VMEM((S5_SLOTS, nsc + 8, 2 * PAIR_STATE), F32),
            pltpu.VMEM((S5_SLOTS, nsc + 8, 2 * PAIR_STATE), F32),
            pltpu.VMEM((PAIRS, 8, 2 * PAIR_STATE), F32),
        ],
        compiler_params=pltpu.CompilerParams(
            dimension_semantics=("arbitrary", "arbitrary"), vmem_limit_bytes=VMEM_LIMIT),
        name="s5_mixer",
    )(u, um, *params)


def _mix_ffn_kernel(h1_ref, ret_ref, ynext_ref, gw32_ref, gb_ref, snw_ref, wo32_ref, n3w_ref,
                    wg32_ref, wu32_ref, wd32_ref, fnw_ref, out_ref,
                    gw_ref, wo_ref, wg_ref, wu_ref, wd_ref, acc_ref, ytok_ref):
    step = pl.program_id(0)
    nsub = ynext_ref.shape[1]

    @pl.when(step < WEIGHT_CHUNKS)
    def _():
        _cast_weight_chunks(step, ((gw32_ref, gw_ref), (wo32_ref, wo_ref), (wg32_ref, wg_ref),
                                   (wu32_ref, wu_ref), (wd32_ref, wd_ref)))

    @pl.when(step == WEIGHT_CHUNKS - 1)
    def _():
        _to_token_major(ynext_ref, ytok_ref.at[0], nsub, jax.nn.gelu)

    @pl.when(step >= WEIGHT_CHUNKS)
    def _():
        slot = lax.rem(step - WEIGHT_CHUNKS, 2)
        y = _token_major_rows(ytok_ref.at[slot], nsub)
        _to_token_major(ynext_ref, ytok_ref.at[1 - slot], nsub, jax.nn.gelu)
        z = y * jax.nn.sigmoid(jnp.dot(y.astype(BF16), gw_ref[...], preferred_element_type=F32)
                               + gb_ref[...])
        ssm = _rms(z, snw_ref[...]).astype(BF16)
        mixed = jnp.dot(jnp.concatenate([ret_ref[...], ssm], axis=-1), wo_ref[...],
                        preferred_element_type=F32)
        h2 = h1_ref[...] + mixed
        h3 = _swiglu_half_step(h2, _rms(h2, n3w_ref[...]).astype(BF16), wg_ref, wu_ref, wd_ref,
                               acc_ref)
        out_ref[...] = _rms(h3, fnw_ref[...])


def _mix_ffn(h1, ret, y, glu_w, glu_b, ssm_norm_w, w_out, n3w, wg, wu, wd, fnw, tm):
    rows = h1.shape[0]
    tiles = rows // tm
    row_blk = lambda i: (_tile_index(i), 0)
    next_blk = lambda i: (0, jnp.minimum(_tile_index(i + 1), tiles - 1), 0)
    return pl.pallas_call(
        _mix_ffn_kernel,
        grid=(WEIGHT_CHUNKS + tiles,),
        in_specs=[
            pl.BlockSpec((tm, D_MODEL), row_blk),
            pl.BlockSpec((tm, RET_WIDTH), row_blk),
            pl.BlockSpec((SSM_GROUPS, tm // SUB, GROUP_LANES), next_blk),
            _weight_chunk_spec((SSM_WIDTH, SSM_WIDTH)),
            _resident((1, SSM_WIDTH)),
            _resident((1, SSM_WIDTH)),
            _weight_chunk_spec((D_MODEL, D_MODEL)),
            _resident((1, D_MODEL)),
            _weight_chunk_spec((D_MODEL, D_FF)),
            _weight_chunk_spec((D_MODEL, D_FF)),
            _weight_chunk_spec((D_FF, D_MODEL)),
            _resident((1, D_MODEL)),
        ],
        out_specs=pl.BlockSpec((tm, D_MODEL), row_blk),
        out_shape=jax.ShapeDtypeStruct((rows, D_MODEL), F32),
        scratch_shapes=[
            pltpu.VMEM((SSM_WIDTH, SSM_WIDTH), BF16),
            pltpu.VMEM((D_MODEL, D_MODEL), BF16),
            pltpu.VMEM((D_MODEL, D_FF), BF16),
            pltpu.VMEM((D_MODEL, D_FF), BF16),
            pltpu.VMEM((D_FF, D_MODEL), BF16),
            pltpu.VMEM((tm, D_FF), BF16),
            pltpu.VMEM((2, OCTETS, tm // SUB * TOK_PITCH, LANES), F32),
        ],
        compiler_params=pltpu.CompilerParams(
            dimension_semantics=("arbitrary",), vmem_limit_bytes=VMEM_LIMIT),
        name="mix_ffn",
    )(h1, ret, y, glu_w, glu_b, ssm_norm_w, w_out, n3w, wg, wu, wd, fnw)


def _rope_tables(n_pos):
    freqs = 1.0 / (ROPE_BASE ** (np.arange(0, HEAD_DIM, 2, dtype=np.float64) / HEAD_DIM))
    ang = np.arange(n_pos, dtype=np.float64)[:, None] * freqs[None, :]
    cos = np.cos(ang)
    sin = np.sin(ang)
    cos = np.concatenate([cos, cos], axis=-1).astype(np.float32)
    sin = np.concatenate([-sin, sin], axis=-1).astype(np.float32)
    split = lambda t: (jnp.asarray(t[:N_META]), jnp.asarray(t[N_META:]))
    return split(cos), split(sin)


def kernel(x, meta_tokens, ffn1_norm_w, ffn1_w_gate, ffn1_w_up, ffn1_w_down, mix_norm_w, w_in,
           ret_norm_w, ssm_lambda_re, ssm_lambda_im, ssm_log_dt, ssm_b_re, ssm_b_im, ssm_c_re,
           ssm_c_im, ssm_d, ssm_glu_w, ssm_glu_b, ssm_norm_w, w_out, ffn2_norm_w, ffn2_w_gate,
           ffn2_w_up, ffn2_w_down, final_norm_w):
    bsz, seq, _ = x.shape
    assert ffn1_norm_w.shape[0] == 1, "single layer only"
    tm = 512
    ret_rows = 2048
    s5_rows = 4096
    assert seq % tm == 0 and seq % ret_rows == 0 and seq % s5_rows == 0

    l = 0
    (cos_m, cos), (sin_m, sin) = _rope_tables(N_META + seq)

    h1, q, k, v, g, u, km, vm, um = _ffn_inproj(
        x.reshape(bsz * seq, D_MODEL), meta_tokens, cos, sin, cos_m, sin_m,
        ffn1_norm_w, ffn1_w_gate[l], ffn1_w_up[l], ffn1_w_down[l], mix_norm_w, w_in[l], tm=tm)

    ret = _retention(q, k, v, g, km, vm, ret_norm_w, bsz, seq, ret_rows)

    params = _s5_param_rows(ssm_lambda_re[l], ssm_lambda_im[l], ssm_log_dt[l], ssm_b_re[l],
                            ssm_b_im[l], ssm_c_re[l], ssm_c_im[l], ssm_d[l])
    y = _s5(u, um, params, bsz, seq, s5_rows)

    out = _mix_ffn(h1, ret, y, ssm_glu_w[l], ssm_glu_b, ssm_norm_w, w_out[l], ffn2_norm_w,
                   ffn2_w_gate[l], ffn2_w_up[l], ffn2_w_down[l], final_norm_w.reshape(1, D_MODEL),
                   tm=tm)
    return out.reshape(bsz, seq, D_MODEL)
```

```python
import functools

import jax
import jax.numpy as jnp
import numpy as np
from jax import lax
from jax.experimental import pallas as pl
from jax.experimental.pallas import tpu as pltpu

D_MODEL = 1024
N_META = 16
RET_HEADS = 4
HEAD_DIM = 128
RET_WIDTH = RET_HEADS * HEAD_DIM
SSM_WIDTH = 512
SSM_GROUP = 16
SSM_GROUPS = SSM_WIDTH // SSM_GROUP
SSM_STATE = 64
RET_BLOCK = 256
D_FF = 2816
FFN_RES = 0.5
ROPE_BASE = 10000.0
EPS = 1e-6
IN_PROJ = 4 * RET_WIDTH + SSM_WIDTH

LANES = 128
SUBLANES = 8
HEAD_ROWS = SUBLANES
ENTER_ROW = HEAD_ROWS - 1
OCTETS = SSM_WIDTH // LANES
GROUPS_PER_OCTET = LANES // SSM_GROUP
SUB = N_META
OCT_STATE = GROUPS_PER_OCTET * SSM_STATE
GROUP_LANES = SUB * SSM_GROUP
PAIRS = SSM_GROUPS // 2
PAIR_STATE = 2 * SSM_STATE
S5_SLOTS = 4
TOK_PITCH = 24
FF_CHUNK = 256
WEIGHT_CHUNKS = 8
VMEM_LIMIT = 56 * 1024 * 1024

F32 = jnp.float32
BF16 = jnp.bfloat16


def _rms(x, w):
    return x * lax.rsqrt(jnp.mean(x * x, axis=-1, keepdims=True) + EPS) * w


def _swiglu_half_step(h, n, wg_ref, wu_ref, wd_ref, acc_ref):
    for lo in range(0, D_FF, FF_CHUNK):
        sl = slice(lo, min(lo + FF_CHUNK, D_FF))
        g = jnp.dot(n, wg_ref[:, sl], preferred_element_type=F32)
        u = jnp.dot(n, wu_ref[:, sl], preferred_element_type=F32)
        acc_ref[:, sl] = (g * jax.nn.sigmoid(g) * u).astype(BF16)
    return h + FFN_RES * jnp.dot(acc_ref[...], wd_ref[...], preferred_element_type=F32)


def _rope(x, cos, sin_signed):
    return x * cos + pltpu.roll(x, HEAD_DIM // 2, axis=1) * sin_signed


def _lane_block(shape, width):
    return lax.broadcasted_iota(jnp.int32, shape, 1) // width


def _transpose_lane_blocks(v):
    n = LANES // SSM_GROUP
    assert len(v) == n == GROUPS_PER_OCTET
    blk = _lane_block(v[0].shape, SSM_GROUP)
    d = n // 2
    while d:
        low = (blk & d) == 0
        nxt = list(v)
        for a in range(n):
            if a & d == 0:
                nxt[a] = jnp.where(low, v[a], pltpu.roll(v[a + d], d * SSM_GROUP, axis=1))
                nxt[a + d] = jnp.where(low, pltpu.roll(v[a], LANES - d * SSM_GROUP, axis=1), v[a + d])
        v = nxt
        d //= 2
    return v


def _to_group_dense(tok_ref, out_ref, nsub, row0=0):
    per_half = LANES // SSM_GROUP
    for o in range(OCTETS):
        for hh in range(GROUP_LANES // LANES):
            slabs = [tok_ref[o, pl.ds(hh * per_half + jj, nsub, stride=SUB), :]
                     for jj in range(per_half)]
            for g8, rows in enumerate(_transpose_lane_blocks(slabs)):
                out_ref[o * GROUPS_PER_OCTET + g8, row0:row0 + nsub,
                        hh * LANES:(hh + 1) * LANES] = rows


def _to_token_major(gd_ref, tok_ref, nsub, fn):
    per_half = LANES // SSM_GROUP
    for o in range(OCTETS):
        for hh in range(GROUP_LANES // LANES):
            rows = [gd_ref[o * GROUPS_PER_OCTET + g8, :, hh * LANES:(hh + 1) * LANES]
                    for g8 in range(GROUPS_PER_OCTET)]
            for jj, slab in enumerate(_transpose_lane_blocks(rows)):
                tok_ref[o, pl.ds(hh * per_half + jj, nsub, stride=TOK_PITCH), :] = fn(slab)


def _token_major_rows(tok_ref, nsub):
    return jnp.concatenate(
        [jnp.concatenate([tok_ref[o, c * TOK_PITCH:c * TOK_PITCH + SUB, :] for c in range(nsub)],
                         axis=0) for o in range(OCTETS)], axis=-1)


def _cast_weight_chunks(step, pairs):
    for src_ref, dst_ref in pairs:
        rows = src_ref.shape[0]
        r0 = pl.multiple_of(step * rows, rows)
        dst_ref[pl.ds(r0, rows), :] = src_ref[...].astype(BF16)


def _weight_chunk_spec(shape):
    rows = shape[0] // WEIGHT_CHUNKS
    assert rows * WEIGHT_CHUNKS == shape[0] and rows % 16 == 0, shape
    return pl.BlockSpec((rows, shape[1]), lambda i: (jnp.minimum(i, WEIGHT_CHUNKS - 1), 0))


def _tile_index(i):
    return jnp.maximum(i - WEIGHT_CHUNKS, 0)


def _ffn_inproj_kernel(x_ref, cos_ref, sin_ref, xm_ref, cosm_ref, sinm_ref, n1w_ref, wg32_ref,
                       wu32_ref, wd32_ref, n2w_ref, win32_ref,
                       h1_ref, q_ref, k_ref, v_ref, g_ref, u_ref, km_ref, vm_ref, um_ref,
                       wg_ref, wu_ref, wd_ref, win_ref, acc_ref, accm_ref, utok_ref, utokm_ref):
    step = pl.program_id(0)
    k_scale = HEAD_DIM ** -0.5

    def ffn_norm(h):
        return _rms(h, n1w_ref[...]).astype(BF16)

    def mix_norm(h1):
        return _rms(h1, n2w_ref[...]).astype(BF16)

    def in_proj(n, part):
        return jnp.dot(n, win_ref[:, part * RET_WIDTH:(part + 1) * RET_WIDTH],
                       preferred_element_type=F32)

    def rope_heads(p, cos, sin, scale, out_ref):
        for h in range(RET_HEADS):
            hs = slice(h * HEAD_DIM, (h + 1) * HEAD_DIM)
            y = _rope(p[:, hs], cos, sin)
            out_ref[:, hs] = (y if scale is None else y * scale).astype(BF16)

    def split_octets(p, tok_ref):
        for o in range(OCTETS):
            tok_ref[o] = p[:, o * LANES:(o + 1) * LANES]

    @pl.when(step < WEIGHT_CHUNKS)
    def _():
        _cast_weight_chunks(step, ((wg32_ref, wg_ref), (wu32_ref, wu_ref), (wd32_ref, wd_ref),
                                   (win32_ref, win_ref)))

    @pl.when(step == WEIGHT_CHUNKS)
    def _():
        xm = xm_ref[...]
        h1 = _swiglu_half_step(xm, ffn_norm(xm), wg_ref, wu_ref, wd_ref, accm_ref)
        n = mix_norm(h1)
        rope_heads(in_proj(n, 1), cosm_ref[...], sinm_ref[...], k_scale, km_ref)
        vm_ref[...] = in_proj(n, 2).astype(BF16)
        split_octets(in_proj(n, 4), utokm_ref)
        um_ref[...] = jnp.zeros(um_ref.shape, F32)
        _to_group_dense(utokm_ref, um_ref, 1, row0=ENTER_ROW)

    @pl.when(step >= WEIGHT_CHUNKS)
    def _():
        x = x_ref[...]
        h1 = _swiglu_half_step(x, ffn_norm(x), wg_ref, wu_ref, wd_ref, acc_ref)
        h1_ref[...] = h1
        n = mix_norm(h1)
        split_octets(in_proj(n, 4), utok_ref)
        _to_group_dense(utok_ref, u_ref, u_ref.shape[1])
        cos = cos_ref[...]
        sin = sin_ref[...]
        rope_heads(in_proj(n, 0), cos, sin, None, q_ref)
        rope_heads(in_proj(n, 1), cos, sin, k_scale, k_ref)
        v_ref[...] = in_proj(n, 2).astype(BF16)
        g_ref[...] = in_proj(n, 3)


def _resident(shape):
    nd = len(shape)
    return pl.BlockSpec(shape, lambda *_: (0,) * nd, pipeline_mode=pl.Buffered(1))


def _ffn_inproj(x2, xm, cos, sin, cos_m, sin_m, n1w, wg, wu, wd, n2w, w_in, tm):
    rows = x2.shape[0]
    pos_blocks = cos.shape[0] // tm
    row_blk = lambda i: (_tile_index(i), 0)
    pos_blk = lambda i: (_tile_index(i) % pos_blocks, 0)
    out_shape = (
        jax.ShapeDtypeStruct((rows, D_MODEL), F32),
        jax.ShapeDtypeStruct((rows, RET_WIDTH), BF16),
        jax.ShapeDtypeStruct((rows, RET_WIDTH), BF16),
        jax.ShapeDtypeStruct((rows, RET_WIDTH), BF16),
        jax.ShapeDtypeStruct((rows, RET_WIDTH), F32),
        jax.ShapeDtypeStruct((SSM_GROUPS, rows // SUB, GROUP_LANES), F32),
        jax.ShapeDtypeStruct((N_META, RET_WIDTH), BF16),
        jax.ShapeDtypeStruct((N_META, RET_WIDTH), BF16),
        jax.ShapeDtypeStruct((SSM_GROUPS, HEAD_ROWS, GROUP_LANES), F32),
    )
    return pl.pallas_call(
        _ffn_inproj_kernel,
        grid=(WEIGHT_CHUNKS + rows // tm,),
        in_specs=[
            pl.BlockSpec((tm, D_MODEL), row_blk),
            pl.BlockSpec((tm, HEAD_DIM), pos_blk),
            pl.BlockSpec((tm, HEAD_DIM), pos_blk),
            _resident((N_META, D_MODEL)),
            _resident((N_META, HEAD_DIM)),
            _resident((N_META, HEAD_DIM)),
            _resident((1, D_MODEL)),
            _weight_chunk_spec((D_MODEL, D_FF)),
            _weight_chunk_spec((D_MODEL, D_FF)),
            _weight_chunk_spec((D_FF, D_MODEL)),
            _resident((1, D_MODEL)),
            _weight_chunk_spec((D_MODEL, IN_PROJ)),
        ],
        out_specs=(
            pl.BlockSpec((tm, D_MODEL), row_blk),
            pl.BlockSpec((tm, RET_WIDTH), row_blk),
            pl.BlockSpec((tm, RET_WIDTH), row_blk),
            pl.BlockSpec((tm, RET_WIDTH), row_blk),
            pl.BlockSpec((tm, RET_WIDTH), row_blk),
            pl.BlockSpec((SSM_GROUPS, tm // SUB, GROUP_LANES), lambda i: (0, _tile_index(i), 0)),
            pl.BlockSpec((N_META, RET_WIDTH), lambda i: (0, 0)),
            pl.BlockSpec((N_META, RET_WIDTH), lambda i: (0, 0)),
            pl.BlockSpec((SSM_GROUPS, HEAD_ROWS, GROUP_LANES), lambda i: (0, 0, 0)),
        ),
        out_shape=out_shape,
        scratch_shapes=[
            pltpu.VMEM((D_MODEL, D_FF), BF16),
            pltpu.VMEM((D_MODEL, D_FF), BF16),
            pltpu.VMEM((D_FF, D_MODEL), BF16),
            pltpu.VMEM((D_MODEL, IN_PROJ), BF16),
            pltpu.VMEM((tm, D_FF), BF16),
            pltpu.VMEM((N_META, D_FF), BF16),
            pltpu.VMEM((OCTETS, tm, LANES), F32),
            pltpu.VMEM((OCTETS, N_META, LANES), F32),
        ],
        compiler_params=pltpu.CompilerParams(
            dimension_semantics=("arbitrary",), vmem_limit_bytes=VMEM_LIMIT),
        name="ffn_inproj",
    )(x2, cos, sin, xm, cos_m, sin_m, n1w, wg, wu, wd, n2w, w_in)


def _retention_kernel(q_ref, k_ref, v_ref, g_ref, km_ref, vm_ref, mask_ref, wq_ref, wk_ref, wm_ref,
                      gc_ref, nw_ref, o_ref, state_ref, *, rows):
    tn = (((0,), (0,)), ((), ()))
    nt = (((1,), (1,)), ((), ()))

    @pl.when(pl.program_id(1) == 0)
    def _():
        for h in range(RET_HEADS):
            hs = slice(h * HEAD_DIM, (h + 1) * HEAD_DIM)
            kw = (km_ref[:, hs].astype(F32) * wm_ref[h]).astype(BF16)
            state_ref[h] = lax.dot_general(kw, vm_ref[:, hs], tn, preferred_element_type=F32)

    def chunk(i, carry):
        r0 = pl.multiple_of(i * RET_BLOCK, RET_BLOCK)
        rs = pl.ds(r0, RET_BLOCK)
        for h in range(RET_HEADS):
            hs = slice(h * HEAD_DIM, (h + 1) * HEAD_DIM)
            qh = q_ref[rs, hs]
            kh = k_ref[rs, hs]
            vh = v_ref[rs, hs]
            st = state_ref[h]
            s = lax.dot_general(qh, kh, nt, preferred_element_type=F32) * mask_ref[h]
            o = (jnp.dot(s.astype(BF16), vh, preferred_element_type=F32)
                 + jnp.dot(qh, st.astype(BF16), preferred_element_type=F32) * wq_ref[h])
            kw = (kh.astype(F32) * wk_ref[h]).astype(BF16)
            state_ref[h] = gc_ref[h] * st + lax.dot_general(kw, vh, tn, preferred_element_type=F32)
            mu = jnp.mean(o, axis=-1, keepdims=True)
            d = o - mu
            var = jnp.mean(d * d, axis=-1, keepdims=True)
            y = d * lax.rsqrt(var + EPS) * nw_ref[:, hs]
            gate = g_ref[rs, hs]
            o_ref[rs, hs] = (gate * jax.nn.sigmoid(gate) * y).astype(BF16)
        return carry

    lax.fori_loop(0, rows // RET_BLOCK, chunk, 0, unroll=4)


def _retention_tables():
    log_g = np.log(1.0 - 2.0 ** (-5.0 - np.arange(RET_HEADS, dtype=np.float64)))
    i = np.arange(RET_BLOCK)
    diff = i[:, None] - i[None, :]
    mask = np.where(diff[None] >= 0, np.exp(log_g[:, None, None] * np.maximum(diff, 0)[None]), 0.0)
    pos = np.arange(RET_BLOCK, dtype=np.float64)
    full = lambda w: np.broadcast_to(w[:, :, None], w.shape + (HEAD_DIM,))
    w_q = full(np.exp(log_g[:, None] * (pos + 1.0)[None]))
    w_k = full(np.exp(log_g[:, None] * (RET_BLOCK - 1 - pos)[None]))
    w_m = full(np.exp(log_g[:, None] * (N_META - 1 - np.arange(N_META, dtype=np.float64))[None]))
    g_c = np.broadcast_to(np.exp(log_g * RET_BLOCK)[:, None, None], (RET_HEADS, HEAD_DIM, HEAD_DIM))
    return tuple(jnp.asarray(t, dtype=F32) for t in (mask, w_q, w_k, w_m, g_c))


def _retention(q, k, v, g, km, vm, ret_norm_w, bsz, seq, rows):
    mask, w_q, w_k, w_m, g_c = _retention_tables()
    steps = seq // rows
    blk = lambda b, c: (b * steps + c, 0)
    full3 = lambda a: pl.BlockSpec(a.shape, lambda b, c: (0, 0, 0))
    full2 = lambda a: pl.BlockSpec(a.shape, lambda b, c: (0, 0))
    return pl.pallas_call(
        functools.partial(_retention_kernel, rows=rows),
        grid=(bsz, steps),
        in_specs=[pl.BlockSpec((rows, RET_WIDTH), blk)] * 4
        + [full2(km), full2(vm), full3(mask), full3(w_q), full3(w_k), full3(w_m), full3(g_c),
           full2(ret_norm_w)],
        out_specs=pl.BlockSpec((rows, RET_WIDTH), blk),
        out_shape=jax.ShapeDtypeStruct((bsz * seq, RET_WIDTH), BF16),
        scratch_shapes=[pltpu.VMEM((RET_HEADS, HEAD_DIM, HEAD_DIM), F32)],
        compiler_params=pltpu.CompilerParams(
            dimension_semantics=("arbitrary", "arbitrary"), vmem_limit_bytes=VMEM_LIMIT),
        name="retention",
    )(q, k, v, g, km, vm, mask, w_q, w_k, w_m, g_c, ret_norm_w)


def _shift_lanes_zero_fill(halves, shift, lane):
    h0, h1 = halves
    whole, s = divmod(shift, LANES)
    r0 = h0 if s == 0 else pltpu.roll(h0, s, axis=1)
    zero = jnp.zeros_like(h0)
    if whole == 1:
        return zero, jnp.where(lane >= s, r0, zero)
    r1 = h1 if s == 0 else pltpu.roll(h1, s, axis=1)
    return jnp.where(lane >= s, r0, zero), jnp.where(lane >= s, r1, r0)


def _s5_assemble_octet(o, vec_ref, mat_ref,
                       toep_ref, win_ref, wout_ref, cpw_ref, pow_ref, xall_ref, wt_ref):
    vec = vec_ref[o]
    lre = vec[0:1]
    lim = vec[1:2]
    dt = jnp.exp(vec[2:3])
    n_pow = pow_ref.shape[1]
    ell = lax.broadcasted_iota(jnp.int32, (n_pow, OCT_STATE), 0).astype(F32)
    mag = jnp.exp(ell * (lre * dt))
    ang = ell * (lim * dt)
    pow_ref[0] = mag * jnp.cos(ang)
    pow_ref[1] = mag * jnp.sin(ang)
    ell = ((lax.broadcasted_iota(jnp.int32, (SUBLANES, OCT_STATE), 0) + 1) * SUB).astype(F32)
    mag = jnp.exp(ell * (lre * dt))
    ang = ell * (lim * dt)
    cp_re = mag * jnp.cos(ang)
    cp_im = mag * jnp.sin(ang)
    a_re = pow_ref[0, 1:2, :]
    a_im = pow_ref[1, 1:2, :]
    den = lre * lre + lim * lim
    num_re = a_re - 1.0
    coef_re = (num_re * lre + a_im * lim) / den
    coef_im = (a_im * lre - num_re * lim) / den
    btr = mat_ref[o, 0]
    bti = mat_ref[o, 1]
    bbar_re = coef_re * btr - coef_im * bti
    bbar_im = coef_re * bti + coef_im * btr
    ctr = mat_ref[o, 2]
    cti = mat_ref[o, 3]

    pairs_per_octet = GROUPS_PER_OCTET // 2
    pair_lane_group = _lane_block((SSM_GROUP, PAIR_STATE), SSM_STATE)
    for l in range(SUB):
        pr = pow_ref[0, l:l + 1, :]
        pi = pow_ref[1, l:l + 1, :]
        x_re = pr * bbar_re - pi * bbar_im
        x_im = pr * bbar_im + pi * bbar_re
        rows = slice(l * SSM_GROUP, (l + 1) * SSM_GROUP)
        xall_ref[0, rows, :] = x_re
        xall_ref[1, rows, :] = x_im
        pr1 = pow_ref[0, l + 1:l + 2, :]
        pi1 = pow_ref[1, l + 1:l + 2, :]
        w_re = pr1 * ctr - pi1 * cti
        w_im = pr1 * cti + pi1 * ctr
        j = SUB - 1 - l
        for pp in range(pairs_per_octet):
            sl = slice(pp * PAIR_STATE, (pp + 1) * PAIR_STATE)
            pair = o * pairs_per_octet + pp
            for gi in range(2):
                keep = pair_lane_group == gi
                in_rows = slice(gi * GROUP_LANES + j * SSM_GROUP, gi * GROUP_LANES + (j + 1) * SSM_GROUP)
                out_rows = slice(gi * GROUP_LANES + l * SSM_GROUP, gi * GROUP_LANES + (l + 1) * SSM_GROUP)
                for part, x, w in ((0, x_re, w_re), (1, x_im, -w_im)):
                    cols = slice(part * PAIR_STATE, (part + 1) * PAIR_STATE)
                    win_ref[pair, in_rows, cols] = jnp.where(keep, x[:, sl], 0.0).astype(BF16)
                    wt_ref[pp, out_rows, cols] = jnp.where(keep, w[:, sl], 0.0)
    for pp in range(pairs_per_octet):
        pair = o * pairs_per_octet + pp
        wout_ref[pair] = wt_ref[pp].T.astype(BF16)
        sl = slice(pp * PAIR_STATE, (pp + 1) * PAIR_STATE)
        cpw_ref[pair, 0] = cp_re[:, sl]
        cpw_ref[pair, 1] = cp_im[:, sl]

    row = lax.broadcasted_iota(jnp.int32, (LANES, OCT_STATE), 0)
    col = lax.broadcasted_iota(jnp.int32, (LANES, OCT_STATE), 1)
    same_group = (row // SSM_GROUP) == (col // SSM_STATE)

    def block_diag_t(x):
        return jnp.where(same_group, jnp.concatenate([x] * GROUPS_PER_OCTET, axis=0), 0.0).T

    k_all = (jnp.dot(xall_ref[0], block_diag_t(ctr), precision=lax.Precision.HIGHEST,
                     preferred_element_type=F32)
             - jnp.dot(xall_ref[1], block_diag_t(cti), precision=lax.Precision.HIGHEST,
                       preferred_element_type=F32))
    per_half = LANES // SSM_GROUP
    blk = _lane_block((SSM_GROUP, LANES), SSM_GROUP)
    lane = lax.broadcasted_iota(jnp.int32, (SSM_GROUP, LANES), 1)
    for g8 in range(GROUPS_PER_OCTET):
        strip = []
        for hh in range(GROUP_LANES // LANES):
            acc = jnp.zeros((SSM_GROUP, LANES), F32)
            for ll in range(per_half):
                lag = hh * per_half + ll
                k_lag = k_all[lag * SSM_GROUP:(lag + 1) * SSM_GROUP, :]
                shift = ((ll - g8) * SSM_GROUP) % LANES
                rolled = k_lag if shift == 0 else pltpu.roll(k_lag, shift, axis=1)
                acc = jnp.where(blk == ll, rolled, acc)
            strip.append(acc)
        for j in range(SUB):
            h0, h1 = _shift_lanes_zero_fill(strip, j * SSM_GROUP, lane)
            rows = slice(j * SSM_GROUP, (j + 1) * SSM_GROUP)
            toep_ref[o * GROUPS_PER_OCTET + g8, rows, 0:LANES] = h0.astype(BF16)
            toep_ref[o * GROUPS_PER_OCTET + g8, rows, LANES:2 * LANES] = h1.astype(BF16)


def _s5_state_scan(st_ref, zb_ref, cpow_ref, nsc):
    tiles = nsc // SUBLANES
    width = st_ref.shape[1] // 2
    re = slice(0, width)
    im = slice(width, 2 * width)
    body = slice(HEAD_ROWS, HEAD_ROWS + nsc)
    rowmod = lax.broadcasted_iota(jnp.int32, (SUBLANES, width), 0)
    src, dst = st_ref, zb_ref
    shift = 1
    while shift < SUBLANES:
        keep = rowmod >= shift
        power = slice(shift - 1, shift)
        c_re = jnp.concatenate([jnp.where(keep, cpow_ref[0, power, :], 0.0)] * tiles, axis=0)
        c_im = jnp.concatenate([jnp.where(keep, cpow_ref[1, power, :], 0.0)] * tiles, axis=0)
        sh_re = src[pl.ds(HEAD_ROWS - shift, nsc), re]
        sh_im = src[pl.ds(HEAD_ROWS - shift, nsc), im]
        dst[body, re] = src[body, re] + c_re * sh_re - c_im * sh_im
        dst[body, im] = src[body, im] + c_re * sh_im + c_im * sh_re
        src, dst = dst, src
        shift *= 2
    ca_re = cpow_ref[0]
    ca_im = cpow_ref[1]
    c_re = st_ref[ENTER_ROW:HEAD_ROWS, re]
    c_im = st_ref[ENTER_ROW:HEAD_ROWS, im]
    for t in range(tiles):
        rows = slice(HEAD_ROWS + SUBLANES * t, HEAD_ROWS + SUBLANES * (t + 1))
        cb_re = jnp.broadcast_to(c_re, (SUBLANES, width))
        cb_im = jnp.broadcast_to(c_im, (SUBLANES, width))
        p_re = src[rows, re] + ca_re * cb_re - ca_im * cb_im
        p_im = src[rows, im] + ca_re * cb_im + ca_im * cb_re
        st_ref[rows, re] = p_re
        st_ref[rows, im] = p_im
        c_re = p_re[SUBLANES - 1:SUBLANES]
        c_im = p_im[SUBLANES - 1:SUBLANES]


def _s5_kernel(u_ref, um_ref, vec_ref, mat_ref, d_ref,
               y_ref, toep_ref, win_ref, wout_ref, cpw_ref, pow_ref, xall_ref, wt_ref, lhs_ref,
               st_ref, zb_ref, carry_ref, *, nsc):
    first_call_step = (pl.program_id(0) == 0) & (pl.program_id(1) == 0)
    r = pl.program_id(1)

    @pl.when(first_call_step)
    def _():
        zb_ref[:, 0:HEAD_ROWS, :] = jnp.zeros((zb_ref.shape[0], HEAD_ROWS, zb_ref.shape[2]), F32)

        def octet(o, carry):
            _s5_assemble_octet(o, vec_ref, mat_ref,
                               toep_ref, win_ref, wout_ref, cpw_ref, pow_ref, xall_ref, wt_ref)
            return carry
        lax.fori_loop(0, OCTETS, octet, 0)

    @pl.when(r == 0)
    def _():
        def meta_state(pair, carry):
            meta = jnp.concatenate([um_ref[2 * pair], um_ref[2 * pair + 1]], axis=1).astype(BF16)
            carry_ref[pair] = jnp.dot(meta, win_ref[pair], preferred_element_type=F32)
            return carry
        lax.fori_loop(0, PAIRS, meta_state, 0)

    lo = slice(0, GROUP_LANES)
    hi = slice(GROUP_LANES, 2 * GROUP_LANES)
    slots = st_ref.shape[0]

    def pair_block(k, carry):
        pairs = [slots * k + s for s in range(slots)]
        for s, pair in enumerate(pairs):
            lhs_ref[s, :, lo] = u_ref[2 * pair].astype(BF16)
            lhs_ref[s, :, hi] = u_ref[2 * pair + 1].astype(BF16)
            st_ref[s, 0:HEAD_ROWS, :] = carry_ref[pair]
            st_ref[s, HEAD_ROWS:HEAD_ROWS + nsc, :] = jnp.dot(lhs_ref[s], win_ref[pair], preferred_element_type=F32)
        for s, pair in enumerate(pairs):
            _s5_state_scan(st_ref.at[s], zb_ref.at[s], cpw_ref.at[pair], nsc)
            carry_ref[pair] = st_ref[s, nsc:nsc + HEAD_ROWS, :]
        for s, pair in enumerate(pairs):
            s_in = st_ref[s, pl.ds(ENTER_ROW, nsc), :].astype(BF16)
            y_state = jnp.dot(s_in, wout_ref[pair], preferred_element_type=F32)
            d = d_ref[pair]
            for gi, cols in ((0, lo), (1, hi)):
                g = 2 * pair + gi
                y = (jnp.dot(lhs_ref[s, :, cols], toep_ref[g], preferred_element_type=F32)
                     + y_state[:, cols] + d[:, cols] * u_ref[g])
                y_ref[g] = y
        return carry

    lax.fori_loop(0, PAIRS // slots, pair_block, 0)


def _s5_param_rows(lam_re, lam_im, log_dt, b_re, b_im, c_re, c_im, d):
    go = GROUPS_PER_OCTET
    lanes = lambda a: a.reshape(OCTETS, OCT_STATE)
    ldt = jnp.broadcast_to(log_dt[:, None], (SSM_GROUPS, SSM_STATE))
    bt = lambda a: jnp.transpose(a.reshape(OCTETS, go, SSM_STATE, SSM_GROUP),
                                 (0, 3, 1, 2)).reshape(OCTETS, SSM_GROUP, OCT_STATE)
    ct = lambda a: jnp.transpose(a.reshape(OCTETS, go, SSM_GROUP, SSM_STATE),
                                 (0, 2, 1, 3)).reshape(OCTETS, SSM_GROUP, OCT_STATE)
    vec = jnp.stack([lanes(lam_re), lanes(lam_im), lanes(ldt)], axis=1)
    mat = jnp.stack([bt(b_re), bt(b_im), ct(c_re), ct(c_im)], axis=1)
    d_pairs = jnp.broadcast_to(d.reshape(PAIRS, 2, 1, SSM_GROUP),
                               (PAIRS, 2, SUB, SSM_GROUP)).reshape(PAIRS, 1, 2 * GROUP_LANES)
    return vec, mat, d_pairs


def _s5(u, um, params, bsz, seq, rows):
    nsc = rows // SUB
    steps = seq // rows
    n_pow = SUB + SUBLANES
    full = lambda a: pl.BlockSpec(a.shape, lambda b, r: (0,) * a.ndim)
    tok_blk = pl.BlockSpec((SSM_GROUPS, nsc, GROUP_LANES), lambda b, r: (0, b * steps + r, 0))
    return pl.pallas_call(
        functools.partial(_s5_kernel, nsc=nsc),
        grid=(bsz, steps),
        in_specs=[tok_blk, full(um)] + [full(p) for p in params],
        out_specs=tok_blk,
        out_shape=jax.ShapeDtypeStruct(u.shape, F32),
        scratch_shapes=[
            pltpu.VMEM((SSM_GROUPS, GROUP_LANES, GROUP_LANES), BF16),
            pltpu.VMEM((PAIRS, 2 * GROUP_LANES, 2 * PAIR_STATE), BF16),
            pltpu.VMEM((PAIRS, 2 * PAIR_STATE, 2 * GROUP_LANES), BF16),
            pltpu.VMEM((PAIRS, 2, SUBLANES, PAIR_STATE), F32),
            pltpu.VMEM((2, n_pow, OCT_STATE), F32),
            pltpu.VMEM((2, SUB * SSM_GROUP, OCT_STATE), F32),
            pltpu.VMEM((GROUPS_PER_OCTET // 2, 2 * GROUP_LANES, 2 * PAIR_STATE), F32),
            pltpu.VMEM((S5_SLOTS, nsc, 2 * GROUP_LANES), BF16),
            pltpu.VMEM((S5_SLOTS, HEAD_ROWS + nsc, 2 * PAIR_STATE), F32),
            pltpu.VMEM((S5_SLOTS, HEAD_ROWS + nsc, 2 * PAIR_STATE), F32),
            pltpu.VMEM((PAIRS, HEAD_ROWS, 2 * PAIR_STATE), F32),
        ],
        compiler_params=pltpu.CompilerParams(
            dimension_semantics=("arbitrary", "arbitrary"), vmem_limit_bytes=VMEM_LIMIT),
        name="s5_mixer",
    )(u, um, *params)


def _mix_ffn_kernel(h1_ref, ret_ref, ynext_ref, gw32_ref, gb_ref, snw_ref, wo32_ref, n3w_ref,
                    wg32_ref, wu32_ref, wd32_ref, fnw_ref, out_ref,
                    gw_ref, wo_ref, wg_ref, wu_ref, wd_ref, acc_ref, ytok_ref):
    step = pl.program_id(0)
    nsub = ynext_ref.shape[1]

    @pl.when(step < WEIGHT_CHUNKS)
    def _():
        _cast_weight_chunks(step, ((gw32_ref, gw_ref), (wo32_ref, wo_ref), (wg32_ref, wg_ref),
                                   (wu32_ref, wu_ref), (wd32_ref, wd_ref)))

    @pl.when(step == WEIGHT_CHUNKS - 1)
    def _():
        _to_token_major(ynext_ref, ytok_ref.at[0], nsub, jax.nn.gelu)

    @pl.when(step >= WEIGHT_CHUNKS)
    def _():
        slot = lax.rem(step - WEIGHT_CHUNKS, 2)
        y = _token_major_rows(ytok_ref.at[slot], nsub)
        _to_token_major(ynext_ref, ytok_ref.at[1 - slot], nsub, jax.nn.gelu)
        z = y * jax.nn.sigmoid(jnp.dot(y.astype(BF16), gw_ref[...], preferred_element_type=F32)
                               + gb_ref[...])
        ssm = _rms(z, snw_ref[...]).astype(BF16)
        mixed = jnp.dot(jnp.concatenate([ret_ref[...], ssm], axis=-1), wo_ref[...],
                        preferred_element_type=F32)
        h2 = h1_ref[...] + mixed
        h3 = _swiglu_half_step(h2, _rms(h2, n3w_ref[...]).astype(BF16), wg_ref, wu_ref, wd_ref,
                               acc_ref)
        out_ref[...] = _rms(h3, fnw_ref[...])


def _mix_ffn(h1, ret, y, glu_w, glu_b, ssm_norm_w, w_out, n3w, wg, wu, wd, fnw, tm):
    rows = h1.shape[0]
    tiles = rows // tm
    row_blk = lambda i: (_tile_index(i), 0)
    next_blk = lambda i: (0, jnp.minimum(_tile_index(i + 1), tiles - 1), 0)
    return pl.pallas_call(
        _mix_ffn_kernel,
        grid=(WEIGHT_CHUNKS + tiles,),
        in_specs=[
            pl.BlockSpec((tm, D_MODEL), row_blk),
            pl.BlockSpec((tm, RET_WIDTH), row_blk),
            pl.BlockSpec((SSM_GROUPS, tm // SUB, GROUP_LANES), next_blk),
            _weight_chunk_spec((SSM_WIDTH, SSM_WIDTH)),
            _resident((1, SSM_WIDTH)),
            _resident((1, SSM_WIDTH)),
            _weight_chunk_spec((D_MODEL, D_MODEL)),
            _resident((1, D_MODEL)),
            _weight_chunk_spec((D_MODEL, D_FF)),
            _weight_chunk_spec((D_MODEL, D_FF)),
            _weight_chunk_spec((D_FF, D_MODEL)),
            _resident((1, D_MODEL)),
        ],
        out_specs=pl.BlockSpec((tm, D_MODEL), row_blk),
        out_shape=jax.ShapeDtypeStruct((rows, D_MODEL), F32),
        scratch_shapes=[
            pltpu.VMEM((SSM_WIDTH, SSM_WIDTH), BF16),
            pltpu.VMEM((D_MODEL, D_MODEL), BF16),
            pltpu.VMEM((D_MODEL, D_FF), BF16),
            pltpu.VMEM((D_MODEL, D_FF), BF16),
            pltpu.VMEM((D_FF, D_MODEL), BF16),
            pltpu.VMEM((tm, D_FF), BF16),
            pltpu.VMEM((2, OCTETS, tm // SUB * TOK_PITCH, LANES), F32),
        ],
        compiler_params=pltpu.CompilerParams(
            dimension_semantics=("arbitrary",), vmem_limit_bytes=VMEM_LIMIT),
        name="mix_ffn",
    )(h1, ret, y, glu_w, glu_b, ssm_norm_w, w_out, n3w, wg, wu, wd, fnw)


def _rope_tables(n_pos):
    freqs = 1.0 / (ROPE_BASE ** (np.arange(0, HEAD_DIM, 2, dtype=np.float64) / HEAD_DIM))
    ang = np.arange(n_pos, dtype=np.float64)[:, None] * freqs[None, :]
    cos = np.cos(ang)
    sin = np.sin(ang)
    cos = np.concatenate([cos, cos], axis=-1).astype(np.float32)
    sin = np.concatenate([-sin, sin], axis=-1).astype(np.float32)
    split = lambda t: (jnp.asarray(t[:N_META]), jnp.asarray(t[N_META:]))
    return split(cos), split(sin)


def kernel(x, meta_tokens, ffn1_norm_w, ffn1_w_gate, ffn1_w_up, ffn1_w_down, mix_norm_w, w_in,
           ret_norm_w, ssm_lambda_re, ssm_lambda_im, ssm_log_dt, ssm_b_re, ssm_b_im, ssm_c_re,
           ssm_c_im, ssm_d, ssm_glu_w, ssm_glu_b, ssm_norm_w, w_out, ffn2_norm_w, ffn2_w_gate,
           ffn2_w_up, ffn2_w_down, final_norm_w):
    bsz, seq, _ = x.shape
    assert ffn1_norm_w.shape[0] == 1, "single layer only"
    tm = 512
    ret_rows = 2048
    s5_rows = 4096
    assert seq % tm == 0 and seq % ret_rows == 0 and seq % s5_rows == 0

    l = 0
    (cos_m, cos), (sin_m, sin) = _rope_tables(N_META + seq)

    h1, q, k, v, g, u, km, vm, um = _ffn_inproj(
        x.reshape(bsz * seq, D_MODEL), meta_tokens, cos, sin, cos_m, sin_m,
        ffn1_norm_w, ffn1_w_gate[l], ffn1_w_up[l], ffn1_w_down[l], mix_norm_w, w_in[l], tm=tm)

    ret = _retention(q, k, v, g, km, vm, ret_norm_w, bsz, seq, ret_rows)

    params = _s5_param_rows(ssm_lambda_re[l], ssm_lambda_im[l], ssm_log_dt[l], ssm_b_re[l],
                            ssm_b_im[l], ssm_c_re[l], ssm_c_im[l], ssm_d[l])
    y = _s5(u, um, params, bsz, seq, s5_rows)

    out = _mix_ffn(h1, ret, y, ssm_glu_w[l], ssm_glu_b, ssm_norm_w, w_out[l], ffn2_norm_w,
                   ffn2_w_gate[l], ffn2_w_up[l], ffn2_w_down[l], final_norm_w.reshape(1, D_MODEL),
                   tm=tm)
    return out.reshape(bsz, seq, D_MODEL)
```

```python
import functools

import jax
import jax.numpy as jnp
import numpy as np
from jax import lax
from jax.experimental import pallas as pl
from jax.experimental.pallas import tpu as pltpu

D_MODEL = 1024
N_META = 16
RET_HEADS = 4
HEAD_DIM = 128
RET_WIDTH = RET_HEADS * HEAD_DIM
SSM_WIDTH = 512
SSM_GROUP = 16
SSM_GROUPS = SSM_WIDTH // SSM_GROUP
SSM_STATE = 64
RET_BLOCK = 256
D_FF = 2816
FFN_RES = 0.5
ROPE_BASE = 10000.0
EPS = 1e-6
IN_PROJ = 4 * RET_WIDTH + SSM_WIDTH

LANES = 128
SUBLANES = 8
HEAD_ROWS = SUBLANES
ENTER_ROW = HEAD_ROWS - 1
OCTETS = SSM_WIDTH // LANES
GROUPS_PER_OCTET = LANES // SSM_GROUP
SUB = N_META
OCT_STATE = GROUPS_PER_OCTET * SSM_STATE
GROUP_LANES = SUB * SSM_GROUP
PAIRS = SSM_GROUPS // 2
PAIR_STATE = 2 * SSM_STATE
S5_SLOTS = 4
TOK_PITCH = 24
FF_CHUNK = 256
WEIGHT_CHUNKS = 8
VMEM_LIMIT = 56 * 1024 * 1024

F32 = jnp.float32
BF16 = jnp.bfloat16


def _rms(x, w):
    return x * lax.rsqrt(jnp.mean(x * x, axis=-1, keepdims=True) + EPS) * w


def _swiglu_half_step(h, n, wg_ref, wu_ref, wd_ref, acc_ref):
    for lo in range(0, D_FF, FF_CHUNK):
        sl = slice(lo, min(lo + FF_CHUNK, D_FF))
        g = jnp.dot(n, wg_ref[:, sl], preferred_element_type=F32)
        u = jnp.dot(n, wu_ref[:, sl], preferred_element_type=F32)
        acc_ref[:, sl] = (g * jax.nn.sigmoid(g) * u).astype(BF16)
    return h + FFN_RES * jnp.dot(acc_ref[...], wd_ref[...], preferred_element_type=F32)


def _rope(x, cos, sin_signed):
    return x * cos + pltpu.roll(x, HEAD_DIM // 2, axis=1) * sin_signed


def _lane_block(shape, width):
    return lax.broadcasted_iota(jnp.int32, shape, 1) // width


def _transpose_lane_blocks(v):
    n = LANES // SSM_GROUP
    assert len(v) == n == GROUPS_PER_OCTET
    blk = _lane_block(v[0].shape, SSM_GROUP)
    d = n // 2
    while d:
        low = (blk & d) == 0
        nxt = list(v)
        for a in range(n):
            if a & d == 0:
                nxt[a] = jnp.where(low, v[a], pltpu.roll(v[a + d], d * SSM_GROUP, axis=1))
                nxt[a + d] = jnp.where(low, pltpu.roll(v[a], LANES - d * SSM_GROUP, axis=1), v[a + d])
        v = nxt
        d //= 2
    return v


def _to_group_dense(tok_ref, out_ref, nsub, row0=0):
    per_half = LANES // SSM_GROUP
    for o in range(OCTETS):
        for hh in range(GROUP_LANES // LANES):
            slabs = [tok_ref[o, pl.ds(hh * per_half + jj, nsub, stride=TOK_PITCH), :]
                     for jj in range(per_half)]
            for g8, rows in enumerate(_transpose_lane_blocks(slabs)):
                out_ref[o * GROUPS_PER_OCTET + g8, row0:row0 + nsub,
                        hh * LANES:(hh + 1) * LANES] = rows


def _to_token_major(gd_ref, tok_ref, nsub, fn):
    per_half = LANES // SSM_GROUP
    for o in range(OCTETS):
        for hh in range(GROUP_LANES // LANES):
            rows = [gd_ref[o * GROUPS_PER_OCTET + g8, :, hh * LANES:(hh + 1) * LANES]
                    for g8 in range(GROUPS_PER_OCTET)]
            for jj, slab in enumerate(_transpose_lane_blocks(rows)):
                tok_ref[o, pl.ds(hh * per_half + jj, nsub, stride=TOK_PITCH), :] = fn(slab)


def _token_major_rows(tok_ref, nsub):
    return jnp.concatenate(
        [jnp.concatenate([tok_ref[o, c * TOK_PITCH:c * TOK_PITCH + SUB, :] for c in range(nsub)],
                         axis=0) for o in range(OCTETS)], axis=-1)


def _cast_weight_chunks(step, pairs):
    for src_ref, dst_ref in pairs:
        rows = src_ref.shape[0]
        r0 = pl.multiple_of(step * rows, rows)
        dst_ref[pl.ds(r0, rows), :] = src_ref[...].astype(BF16)


def _weight_chunk_spec(shape):
    rows = shape[0] // WEIGHT_CHUNKS
    assert rows * WEIGHT_CHUNKS == shape[0] and rows % 16 == 0, shape
    return pl.BlockSpec((rows, shape[1]), lambda i: (jnp.minimum(i, WEIGHT_CHUNKS - 1), 0))


def _tile_index(i):
    return jnp.maximum(i - WEIGHT_CHUNKS, 0)


def _ffn_inproj_kernel(x_ref, cos_ref, sin_ref, xm_ref, cosm_ref, sinm_ref, n1w_ref, wg32_ref,
                       wu32_ref, wd32_ref, n2w_ref, win32_ref,
                       h1_ref, q_ref, k_ref, v_ref, g_ref, u_ref, km_ref, vm_ref, um_ref,
                       wg_ref, wu_ref, wd_ref, win_ref, acc_ref, accm_ref, utok_ref, utokm_ref):
    step = pl.program_id(0)
    k_scale = HEAD_DIM ** -0.5

    def ffn_norm(h):
        return _rms(h, n1w_ref[...]).astype(BF16)

    def mix_norm(h1):
        return _rms(h1, n2w_ref[...]).astype(BF16)

    def in_proj(n, part):
        return jnp.dot(n, win_ref[:, part * RET_WIDTH:(part + 1) * RET_WIDTH],
                       preferred_element_type=F32)

    def rope_heads(p, cos, sin, scale, out_ref):
        for h in range(RET_HEADS):
            hs = slice(h * HEAD_DIM, (h + 1) * HEAD_DIM)
            y = _rope(p[:, hs], cos, sin)
            out_ref[:, hs] = (y if scale is None else y * scale).astype(BF16)

    def split_octets(p, tok_ref):
        for o in range(OCTETS):
            for c in range(p.shape[0] // SUB):
                tok_ref[o, c * TOK_PITCH:c * TOK_PITCH + SUB, :] = (
                    p[c * SUB:(c + 1) * SUB, o * LANES:(o + 1) * LANES])

    @pl.when(step < WEIGHT_CHUNKS)
    def _():
        _cast_weight_chunks(step, ((wg32_ref, wg_ref), (wu32_ref, wu_ref), (wd32_ref, wd_ref),
                                   (win32_ref, win_ref)))

    @pl.when(step == WEIGHT_CHUNKS)
    def _():
        xm = xm_ref[...]
        h1 = _swiglu_half_step(xm, ffn_norm(xm), wg_ref, wu_ref, wd_ref, accm_ref)
        n = mix_norm(h1)
        rope_heads(in_proj(n, 1), cosm_ref[...], sinm_ref[...], k_scale, km_ref)
        vm_ref[...] = in_proj(n, 2).astype(BF16)
        split_octets(in_proj(n, 4), utokm_ref)
        um_ref[...] = jnp.zeros(um_ref.shape, F32)
        _to_group_dense(utokm_ref, um_ref, 1, row0=ENTER_ROW)

    @pl.when(step >= WEIGHT_CHUNKS)
    def _():
        x = x_ref[...]
        h1 = _swiglu_half_step(x, ffn_norm(x), wg_ref, wu_ref, wd_ref, acc_ref)
        h1_ref[...] = h1
        n = mix_norm(h1)
        split_octets(in_proj(n, 4), utok_ref)
        _to_group_dense(utok_ref, u_ref, u_ref.shape[1])
        cos = cos_ref[...]
        sin = sin_ref[...]
        rope_heads(in_proj(n, 0), cos, sin, None, q_ref)
        rope_heads(in_proj(n, 1), cos, sin, k_scale, k_ref)
        v_ref[...] = in_proj(n, 2).astype(BF16)
        g_ref[...] = in_proj(n, 3)


def _resident(shape):
    nd = len(shape)
    return pl.BlockSpec(shape, lambda *_: (0,) * nd, pipeline_mode=pl.Buffered(1))


def _ffn_inproj(x2, xm, cos, sin, cos_m, sin_m, n1w, wg, wu, wd, n2w, w_in, tm):
    rows = x2.shape[0]
    pos_blocks = cos.shape[0] // tm
    row_blk = lambda i: (_tile_index(i), 0)
    pos_blk = lambda i: (_tile_index(i) % pos_blocks, 0)
    out_shape = (
        jax.ShapeDtypeStruct((rows, D_MODEL), F32),
        jax.ShapeDtypeStruct((rows, RET_WIDTH), BF16),
        jax.ShapeDtypeStruct((rows, RET_WIDTH), BF16),
        jax.ShapeDtypeStruct((rows, RET_WIDTH), BF16),
        jax.ShapeDtypeStruct((rows, RET_WIDTH), F32),
        jax.ShapeDtypeStruct((SSM_GROUPS, rows // SUB, GROUP_LANES), F32),
        jax.ShapeDtypeStruct((N_META, RET_WIDTH), BF16),
        jax.ShapeDtypeStruct((N_META, RET_WIDTH), BF16),
        jax.ShapeDtypeStruct((SSM_GROUPS, HEAD_ROWS, GROUP_LANES), F32),
    )
    return pl.pallas_call(
        _ffn_inproj_kernel,
        grid=(WEIGHT_CHUNKS + rows // tm,),
        in_specs=[
            pl.BlockSpec((tm, D_MODEL), row_blk),
            pl.BlockSpec((tm, HEAD_DIM), pos_blk),
            pl.BlockSpec((tm, HEAD_DIM), pos_blk),
            _resident((N_META, D_MODEL)),
            _resident((N_META, HEAD_DIM)),
            _resident((N_META, HEAD_DIM)),
            _resident((1, D_MODEL)),
            _weight_chunk_spec((D_MODEL, D_FF)),
            _weight_chunk_spec((D_MODEL, D_FF)),
            _weight_chunk_spec((D_FF, D_MODEL)),
            _resident((1, D_MODEL)),
            _weight_chunk_spec((D_MODEL, IN_PROJ)),
        ],
        out_specs=(
            pl.BlockSpec((tm, D_MODEL), row_blk),
            pl.BlockSpec((tm, RET_WIDTH), row_blk),
            pl.BlockSpec((tm, RET_WIDTH), row_blk),
            pl.BlockSpec((tm, RET_WIDTH), row_blk),
            pl.BlockSpec((tm, RET_WIDTH), row_blk),
            pl.BlockSpec((SSM_GROUPS, tm // SUB, GROUP_LANES), lambda i: (0, _tile_index(i), 0)),
            pl.BlockSpec((N_META, RET_WIDTH), lambda i: (0, 0)),
            pl.BlockSpec((N_META, RET_WIDTH), lambda i: (0, 0)),
            pl.BlockSpec((SSM_GROUPS, HEAD_ROWS, GROUP_LANES), lambda i: (0, 0, 0)),
        ),
        out_shape=out_shape,
        scratch_shapes=[
            pltpu.VMEM((D_MODEL, D_FF), BF16),
            pltpu.VMEM((D_MODEL, D_FF), BF16),
            pltpu.VMEM((D_FF, D_MODEL), BF16),
            pltpu.VMEM((D_MODEL, IN_PROJ), BF16),
            pltpu.VMEM((tm, D_FF), BF16),
            pltpu.VMEM((N_META, D_FF), BF16),
            pltpu.VMEM((OCTETS, tm // SUB * TOK_PITCH, LANES), F32),
            pltpu.VMEM((OCTETS, TOK_PITCH, LANES), F32),
        ],
        compiler_params=pltpu.CompilerParams(
            dimension_semantics=("arbitrary",), vmem_limit_bytes=VMEM_LIMIT),
        name="ffn_inproj",
    )(x2, cos, sin, xm, cos_m, sin_m, n1w, wg, wu, wd, n2w, w_in)


def _retention_kernel(q_ref, k_ref, v_ref, g_ref, km_ref, vm_ref, mask_ref, wq_ref, wk_ref, wm_ref,
                      gc_ref, nw_ref, o_ref, state_ref, *, rows):
    tn = (((0,), (0,)), ((), ()))
    nt = (((1,), (1,)), ((), ()))

    @pl.when(pl.program_id(1) == 0)
    def _():
        for h in range(RET_HEADS):
            hs = slice(h * HEAD_DIM, (h + 1) * HEAD_DIM)
            kw = (km_ref[:, hs].astype(F32) * wm_ref[h]).astype(BF16)
            state_ref[h] = lax.dot_general(kw, vm_ref[:, hs], tn, preferred_element_type=F32)

    def chunk(i, carry):
        r0 = pl.multiple_of(i * RET_BLOCK, RET_BLOCK)
        rs = pl.ds(r0, RET_BLOCK)
        for h in range(RET_HEADS):
            hs = slice(h * HEAD_DIM, (h + 1) * HEAD_DIM)
            qh = q_ref[rs, hs]
            kh = k_ref[rs, hs]
            vh = v_ref[rs, hs]
            st = state_ref[h]
            s = lax.dot_general(qh, kh, nt, preferred_element_type=F32) * mask_ref[h]
            o = (jnp.dot(s.astype(BF16), vh, preferred_element_type=F32)
                 + jnp.dot(qh, st.astype(BF16), preferred_element_type=F32) * wq_ref[h])
            kw = (kh.astype(F32) * wk_ref[h]).astype(BF16)
            state_ref[h] = gc_ref[h] * st + lax.dot_general(kw, vh, tn, preferred_element_type=F32)
            mu = jnp.mean(o, axis=-1, keepdims=True)
            d = o - mu
            var = jnp.mean(d * d, axis=-1, keepdims=True)
            y = d * lax.rsqrt(var + EPS) * nw_ref[:, hs]
            gate = g_ref[rs, hs]
            o_ref[rs, hs] = (gate * jax.nn.sigmoid(gate) * y).astype(BF16)
        return carry

    lax.fori_loop(0, rows // RET_BLOCK, chunk, 0, unroll=4)


def _retention_tables():
    log_g = np.log(1.0 - 2.0 ** (-5.0 - np.arange(RET_HEADS, dtype=np.float64)))
    i = np.arange(RET_BLOCK)
    diff = i[:, None] - i[None, :]
    mask = np.where(diff[None] >= 0, np.exp(log_g[:, None, None] * np.maximum(diff, 0)[None]), 0.0)
    pos = np.arange(RET_BLOCK, dtype=np.float64)
    full = lambda w: np.broadcast_to(w[:, :, None], w.shape + (HEAD_DIM,))
    w_q = full(np.exp(log_g[:, None] * (pos + 1.0)[None]))
    w_k = full(np.exp(log_g[:, None] * (RET_BLOCK - 1 - pos)[None]))
    w_m = full(np.exp(log_g[:, None] * (N_META - 1 - np.arange(N_META, dtype=np.float64))[None]))
    g_c = np.broadcast_to(np.exp(log_g * RET_BLOCK)[:, None, None], (RET_HEADS, HEAD_DIM, HEAD_DIM))
    return tuple(jnp.asarray(t, dtype=F32) for t in (mask, w_q, w_k, w_m, g_c))


def _retention(q, k, v, g, km, vm, ret_norm_w, bsz, seq, rows):
    mask, w_q, w_k, w_m, g_c = _retention_tables()
    steps = seq // rows
    blk = lambda b, c: (b * steps + c, 0)
    full3 = lambda a: pl.BlockSpec(a.shape, lambda b, c: (0, 0, 0))
    full2 = lambda a: pl.BlockSpec(a.shape, lambda b, c: (0, 0))
    return pl.pallas_call(
        functools.partial(_retention_kernel, rows=rows),
        grid=(bsz, steps),
        in_specs=[pl.BlockSpec((rows, RET_WIDTH), blk)] * 4
        + [full2(km), full2(vm), full3(mask), full3(w_q), full3(w_k), full3(w_m), full3(g_c),
           full2(ret_norm_w)],
        out_specs=pl.BlockSpec((rows, RET_WIDTH), blk),
        out_shape=jax.ShapeDtypeStruct((bsz * seq, RET_WIDTH), BF16),
        scratch_shapes=[pltpu.VMEM((RET_HEADS, HEAD_DIM, HEAD_DIM), F32)],
        compiler_params=pltpu.CompilerParams(
            dimension_semantics=("arbitrary", "arbitrary"), vmem_limit_bytes=VMEM_LIMIT),
        name="retention",
    )(q, k, v, g, km, vm, mask, w_q, w_k, w_m, g_c, ret_norm_w)


def _shift_lanes_zero_fill(halves, shift, lane):
    h0, h1 = halves
    whole, s = divmod(shift, LANES)
    r0 = h0 if s == 0 else pltpu.roll(h0, s, axis=1)
    zero = jnp.zeros_like(h0)
    if whole == 1:
        return zero, jnp.where(lane >= s, r0, zero)
    r1 = h1 if s == 0 else pltpu.roll(h1, s, axis=1)
    return jnp.where(lane >= s, r0, zero), jnp.where(lane >= s, r1, r0)


def _s5_assemble_octet(o, vec_ref, mat_ref,
                       toep_ref, win_ref, wout_ref, cpw_ref, pow_ref, xall_ref, wt_ref):
    vec = vec_ref[o]
    lre = vec[0:1]
    lim = vec[1:2]
    dt = jnp.exp(vec[2:3])
    n_pow = pow_ref.shape[1]
    ell = lax.broadcasted_iota(jnp.int32, (n_pow, OCT_STATE), 0).astype(F32)
    mag = jnp.exp(ell * (lre * dt))
    ang = ell * (lim * dt)
    pow_ref[0] = mag * jnp.cos(ang)
    pow_ref[1] = mag * jnp.sin(ang)
    ell = ((lax.broadcasted_iota(jnp.int32, (SUBLANES, OCT_STATE), 0) + 1) * SUB).astype(F32)
    mag = jnp.exp(ell * (lre * dt))
    ang = ell * (lim * dt)
    cp_re = mag * jnp.cos(ang)
    cp_im = mag * jnp.sin(ang)
    a_re = pow_ref[0, 1:2, :]
    a_im = pow_ref[1, 1:2, :]
    den = lre * lre + lim * lim
    num_re = a_re - 1.0
    coef_re = (num_re * lre + a_im * lim) / den
    coef_im = (a_im * lre - num_re * lim) / den
    btr = mat_ref[o, 0]
    bti = mat_ref[o, 1]
    bbar_re = coef_re * btr - coef_im * bti
    bbar_im = coef_re * bti + coef_im * btr
    ctr = mat_ref[o, 2]
    cti = mat_ref[o, 3]

    pairs_per_octet = GROUPS_PER_OCTET // 2
    pair_lane_group = _lane_block((SSM_GROUP, PAIR_STATE), SSM_STATE)
    for l in range(SUB):
        pr = pow_ref[0, l:l + 1, :]
        pi = pow_ref[1, l:l + 1, :]
        x_re = pr * bbar_re - pi * bbar_im
        x_im = pr * bbar_im + pi * bbar_re
        rows = slice(l * SSM_GROUP, (l + 1) * SSM_GROUP)
        xall_ref[0, rows, :] = x_re
        xall_ref[1, rows, :] = x_im
        pr1 = pow_ref[0, l + 1:l + 2, :]
        pi1 = pow_ref[1, l + 1:l + 2, :]
        w_re = pr1 * ctr - pi1 * cti
        w_im = pr1 * cti + pi1 * ctr
        j = SUB - 1 - l
        for pp in range(pairs_per_octet):
            sl = slice(pp * PAIR_STATE, (pp + 1) * PAIR_STATE)
            pair = o * pairs_per_octet + pp
            for gi in range(2):
                keep = pair_lane_group == gi
                in_rows = slice(gi * GROUP_LANES + j * SSM_GROUP, gi * GROUP_LANES + (j + 1) * SSM_GROUP)
                out_rows = slice(gi * GROUP_LANES + l * SSM_GROUP, gi * GROUP_LANES + (l + 1) * SSM_GROUP)
                for part, x, w in ((0, x_re, w_re), (1, x_im, -w_im)):
                    cols = slice(part * PAIR_STATE, (part + 1) * PAIR_STATE)
                    win_ref[pair, in_rows, cols] = jnp.where(keep, x[:, sl], 0.0).astype(BF16)
                    wt_ref[pp, out_rows, cols] = jnp.where(keep, w[:, sl], 0.0)
    for pp in range(pairs_per_octet):
        pair = o * pairs_per_octet + pp
        wout_ref[pair] = wt_ref[pp].T.astype(BF16)
        sl = slice(pp * PAIR_STATE, (pp + 1) * PAIR_STATE)
        cpw_ref[pair, 0] = cp_re[:, sl]
        cpw_ref[pair, 1] = cp_im[:, sl]

    row = lax.broadcasted_iota(jnp.int32, (LANES, OCT_STATE), 0)
    col = lax.broadcasted_iota(jnp.int32, (LANES, OCT_STATE), 1)
    same_group = (row // SSM_GROUP) == (col // SSM_STATE)

    def block_diag_t(x):
        return jnp.where(same_group, jnp.concatenate([x] * GROUPS_PER_OCTET, axis=0), 0.0).T

    k_all = (jnp.dot(xall_ref[0], block_diag_t(ctr), precision=lax.Precision.HIGHEST,
                     preferred_element_type=F32)
             - jnp.dot(xall_ref[1], block_diag_t(cti), precision=lax.Precision.HIGHEST,
                       preferred_element_type=F32))
    per_half = LANES // SSM_GROUP
    blk = _lane_block((SSM_GROUP, LANES), SSM_GROUP)
    lane = lax.broadcasted_iota(jnp.int32, (SSM_GROUP, LANES), 1)
    for g8 in range(GROUPS_PER_OCTET):
        strip = []
        for hh in range(GROUP_LANES // LANES):
            acc = jnp.zeros((SSM_GROUP, LANES), F32)
            for ll in range(per_half):
                lag = hh * per_half + ll
                k_lag = k_all[lag * SSM_GROUP:(lag + 1) * SSM_GROUP, :]
                shift = ((ll - g8) * SSM_GROUP) % LANES
                rolled = k_lag if shift == 0 else pltpu.roll(k_lag, shift, axis=1)
                acc = jnp.where(blk == ll, rolled, acc)
            strip.append(acc)
        for j in range(SUB):
            h0, h1 = _shift_lanes_zero_fill(strip, j * SSM_GROUP, lane)
            rows = slice(j * SSM_GROUP, (j + 1) * SSM_GROUP)
            toep_ref[o * GROUPS_PER_OCTET + g8, rows, 0:LANES] = h0.astype(BF16)
            toep_ref[o * GROUPS_PER_OCTET + g8, rows, LANES:2 * LANES] = h1.astype(BF16)


def _s5_state_scan(st_ref, zb_ref, cpow_ref, nsc):
    tiles = nsc // SUBLANES
    width = st_ref.shape[1] // 2
    re = slice(0, width)
    im = slice(width, 2 * width)
    body = slice(HEAD_ROWS, HEAD_ROWS + nsc)
    rowmod = lax.broadcasted_iota(jnp.int32, (SUBLANES, width), 0)
    src, dst = st_ref, zb_ref
    shift = 1
    while shift < SUBLANES:
        keep = rowmod >= shift
        power = slice(shift - 1, shift)
        c_re = jnp.concatenate([jnp.where(keep, cpow_ref[0, power, :], 0.0)] * tiles, axis=0)
        c_im = jnp.concatenate([jnp.where(keep, cpow_ref[1, power, :], 0.0)] * tiles, axis=0)
        sh_re = src[pl.ds(HEAD_ROWS - shift, nsc), re]
        sh_im = src[pl.ds(HEAD_ROWS - shift, nsc), im]
        dst[body, re] = src[body, re] + c_re * sh_re - c_im * sh_im
        dst[body, im] = src[body, im] + c_re * sh_im + c_im * sh_re
        src, dst = dst, src
        shift *= 2
    ca_re = cpow_ref[0]
    ca_im = cpow_ref[1]
    c_re = st_ref[ENTER_ROW:HEAD_ROWS, re]
    c_im = st_ref[ENTER_ROW:HEAD_ROWS, im]
    for t in range(tiles):
        rows = slice(HEAD_ROWS + SUBLANES * t, HEAD_ROWS + SUBLANES * (t + 1))
        cb_re = jnp.broadcast_to(c_re, (SUBLANES, width))
        cb_im = jnp.broadcast_to(c_im, (SUBLANES, width))
        p_re = src[rows, re] + ca_re * cb_re - ca_im * cb_im
        p_im = src[rows, im] + ca_re * cb_im + ca_im * cb_re
        st_ref[rows, re] = p_re
        st_ref[rows, im] = p_im
        c_re = p_re[SUBLANES - 1:SUBLANES]
        c_im = p_im[SUBLANES - 1:SUBLANES]


def _s5_kernel(u_ref, um_ref, vec_ref, mat_ref, d_ref,
               y_ref, toep_ref, win_ref, wout_ref, cpw_ref, pow_ref, xall_ref, wt_ref, lhs_ref,
               st_ref, zb_ref, carry_ref, meta_ref, *, nsc):
    first_call_step = (pl.program_id(0) == 0) & (pl.program_id(1) == 0)
    r = pl.program_id(1)

    @pl.when(first_call_step)
    def _():
        zb_ref[:, 0:HEAD_ROWS, :] = jnp.zeros((zb_ref.shape[0], HEAD_ROWS, zb_ref.shape[2]), F32)

        def octet(o, carry):
            _s5_assemble_octet(o, vec_ref, mat_ref,
                               toep_ref, win_ref, wout_ref, cpw_ref, pow_ref, xall_ref, wt_ref)
            return carry
        lax.fori_loop(0, OCTETS, octet, 0)

        def meta_state(pair, carry):
            meta = jnp.concatenate([um_ref[2 * pair], um_ref[2 * pair + 1]], axis=1).astype(BF16)
            meta_ref[pair] = jnp.dot(meta, win_ref[pair], preferred_element_type=F32)
            return carry
        lax.fori_loop(0, PAIRS, meta_state, 0)

    @pl.when(r == 0)
    def _():
        carry_ref[...] = meta_ref[...]

    lo = slice(0, GROUP_LANES)
    hi = slice(GROUP_LANES, 2 * GROUP_LANES)
    slots = st_ref.shape[0]

    def pair_block(k, carry):
        pairs = [slots * k + s for s in range(slots)]
        for s, pair in enumerate(pairs):
            lhs_ref[s, :, lo] = u_ref[2 * pair].astype(BF16)
            lhs_ref[s, :, hi] = u_ref[2 * pair + 1].astype(BF16)
            st_ref[s, 0:HEAD_ROWS, :] = carry_ref[pair]
            st_ref[s, HEAD_ROWS:HEAD_ROWS + nsc, :] = jnp.dot(lhs_ref[s], win_ref[pair], preferred_element_type=F32)
        for s, pair in enumerate(pairs):
            _s5_state_scan(st_ref.at[s], zb_ref.at[s], cpw_ref.at[pair], nsc)
            carry_ref[pair] = st_ref[s, nsc:nsc + HEAD_ROWS, :]
        for s, pair in enumerate(pairs):
            s_in = st_ref[s, pl.ds(ENTER_ROW, nsc), :].astype(BF16)
            y_state = jnp.dot(s_in, wout_ref[pair], preferred_element_type=F32)
            d = d_ref[pair]
            for gi, cols in ((0, lo), (1, hi)):
                g = 2 * pair + gi
                y = (jnp.dot(lhs_ref[s, :, cols], toep_ref[g], preferred_element_type=F32)
                     + y_state[:, cols] + d[:, cols] * u_ref[g])
                y_ref[g] = y
        return carry

    lax.fori_loop(0, PAIRS // slots, pair_block, 0)


def _s5_param_rows(lam_re, lam_im, log_dt, b_re, b_im, c_re, c_im, d):
    go = GROUPS_PER_OCTET
    lanes = lambda a: a.reshape(OCTETS, OCT_STATE)
    ldt = jnp.broadcast_to(log_dt[:, None], (SSM_GROUPS, SSM_STATE))
    bt = lambda a: jnp.transpose(a.reshape(OCTETS, go, SSM_STATE, SSM_GROUP),
                                 (0, 3, 1, 2)).reshape(OCTETS, SSM_GROUP, OCT_STATE)
    ct = lambda a: jnp.transpose(a.reshape(OCTETS, go, SSM_GROUP, SSM_STATE),
                                 (0, 2, 1, 3)).reshape(OCTETS, SSM_GROUP, OCT_STATE)
    vec = jnp.stack([lanes(lam_re), lanes(lam_im), lanes(ldt)], axis=1)
    mat = jnp.stack([bt(b_re), bt(b_im), ct(c_re), ct(c_im)], axis=1)
    d_pairs = jnp.broadcast_to(d.reshape(PAIRS, 2, 1, SSM_GROUP),
                               (PAIRS, 2, SUB, SSM_GROUP)).reshape(PAIRS, 1, 2 * GROUP_LANES)
    return vec, mat, d_pairs


def _s5(u, um, params, bsz, seq, rows):
    nsc = rows // SUB
    steps = seq // rows
    n_pow = SUB + SUBLANES
    full = lambda a: pl.BlockSpec(a.shape, lambda b, r: (0,) * a.ndim)
    tok_blk = pl.BlockSpec((SSM_GROUPS, nsc, GROUP_LANES), lambda b, r: (0, b * steps + r, 0))
    return pl.pallas_call(
        functools.partial(_s5_kernel, nsc=nsc),
        grid=(bsz, steps),
        in_specs=[tok_blk, full(um)] + [full(p) for p in params],
        out_specs=tok_blk,
        out_shape=jax.ShapeDtypeStruct(u.shape, F32),
        scratch_shapes=[
            pltpu.VMEM((SSM_GROUPS, GROUP_LANES, GROUP_LANES), BF16),
            pltpu.VMEM((PAIRS, 2 * GROUP_LANES, 2 * PAIR_STATE), BF16),
            pltpu.VMEM((PAIRS, 2 * PAIR_STATE, 2 * GROUP_LANES), BF16),
            pltpu.VMEM((PAIRS, 2, SUBLANES, PAIR_STATE), F32),
            pltpu.VMEM((2, n_pow, OCT_STATE), F32),
            pltpu.VMEM((2, SUB * SSM_GROUP, OCT_STATE), F32),
            pltpu.VMEM((GROUPS_PER_OCTET // 2, 2 * GROUP_LANES, 2 * PAIR_STATE), F32),
            pltpu.VMEM((S5_SLOTS, nsc, 2 * GROUP_LANES), BF16),
            pltpu.VMEM((S5_SLOTS, HEAD_ROWS + nsc, 2 * PAIR_STATE), F32),
            pltpu.VMEM((S5_SLOTS, HEAD_ROWS + nsc, 2 * PAIR_STATE), F32),
            pltpu.VMEM((PAIRS, HEAD_ROWS, 2 * PAIR_STATE), F32),
            pltpu.VMEM((PAIRS, HEAD_ROWS, 2 * PAIR_STATE), F32),
        ],
        compiler_params=pltpu.CompilerParams(
            dimension_semantics=("arbitrary", "arbitrary"), vmem_limit_bytes=VMEM_LIMIT),
        name="s5_mixer",
    )(u, um, *params)


def _mix_ffn_kernel(h1_ref, ret_ref, ynext_ref, gw32_ref, gb_ref, snw_ref, wo32_ref, n3w_ref,
                    wg32_ref, wu32_ref, wd32_ref, fnw_ref, out_ref,
                    gw_ref, wo_ref, wg_ref, wu_ref, wd_ref, acc_ref, ytok_ref):
    step = pl.program_id(0)
    nsub = ynext_ref.shape[1]

    @pl.when(step < WEIGHT_CHUNKS)
    def _():
        _cast_weight_chunks(step, ((gw32_ref, gw_ref), (wo32_ref, wo_ref), (wg32_ref, wg_ref),
                                   (wu32_ref, wu_ref), (wd32_ref, wd_ref)))

    @pl.when(step == WEIGHT_CHUNKS - 1)
    def _():
        _to_token_major(ynext_ref, ytok_ref.at[0], nsub, jax.nn.gelu)

    @pl.when(step >= WEIGHT_CHUNKS)
    def _():
        slot = lax.rem(step - WEIGHT_CHUNKS, 2)
        y = _token_major_rows(ytok_ref.at[slot], nsub)
        _to_token_major(ynext_ref, ytok_ref.at[1 - slot], nsub, jax.nn.gelu)
        z = y * jax.nn.sigmoid(jnp.dot(y.astype(BF16), gw_ref[...], preferred_element_type=F32)
                               + gb_ref[...])
        ssm = _rms(z, snw_ref[...]).astype(BF16)
        mixed = jnp.dot(jnp.concatenate([ret_ref[...], ssm], axis=-1), wo_ref[...],
                        preferred_element_type=F32)
        h2 = h1_ref[...] + mixed
        h3 = _swiglu_half_step(h2, _rms(h2, n3w_ref[...]).astype(BF16), wg_ref, wu_ref, wd_ref,
                               acc_ref)
        out_ref[...] = _rms(h3, fnw_ref[...])


def _mix_ffn(h1, ret, y, glu_w, glu_b, ssm_norm_w, w_out, n3w, wg, wu, wd, fnw, tm):
    rows = h1.shape[0]
    tiles = rows // tm
    row_blk = lambda i: (_tile_index(i), 0)
    next_blk = lambda i: (0, jnp.minimum(_tile_index(i + 1), tiles - 1), 0)
    return pl.pallas_call(
        _mix_ffn_kernel,
        grid=(WEIGHT_CHUNKS + tiles,),
        in_specs=[
            pl.BlockSpec((tm, D_MODEL), row_blk),
            pl.BlockSpec((tm, RET_WIDTH), row_blk),
            pl.BlockSpec((SSM_GROUPS, tm // SUB, GROUP_LANES), next_blk),
            _weight_chunk_spec((SSM_WIDTH, SSM_WIDTH)),
            _resident((1, SSM_WIDTH)),
            _resident((1, SSM_WIDTH)),
            _weight_chunk_spec((D_MODEL, D_MODEL)),
            _resident((1, D_MODEL)),
            _weight_chunk_spec((D_MODEL, D_FF)),
            _weight_chunk_spec((D_MODEL, D_FF)),
            _weight_chunk_spec((D_FF, D_MODEL)),
            _resident((1, D_MODEL)),
        ],
        out_specs=pl.BlockSpec((tm, D_MODEL), row_blk),
        out_shape=jax.ShapeDtypeStruct((rows, D_MODEL), F32),
        scratch_shapes=[
            pltpu.VMEM((SSM_WIDTH, SSM_WIDTH), BF16),
            pltpu.VMEM((D_MODEL, D_MODEL), BF16),
            pltpu.VMEM((D_MODEL, D_FF), BF16),
            pltpu.VMEM((D_MODEL, D_FF), BF16),
            pltpu.VMEM((D_FF, D_MODEL), BF16),
            pltpu.VMEM((tm, D_FF), BF16),
            pltpu.VMEM((2, OCTETS, tm // SUB * TOK_PITCH, LANES), F32),
        ],
        compiler_params=pltpu.CompilerParams(
            dimension_semantics=("arbitrary",), vmem_limit_bytes=VMEM_LIMIT),
        name="mix_ffn",
    )(h1, ret, y, glu_w, glu_b, ssm_norm_w, w_out, n3w, wg, wu, wd, fnw)


def _rope_tables(n_pos):
    freqs = 1.0 / (ROPE_BASE ** (np.arange(0, HEAD_DIM, 2, dtype=np.float64) / HEAD_DIM))
    ang = np.arange(n_pos, dtype=np.float64)[:, None] * freqs[None, :]
    cos = np.cos(ang)
    sin = np.sin(ang)
    cos = np.concatenate([cos, cos], axis=-1).astype(np.float32)
    sin = np.concatenate([-sin, sin], axis=-1).astype(np.float32)
    split = lambda t: (jnp.asarray(t[:N_META]), jnp.asarray(t[N_META:]))
    return split(cos), split(sin)


def kernel(x, meta_tokens, ffn1_norm_w, ffn1_w_gate, ffn1_w_up, ffn1_w_down, mix_norm_w, w_in,
           ret_norm_w, ssm_lambda_re, ssm_lambda_im, ssm_log_dt, ssm_b_re, ssm_b_im, ssm_c_re,
           ssm_c_im, ssm_d, ssm_glu_w, ssm_glu_b, ssm_norm_w, w_out, ffn2_norm_w, ffn2_w_gate,
           ffn2_w_up, ffn2_w_down, final_norm_w):
    bsz, seq, _ = x.shape
    assert ffn1_norm_w.shape[0] == 1, "single layer only"
    tm = 512
    ret_rows = 2048
    s5_rows = 4096
    assert seq % tm == 0 and seq % ret_rows == 0 and seq % s5_rows == 0

    l = 0
    (cos_m, cos), (sin_m, sin) = _rope_tables(N_META + seq)

    h1, q, k, v, g, u, km, vm, um = _ffn_inproj(
        x.reshape(bsz * seq, D_MODEL), meta_tokens, cos, sin, cos_m, sin_m,
        ffn1_norm_w, ffn1_w_gate[l], ffn1_w_up[l], ffn1_w_down[l], mix_norm_w, w_in[l], tm=tm)

    ret = _retention(q, k, v, g, km, vm, ret_norm_w, bsz, seq, ret_rows)

    params = _s5_param_rows(ssm_lambda_re[l], ssm_lambda_im[l], ssm_log_dt[l], ssm_b_re[l],
                            ssm_b_im[l], ssm_c_re[l], ssm_c_im[l], ssm_d[l])
    y = _s5(u, um, params, bsz, seq, s5_rows)

    out = _mix_ffn(h1, ret, y, ssm_glu_w[l], ssm_glu_b, ssm_norm_w, w_out[l], ffn2_norm_w,
                   ffn2_w_gate[l], ffn2_w_up[l], ffn2_w_down[l], final_norm_w.reshape(1, D_MODEL),
                   tm=tm)
    return out.reshape(bsz, seq, D_MODEL)
```

```python
import functools

import jax
import jax.numpy as jnp
import numpy as np
from jax import lax
from jax.experimental import pallas as pl
from jax.experimental.pallas import tpu as pltpu

D_MODEL = 1024
N_META = 16
RET_HEADS = 4
HEAD_DIM = 128
RET_WIDTH = RET_HEADS * HEAD_DIM
SSM_WIDTH = 512
SSM_GROUP = 16
SSM_GROUPS = SSM_WIDTH // SSM_GROUP
SSM_STATE = 64
RET_BLOCK = 256
D_FF = 2816
FFN_RES = 0.5
ROPE_BASE = 10000.0
EPS = 1e-6
IN_PROJ = 4 * RET_WIDTH + SSM_WIDTH

LANES = 128
SUBLANES = 8
HEAD_ROWS = SUBLANES
ENTER_ROW = HEAD_ROWS - 1
OCTETS = SSM_WIDTH // LANES
GROUPS_PER_OCTET = LANES // SSM_GROUP
SUB = N_META
OCT_STATE = GROUPS_PER_OCTET * SSM_STATE
GROUP_LANES = SUB * SSM_GROUP
PAIRS = SSM_GROUPS // 2
PAIR_STATE = 2 * SSM_STATE
S5_SLOTS = 4
TOK_PITCH = 24
FF_CHUNK = 256
WEIGHT_CHUNKS = 8
VMEM_LIMIT = 60 * 1024 * 1024

F32 = jnp.float32
BF16 = jnp.bfloat16


def _rms(x, w):
    return x * lax.rsqrt(jnp.mean(x * x, axis=-1, keepdims=True) + EPS) * w


def _swiglu_half_step(h, n, wg_ref, wu_ref, wd_ref, acc_ref):
    for lo in range(0, D_FF, FF_CHUNK):
        sl = slice(lo, min(lo + FF_CHUNK, D_FF))
        g = jnp.dot(n, wg_ref[:, sl], preferred_element_type=F32)
        u = jnp.dot(n, wu_ref[:, sl], preferred_element_type=F32)
        acc_ref[:, sl] = (g * jax.nn.sigmoid(g) * u).astype(BF16)
    return h + FFN_RES * jnp.dot(acc_ref[...], wd_ref[...], preferred_element_type=F32)


def _rope(x, cos, sin_signed):
    return x * cos + pltpu.roll(x, HEAD_DIM // 2, axis=1) * sin_signed


def _lane_block(shape, width):
    return lax.broadcasted_iota(jnp.int32, shape, 1) // width


def _transpose_lane_blocks(v):
    n = LANES // SSM_GROUP
    assert len(v) == n == GROUPS_PER_OCTET
    blk = _lane_block(v[0].shape, SSM_GROUP)
    d = n // 2
    while d:
        low = (blk & d) == 0
        nxt = list(v)
        for a in range(n):
            if a & d == 0:
                nxt[a] = jnp.where(low, v[a], pltpu.roll(v[a + d], d * SSM_GROUP, axis=1))
                nxt[a + d] = jnp.where(low, pltpu.roll(v[a], LANES - d * SSM_GROUP, axis=1), v[a + d])
        v = nxt
        d //= 2
    return v


def _to_group_dense(tok_ref, out_ref, nsub, row0=0):
    per_half = LANES // SSM_GROUP
    for o in range(OCTETS):
        for hh in range(GROUP_LANES // LANES):
            slabs = [tok_ref[o, pl.ds(hh * per_half + jj, nsub, stride=TOK_PITCH), :]
                     for jj in range(per_half)]
            for g8, rows in enumerate(_transpose_lane_blocks(slabs)):
                out_ref[o * GROUPS_PER_OCTET + g8, row0:row0 + nsub,
                        hh * LANES:(hh + 1) * LANES] = rows


def _to_token_major(gd_ref, tok_ref, nsub, fn):
    per_half = LANES // SSM_GROUP
    for o in range(OCTETS):
        for hh in range(GROUP_LANES // LANES):
            rows = [gd_ref[o * GROUPS_PER_OCTET + g8, :, hh * LANES:(hh + 1) * LANES]
                    for g8 in range(GROUPS_PER_OCTET)]
            for jj, slab in enumerate(_transpose_lane_blocks(rows)):
                tok_ref[o, pl.ds(hh * per_half + jj, nsub, stride=TOK_PITCH), :] = fn(slab)


def _token_major_rows(tok_ref, nsub):
    return jnp.concatenate(
        [jnp.concatenate([tok_ref[o, c * TOK_PITCH:c * TOK_PITCH + SUB, :] for c in range(nsub)],
                         axis=0) for o in range(OCTETS)], axis=-1)


def _cast_weight_chunks(step, pairs):
    for src_ref, dst_ref in pairs:
        rows = src_ref.shape[0]
        r0 = pl.multiple_of(step * rows, rows)
        dst_ref[pl.ds(r0, rows), :] = src_ref[...].astype(BF16)


def _weight_chunk_spec(shape):
    rows = shape[0] // WEIGHT_CHUNKS
    assert rows * WEIGHT_CHUNKS == shape[0] and rows % 16 == 0, shape
    return pl.BlockSpec((rows, shape[1]), lambda i: (jnp.minimum(i, WEIGHT_CHUNKS - 1), 0))


def _tile_index(i):
    return jnp.maximum(i - WEIGHT_CHUNKS, 0)


def _ffn_inproj_kernel(x_ref, cos_ref, sin_ref, xm_ref, cosm_ref, sinm_ref, n1w_ref, wg32_ref,
                       wu32_ref, wd32_ref, n2w_ref, win32_ref,
                       h1_ref, q_ref, k_ref, v_ref, g_ref, u_ref, km_ref, vm_ref, um_ref,
                       wg_ref, wu_ref, wd_ref, win_ref, acc_ref, utok_ref, utokm_ref):
    step = pl.program_id(0)
    k_scale = HEAD_DIM ** -0.5

    def ffn_norm(h):
        return _rms(h, n1w_ref[...]).astype(BF16)

    def mix_norm(h1):
        return _rms(h1, n2w_ref[...]).astype(BF16)

    def in_proj(n, part):
        return jnp.dot(n, win_ref[:, part * RET_WIDTH:(part + 1) * RET_WIDTH],
                       preferred_element_type=F32)

    def rope_heads(p, cos, sin, scale, out_ref):
        for h in range(RET_HEADS):
            hs = slice(h * HEAD_DIM, (h + 1) * HEAD_DIM)
            y = _rope(p[:, hs], cos, sin)
            out_ref[:, hs] = (y if scale is None else y * scale).astype(BF16)

    def split_octets(p, tok_ref):
        for o in range(OCTETS):
            for c in range(p.shape[0] // SUB):
                tok_ref[o, c * TOK_PITCH:c * TOK_PITCH + SUB, :] = (
                    p[c * SUB:(c + 1) * SUB, o * LANES:(o + 1) * LANES])

    @pl.when(step < WEIGHT_CHUNKS)
    def _():
        _cast_weight_chunks(step, ((wg32_ref, wg_ref), (wu32_ref, wu_ref), (wd32_ref, wd_ref),
                                   (win32_ref, win_ref)))

    def token_tile(with_meta):
        tm = x_ref.shape[0]
        x = x_ref[...]
        acc = acc_ref
        if with_meta:
            x = jnp.concatenate([x, xm_ref[...]], axis=0)
        else:
            acc = acc_ref.at[pl.ds(0, tm)]
        h1 = _swiglu_half_step(x, ffn_norm(x), wg_ref, wu_ref, wd_ref, acc)
        h1_ref[...] = h1[0:tm]
        n = mix_norm(h1)
        pu = in_proj(n, 4)
        split_octets(pu[0:tm], utok_ref)
        _to_group_dense(utok_ref, u_ref, u_ref.shape[1])
        cos = cos_ref[...]
        sin = sin_ref[...]
        rope_heads(in_proj(n[0:tm], 0), cos, sin, None, q_ref)
        pk = in_proj(n, 1)
        rope_heads(pk[0:tm], cos, sin, k_scale, k_ref)
        pv = in_proj(n, 2)
        v_ref[...] = pv[0:tm].astype(BF16)
        g_ref[...] = in_proj(n[0:tm], 3)
        if with_meta:
            rope_heads(pk[tm:], cosm_ref[...], sinm_ref[...], k_scale, km_ref)
            vm_ref[...] = pv[tm:].astype(BF16)
            split_octets(pu[tm:], utokm_ref)
            um_ref[...] = jnp.zeros(um_ref.shape, F32)
            _to_group_dense(utokm_ref, um_ref, 1, row0=ENTER_ROW)

    @pl.when(step == WEIGHT_CHUNKS)
    def _():
        token_tile(with_meta=True)

    @pl.when(step > WEIGHT_CHUNKS)
    def _():
        token_tile(with_meta=False)


def _resident(shape):
    nd = len(shape)
    return pl.BlockSpec(shape, lambda *_: (0,) * nd, pipeline_mode=pl.Buffered(1))


def _ffn_inproj(x2, xm, cos, sin, cos_m, sin_m, n1w, wg, wu, wd, n2w, w_in, tm):
    rows = x2.shape[0]
    pos_blocks = cos.shape[0] // tm
    row_blk = lambda i: (_tile_index(i), 0)
    pos_blk = lambda i: (_tile_index(i) % pos_blocks, 0)
    out_shape = (
        jax.ShapeDtypeStruct((rows, D_MODEL), F32),
        jax.ShapeDtypeStruct((rows, RET_WIDTH), BF16),
        jax.ShapeDtypeStruct((rows, RET_WIDTH), BF16),
        jax.ShapeDtypeStruct((rows, RET_WIDTH), BF16),
        jax.ShapeDtypeStruct((rows, RET_WIDTH), F32),
        jax.ShapeDtypeStruct((SSM_GROUPS, rows // SUB, GROUP_LANES), F32),
        jax.ShapeDtypeStruct((N_META, RET_WIDTH), BF16),
        jax.ShapeDtypeStruct((N_META, RET_WIDTH), BF16),
        jax.ShapeDtypeStruct((SSM_GROUPS, HEAD_ROWS, GROUP_LANES), F32),
    )
    return pl.pallas_call(
        _ffn_inproj_kernel,
        grid=(WEIGHT_CHUNKS + rows // tm,),
        in_specs=[
            pl.BlockSpec((tm, D_MODEL), row_blk),
            pl.BlockSpec((tm, HEAD_DIM), pos_blk),
            pl.BlockSpec((tm, HEAD_DIM), pos_blk),
            _resident((N_META, D_MODEL)),
            _resident((N_META, HEAD_DIM)),
            _resident((N_META, HEAD_DIM)),
            _resident((1, D_MODEL)),
            _weight_chunk_spec((D_MODEL, D_FF)),
            _weight_chunk_spec((D_MODEL, D_FF)),
            _weight_chunk_spec((D_FF, D_MODEL)),
            _resident((1, D_MODEL)),
            _weight_chunk_spec((D_MODEL, IN_PROJ)),
        ],
        out_specs=(
            pl.BlockSpec((tm, D_MODEL), row_blk),
            pl.BlockSpec((tm, RET_WIDTH), row_blk),
            pl.BlockSpec((tm, RET_WIDTH), row_blk),
            pl.BlockSpec((tm, RET_WIDTH), row_blk),
            pl.BlockSpec((tm, RET_WIDTH), row_blk),
            pl.BlockSpec((SSM_GROUPS, tm // SUB, GROUP_LANES), lambda i: (0, _tile_index(i), 0)),
            pl.BlockSpec((N_META, RET_WIDTH), lambda i: (0, 0)),
            pl.BlockSpec((N_META, RET_WIDTH), lambda i: (0, 0)),
            pl.BlockSpec((SSM_GROUPS, HEAD_ROWS, GROUP_LANES), lambda i: (0, 0, 0)),
        ),
        out_shape=out_shape,
        scratch_shapes=[
            pltpu.VMEM((D_MODEL, D_FF), BF16),
            pltpu.VMEM((D_MODEL, D_FF), BF16),
            pltpu.VMEM((D_FF, D_MODEL), BF16),
            pltpu.VMEM((D_MODEL, IN_PROJ), BF16),
            pltpu.VMEM((tm + N_META, D_FF), BF16),
            pltpu.VMEM((OCTETS, tm // SUB * TOK_PITCH, LANES), F32),
            pltpu.VMEM((OCTETS, TOK_PITCH, LANES), F32),
        ],
        compiler_params=pltpu.CompilerParams(
            dimension_semantics=("arbitrary",), vmem_limit_bytes=VMEM_LIMIT),
        name="ffn_inproj",
    )(x2, cos, sin, xm, cos_m, sin_m, n1w, wg, wu, wd, n2w, w_in)


def _retention_kernel(q_ref, k_ref, v_ref, g_ref, km_ref, vm_ref, mask_ref, wq_ref, wk_ref, wm_ref,
                      gc_ref, nw_ref, o_ref, state_ref, *, rows):
    tn = (((0,), (0,)), ((), ()))
    nt = (((1,), (1,)), ((), ()))

    @pl.when(pl.program_id(1) == 0)
    def _():
        for h in range(RET_HEADS):
            hs = slice(h * HEAD_DIM, (h + 1) * HEAD_DIM)
            kw = (km_ref[:, hs].astype(F32) * wm_ref[h]).astype(BF16)
            state_ref[h] = lax.dot_general(kw, vm_ref[:, hs], tn, preferred_element_type=F32)

    def chunk(i, carry):
        r0 = pl.multiple_of(i * RET_BLOCK, RET_BLOCK)
        rs = pl.ds(r0, RET_BLOCK)
        for h in range(RET_HEADS):
            hs = slice(h * HEAD_DIM, (h + 1) * HEAD_DIM)
            qh = q_ref[rs, hs]
            kh = k_ref[rs, hs]
            vh = v_ref[rs, hs]
            st = state_ref[h]
            s = lax.dot_general(qh, kh, nt, preferred_element_type=F32) * mask_ref[h]
            o = (jnp.dot(s.astype(BF16), vh, preferred_element_type=F32)
                 + jnp.dot(qh, st.astype(BF16), preferred_element_type=F32) * wq_ref[h])
            kw = (kh.astype(F32) * wk_ref[h]).astype(BF16)
            state_ref[h] = gc_ref[h] * st + lax.dot_general(kw, vh, tn, preferred_element_type=F32)
            mu = jnp.mean(o, axis=-1, keepdims=True)
            d = o - mu
            var = jnp.mean(d * d, axis=-1, keepdims=True)
            y = d * lax.rsqrt(var + EPS) * nw_ref[:, hs]
            gate = g_ref[rs, hs]
            o_ref[rs, hs] = (gate * jax.nn.sigmoid(gate) * y).astype(BF16)
        return carry

    lax.fori_loop(0, rows // RET_BLOCK, chunk, 0, unroll=4)


def _retention_tables():
    log_g = np.log(1.0 - 2.0 ** (-5.0 - np.arange(RET_HEADS, dtype=np.float64)))
    i = np.arange(RET_BLOCK)
    diff = i[:, None] - i[None, :]
    mask = np.where(diff[None] >= 0, np.exp(log_g[:, None, None] * np.maximum(diff, 0)[None]), 0.0)
    pos = np.arange(RET_BLOCK, dtype=np.float64)
    full = lambda w: np.broadcast_to(w[:, :, None], w.shape + (HEAD_DIM,))
    w_q = full(np.exp(log_g[:, None] * (pos + 1.0)[None]))
    w_k = full(np.exp(log_g[:, None] * (RET_BLOCK - 1 - pos)[None]))
    w_m = full(np.exp(log_g[:, None] * (N_META - 1 - np.arange(N_META, dtype=np.float64))[None]))
    g_c = np.broadcast_to(np.exp(log_g * RET_BLOCK)[:, None, None], (RET_HEADS, HEAD_DIM, HEAD_DIM))
    return tuple(jnp.asarray(t, dtype=F32) for t in (mask, w_q, w_k, w_m, g_c))


def _retention(q, k, v, g, km, vm, ret_norm_w, bsz, seq, rows):
    mask, w_q, w_k, w_m, g_c = _retention_tables()
    steps = seq // rows
    blk = lambda b, c: (b * steps + c, 0)
    full3 = lambda a: pl.BlockSpec(a.shape, lambda b, c: (0, 0, 0))
    full2 = lambda a: pl.BlockSpec(a.shape, lambda b, c: (0, 0))
    return pl.pallas_call(
        functools.partial(_retention_kernel, rows=rows),
        grid=(bsz, steps),
        in_specs=[pl.BlockSpec((rows, RET_WIDTH), blk)] * 4
        + [full2(km), full2(vm), full3(mask), full3(w_q), full3(w_k), full3(w_m), full3(g_c),
           full2(ret_norm_w)],
        out_specs=pl.BlockSpec((rows, RET_WIDTH), blk),
        out_shape=jax.ShapeDtypeStruct((bsz * seq, RET_WIDTH), BF16),
        scratch_shapes=[pltpu.VMEM((RET_HEADS, HEAD_DIM, HEAD_DIM), F32)],
        compiler_params=pltpu.CompilerParams(
            dimension_semantics=("arbitrary", "arbitrary"), vmem_limit_bytes=VMEM_LIMIT),
        name="retention",
    )(q, k, v, g, km, vm, mask, w_q, w_k, w_m, g_c, ret_norm_w)


def _shift_lanes_zero_fill(halves, shift, lane):
    h0, h1 = halves
    whole, s = divmod(shift, LANES)
    r0 = h0 if s == 0 else pltpu.roll(h0, s, axis=1)
    zero = jnp.zeros_like(h0)
    if whole == 1:
        return zero, jnp.where(lane >= s, r0, zero)
    r1 = h1 if s == 0 else pltpu.roll(h1, s, axis=1)
    return jnp.where(lane >= s, r0, zero), jnp.where(lane >= s, r1, r0)


def _s5_assemble_octet(o, vec_ref, mat_ref,
                       toep_ref, win_ref, wout_ref, cpw_ref, pow_ref, xall_ref, wt_ref):
    vec = vec_ref[o]
    lre = vec[0:1]
    lim = vec[1:2]
    dt = jnp.exp(vec[2:3])
    n_pow = pow_ref.shape[1]
    ell = lax.broadcasted_iota(jnp.int32, (n_pow, OCT_STATE), 0).astype(F32)
    mag = jnp.exp(ell * (lre * dt))
    ang = ell * (lim * dt)
    pow_ref[0] = mag * jnp.cos(ang)
    pow_ref[1] = mag * jnp.sin(ang)
    ell = ((lax.broadcasted_iota(jnp.int32, (SUBLANES, OCT_STATE), 0) + 1) * SUB).astype(F32)
    mag = jnp.exp(ell * (lre * dt))
    ang = ell * (lim * dt)
    cp_re = mag * jnp.cos(ang)
    cp_im = mag * jnp.sin(ang)
    a_re = pow_ref[0, 1:2, :]
    a_im = pow_ref[1, 1:2, :]
    den = lre * lre + lim * lim
    num_re = a_re - 1.0
    coef_re = (num_re * lre + a_im * lim) / den
    coef_im = (a_im * lre - num_re * lim) / den
    btr = mat_ref[o, 0]
    bti = mat_ref[o, 1]
    bbar_re = coef_re * btr - coef_im * bti
    bbar_im = coef_re * bti + coef_im * btr
    ctr = mat_ref[o, 2]
    cti = mat_ref[o, 3]

    pairs_per_octet = GROUPS_PER_OCTET // 2
    pair_lane_group = _lane_block((SSM_GROUP, PAIR_STATE), SSM_STATE)
    for l in range(SUB):
        pr = pow_ref[0, l:l + 1, :]
        pi = pow_ref[1, l:l + 1, :]
        x_re = pr * bbar_re - pi * bbar_im
        x_im = pr * bbar_im + pi * bbar_re
        rows = slice(l * SSM_GROUP, (l + 1) * SSM_GROUP)
        xall_ref[0, rows, :] = x_re
        xall_ref[1, rows, :] = x_im
        pr1 = pow_ref[0, l + 1:l + 2, :]
        pi1 = pow_ref[1, l + 1:l + 2, :]
        w_re = pr1 * ctr - pi1 * cti
        w_im = pr1 * cti + pi1 * ctr
        j = SUB - 1 - l
        for pp in range(pairs_per_octet):
            sl = slice(pp * PAIR_STATE, (pp + 1) * PAIR_STATE)
            pair = o * pairs_per_octet + pp
            for gi in range(2):
                keep = pair_lane_group == gi
                in_rows = slice(gi * GROUP_LANES + j * SSM_GROUP, gi * GROUP_LANES + (j + 1) * SSM_GROUP)
                out_rows = slice(gi * GROUP_LANES + l * SSM_GROUP, gi * GROUP_LANES + (l + 1) * SSM_GROUP)
                for part, x, w in ((0, x_re, w_re), (1, x_im, -w_im)):
                    cols = slice(part * PAIR_STATE, (part + 1) * PAIR_STATE)
                    win_ref[pair, in_rows, cols] = jnp.where(keep, x[:, sl], 0.0).astype(BF16)
                    wt_ref[pp, out_rows, cols] = jnp.where(keep, w[:, sl], 0.0)
    for pp in range(pairs_per_octet):
        pair = o * pairs_per_octet + pp
        wout_ref[pair] = wt_ref[pp].T.astype(BF16)
        sl = slice(pp * PAIR_STATE, (pp + 1) * PAIR_STATE)
        cpw_ref[pair, 0] = cp_re[:, sl]
        cpw_ref[pair, 1] = cp_im[:, sl]

    row = lax.broadcasted_iota(jnp.int32, (LANES, OCT_STATE), 0)
    col = lax.broadcasted_iota(jnp.int32, (LANES, OCT_STATE), 1)
    same_group = (row // SSM_GROUP) == (col // SSM_STATE)

    def block_diag_t(x):
        return jnp.where(same_group, jnp.concatenate([x] * GROUPS_PER_OCTET, axis=0), 0.0).T

    k_all = (jnp.dot(xall_ref[0], block_diag_t(ctr), precision=lax.Precision.HIGHEST,
                     preferred_element_type=F32)
             - jnp.dot(xall_ref[1], block_diag_t(cti), precision=lax.Precision.HIGHEST,
                       preferred_element_type=F32))
    per_half = LANES // SSM_GROUP
    blk = _lane_block((SSM_GROUP, LANES), SSM_GROUP)
    lane = lax.broadcasted_iota(jnp.int32, (SSM_GROUP, LANES), 1)
    for g8 in range(GROUPS_PER_OCTET):
        strip = []
        for hh in range(GROUP_LANES // LANES):
            acc = jnp.zeros((SSM_GROUP, LANES), F32)
            for ll in range(per_half):
                lag = hh * per_half + ll
                k_lag = k_all[lag * SSM_GROUP:(lag + 1) * SSM_GROUP, :]
                shift = ((ll - g8) * SSM_GROUP) % LANES
                rolled = k_lag if shift == 0 else pltpu.roll(k_lag, shift, axis=1)
                acc = jnp.where(blk == ll, rolled, acc)
            strip.append(acc)
        for j in range(SUB):
            h0, h1 = _shift_lanes_zero_fill(strip, j * SSM_GROUP, lane)
            rows = slice(j * SSM_GROUP, (j + 1) * SSM_GROUP)
            toep_ref[o * GROUPS_PER_OCTET + g8, rows, 0:LANES] = h0.astype(BF16)
            toep_ref[o * GROUPS_PER_OCTET + g8, rows, LANES:2 * LANES] = h1.astype(BF16)


def _s5_state_scan(st_ref, zb_ref, cpow_ref, nsc):
    tiles = nsc // SUBLANES
    width = st_ref.shape[1] // 2
    re = slice(0, width)
    im = slice(width, 2 * width)
    body = slice(HEAD_ROWS, HEAD_ROWS + nsc)
    rowmod = lax.broadcasted_iota(jnp.int32, (SUBLANES, width), 0)
    src, dst = st_ref, zb_ref
    shift = 1
    while shift < SUBLANES:
        keep = rowmod >= shift
        power = slice(shift - 1, shift)
        c_re = jnp.concatenate([jnp.where(keep, cpow_ref[0, power, :], 0.0)] * tiles, axis=0)
        c_im = jnp.concatenate([jnp.where(keep, cpow_ref[1, power, :], 0.0)] * tiles, axis=0)
        sh_re = src[pl.ds(HEAD_ROWS - shift, nsc), re]
        sh_im = src[pl.ds(HEAD_ROWS - shift, nsc), im]
        dst[body, re] = src[body, re] + c_re * sh_re - c_im * sh_im
        dst[body, im] = src[body, im] + c_re * sh_im + c_im * sh_re
        src, dst = dst, src
        shift *= 2
    ca_re = cpow_ref[0]
    ca_im = cpow_ref[1]
    c_re = st_ref[ENTER_ROW:HEAD_ROWS, re]
    c_im = st_ref[ENTER_ROW:HEAD_ROWS, im]
    for t in range(tiles):
        rows = slice(HEAD_ROWS + SUBLANES * t, HEAD_ROWS + SUBLANES * (t + 1))
        cb_re = jnp.broadcast_to(c_re, (SUBLANES, width))
        cb_im = jnp.broadcast_to(c_im, (SUBLANES, width))
        p_re = src[rows, re] + ca_re * cb_re - ca_im * cb_im
        p_im = src[rows, im] + ca_re * cb_im + ca_im * cb_re
        st_ref[rows, re] = p_re
        st_ref[rows, im] = p_im
        c_re = p_re[SUBLANES - 1:SUBLANES]
        c_im = p_im[SUBLANES - 1:SUBLANES]


def _s5_kernel(u_ref, um_ref, vec_ref, mat_ref, d_ref,
               y_ref, toep_ref, win_ref, wout_ref, cpw_ref, pow_ref, xall_ref, wt_ref, lhs_ref,
               st_ref, zb_ref, carry_ref, meta_ref, *, nsc):
    first_call_step = (pl.program_id(0) == 0) & (pl.program_id(1) == 0)
    r = pl.program_id(1)

    @pl.when(first_call_step)
    def _():
        zb_ref[:, 0:HEAD_ROWS, :] = jnp.zeros((zb_ref.shape[0], HEAD_ROWS, zb_ref.shape[2]), F32)

        def octet(o, carry):
            _s5_assemble_octet(o, vec_ref, mat_ref,
                               toep_ref, win_ref, wout_ref, cpw_ref, pow_ref, xall_ref, wt_ref)
            return carry
        lax.fori_loop(0, OCTETS, octet, 0)

        def meta_state(pair, carry):
            meta = jnp.concatenate([um_ref[2 * pair], um_ref[2 * pair + 1]], axis=1).astype(BF16)
            meta_ref[pair] = jnp.dot(meta, win_ref[pair], preferred_element_type=F32)
            return carry
        lax.fori_loop(0, PAIRS, meta_state, 0)

    @pl.when(r == 0)
    def _():
        carry_ref[...] = meta_ref[...]

    lo = slice(0, GROUP_LANES)
    hi = slice(GROUP_LANES, 2 * GROUP_LANES)
    slots = st_ref.shape[0]

    def pair_block(k, carry):
        pairs = [slots * k + s for s in range(slots)]
        for s, pair in enumerate(pairs):
            lhs_ref[s, :, lo] = u_ref[2 * pair].astype(BF16)
            lhs_ref[s, :, hi] = u_ref[2 * pair + 1].astype(BF16)
            st_ref[s, 0:HEAD_ROWS, :] = carry_ref[pair]
            st_ref[s, HEAD_ROWS:HEAD_ROWS + nsc, :] = jnp.dot(lhs_ref[s], win_ref[pair], preferred_element_type=F32)
        for s, pair in enumerate(pairs):
            _s5_state_scan(st_ref.at[s], zb_ref.at[s], cpw_ref.at[pair], nsc)
            carry_ref[pair] = st_ref[s, nsc:nsc + HEAD_ROWS, :]
        for s, pair in enumerate(pairs):
            s_in = st_ref[s, pl.ds(ENTER_ROW, nsc), :].astype(BF16)
            y_state = jnp.dot(s_in, wout_ref[pair], preferred_element_type=F32)
            d = d_ref[pair]
            for gi, cols in ((0, lo), (1, hi)):
                g = 2 * pair + gi
                y = (jnp.dot(lhs_ref[s, :, cols], toep_ref[g], preferred_element_type=F32)
                     + y_state[:, cols] + d[:, cols] * u_ref[g])
                y_ref[g] = y
        return carry

    lax.fori_loop(0, PAIRS // slots, pair_block, 0)


def _s5_param_rows(lam_re, lam_im, log_dt, b_re, b_im, c_re, c_im, d):
    go = GROUPS_PER_OCTET
    lanes = lambda a: a.reshape(OCTETS, OCT_STATE)
    ldt = jnp.broadcast_to(log_dt[:, None], (SSM_GROUPS, SSM_STATE))
    bt = lambda a: jnp.transpose(a.reshape(OCTETS, go, SSM_STATE, SSM_GROUP),
                                 (0, 3, 1, 2)).reshape(OCTETS, SSM_GROUP, OCT_STATE)
    ct = lambda a: jnp.transpose(a.reshape(OCTETS, go, SSM_GROUP, SSM_STATE),
                                 (0, 2, 1, 3)).reshape(OCTETS, SSM_GROUP, OCT_STATE)
    vec = jnp.stack([lanes(lam_re), lanes(lam_im), lanes(ldt)], axis=1)
    mat = jnp.stack([bt(b_re), bt(b_im), ct(c_re), ct(c_im)], axis=1)
    d_pairs = jnp.broadcast_to(d.reshape(PAIRS, 2, 1, SSM_GROUP),
                               (PAIRS, 2, SUB, SSM_GROUP)).reshape(PAIRS, 1, 2 * GROUP_LANES)
    return vec, mat, d_pairs


def _s5(u, um, params, bsz, seq, rows):
    nsc = rows // SUB
    steps = seq // rows
    n_pow = SUB + SUBLANES
    full = lambda a: pl.BlockSpec(a.shape, lambda b, r: (0,) * a.ndim)
    tok_blk = pl.BlockSpec((SSM_GROUPS, nsc, GROUP_LANES), lambda b, r: (0, b * steps + r, 0))
    return pl.pallas_call(
        functools.partial(_s5_kernel, nsc=nsc),
        grid=(bsz, steps),
        in_specs=[tok_blk, full(um)] + [full(p) for p in params],
        out_specs=tok_blk,
        out_shape=jax.ShapeDtypeStruct(u.shape, F32),
        scratch_shapes=[
            pltpu.VMEM((SSM_GROUPS, GROUP_LANES, GROUP_LANES), BF16),
            pltpu.VMEM((PAIRS, 2 * GROUP_LANES, 2 * PAIR_STATE), BF16),
            pltpu.VMEM((PAIRS, 2 * PAIR_STATE, 2 * GROUP_LANES), BF16),
            pltpu.VMEM((PAIRS, 2, SUBLANES, PAIR_STATE), F32),
            pltpu.VMEM((2, n_pow, OCT_STATE), F32),
            pltpu.VMEM((2, SUB * SSM_GROUP, OCT_STATE), F32),
            pltpu.VMEM((GROUPS_PER_OCTET // 2, 2 * GROUP_LANES, 2 * PAIR_STATE), F32),
            pltpu.VMEM((S5_SLOTS, nsc, 2 * GROUP_LANES), BF16),
            pltpu.VMEM((S5_SLOTS, HEAD_ROWS + nsc, 2 * PAIR_STATE), F32),
            pltpu.VMEM((S5_SLOTS, HEAD_ROWS + nsc, 2 * PAIR_STATE), F32),
            pltpu.VMEM((PAIRS, HEAD_ROWS, 2 * PAIR_STATE), F32),
            pltpu.VMEM((PAIRS, HEAD_ROWS, 2 * PAIR_STATE), F32),
        ],
        compiler_params=pltpu.CompilerParams(
            dimension_semantics=("arbitrary", "arbitrary"), vmem_limit_bytes=VMEM_LIMIT),
        name="s5_mixer",
    )(u, um, *params)


def _mix_ffn_kernel(h1_ref, ret_ref, ynext_ref, gw32_ref, gb_ref, snw_ref, wo32_ref, n3w_ref,
                    wg32_ref, wu32_ref, wd32_ref, fnw_ref, out_ref,
                    gw_ref, wo_ref, wg_ref, wu_ref, wd_ref, acc_ref, ytok_ref):
    step = pl.program_id(0)
    nsub = ynext_ref.shape[1]

    @pl.when(step < WEIGHT_CHUNKS)
    def _():
        _cast_weight_chunks(step, ((gw32_ref, gw_ref), (wo32_ref, wo_ref), (wg32_ref, wg_ref),
                                   (wu32_ref, wu_ref), (wd32_ref, wd_ref)))

    @pl.when(step == WEIGHT_CHUNKS - 1)
    def _():
        _to_token_major(ynext_ref, ytok_ref.at[0], nsub, jax.nn.gelu)

    @pl.when(step >= WEIGHT_CHUNKS)
    def _():
        slot = lax.rem(step - WEIGHT_CHUNKS, 2)
        y = _token_major_rows(ytok_ref.at[slot], nsub)
        _to_token_major(ynext_ref, ytok_ref.at[1 - slot], nsub, jax.nn.gelu)
        z = y * jax.nn.sigmoid(jnp.dot(y.astype(BF16), gw_ref[...], preferred_element_type=F32)
                               + gb_ref[...])
        ssm = _rms(z, snw_ref[...]).astype(BF16)
        mixed = jnp.dot(jnp.concatenate([ret_ref[...], ssm], axis=-1), wo_ref[...],
                        preferred_element_type=F32)
        h2 = h1_ref[...] + mixed
        h3 = _swiglu_half_step(h2, _rms(h2, n3w_ref[...]).astype(BF16), wg_ref, wu_ref, wd_ref,
                               acc_ref)
        out_ref[...] = _rms(h3, fnw_ref[...])


def _mix_ffn(h1, ret, y, glu_w, glu_b, ssm_norm_w, w_out, n3w, wg, wu, wd, fnw, tm):
    rows = h1.shape[0]
    tiles = rows // tm
    row_blk = lambda i: (_tile_index(i), 0)
    next_blk = lambda i: (0, jnp.minimum(_tile_index(i + 1), tiles - 1), 0)
    return pl.pallas_call(
        _mix_ffn_kernel,
        grid=(WEIGHT_CHUNKS + tiles,),
        in_specs=[
            pl.BlockSpec((tm, D_MODEL), row_blk),
            pl.BlockSpec((tm, RET_WIDTH), row_blk),
            pl.BlockSpec((SSM_GROUPS, tm // SUB, GROUP_LANES), next_blk),
            _weight_chunk_spec((SSM_WIDTH, SSM_WIDTH)),
            _resident((1, SSM_WIDTH)),
            _resident((1, SSM_WIDTH)),
            _weight_chunk_spec((D_MODEL, D_MODEL)),
            _resident((1, D_MODEL)),
            _weight_chunk_spec((D_MODEL, D_FF)),
            _weight_chunk_spec((D_MODEL, D_FF)),
            _weight_chunk_spec((D_FF, D_MODEL)),
            _resident((1, D_MODEL)),
        ],
        out_specs=pl.BlockSpec((tm, D_MODEL), row_blk),
        out_shape=jax.ShapeDtypeStruct((rows, D_MODEL), F32),
        scratch_shapes=[
            pltpu.VMEM((SSM_WIDTH, SSM_WIDTH), BF16),
            pltpu.VMEM((D_MODEL, D_MODEL), BF16),
            pltpu.VMEM((D_MODEL, D_FF), BF16),
            pltpu.VMEM((D_MODEL, D_FF), BF16),
            pltpu.VMEM((D_FF, D_MODEL), BF16),
            pltpu.VMEM((tm, D_FF), BF16),
            pltpu.VMEM((2, OCTETS, tm // SUB * TOK_PITCH, LANES), F32),
        ],
        compiler_params=pltpu.CompilerParams(
            dimension_semantics=("arbitrary",), vmem_limit_bytes=VMEM_LIMIT),
        name="mix_ffn",
    )(h1, ret, y, glu_w, glu_b, ssm_norm_w, w_out, n3w, wg, wu, wd, fnw)


def _rope_tables(n_pos):
    freqs = 1.0 / (ROPE_BASE ** (np.arange(0, HEAD_DIM, 2, dtype=np.float64) / HEAD_DIM))
    ang = np.arange(n_pos, dtype=np.float64)[:, None] * freqs[None, :]
    cos = np.cos(ang)
    sin = np.sin(ang)
    cos = np.concatenate([cos, cos], axis=-1).astype(np.float32)
    sin = np.concatenate([-sin, sin], axis=-1).astype(np.float32)
    split = lambda t: (jnp.asarray(t[:N_META]), jnp.asarray(t[N_META:]))
    return split(cos), split(sin)


def kernel(x, meta_tokens, ffn1_norm_w, ffn1_w_gate, ffn1_w_up, ffn1_w_down, mix_norm_w, w_in,
           ret_norm_w, ssm_lambda_re, ssm_lambda_im, ssm_log_dt, ssm_b_re, ssm_b_im, ssm_c_re,
           ssm_c_im, ssm_d, ssm_glu_w, ssm_glu_b, ssm_norm_w, w_out, ffn2_norm_w, ffn2_w_gate,
           ffn2_w_up, ffn2_w_down, final_norm_w):
    bsz, seq, _ = x.shape
    assert ffn1_norm_w.shape[0] == 1, "single layer only"
    tm = 512
    ret_rows = 2048
    s5_rows = 4096
    assert seq % tm == 0 and seq % ret_rows == 0 and seq % s5_rows == 0

    l = 0
    (cos_m, cos), (sin_m, sin) = _rope_tables(N_META + seq)

    h1, q, k, v, g, u, km, vm, um = _ffn_inproj(
        x.reshape(bsz * seq, D_MODEL), meta_tokens, cos, sin, cos_m, sin_m,
        ffn1_norm_w, ffn1_w_gate[l], ffn1_w_up[l], ffn1_w_down[l], mix_norm_w, w_in[l], tm=tm)

    ret = _retention(q, k, v, g, km, vm, ret_norm_w, bsz, seq, ret_rows)

    params = _s5_param_rows(ssm_lambda_re[l], ssm_lambda_im[l], ssm_log_dt[l], ssm_b_re[l],
                            ssm_b_im[l], ssm_c_re[l], ssm_c_im[l], ssm_d[l])
    y = _s5(u, um, params, bsz, seq, s5_rows)

    out = _mix_ffn(h1, ret, y, ssm_glu_w[l], ssm_glu_b, ssm_norm_w, w_out[l], ffn2_norm_w,
                   ffn2_w_gate[l], ffn2_w_up[l], ffn2_w_down[l], final_norm_w.reshape(1, D_MODEL),
                   tm=tm)
    return out.reshape(bsz, seq, D_MODEL)
```

```python
import functools

import jax
import jax.numpy as jnp
import numpy as np
from jax import lax
from jax.experimental import pallas as pl
from jax.experimental.pallas import tpu as pltpu

D_MODEL = 1024
N_META = 16
RET_HEADS = 4
HEAD_DIM = 128
RET_WIDTH = RET_HEADS * HEAD_DIM
SSM_WIDTH = 512
SSM_GROUP = 16
SSM_GROUPS = SSM_WIDTH // SSM_GROUP
SSM_STATE = 64
RET_BLOCK = 256
D_FF = 2816
FFN_RES = 0.5
ROPE_BASE = 10000.0
EPS = 1e-6
IN_PROJ = 4 * RET_WIDTH + SSM_WIDTH

LANES = 128
SUBLANES = 8
HEAD_ROWS = SUBLANES
ENTER_ROW = HEAD_ROWS - 1
OCTETS = SSM_WIDTH // LANES
GROUPS_PER_OCTET = LANES // SSM_GROUP
SUB = N_META
OCT_STATE = GROUPS_PER_OCTET * SSM_STATE
GROUP_LANES = SUB * SSM_GROUP
PAIRS = SSM_GROUPS // 2
PAIR_STATE = 2 * SSM_STATE
S5_SLOTS = 8
TOK_PITCH = 24
FF_CHUNK = 256
WEIGHT_CHUNKS = 8
VMEM_LIMIT = 60 * 1024 * 1024

F32 = jnp.float32
BF16 = jnp.bfloat16


def _rms(x, w):
    return x * lax.rsqrt(jnp.mean(x * x, axis=-1, keepdims=True) + EPS) * w


def _swiglu_half_step(h, n, wg_ref, wu_ref, wd_ref, acc_ref):
    for lo in range(0, D_FF, FF_CHUNK):
        sl = slice(lo, min(lo + FF_CHUNK, D_FF))
        g = jnp.dot(n, wg_ref[:, sl], preferred_element_type=F32)
        u = jnp.dot(n, wu_ref[:, sl], preferred_element_type=F32)
        acc_ref[:, sl] = (g * jax.nn.sigmoid(g) * u).astype(BF16)
    return h + FFN_RES * jnp.dot(acc_ref[...], wd_ref[...], preferred_element_type=F32)


def _rope(x, cos, sin_signed):
    return x * cos + pltpu.roll(x, HEAD_DIM // 2, axis=1) * sin_signed


def _lane_block(shape, width):
    return lax.broadcasted_iota(jnp.int32, shape, 1) // width


def _transpose_lane_blocks(v):
    n = LANES // SSM_GROUP
    assert len(v) == n == GROUPS_PER_OCTET
    blk = _lane_block(v[0].shape, SSM_GROUP)
    d = n // 2
    while d:
        low = (blk & d) == 0
        nxt = list(v)
        for a in range(n):
            if a & d == 0:
                nxt[a] = jnp.where(low, v[a], pltpu.roll(v[a + d], d * SSM_GROUP, axis=1))
                nxt[a + d] = jnp.where(low, pltpu.roll(v[a], LANES - d * SSM_GROUP, axis=1), v[a + d])
        v = nxt
        d //= 2
    return v


def _to_group_dense(tok_ref, out_ref, nsub, row0=0):
    per_half = LANES // SSM_GROUP
    for o in range(OCTETS):
        for hh in range(GROUP_LANES // LANES):
            slabs = [tok_ref[o, pl.ds(hh * per_half + jj, nsub, stride=TOK_PITCH), :]
                     for jj in range(per_half)]
            for g8, rows in enumerate(_transpose_lane_blocks(slabs)):
                out_ref[o * GROUPS_PER_OCTET + g8, row0:row0 + nsub,
                        hh * LANES:(hh + 1) * LANES] = rows


def _to_token_major(gd_ref, tok_ref, nsub, fn):
    per_half = LANES // SSM_GROUP
    for o in range(OCTETS):
        for hh in range(GROUP_LANES // LANES):
            rows = [gd_ref[o * GROUPS_PER_OCTET + g8, :, hh * LANES:(hh + 1) * LANES]
                    for g8 in range(GROUPS_PER_OCTET)]
            for jj, slab in enumerate(_transpose_lane_blocks(rows)):
                tok_ref[o, pl.ds(hh * per_half + jj, nsub, stride=TOK_PITCH), :] = fn(slab)


def _token_major_rows(tok_ref, nsub):
    return jnp.concatenate(
        [jnp.concatenate([tok_ref[o, c * TOK_PITCH:c * TOK_PITCH + SUB, :] for c in range(nsub)],
                         axis=0) for o in range(OCTETS)], axis=-1)


def _cast_weight_chunks(step, pairs):
    for src_ref, dst_ref in pairs:
        rows = src_ref.shape[0]
        r0 = pl.multiple_of(step * rows, rows)
        dst_ref[pl.ds(r0, rows), :] = src_ref[...].astype(BF16)


def _weight_chunk_spec(shape):
    rows = shape[0] // WEIGHT_CHUNKS
    assert rows * WEIGHT_CHUNKS == shape[0] and rows % 16 == 0, shape
    return pl.BlockSpec((rows, shape[1]), lambda i: (jnp.minimum(i, WEIGHT_CHUNKS - 1), 0))


def _tile_index(i):
    return jnp.maximum(i - WEIGHT_CHUNKS, 0)


def _ffn_inproj_kernel(x_ref, cos_ref, sin_ref, xm_ref, cosm_ref, sinm_ref, n1w_ref, wg32_ref,
                       wu32_ref, wd32_ref, n2w_ref, win32_ref,
                       h1_ref, q_ref, k_ref, v_ref, g_ref, u_ref, km_ref, vm_ref, um_ref,
                       wg_ref, wu_ref, wd_ref, win_ref, acc_ref, utok_ref, utokm_ref):
    step = pl.program_id(0)
    k_scale = HEAD_DIM ** -0.5

    def ffn_norm(h):
        return _rms(h, n1w_ref[...]).astype(BF16)

    def mix_norm(h1):
        return _rms(h1, n2w_ref[...]).astype(BF16)

    def in_proj(n, part):
        return jnp.dot(n, win_ref[:, part * RET_WIDTH:(part + 1) * RET_WIDTH],
                       preferred_element_type=F32)

    def rope_heads(p, cos, sin, scale, out_ref):
        for h in range(RET_HEADS):
            hs = slice(h * HEAD_DIM, (h + 1) * HEAD_DIM)
            y = _rope(p[:, hs], cos, sin)
            out_ref[:, hs] = (y if scale is None else y * scale).astype(BF16)

    def split_octets(p, tok_ref):
        for o in range(OCTETS):
            for c in range(p.shape[0] // SUB):
                tok_ref[o, c * TOK_PITCH:c * TOK_PITCH + SUB, :] = (
                    p[c * SUB:(c + 1) * SUB, o * LANES:(o + 1) * LANES])

    @pl.when(step < WEIGHT_CHUNKS)
    def _():
        _cast_weight_chunks(step, ((wg32_ref, wg_ref), (wu32_ref, wu_ref), (wd32_ref, wd_ref),
                                   (win32_ref, win_ref)))

    def token_tile(with_meta):
        tm = x_ref.shape[0]
        x = x_ref[...]
        acc = acc_ref
        if with_meta:
            x = jnp.concatenate([x, xm_ref[...]], axis=0)
        else:
            acc = acc_ref.at[pl.ds(0, tm)]
        h1 = _swiglu_half_step(x, ffn_norm(x), wg_ref, wu_ref, wd_ref, acc)
        h1_ref[...] = h1[0:tm]
        n = mix_norm(h1)
        pu = in_proj(n, 4)
        split_octets(pu[0:tm], utok_ref)
        _to_group_dense(utok_ref, u_ref, u_ref.shape[1])
        cos = cos_ref[...]
        sin = sin_ref[...]
        rope_heads(in_proj(n[0:tm], 0), cos, sin, None, q_ref)
        pk = in_proj(n, 1)
        rope_heads(pk[0:tm], cos, sin, k_scale, k_ref)
        pv = in_proj(n, 2)
        v_ref[...] = pv[0:tm].astype(BF16)
        g_ref[...] = in_proj(n[0:tm], 3)
        if with_meta:
            rope_heads(pk[tm:], cosm_ref[...], sinm_ref[...], k_scale, km_ref)
            vm_ref[...] = pv[tm:].astype(BF16)
            split_octets(pu[tm:], utokm_ref)
            um_ref[...] = jnp.zeros(um_ref.shape, F32)
            _to_group_dense(utokm_ref, um_ref, 1, row0=ENTER_ROW)

    @pl.when(step == WEIGHT_CHUNKS)
    def _():
        token_tile(with_meta=True)

    @pl.when(step > WEIGHT_CHUNKS)
    def _():
        token_tile(with_meta=False)


def _resident(shape):
    nd = len(shape)
    return pl.BlockSpec(shape, lambda *_: (0,) * nd, pipeline_mode=pl.Buffered(1))


def _ffn_inproj(x2, xm, cos, sin, cos_m, sin_m, n1w, wg, wu, wd, n2w, w_in, tm):
    rows = x2.shape[0]
    pos_blocks = cos.shape[0] // tm
    row_blk = lambda i: (_tile_index(i), 0)
    pos_blk = lambda i: (_tile_index(i) % pos_blocks, 0)
    out_shape = (
        jax.ShapeDtypeStruct((rows, D_MODEL), F32),
        jax.ShapeDtypeStruct((rows, RET_WIDTH), BF16),
        jax.ShapeDtypeStruct((rows, RET_WIDTH), BF16),
        jax.ShapeDtypeStruct((rows, RET_WIDTH), BF16),
        jax.ShapeDtypeStruct((rows, RET_WIDTH), F32),
        jax.ShapeDtypeStruct((SSM_GROUPS, rows // SUB, GROUP_LANES), F32),
        jax.ShapeDtypeStruct((N_META, RET_WIDTH), BF16),
        jax.ShapeDtypeStruct((N_META, RET_WIDTH), BF16),
        jax.ShapeDtypeStruct((SSM_GROUPS, HEAD_ROWS, GROUP_LANES), F32),
    )
    return pl.pallas_call(
        _ffn_inproj_kernel,
        grid=(WEIGHT_CHUNKS + rows // tm,),
        in_specs=[
            pl.BlockSpec((tm, D_MODEL), row_blk),
            pl.BlockSpec((tm, HEAD_DIM), pos_blk),
            pl.BlockSpec((tm, HEAD_DIM), pos_blk),
            _resident((N_META, D_MODEL)),
            _resident((N_META, HEAD_DIM)),
            _resident((N_META, HEAD_DIM)),
            _resident((1, D_MODEL)),
            _weight_chunk_spec((D_MODEL, D_FF)),
            _weight_chunk_spec((D_MODEL, D_FF)),
            _weight_chunk_spec((D_FF, D_MODEL)),
            _resident((1, D_MODEL)),
            _weight_chunk_spec((D_MODEL, IN_PROJ)),
        ],
        out_specs=(
            pl.BlockSpec((tm, D_MODEL), row_blk),
            pl.BlockSpec((tm, RET_WIDTH), row_blk),
            pl.BlockSpec((tm, RET_WIDTH), row_blk),
            pl.BlockSpec((tm, RET_WIDTH), row_blk),
            pl.BlockSpec((tm, RET_WIDTH), row_blk),
            pl.BlockSpec((SSM_GROUPS, tm // SUB, GROUP_LANES), lambda i: (0, _tile_index(i), 0)),
            pl.BlockSpec((N_META, RET_WIDTH), lambda i: (0, 0)),
            pl.BlockSpec((N_META, RET_WIDTH), lambda i: (0, 0)),
            pl.BlockSpec((SSM_GROUPS, HEAD_ROWS, GROUP_LANES), lambda i: (0, 0, 0)),
        ),
        out_shape=out_shape,
        scratch_shapes=[
            pltpu.VMEM((D_MODEL, D_FF), BF16),
            pltpu.VMEM((D_MODEL, D_FF), BF16),
            pltpu.VMEM((D_FF, D_MODEL), BF16),
            pltpu.VMEM((D_MODEL, IN_PROJ), BF16),
            pltpu.VMEM((tm + N_META, D_FF), BF16),
            pltpu.VMEM((OCTETS, tm // SUB * TOK_PITCH, LANES), F32),
            pltpu.VMEM((OCTETS, TOK_PITCH, LANES), F32),
        ],
        compiler_params=pltpu.CompilerParams(
            dimension_semantics=("arbitrary",), vmem_limit_bytes=VMEM_LIMIT),
        name="ffn_inproj",
    )(x2, cos, sin, xm, cos_m, sin_m, n1w, wg, wu, wd, n2w, w_in)


def _retention_kernel(q_ref, k_ref, v_ref, g_ref, km_ref, vm_ref, mask_ref, wq_ref, wk_ref, wm_ref,
                      gc_ref, nw_ref, o_ref, state_ref, *, rows):
    tn = (((0,), (0,)), ((), ()))
    nt = (((1,), (1,)), ((), ()))

    @pl.when(pl.program_id(1) == 0)
    def _():
        for h in range(RET_HEADS):
            hs = slice(h * HEAD_DIM, (h + 1) * HEAD_DIM)
            kw = (km_ref[:, hs].astype(F32) * wm_ref[h]).astype(BF16)
            state_ref[h] = lax.dot_general(kw, vm_ref[:, hs], tn, preferred_element_type=F32)

    def chunk(i, carry):
        r0 = pl.multiple_of(i * RET_BLOCK, RET_BLOCK)
        rs = pl.ds(r0, RET_BLOCK)
        for h in range(RET_HEADS):
            hs = slice(h * HEAD_DIM, (h + 1) * HEAD_DIM)
            qh = q_ref[rs, hs]
            kh = k_ref[rs, hs]
            vh = v_ref[rs, hs]
            st = state_ref[h]
            s = lax.dot_general(qh, kh, nt, preferred_element_type=F32) * mask_ref[h]
            o = (jnp.dot(s.astype(BF16), vh, preferred_element_type=F32)
                 + jnp.dot(qh, st.astype(BF16), preferred_element_type=F32) * wq_ref[h])
            kw = (kh.astype(F32) * wk_ref[h]).astype(BF16)
            state_ref[h] = gc_ref[h] * st + lax.dot_general(kw, vh, tn, preferred_element_type=F32)
            mu = jnp.mean(o, axis=-1, keepdims=True)
            d = o - mu
            var = jnp.mean(d * d, axis=-1, keepdims=True)
            y = d * lax.rsqrt(var + EPS) * nw_ref[:, hs]
            gate = g_ref[rs, hs]
            o_ref[rs, hs] = (gate * jax.nn.sigmoid(gate) * y).astype(BF16)
        return carry

    lax.fori_loop(0, rows // RET_BLOCK, chunk, 0, unroll=4)


def _retention_tables():
    log_g = np.log(1.0 - 2.0 ** (-5.0 - np.arange(RET_HEADS, dtype=np.float64)))
    i = np.arange(RET_BLOCK)
    diff = i[:, None] - i[None, :]
    mask = np.where(diff[None] >= 0, np.exp(log_g[:, None, None] * np.maximum(diff, 0)[None]), 0.0)
    pos = np.arange(RET_BLOCK, dtype=np.float64)
    full = lambda w: np.broadcast_to(w[:, :, None], w.shape + (HEAD_DIM,))
    w_q = full(np.exp(log_g[:, None] * (pos + 1.0)[None]))
    w_k = full(np.exp(log_g[:, None] * (RET_BLOCK - 1 - pos)[None]))
    w_m = full(np.exp(log_g[:, None] * (N_META - 1 - np.arange(N_META, dtype=np.float64))[None]))
    g_c = np.broadcast_to(np.exp(log_g * RET_BLOCK)[:, None, None], (RET_HEADS, HEAD_DIM, HEAD_DIM))
    return tuple(jnp.asarray(t, dtype=F32) for t in (mask, w_q, w_k, w_m, g_c))


def _retention(q, k, v, g, km, vm, ret_norm_w, bsz, seq, rows):
    mask, w_q, w_k, w_m, g_c = _retention_tables()
    steps = seq // rows
    blk = lambda b, c: (b * steps + c, 0)
    full3 = lambda a: pl.BlockSpec(a.shape, lambda b, c: (0, 0, 0))
    full2 = lambda a: pl.BlockSpec(a.shape, lambda b, c: (0, 0))
    return pl.pallas_call(
        functools.partial(_retention_kernel, rows=rows),
        grid=(bsz, steps),
        in_specs=[pl.BlockSpec((rows, RET_WIDTH), blk)] * 4
        + [full2(km), full2(vm), full3(mask), full3(w_q), full3(w_k), full3(w_m), full3(g_c),
           full2(ret_norm_w)],
        out_specs=pl.BlockSpec((rows, RET_WIDTH), blk),
        out_shape=jax.ShapeDtypeStruct((bsz * seq, RET_WIDTH), BF16),
        scratch_shapes=[pltpu.VMEM((RET_HEADS, HEAD_DIM, HEAD_DIM), F32)],
        compiler_params=pltpu.CompilerParams(
            dimension_semantics=("arbitrary", "arbitrary"), vmem_limit_bytes=VMEM_LIMIT),
        name="retention",
    )(q, k, v, g, km, vm, mask, w_q, w_k, w_m, g_c, ret_norm_w)


def _shift_lanes_zero_fill(halves, shift, lane):
    h0, h1 = halves
    whole, s = divmod(shift, LANES)
    r0 = h0 if s == 0 else pltpu.roll(h0, s, axis=1)
    zero = jnp.zeros_like(h0)
    if whole == 1:
        return zero, jnp.where(lane >= s, r0, zero)
    r1 = h1 if s == 0 else pltpu.roll(h1, s, axis=1)
    return jnp.where(lane >= s, r0, zero), jnp.where(lane >= s, r1, r0)


def _s5_assemble_octet(o, vec_ref, mat_ref,
                       toep_ref, win_ref, wout_ref, cpw_ref, pow_ref, xall_ref, wt_ref):
    vec = vec_ref[o]
    lre = vec[0:1]
    lim = vec[1:2]
    dt = jnp.exp(vec[2:3])
    n_pow = pow_ref.shape[1]
    ell = lax.broadcasted_iota(jnp.int32, (n_pow, OCT_STATE), 0).astype(F32)
    mag = jnp.exp(ell * (lre * dt))
    ang = ell * (lim * dt)
    pow_ref[0] = mag * jnp.cos(ang)
    pow_ref[1] = mag * jnp.sin(ang)
    ell = ((lax.broadcasted_iota(jnp.int32, (SUBLANES, OCT_STATE), 0) + 1) * SUB).astype(F32)
    mag = jnp.exp(ell * (lre * dt))
    ang = ell * (lim * dt)
    cp_re = mag * jnp.cos(ang)
    cp_im = mag * jnp.sin(ang)
    a_re = pow_ref[0, 1:2, :]
    a_im = pow_ref[1, 1:2, :]
    den = lre * lre + lim * lim
    num_re = a_re - 1.0
    coef_re = (num_re * lre + a_im * lim) / den
    coef_im = (a_im * lre - num_re * lim) / den
    btr = mat_ref[o, 0]
    bti = mat_ref[o, 1]
    bbar_re = coef_re * btr - coef_im * bti
    bbar_im = coef_re * bti + coef_im * btr
    ctr = mat_ref[o, 2]
    cti = mat_ref[o, 3]

    pairs_per_octet = GROUPS_PER_OCTET // 2
    pair_lane_group = _lane_block((SSM_GROUP, PAIR_STATE), SSM_STATE)
    for l in range(SUB):
        pr = pow_ref[0, l:l + 1, :]
        pi = pow_ref[1, l:l + 1, :]
        x_re = pr * bbar_re - pi * bbar_im
        x_im = pr * bbar_im + pi * bbar_re
        rows = slice(l * SSM_GROUP, (l + 1) * SSM_GROUP)
        xall_ref[0, rows, :] = x_re
        xall_ref[1, rows, :] = x_im
        pr1 = pow_ref[0, l + 1:l + 2, :]
        pi1 = pow_ref[1, l + 1:l + 2, :]
        w_re = pr1 * ctr - pi1 * cti
        w_im = pr1 * cti + pi1 * ctr
        j = SUB - 1 - l
        for pp in range(pairs_per_octet):
            sl = slice(pp * PAIR_STATE, (pp + 1) * PAIR_STATE)
            pair = o * pairs_per_octet + pp
            for gi in range(2):
                keep = pair_lane_group == gi
                in_rows = slice(gi * GROUP_LANES + j * SSM_GROUP, gi * GROUP_LANES + (j + 1) * SSM_GROUP)
                out_rows = slice(gi * GROUP_LANES + l * SSM_GROUP, gi * GROUP_LANES + (l + 1) * SSM_GROUP)
                for part, x, w in ((0, x_re, w_re), (1, x_im, -w_im)):
                    cols = slice(part * PAIR_STATE, (part + 1) * PAIR_STATE)
                    win_ref[pair, in_rows, cols] = jnp.where(keep, x[:, sl], 0.0).astype(BF16)
                    wt_ref[pp, out_rows, cols] = jnp.where(keep, w[:, sl], 0.0)
    for pp in range(pairs_per_octet):
        pair = o * pairs_per_octet + pp
        wout_ref[pair] = wt_ref[pp].T.astype(BF16)
        sl = slice(pp * PAIR_STATE, (pp + 1) * PAIR_STATE)
        cpw_ref[pair, 0] = cp_re[:, sl]
        cpw_ref[pair, 1] = cp_im[:, sl]

    row = lax.broadcasted_iota(jnp.int32, (LANES, OCT_STATE), 0)
    col = lax.broadcasted_iota(jnp.int32, (LANES, OCT_STATE), 1)
    same_group = (row // SSM_GROUP) == (col // SSM_STATE)

    def block_diag_t(x):
        return jnp.where(same_group, jnp.concatenate([x] * GROUPS_PER_OCTET, axis=0), 0.0).T

    k_all = (jnp.dot(xall_ref[0], block_diag_t(ctr), precision=lax.Precision.HIGHEST,
                     preferred_element_type=F32)
             - jnp.dot(xall_ref[1], block_diag_t(cti), precision=lax.Precision.HIGHEST,
                       preferred_element_type=F32))
    per_half = LANES // SSM_GROUP
    blk = _lane_block((SSM_GROUP, LANES), SSM_GROUP)
    lane = lax.broadcasted_iota(jnp.int32, (SSM_GROUP, LANES), 1)
    for g8 in range(GROUPS_PER_OCTET):
        strip = []
        for hh in range(GROUP_LANES // LANES):
            acc = jnp.zeros((SSM_GROUP, LANES), F32)
            for ll in range(per_half):
                lag = hh * per_half + ll
                k_lag = k_all[lag * SSM_GROUP:(lag + 1) * SSM_GROUP, :]
                shift = ((ll - g8) * SSM_GROUP) % LANES
                rolled = k_lag if shift == 0 else pltpu.roll(k_lag, shift, axis=1)
                acc = jnp.where(blk == ll, rolled, acc)
            strip.append(acc)
        for j in range(SUB):
            h0, h1 = _shift_lanes_zero_fill(strip, j * SSM_GROUP, lane)
            rows = slice(j * SSM_GROUP, (j + 1) * SSM_GROUP)
            toep_ref[o * GROUPS_PER_OCTET + g8, rows, 0:LANES] = h0.astype(BF16)
            toep_ref[o * GROUPS_PER_OCTET + g8, rows, LANES:2 * LANES] = h1.astype(BF16)


def _s5_state_scan(st_ref, zb_ref, cpow_ref, nsc):
    tiles = nsc // SUBLANES
    width = st_ref.shape[1] // 2
    re = slice(0, width)
    im = slice(width, 2 * width)
    body = slice(HEAD_ROWS, HEAD_ROWS + nsc)
    rowmod = lax.broadcasted_iota(jnp.int32, (SUBLANES, width), 0)
    src, dst = st_ref, zb_ref
    shift = 1
    while shift < SUBLANES:
        keep = rowmod >= shift
        power = slice(shift - 1, shift)
        c_re = jnp.concatenate([jnp.where(keep, cpow_ref[0, power, :], 0.0)] * tiles, axis=0)
        c_im = jnp.concatenate([jnp.where(keep, cpow_ref[1, power, :], 0.0)] * tiles, axis=0)
        sh_re = src[pl.ds(HEAD_ROWS - shift, nsc), re]
        sh_im = src[pl.ds(HEAD_ROWS - shift, nsc), im]
        dst[body, re] = src[body, re] + c_re * sh_re - c_im * sh_im
        dst[body, im] = src[body, im] + c_re * sh_im + c_im * sh_re
        src, dst = dst, src
        shift *= 2
    ca_re = cpow_ref[0]
    ca_im = cpow_ref[1]
    c_re = st_ref[ENTER_ROW:HEAD_ROWS, re]
    c_im = st_ref[ENTER_ROW:HEAD_ROWS, im]
    for t in range(tiles):
        rows = slice(HEAD_ROWS + SUBLANES * t, HEAD_ROWS + SUBLANES * (t + 1))
        cb_re = jnp.broadcast_to(c_re, (SUBLANES, width))
        cb_im = jnp.broadcast_to(c_im, (SUBLANES, width))
        p_re = src[rows, re] + ca_re * cb_re - ca_im * cb_im
        p_im = src[rows, im] + ca_re * cb_im + ca_im * cb_re
        st_ref[rows, re] = p_re
        st_ref[rows, im] = p_im
        c_re = p_re[SUBLANES - 1:SUBLANES]
        c_im = p_im[SUBLANES - 1:SUBLANES]


def _s5_kernel(u_ref, um_ref, vec_ref, mat_ref, d_ref,
               y_ref, toep_ref, win_ref, wout_ref, cpw_ref, pow_ref, xall_ref, wt_ref, lhs_ref,
               st_ref, zb_ref, carry_ref, meta_ref, *, nsc):
    first_call_step = (pl.program_id(0) == 0) & (pl.program_id(1) == 0)
    r = pl.program_id(1)

    @pl.when(first_call_step)
    def _():
        zb_ref[:, 0:HEAD_ROWS, :] = jnp.zeros((zb_ref.shape[0], HEAD_ROWS, zb_ref.shape[2]), F32)

        def octet(o, carry):
            _s5_assemble_octet(o, vec_ref, mat_ref,
                               toep_ref, win_ref, wout_ref, cpw_ref, pow_ref, xall_ref, wt_ref)
            return carry
        lax.fori_loop(0, OCTETS, octet, 0)

        def meta_state(pair, carry):
            meta = jnp.concatenate([um_ref[2 * pair], um_ref[2 * pair + 1]], axis=1).astype(BF16)
            meta_ref[pair] = jnp.dot(meta, win_ref[pair], preferred_element_type=F32)
            return carry
        lax.fori_loop(0, PAIRS, meta_state, 0)

    @pl.when(r == 0)
    def _():
        carry_ref[...] = meta_ref[...]

    lo = slice(0, GROUP_LANES)
    hi = slice(GROUP_LANES, 2 * GROUP_LANES)
    slots = st_ref.shape[0]

    def pair_block(k, carry):
        pairs = [slots * k + s for s in range(slots)]
        for s, pair in enumerate(pairs):
            lhs_ref[s, :, lo] = u_ref[2 * pair].astype(BF16)
            lhs_ref[s, :, hi] = u_ref[2 * pair + 1].astype(BF16)
            st_ref[s, 0:HEAD_ROWS, :] = carry_ref[pair]
            st_ref[s, HEAD_ROWS:HEAD_ROWS + nsc, :] = jnp.dot(lhs_ref[s], win_ref[pair], preferred_element_type=F32)
        for s, pair in enumerate(pairs):
            _s5_state_scan(st_ref.at[s], zb_ref.at[s], cpw_ref.at[pair], nsc)
            carry_ref[pair] = st_ref[s, nsc:nsc + HEAD_ROWS, :]
        for s, pair in enumerate(pairs):
            s_in = st_ref[s, pl.ds(ENTER_ROW, nsc), :].astype(BF16)
            y_state = jnp.dot(s_in, wout_ref[pair], preferred_element_type=F32)
            d = d_ref[pair]
            for gi, cols in ((0, lo), (1, hi)):
                g = 2 * pair + gi
                y = (jnp.dot(lhs_ref[s, :, cols], toep_ref[g], preferred_element_type=F32)
                     + y_state[:, cols] + d[:, cols] * u_ref[g])
                y_ref[g] = y
        return carry

    lax.fori_loop(0, PAIRS // slots, pair_block, 0)


def _s5_param_rows(lam_re, lam_im, log_dt, b_re, b_im, c_re, c_im, d):
    go = GROUPS_PER_OCTET
    lanes = lambda a: a.reshape(OCTETS, OCT_STATE)
    ldt = jnp.broadcast_to(log_dt[:, None], (SSM_GROUPS, SSM_STATE))
    bt = lambda a: jnp.transpose(a.reshape(OCTETS, go, SSM_STATE, SSM_GROUP),
                                 (0, 3, 1, 2)).reshape(OCTETS, SSM_GROUP, OCT_STATE)
    ct = lambda a: jnp.transpose(a.reshape(OCTETS, go, SSM_GROUP, SSM_STATE),
                                 (0, 2, 1, 3)).reshape(OCTETS, SSM_GROUP, OCT_STATE)
    vec = jnp.stack([lanes(lam_re), lanes(lam_im), lanes(ldt)], axis=1)
    mat = jnp.stack([bt(b_re), bt(b_im), ct(c_re), ct(c_im)], axis=1)
    d_pairs = jnp.broadcast_to(d.reshape(PAIRS, 2, 1, SSM_GROUP),
                               (PAIRS, 2, SUB, SSM_GROUP)).reshape(PAIRS, 1, 2 * GROUP_LANES)
    return vec, mat, d_pairs


def _s5(u, um, params, bsz, seq, rows):
    nsc = rows // SUB
    steps = seq // rows
    n_pow = SUB + SUBLANES
    full = lambda a: pl.BlockSpec(a.shape, lambda b, r: (0,) * a.ndim)
    tok_blk = pl.BlockSpec((SSM_GROUPS, nsc, GROUP_LANES), lambda b, r: (0, b * steps + r, 0))
    return pl.pallas_call(
        functools.partial(_s5_kernel, nsc=nsc),
        grid=(bsz, steps),
        in_specs=[tok_blk, full(um)] + [full(p) for p in params],
        out_specs=tok_blk,
        out_shape=jax.ShapeDtypeStruct(u.shape, F32),
        scratch_shapes=[
            pltpu.VMEM((SSM_GROUPS, GROUP_LANES, GROUP_LANES), BF16),
            pltpu.VMEM((PAIRS, 2 * GROUP_LANES, 2 * PAIR_STATE), BF16),
            pltpu.VMEM((PAIRS, 2 * PAIR_STATE, 2 * GROUP_LANES), BF16),
            pltpu.VMEM((PAIRS, 2, SUBLANES, PAIR_STATE), F32),
            pltpu.VMEM((2, n_pow, OCT_STATE), F32),
            pltpu.VMEM((2, SUB * SSM_GROUP, OCT_STATE), F32),
            pltpu.VMEM((GROUPS_PER_OCTET // 2, 2 * GROUP_LANES, 2 * PAIR_STATE), F32),
            pltpu.VMEM((S5_SLOTS, nsc, 2 * GROUP_LANES), BF16),
            pltpu.VMEM((S5_SLOTS, HEAD_ROWS + nsc, 2 * PAIR_STATE), F32),
            pltpu.VMEM((S5_SLOTS, HEAD_ROWS + nsc, 2 * PAIR_STATE), F32),
            pltpu.VMEM((PAIRS, HEAD_ROWS, 2 * PAIR_STATE), F32),
            pltpu.VMEM((PAIRS, HEAD_ROWS, 2 * PAIR_STATE), F32),
        ],
        compiler_params=pltpu.CompilerParams(
            dimension_semantics=("arbitrary", "arbitrary"), vmem_limit_bytes=VMEM_LIMIT),
        name="s5_mixer",
    )(u, um, *params)


def _mix_ffn_kernel(h1_ref, ret_ref, ynext_ref, gw32_ref, gb_ref, snw_ref, wo32_ref, n3w_ref,
                    wg32_ref, wu32_ref, wd32_ref, fnw_ref, out_ref,
                    gw_ref, wo_ref, wg_ref, wu_ref, wd_ref, acc_ref, ytok_ref):
    step = pl.program_id(0)
    nsub = ynext_ref.shape[1]

    @pl.when(step < WEIGHT_CHUNKS)
    def _():
        _cast_weight_chunks(step, ((gw32_ref, gw_ref), (wo32_ref, wo_ref), (wg32_ref, wg_ref),
                                   (wu32_ref, wu_ref), (wd32_ref, wd_ref)))

    @pl.when(step == WEIGHT_CHUNKS - 1)
    def _():
        _to_token_major(ynext_ref, ytok_ref.at[0], nsub, jax.nn.gelu)

    @pl.when(step >= WEIGHT_CHUNKS)
    def _():
        slot = lax.rem(step - WEIGHT_CHUNKS, 2)
        y = _token_major_rows(ytok_ref.at[slot], nsub)
        _to_token_major(ynext_ref, ytok_ref.at[1 - slot], nsub, jax.nn.gelu)
        z = y * jax.nn.sigmoid(jnp.dot(y.astype(BF16), gw_ref[...], preferred_element_type=F32)
                               + gb_ref[...])
        ssm = _rms(z, snw_ref[...]).astype(BF16)
        mixed = jnp.dot(jnp.concatenate([ret_ref[...], ssm], axis=-1), wo_ref[...],
                        preferred_element_type=F32)
        h2 = h1_ref[...] + mixed
        h3 = _swiglu_half_step(h2, _rms(h2, n3w_ref[...]).astype(BF16), wg_ref, wu_ref, wd_ref,
                               acc_ref)
        out_ref[...] = _rms(h3, fnw_ref[...])


def _mix_ffn(h1, ret, y, glu_w, glu_b, ssm_norm_w, w_out, n3w, wg, wu, wd, fnw, tm):
    rows = h1.shape[0]
    tiles = rows // tm
    row_blk = lambda i: (_tile_index(i), 0)
    next_blk = lambda i: (0, jnp.minimum(_tile_index(i + 1), tiles - 1), 0)
    return pl.pallas_call(
        _mix_ffn_kernel,
        grid=(WEIGHT_CHUNKS + tiles,),
        in_specs=[
            pl.BlockSpec((tm, D_MODEL), row_blk),
            pl.BlockSpec((tm, RET_WIDTH), row_blk),
            pl.BlockSpec((SSM_GROUPS, tm // SUB, GROUP_LANES), next_blk),
            _weight_chunk_spec((SSM_WIDTH, SSM_WIDTH)),
            _resident((1, SSM_WIDTH)),
            _resident((1, SSM_WIDTH)),
            _weight_chunk_spec((D_MODEL, D_MODEL)),
            _resident((1, D_MODEL)),
            _weight_chunk_spec((D_MODEL, D_FF)),
            _weight_chunk_spec((D_MODEL, D_FF)),
            _weight_chunk_spec((D_FF, D_MODEL)),
            _resident((1, D_MODEL)),
        ],
        out_specs=pl.BlockSpec((tm, D_MODEL), row_blk),
        out_shape=jax.ShapeDtypeStruct((rows, D_MODEL), F32),
        scratch_shapes=[
            pltpu.VMEM((SSM_WIDTH, SSM_WIDTH), BF16),
            pltpu.VMEM((D_MODEL, D_MODEL), BF16),
            pltpu.VMEM((D_MODEL, D_FF), BF16),
            pltpu.VMEM((D_MODEL, D_FF), BF16),
            pltpu.VMEM((D_FF, D_MODEL), BF16),
            pltpu.VMEM((tm, D_FF), BF16),
            pltpu.VMEM((2, OCTETS, tm // SUB * TOK_PITCH, LANES), F32),
        ],
        compiler_params=pltpu.CompilerParams(
            dimension_semantics=("arbitrary",), vmem_limit_bytes=VMEM_LIMIT),
        name="mix_ffn",
    )(h1, ret, y, glu_w, glu_b, ssm_norm_w, w_out, n3w, wg, wu, wd, fnw)


def _rope_tables(n_pos):
    freqs = 1.0 / (ROPE_BASE ** (np.arange(0, HEAD_DIM, 2, dtype=np.float64) / HEAD_DIM))
    ang = np.arange(n_pos, dtype=np.float64)[:, None] * freqs[None, :]
    cos = np.cos(ang)
    sin = np.sin(ang)
    cos = np.concatenate([cos, cos], axis=-1).astype(np.float32)
    sin = np.concatenate([-sin, sin], axis=-1).astype(np.float32)
    split = lambda t: (jnp.asarray(t[:N_META]), jnp.asarray(t[N_META:]))
    return split(cos), split(sin)


def kernel(x, meta_tokens, ffn1_norm_w, ffn1_w_gate, ffn1_w_up, ffn1_w_down, mix_norm_w, w_in,
           ret_norm_w, ssm_lambda_re, ssm_lambda_im, ssm_log_dt, ssm_b_re, ssm_b_im, ssm_c_re,
           ssm_c_im, ssm_d, ssm_glu_w, ssm_glu_b, ssm_norm_w, w_out, ffn2_norm_w, ffn2_w_gate,
           ffn2_w_up, ffn2_w_down, final_norm_w):
    bsz, seq, _ = x.shape
    assert ffn1_norm_w.shape[0] == 1, "single layer only"
    tm = 512
    ret_rows = 2048
    s5_rows = 4096
    assert seq % tm == 0 and seq % ret_rows == 0 and seq % s5_rows == 0

    l = 0
    (cos_m, cos), (sin_m, sin) = _rope_tables(N_META + seq)

    h1, q, k, v, g, u, km, vm, um = _ffn_inproj(
        x.reshape(bsz * seq, D_MODEL), meta_tokens, cos, sin, cos_m, sin_m,
        ffn1_norm_w, ffn1_w_gate[l], ffn1_w_up[l], ffn1_w_down[l], mix_norm_w, w_in[l], tm=tm)

    ret = _retention(q, k, v, g, km, vm, ret_norm_w, bsz, seq, ret_rows)

    params = _s5_param_rows(ssm_lambda_re[l], ssm_lambda_im[l], ssm_log_dt[l], ssm_b_re[l],
                            ssm_b_im[l], ssm_c_re[l], ssm_c_im[l], ssm_d[l])
    y = _s5(u, um, params, bsz, seq, s5_rows)

    out = _mix_ffn(h1, ret, y, ssm_glu_w[l], ssm_glu_b, ssm_norm_w, w_out[l], ffn2_norm_w,
                   ffn2_w_gate[l], ffn2_w_up[l], ffn2_w_down[l], final_norm_w.reshape(1, D_MODEL),
                   tm=tm)
    return out.reshape(bsz, seq, D_MODEL)
```

```python
import functools

import jax
import jax.numpy as jnp
import numpy as np
from jax import lax
from jax.experimental import pallas as pl
from jax.experimental.pallas import tpu as pltpu

D_MODEL = 1024
N_META = 16
RET_HEADS = 4
HEAD_DIM = 128
RET_WIDTH = RET_HEADS * HEAD_DIM
SSM_WIDTH = 512
SSM_GROUP = 16
SSM_GROUPS = SSM_WIDTH // SSM_GROUP
SSM_STATE = 64
RET_BLOCK = 256
D_FF = 2816
FFN_RES = 0.5
ROPE_BASE = 10000.0
EPS = 1e-6
IN_PROJ = 4 * RET_WIDTH + SSM_WIDTH

LANES = 128
SUBLANES = 8
HEAD_ROWS = SUBLANES
ENTER_ROW = HEAD_ROWS - 1
OCTETS = SSM_WIDTH // LANES
GROUPS_PER_OCTET = LANES // SSM_GROUP
SUB = N_META
OCT_STATE = GROUPS_PER_OCTET * SSM_STATE
GROUP_LANES = SUB * SSM_GROUP
PAIRS = SSM_GROUPS // 2
PAIR_STATE = 2 * SSM_STATE
S5_SLOTS = 16
TOK_PITCH = 24
FF_CHUNK = 256
WEIGHT_CHUNKS = 8
VMEM_LIMIT = 60 * 1024 * 1024

F32 = jnp.float32
BF16 = jnp.bfloat16


def _rms(x, w):
    return x * lax.rsqrt(jnp.mean(x * x, axis=-1, keepdims=True) + EPS) * w


def _swiglu_half_step(h, n, wg_ref, wu_ref, wd_ref, acc_ref):
    for lo in range(0, D_FF, FF_CHUNK):
        sl = slice(lo, min(lo + FF_CHUNK, D_FF))
        g = jnp.dot(n, wg_ref[:, sl], preferred_element_type=F32)
        u = jnp.dot(n, wu_ref[:, sl], preferred_element_type=F32)
        acc_ref[:, sl] = (g * jax.nn.sigmoid(g) * u).astype(BF16)
    return h + FFN_RES * jnp.dot(acc_ref[...], wd_ref[...], preferred_element_type=F32)


def _rope(x, cos, sin_signed):
    return x * cos + pltpu.roll(x, HEAD_DIM // 2, axis=1) * sin_signed


def _lane_block(shape, width):
    return lax.broadcasted_iota(jnp.int32, shape, 1) // width


def _transpose_lane_blocks(v):
    n = LANES // SSM_GROUP
    assert len(v) == n == GROUPS_PER_OCTET
    blk = _lane_block(v[0].shape, SSM_GROUP)
    d = n // 2
    while d:
        low = (blk & d) == 0
        nxt = list(v)
        for a in range(n):
            if a & d == 0:
                nxt[a] = jnp.where(low, v[a], pltpu.roll(v[a + d], d * SSM_GROUP, axis=1))
                nxt[a + d] = jnp.where(low, pltpu.roll(v[a], LANES - d * SSM_GROUP, axis=1), v[a + d])
        v = nxt
        d //= 2
    return v


def _to_group_dense(tok_ref, out_ref, nsub, row0=0):
    per_half = LANES // SSM_GROUP
    for o in range(OCTETS):
        for hh in range(GROUP_LANES // LANES):
            slabs = [tok_ref[o, pl.ds(hh * per_half + jj, nsub, stride=TOK_PITCH), :]
                     for jj in range(per_half)]
            for g8, rows in enumerate(_transpose_lane_blocks(slabs)):
                out_ref[o * GROUPS_PER_OCTET + g8, row0:row0 + nsub,
                        hh * LANES:(hh + 1) * LANES] = rows


def _to_token_major(gd_ref, tok_ref, nsub, fn):
    per_half = LANES // SSM_GROUP
    for o in range(OCTETS):
        for hh in range(GROUP_LANES // LANES):
            rows = [gd_ref[o * GROUPS_PER_OCTET + g8, :, hh * LANES:(hh + 1) * LANES]
                    for g8 in range(GROUPS_PER_OCTET)]
            for jj, slab in enumerate(_transpose_lane_blocks(rows)):
                tok_ref[o, pl.ds(hh * per_half + jj, nsub, stride=TOK_PITCH), :] = fn(slab)


def _token_major_rows(tok_ref, nsub):
    return jnp.concatenate(
        [jnp.concatenate([tok_ref[o, c * TOK_PITCH:c * TOK_PITCH + SUB, :] for c in range(nsub)],
                         axis=0) for o in range(OCTETS)], axis=-1)


def _cast_weight_chunks(step, pairs):
    for src_ref, dst_ref in pairs:
        rows = src_ref.shape[0]
        r0 = pl.multiple_of(step * rows, rows)
        dst_ref[pl.ds(r0, rows), :] = src_ref[...].astype(BF16)


def _weight_chunk_spec(shape):
    rows = shape[0] // WEIGHT_CHUNKS
    assert rows * WEIGHT_CHUNKS == shape[0] and rows % 16 == 0, shape
    return pl.BlockSpec((rows, shape[1]), lambda i: (jnp.minimum(i, WEIGHT_CHUNKS - 1), 0))


def _tile_index(i):
    return jnp.maximum(i - WEIGHT_CHUNKS, 0)


def _ffn_inproj_kernel(x_ref, cos_ref, sin_ref, xm_ref, cosm_ref, sinm_ref, n1w_ref, wg32_ref,
                       wu32_ref, wd32_ref, n2w_ref, win32_ref,
                       h1_ref, q_ref, k_ref, v_ref, g_ref, u_ref, km_ref, vm_ref, um_ref,
                       wg_ref, wu_ref, wd_ref, win_ref, acc_ref, utok_ref, utokm_ref):
    step = pl.program_id(0)
    k_scale = HEAD_DIM ** -0.5

    def ffn_norm(h):
        return _rms(h, n1w_ref[...]).astype(BF16)

    def mix_norm(h1):
        return _rms(h1, n2w_ref[...]).astype(BF16)

    def in_proj(n, part):
        return jnp.dot(n, win_ref[:, part * RET_WIDTH:(part + 1) * RET_WIDTH],
                       preferred_element_type=F32)

    def rope_heads(p, cos, sin, scale, out_ref):
        for h in range(RET_HEADS):
            hs = slice(h * HEAD_DIM, (h + 1) * HEAD_DIM)
            y = _rope(p[:, hs], cos, sin)
            out_ref[:, hs] = (y if scale is None else y * scale).astype(BF16)

    def split_octets(p, tok_ref):
        for o in range(OCTETS):
            for c in range(p.shape[0] // SUB):
                tok_ref[o, c * TOK_PITCH:c * TOK_PITCH + SUB, :] = (
                    p[c * SUB:(c + 1) * SUB, o * LANES:(o + 1) * LANES])

    @pl.when(step < WEIGHT_CHUNKS)
    def _():
        _cast_weight_chunks(step, ((wg32_ref, wg_ref), (wu32_ref, wu_ref), (wd32_ref, wd_ref),
                                   (win32_ref, win_ref)))

    def token_tile(with_meta):
        tm = x_ref.shape[0]
        x = x_ref[...]
        acc = acc_ref
        if with_meta:
            x = jnp.concatenate([x, xm_ref[...]], axis=0)
        else:
            acc = acc_ref.at[pl.ds(0, tm)]
        h1 = _swiglu_half_step(x, ffn_norm(x), wg_ref, wu_ref, wd_ref, acc)
        h1_ref[...] = h1[0:tm]
        n = mix_norm(h1)
        pu = in_proj(n, 4)
        split_octets(pu[0:tm], utok_ref)
        _to_group_dense(utok_ref, u_ref, u_ref.shape[1])
        cos = cos_ref[...]
        sin = sin_ref[...]
        rope_heads(in_proj(n[0:tm], 0), cos, sin, None, q_ref)
        pk = in_proj(n, 1)
        rope_heads(pk[0:tm], cos, sin, k_scale, k_ref)
        pv = in_proj(n, 2)
        v_ref[...] = pv[0:tm].astype(BF16)
        g_ref[...] = in_proj(n[0:tm], 3)
        if with_meta:
            rope_heads(pk[tm:], cosm_ref[...], sinm_ref[...], k_scale, km_ref)
            vm_ref[...] = pv[tm:].astype(BF16)
            split_octets(pu[tm:], utokm_ref)
            um_ref[...] = jnp.zeros(um_ref.shape, F32)
            _to_group_dense(utokm_ref, um_ref, 1, row0=ENTER_ROW)

    @pl.when(step == WEIGHT_CHUNKS)
    def _():
        token_tile(with_meta=True)

    @pl.when(step > WEIGHT_CHUNKS)
    def _():
        token_tile(with_meta=False)


def _resident(shape):
    nd = len(shape)
    return pl.BlockSpec(shape, lambda *_: (0,) * nd, pipeline_mode=pl.Buffered(1))


def _ffn_inproj(x2, xm, cos, sin, cos_m, sin_m, n1w, wg, wu, wd, n2w, w_in, tm):
    rows = x2.shape[0]
    pos_blocks = cos.shape[0] // tm
    row_blk = lambda i: (_tile_index(i), 0)
    pos_blk = lambda i: (_tile_index(i) % pos_blocks, 0)
    out_shape = (
        jax.ShapeDtypeStruct((rows, D_MODEL), F32),
        jax.ShapeDtypeStruct((rows, RET_WIDTH), BF16),
        jax.ShapeDtypeStruct((rows, RET_WIDTH), BF16),
        jax.ShapeDtypeStruct((rows, RET_WIDTH), BF16),
        jax.ShapeDtypeStruct((rows, RET_WIDTH), F32),
        jax.ShapeDtypeStruct((SSM_GROUPS, rows // SUB, GROUP_LANES), F32),
        jax.ShapeDtypeStruct((N_META, RET_WIDTH), BF16),
        jax.ShapeDtypeStruct((N_META, RET_WIDTH), BF16),
        jax.ShapeDtypeStruct((SSM_GROUPS, HEAD_ROWS, GROUP_LANES), F32),
    )
    return pl.pallas_call(
        _ffn_inproj_kernel,
        grid=(WEIGHT_CHUNKS + rows // tm,),
        in_specs=[
            pl.BlockSpec((tm, D_MODEL), row_blk),
            pl.BlockSpec((tm, HEAD_DIM), pos_blk),
            pl.BlockSpec((tm, HEAD_DIM), pos_blk),
            _resident((N_META, D_MODEL)),
            _resident((N_META, HEAD_DIM)),
            _resident((N_META, HEAD_DIM)),
            _resident((1, D_MODEL)),
            _weight_chunk_spec((D_MODEL, D_FF)),
            _weight_chunk_spec((D_MODEL, D_FF)),
            _weight_chunk_spec((D_FF, D_MODEL)),
            _resident((1, D_MODEL)),
            _weight_chunk_spec((D_MODEL, IN_PROJ)),
        ],
        out_specs=(
            pl.BlockSpec((tm, D_MODEL), row_blk),
            pl.BlockSpec((tm, RET_WIDTH), row_blk),
            pl.BlockSpec((tm, RET_WIDTH), row_blk),
            pl.BlockSpec((tm, RET_WIDTH), row_blk),
            pl.BlockSpec((tm, RET_WIDTH), row_blk),
            pl.BlockSpec((SSM_GROUPS, tm // SUB, GROUP_LANES), lambda i: (0, _tile_index(i), 0)),
            pl.BlockSpec((N_META, RET_WIDTH), lambda i: (0, 0)),
            pl.BlockSpec((N_META, RET_WIDTH), lambda i: (0, 0)),
            pl.BlockSpec((SSM_GROUPS, HEAD_ROWS, GROUP_LANES), lambda i: (0, 0, 0)),
        ),
        out_shape=out_shape,
        scratch_shapes=[
            pltpu.VMEM((D_MODEL, D_FF), BF16),
            pltpu.VMEM((D_MODEL, D_FF), BF16),
            pltpu.VMEM((D_FF, D_MODEL), BF16),
            pltpu.VMEM((D_MODEL, IN_PROJ), BF16),
            pltpu.VMEM((tm + N_META, D_FF), BF16),
            pltpu.VMEM((OCTETS, tm // SUB * TOK_PITCH, LANES), F32),
            pltpu.VMEM((OCTETS, TOK_PITCH, LANES), F32),
        ],
        compiler_params=pltpu.CompilerParams(
            dimension_semantics=("arbitrary",), vmem_limit_bytes=VMEM_LIMIT),
        name="ffn_inproj",
    )(x2, cos, sin, xm, cos_m, sin_m, n1w, wg, wu, wd, n2w, w_in)


def _retention_kernel(q_ref, k_ref, v_ref, g_ref, km_ref, vm_ref, mask_ref, wq_ref, wk_ref, wm_ref,
                      gc_ref, nw_ref, o_ref, state_ref, *, rows):
    tn = (((0,), (0,)), ((), ()))
    nt = (((1,), (1,)), ((), ()))

    @pl.when(pl.program_id(1) == 0)
    def _():
        for h in range(RET_HEADS):
            hs = slice(h * HEAD_DIM, (h + 1) * HEAD_DIM)
            kw = (km_ref[:, hs].astype(F32) * wm_ref[h]).astype(BF16)
            state_ref[h] = lax.dot_general(kw, vm_ref[:, hs], tn, preferred_element_type=F32)

    def chunk(i, carry):
        r0 = pl.multiple_of(i * RET_BLOCK, RET_BLOCK)
        rs = pl.ds(r0, RET_BLOCK)
        for h in range(RET_HEADS):
            hs = slice(h * HEAD_DIM, (h + 1) * HEAD_DIM)
            qh = q_ref[rs, hs]
            kh = k_ref[rs, hs]
            vh = v_ref[rs, hs]
            st = state_ref[h]
            s = lax.dot_general(qh, kh, nt, preferred_element_type=F32) * mask_ref[h]
            o = (jnp.dot(s.astype(BF16), vh, preferred_element_type=F32)
                 + jnp.dot(qh, st.astype(BF16), preferred_element_type=F32) * wq_ref[h])
            kw = (kh.astype(F32) * wk_ref[h]).astype(BF16)
            state_ref[h] = gc_ref[h] * st + lax.dot_general(kw, vh, tn, preferred_element_type=F32)
            mu = jnp.mean(o, axis=-1, keepdims=True)
            d = o - mu
            var = jnp.mean(d * d, axis=-1, keepdims=True)
            y = d * lax.rsqrt(var + EPS) * nw_ref[:, hs]
            gate = g_ref[rs, hs]
            o_ref[rs, hs] = (gate * jax.nn.sigmoid(gate) * y).astype(BF16)
        return carry

    lax.fori_loop(0, rows // RET_BLOCK, chunk, 0, unroll=4)


def _retention_tables():
    log_g = np.log(1.0 - 2.0 ** (-5.0 - np.arange(RET_HEADS, dtype=np.float64)))
    i = np.arange(RET_BLOCK)
    diff = i[:, None] - i[None, :]
    mask = np.where(diff[None] >= 0, np.exp(log_g[:, None, None] * np.maximum(diff, 0)[None]), 0.0)
    pos = np.arange(RET_BLOCK, dtype=np.float64)
    full = lambda w: np.broadcast_to(w[:, :, None], w.shape + (HEAD_DIM,))
    w_q = full(np.exp(log_g[:, None] * (pos + 1.0)[None]))
    w_k = full(np.exp(log_g[:, None] * (RET_BLOCK - 1 - pos)[None]))
    w_m = full(np.exp(log_g[:, None] * (N_META - 1 - np.arange(N_META, dtype=np.float64))[None]))
    g_c = np.broadcast_to(np.exp(log_g * RET_BLOCK)[:, None, None], (RET_HEADS, HEAD_DIM, HEAD_DIM))
    return tuple(jnp.asarray(t, dtype=F32) for t in (mask, w_q, w_k, w_m, g_c))


def _retention(q, k, v, g, km, vm, ret_norm_w, bsz, seq, rows):
    mask, w_q, w_k, w_m, g_c = _retention_tables()
    steps = seq // rows
    blk = lambda b, c: (b * steps + c, 0)
    full3 = lambda a: pl.BlockSpec(a.shape, lambda b, c: (0, 0, 0))
    full2 = lambda a: pl.BlockSpec(a.shape, lambda b, c: (0, 0))
    return pl.pallas_call(
        functools.partial(_retention_kernel, rows=rows),
        grid=(bsz, steps),
        in_specs=[pl.BlockSpec((rows, RET_WIDTH), blk)] * 4
        + [full2(km), full2(vm), full3(mask), full3(w_q), full3(w_k), full3(w_m), full3(g_c),
           full2(ret_norm_w)],
        out_specs=pl.BlockSpec((rows, RET_WIDTH), blk),
        out_shape=jax.ShapeDtypeStruct((bsz * seq, RET_WIDTH), BF16),
        scratch_shapes=[pltpu.VMEM((RET_HEADS, HEAD_DIM, HEAD_DIM), F32)],
        compiler_params=pltpu.CompilerParams(
            dimension_semantics=("arbitrary", "arbitrary"), vmem_limit_bytes=VMEM_LIMIT),
        name="retention",
    )(q, k, v, g, km, vm, mask, w_q, w_k, w_m, g_c, ret_norm_w)


def _shift_lanes_zero_fill(halves, shift, lane):
    h0, h1 = halves
    whole, s = divmod(shift, LANES)
    r0 = h0 if s == 0 else pltpu.roll(h0, s, axis=1)
    zero = jnp.zeros_like(h0)
    if whole == 1:
        return zero, jnp.where(lane >= s, r0, zero)
    r1 = h1 if s == 0 else pltpu.roll(h1, s, axis=1)
    return jnp.where(lane >= s, r0, zero), jnp.where(lane >= s, r1, r0)


def _s5_assemble_octet(o, vec_ref, mat_ref,
                       toep_ref, win_ref, wout_ref, cpw_ref, pow_ref, xall_ref, wt_ref):
    vec = vec_ref[o]
    lre = vec[0:1]
    lim = vec[1:2]
    dt = jnp.exp(vec[2:3])
    n_pow = pow_ref.shape[1]
    ell = lax.broadcasted_iota(jnp.int32, (n_pow, OCT_STATE), 0).astype(F32)
    mag = jnp.exp(ell * (lre * dt))
    ang = ell * (lim * dt)
    pow_ref[0] = mag * jnp.cos(ang)
    pow_ref[1] = mag * jnp.sin(ang)
    ell = ((lax.broadcasted_iota(jnp.int32, (SUBLANES, OCT_STATE), 0) + 1) * SUB).astype(F32)
    mag = jnp.exp(ell * (lre * dt))
    ang = ell * (lim * dt)
    cp_re = mag * jnp.cos(ang)
    cp_im = mag * jnp.sin(ang)
    a_re = pow_ref[0, 1:2, :]
    a_im = pow_ref[1, 1:2, :]
    den = lre * lre + lim * lim
    num_re = a_re - 1.0
    coef_re = (num_re * lre + a_im * lim) / den
    coef_im = (a_im * lre - num_re * lim) / den
    btr = mat_ref[o, 0]
    bti = mat_ref[o, 1]
    bbar_re = coef_re * btr - coef_im * bti
    bbar_im = coef_re * bti + coef_im * btr
    ctr = mat_ref[o, 2]
    cti = mat_ref[o, 3]

    pairs_per_octet = GROUPS_PER_OCTET // 2
    pair_lane_group = _lane_block((SSM_GROUP, PAIR_STATE), SSM_STATE)
    for l in range(SUB):
        pr = pow_ref[0, l:l + 1, :]
        pi = pow_ref[1, l:l + 1, :]
        x_re = pr * bbar_re - pi * bbar_im
        x_im = pr * bbar_im + pi * bbar_re
        rows = slice(l * SSM_GROUP, (l + 1) * SSM_GROUP)
        xall_ref[0, rows, :] = x_re
        xall_ref[1, rows, :] = x_im
        pr1 = pow_ref[0, l + 1:l + 2, :]
        pi1 = pow_ref[1, l + 1:l + 2, :]
        w_re = pr1 * ctr - pi1 * cti
        w_im = pr1 * cti + pi1 * ctr
        j = SUB - 1 - l
        for pp in range(pairs_per_octet):
            sl = slice(pp * PAIR_STATE, (pp + 1) * PAIR_STATE)
            pair = o * pairs_per_octet + pp
            for gi in range(2):
                keep = pair_lane_group == gi
                in_rows = slice(gi * GROUP_LANES + j * SSM_GROUP, gi * GROUP_LANES + (j + 1) * SSM_GROUP)
                out_rows = slice(gi * GROUP_LANES + l * SSM_GROUP, gi * GROUP_LANES + (l + 1) * SSM_GROUP)
                for part, x, w in ((0, x_re, w_re), (1, x_im, -w_im)):
                    cols = slice(part * PAIR_STATE, (part + 1) * PAIR_STATE)
                    win_ref[pair, in_rows, cols] = jnp.where(keep, x[:, sl], 0.0).astype(BF16)
                    wt_ref[pp, out_rows, cols] = jnp.where(keep, w[:, sl], 0.0)
    for pp in range(pairs_per_octet):
        pair = o * pairs_per_octet + pp
        wout_ref[pair] = wt_ref[pp].T.astype(BF16)
        sl = slice(pp * PAIR_STATE, (pp + 1) * PAIR_STATE)
        cpw_ref[pair, 0] = cp_re[:, sl]
        cpw_ref[pair, 1] = cp_im[:, sl]

    row = lax.broadcasted_iota(jnp.int32, (LANES, OCT_STATE), 0)
    col = lax.broadcasted_iota(jnp.int32, (LANES, OCT_STATE), 1)
    same_group = (row // SSM_GROUP) == (col // SSM_STATE)

    def block_diag_t(x):
        return jnp.where(same_group, jnp.concatenate([x] * GROUPS_PER_OCTET, axis=0), 0.0).T

    k_all = (jnp.dot(xall_ref[0], block_diag_t(ctr), precision=lax.Precision.HIGHEST,
                     preferred_element_type=F32)
             - jnp.dot(xall_ref[1], block_diag_t(cti), precision=lax.Precision.HIGHEST,
                       preferred_element_type=F32))
    per_half = LANES // SSM_GROUP
    blk = _lane_block((SSM_GROUP, LANES), SSM_GROUP)
    lane = lax.broadcasted_iota(jnp.int32, (SSM_GROUP, LANES), 1)
    for g8 in range(GROUPS_PER_OCTET):
        strip = []
        for hh in range(GROUP_LANES // LANES):
            acc = jnp.zeros((SSM_GROUP, LANES), F32)
            for ll in range(per_half):
                lag = hh * per_half + ll
                k_lag = k_all[lag * SSM_GROUP:(lag + 1) * SSM_GROUP, :]
                shift = ((ll - g8) * SSM_GROUP) % LANES
                rolled = k_lag if shift == 0 else pltpu.roll(k_lag, shift, axis=1)
                acc = jnp.where(blk == ll, rolled, acc)
            strip.append(acc)
        for j in range(SUB):
            h0, h1 = _shift_lanes_zero_fill(strip, j * SSM_GROUP, lane)
            rows = slice(j * SSM_GROUP, (j + 1) * SSM_GROUP)
            toep_ref[o * GROUPS_PER_OCTET + g8, rows, 0:LANES] = h0.astype(BF16)
            toep_ref[o * GROUPS_PER_OCTET + g8, rows, LANES:2 * LANES] = h1.astype(BF16)


def _s5_state_scan(st_ref, zb_ref, cpow_ref, nsc):
    tiles = nsc // SUBLANES
    width = st_ref.shape[1] // 2
    re = slice(0, width)
    im = slice(width, 2 * width)
    body = slice(HEAD_ROWS, HEAD_ROWS + nsc)
    rowmod = lax.broadcasted_iota(jnp.int32, (SUBLANES, width), 0)
    src, dst = st_ref, zb_ref
    shift = 1
    while shift < SUBLANES:
        keep = rowmod >= shift
        power = slice(shift - 1, shift)
        c_re = jnp.concatenate([jnp.where(keep, cpow_ref[0, power, :], 0.0)] * tiles, axis=0)
        c_im = jnp.concatenate([jnp.where(keep, cpow_ref[1, power, :], 0.0)] * tiles, axis=0)
        sh_re = src[pl.ds(HEAD_ROWS - shift, nsc), re]
        sh_im = src[pl.ds(HEAD_ROWS - shift, nsc), im]
        dst[body, re] = src[body, re] + c_re * sh_re - c_im * sh_im
        dst[body, im] = src[body, im] + c_re * sh_im + c_im * sh_re
        src, dst = dst, src
        shift *= 2
    ca_re = cpow_ref[0]
    ca_im = cpow_ref[1]
    c_re = st_ref[ENTER_ROW:HEAD_ROWS, re]
    c_im = st_ref[ENTER_ROW:HEAD_ROWS, im]
    for t in range(tiles):
        rows = slice(HEAD_ROWS + SUBLANES * t, HEAD_ROWS + SUBLANES * (t + 1))
        cb_re = jnp.broadcast_to(c_re, (SUBLANES, width))
        cb_im = jnp.broadcast_to(c_im, (SUBLANES, width))
        p_re = src[rows, re] + ca_re * cb_re - ca_im * cb_im
        p_im = src[rows, im] + ca_re * cb_im + ca_im * cb_re
        st_ref[rows, re] = p_re
        st_ref[rows, im] = p_im
        c_re = p_re[SUBLANES - 1:SUBLANES]
        c_im = p_im[SUBLANES - 1:SUBLANES]


def _s5_kernel(u_ref, um_ref, vec_ref, mat_ref, d_ref,
               y_ref, toep_ref, win_ref, wout_ref, cpw_ref, carry_ref, meta_ref, *, nsc):
    first_call_step = (pl.program_id(0) == 0) & (pl.program_id(1) == 0)
    r = pl.program_id(1)
    n_pow = SUB + SUBLANES

    @pl.when(first_call_step)
    def _():
        def assemble(pow_ref, xall_ref, wt_ref):
            def octet(o, carry):
                _s5_assemble_octet(o, vec_ref, mat_ref,
                                   toep_ref, win_ref, wout_ref, cpw_ref, pow_ref, xall_ref, wt_ref)
                return carry
            lax.fori_loop(0, OCTETS, octet, 0)

        pl.run_scoped(
            assemble,
            pltpu.VMEM((2, n_pow, OCT_STATE), F32),
            pltpu.VMEM((2, SUB * SSM_GROUP, OCT_STATE), F32),
            pltpu.VMEM((GROUPS_PER_OCTET // 2, 2 * GROUP_LANES, 2 * PAIR_STATE), F32))

        def meta_state(pair, carry):
            meta = jnp.concatenate([um_ref[2 * pair], um_ref[2 * pair + 1]], axis=1).astype(BF16)
            meta_ref[pair] = jnp.dot(meta, win_ref[pair], preferred_element_type=F32)
            return carry
        lax.fori_loop(0, PAIRS, meta_state, 0)

    @pl.when(r == 0)
    def _():
        carry_ref[...] = meta_ref[...]

    lo = slice(0, GROUP_LANES)
    hi = slice(GROUP_LANES, 2 * GROUP_LANES)

    def pair_loop(lhs_ref, st_ref, zb_ref):
        zb_ref[:, 0:HEAD_ROWS, :] = jnp.zeros((S5_SLOTS, HEAD_ROWS, zb_ref.shape[2]), F32)

        def pair_block(k, carry):
            pairs = [S5_SLOTS * k + s for s in range(S5_SLOTS)]
            for s, pair in enumerate(pairs):
                lhs_ref[s, :, lo] = u_ref[2 * pair].astype(BF16)
                lhs_ref[s, :, hi] = u_ref[2 * pair + 1].astype(BF16)
                st_ref[s, 0:HEAD_ROWS, :] = carry_ref[pair]
                st_ref[s, HEAD_ROWS:HEAD_ROWS + nsc, :] = jnp.dot(lhs_ref[s], win_ref[pair],
                                                                  preferred_element_type=F32)
            for s, pair in enumerate(pairs):
                _s5_state_scan(st_ref.at[s], zb_ref.at[s], cpw_ref.at[pair], nsc)
                carry_ref[pair] = st_ref[s, nsc:nsc + HEAD_ROWS, :]
            for s, pair in enumerate(pairs):
                s_in = st_ref[s, pl.ds(ENTER_ROW, nsc), :].astype(BF16)
                y_state = jnp.dot(s_in, wout_ref[pair], preferred_element_type=F32)
                d = d_ref[pair]
                for gi, cols in ((0, lo), (1, hi)):
                    g = 2 * pair + gi
                    y = (jnp.dot(lhs_ref[s, :, cols], toep_ref[g], preferred_element_type=F32)
                         + y_state[:, cols] + d[:, cols] * u_ref[g])
                    y_ref[g] = y
            return carry

        lax.fori_loop(0, PAIRS // S5_SLOTS, pair_block, 0)

    pl.run_scoped(
        pair_loop,
        pltpu.VMEM((S5_SLOTS, nsc, 2 * GROUP_LANES), BF16),
        pltpu.VMEM((S5_SLOTS, HEAD_ROWS + nsc, 2 * PAIR_STATE), F32),
        pltpu.VMEM((S5_SLOTS, HEAD_ROWS + nsc, 2 * PAIR_STATE), F32))


def _s5_param_rows(lam_re, lam_im, log_dt, b_re, b_im, c_re, c_im, d):
    go = GROUPS_PER_OCTET
    lanes = lambda a: a.reshape(OCTETS, OCT_STATE)
    ldt = jnp.broadcast_to(log_dt[:, None], (SSM_GROUPS, SSM_STATE))
    bt = lambda a: jnp.transpose(a.reshape(OCTETS, go, SSM_STATE, SSM_GROUP),
                                 (0, 3, 1, 2)).reshape(OCTETS, SSM_GROUP, OCT_STATE)
    ct = lambda a: jnp.transpose(a.reshape(OCTETS, go, SSM_GROUP, SSM_STATE),
                                 (0, 2, 1, 3)).reshape(OCTETS, SSM_GROUP, OCT_STATE)
    vec = jnp.stack([lanes(lam_re), lanes(lam_im), lanes(ldt)], axis=1)
    mat = jnp.stack([bt(b_re), bt(b_im), ct(c_re), ct(c_im)], axis=1)
    d_pairs = jnp.broadcast_to(d.reshape(PAIRS, 2, 1, SSM_GROUP),
                               (PAIRS, 2, SUB, SSM_GROUP)).reshape(PAIRS, 1, 2 * GROUP_LANES)
    return vec, mat, d_pairs


def _s5(u, um, params, bsz, seq, rows):
    nsc = rows // SUB
    steps = seq // rows
    full = lambda a: pl.BlockSpec(a.shape, lambda b, r: (0,) * a.ndim)
    tok_blk = pl.BlockSpec((SSM_GROUPS, nsc, GROUP_LANES), lambda b, r: (0, b * steps + r, 0))
    return pl.pallas_call(
        functools.partial(_s5_kernel, nsc=nsc),
        grid=(bsz, steps),
        in_specs=[tok_blk, full(um)] + [full(p) for p in params],
        out_specs=tok_blk,
        out_shape=jax.ShapeDtypeStruct(u.shape, F32),
        scratch_shapes=[
            pltpu.VMEM((SSM_GROUPS, GROUP_LANES, GROUP_LANES), BF16),
            pltpu.VMEM((PAIRS, 2 * GROUP_LANES, 2 * PAIR_STATE), BF16),
            pltpu.VMEM((PAIRS, 2 * PAIR_STATE, 2 * GROUP_LANES), BF16),
            pltpu.VMEM((PAIRS, 2, SUBLANES, PAIR_STATE), F32),
            pltpu.VMEM((PAIRS, HEAD_ROWS, 2 * PAIR_STATE), F32),
            pltpu.VMEM((PAIRS, HEAD_ROWS, 2 * PAIR_STATE), F32),
        ],
        compiler_params=pltpu.CompilerParams(
            dimension_semantics=("arbitrary", "arbitrary"), vmem_limit_bytes=VMEM_LIMIT),
        name="s5_mixer",
    )(u, um, *params)


def _mix_ffn_kernel(h1_ref, ret_ref, ynext_ref, gw32_ref, gb_ref, snw_ref, wo32_ref, n3w_ref,
                    wg32_ref, wu32_ref, wd32_ref, fnw_ref, out_ref,
                    gw_ref, wo_ref, wg_ref, wu_ref, wd_ref, acc_ref, ytok_ref):
    step = pl.program_id(0)
    nsub = ynext_ref.shape[1]

    @pl.when(step < WEIGHT_CHUNKS)
    def _():
        _cast_weight_chunks(step, ((gw32_ref, gw_ref), (wo32_ref, wo_ref), (wg32_ref, wg_ref),
                                   (wu32_ref, wu_ref), (wd32_ref, wd_ref)))

    @pl.when(step == WEIGHT_CHUNKS - 1)
    def _():
        _to_token_major(ynext_ref, ytok_ref.at[0], nsub, jax.nn.gelu)

    @pl.when(step >= WEIGHT_CHUNKS)
    def _():
        slot = lax.rem(step - WEIGHT_CHUNKS, 2)
        y = _token_major_rows(ytok_ref.at[slot], nsub)
        _to_token_major(ynext_ref, ytok_ref.at[1 - slot], nsub, jax.nn.gelu)
        z = y * jax.nn.sigmoid(jnp.dot(y.astype(BF16), gw_ref[...], preferred_element_type=F32)
                               + gb_ref[...])
        ssm = _rms(z, snw_ref[...]).astype(BF16)
        mixed = jnp.dot(jnp.concatenate([ret_ref[...], ssm], axis=-1), wo_ref[...],
                        preferred_element_type=F32)
        h2 = h1_ref[...] + mixed
        h3 = _swiglu_half_step(h2, _rms(h2, n3w_ref[...]).astype(BF16), wg_ref, wu_ref, wd_ref,
                               acc_ref)
        out_ref[...] = _rms(h3, fnw_ref[...])


def _mix_ffn(h1, ret, y, glu_w, glu_b, ssm_norm_w, w_out, n3w, wg, wu, wd, fnw, tm):
    rows = h1.shape[0]
    tiles = rows // tm
    row_blk = lambda i: (_tile_index(i), 0)
    next_blk = lambda i: (0, jnp.minimum(_tile_index(i + 1), tiles - 1), 0)
    return pl.pallas_call(
        _mix_ffn_kernel,
        grid=(WEIGHT_CHUNKS + tiles,),
        in_specs=[
            pl.BlockSpec((tm, D_MODEL), row_blk),
            pl.BlockSpec((tm, RET_WIDTH), row_blk),
            pl.BlockSpec((SSM_GROUPS, tm // SUB, GROUP_LANES), next_blk),
            _weight_chunk_spec((SSM_WIDTH, SSM_WIDTH)),
            _resident((1, SSM_WIDTH)),
            _resident((1, SSM_WIDTH)),
            _weight_chunk_spec((D_MODEL, D_MODEL)),
            _resident((1, D_MODEL)),
            _weight_chunk_spec((D_MODEL, D_FF)),
            _weight_chunk_spec((D_MODEL, D_FF)),
            _weight_chunk_spec((D_FF, D_MODEL)),
            _resident((1, D_MODEL)),
        ],
        out_specs=pl.BlockSpec((tm, D_MODEL), row_blk),
        out_shape=jax.ShapeDtypeStruct((rows, D_MODEL), F32),
        scratch_shapes=[
            pltpu.VMEM((SSM_WIDTH, SSM_WIDTH), BF16),
            pltpu.VMEM((D_MODEL, D_MODEL), BF16),
            pltpu.VMEM((D_MODEL, D_FF), BF16),
            pltpu.VMEM((D_MODEL, D_FF), BF16),
            pltpu.VMEM((D_FF, D_MODEL), BF16),
            pltpu.VMEM((tm, D_FF), BF16),
            pltpu.VMEM((2, OCTETS, tm // SUB * TOK_PITCH, LANES), F32),
        ],
        compiler_params=pltpu.CompilerParams(
            dimension_semantics=("arbitrary",), vmem_limit_bytes=VMEM_LIMIT),
        name="mix_ffn",
    )(h1, ret, y, glu_w, glu_b, ssm_norm_w, w_out, n3w, wg, wu, wd, fnw)


def _rope_tables(n_pos):
    freqs = 1.0 / (ROPE_BASE ** (np.arange(0, HEAD_DIM, 2, dtype=np.float64) / HEAD_DIM))
    ang = np.arange(n_pos, dtype=np.float64)[:, None] * freqs[None, :]
    cos = np.cos(ang)
    sin = np.sin(ang)
    cos = np.concatenate([cos, cos], axis=-1).astype(np.float32)
    sin = np.concatenate([-sin, sin], axis=-1).astype(np.float32)
    split = lambda t: (jnp.asarray(t[:N_META]), jnp.asarray(t[N_META:]))
    return split(cos), split(sin)


def kernel(x, meta_tokens, ffn1_norm_w, ffn1_w_gate, ffn1_w_up, ffn1_w_down, mix_norm_w, w_in,
           ret_norm_w, ssm_lambda_re, ssm_lambda_im, ssm_log_dt, ssm_b_re, ssm_b_im, ssm_c_re,
           ssm_c_im, ssm_d, ssm_glu_w, ssm_glu_b, ssm_norm_w, w_out, ffn2_norm_w, ffn2_w_gate,
           ffn2_w_up, ffn2_w_down, final_norm_w):
    bsz, seq, _ = x.shape
    assert ffn1_norm_w.shape[0] == 1, "single layer only"
    tm = 512
    ret_rows = 2048
    s5_rows = 4096
    assert seq % tm == 0 and seq % ret_rows == 0 and seq % s5_rows == 0

    l = 0
    (cos_m, cos), (sin_m, sin) = _rope_tables(N_META + seq)

    h1, q, k, v, g, u, km, vm, um = _ffn_inproj(
        x.reshape(bsz * seq, D_MODEL), meta_tokens, cos, sin, cos_m, sin_m,
        ffn1_norm_w, ffn1_w_gate[l], ffn1_w_up[l], ffn1_w_down[l], mix_norm_w, w_in[l], tm=tm)

    ret = _retention(q, k, v, g, km, vm, ret_norm_w, bsz, seq, ret_rows)

    params = _s5_param_rows(ssm_lambda_re[l], ssm_lambda_im[l], ssm_log_dt[l], ssm_b_re[l],
                            ssm_b_im[l], ssm_c_re[l], ssm_c_im[l], ssm_d[l])
    y = _s5(u, um, params, bsz, seq, s5_rows)

    out = _mix_ffn(h1, ret, y, ssm_glu_w[l], ssm_glu_b, ssm_norm_w, w_out[l], ffn2_norm_w,
                   ffn2_w_gate[l], ffn2_w_up[l], ffn2_w_down[l], final_norm_w.reshape(1, D_MODEL),
                   tm=tm)
    return out.reshape(bsz, seq, D_MODEL)
```

```python
import functools

import jax
import jax.numpy as jnp
import numpy as np
from jax import lax
from jax.experimental import pallas as pl
from jax.experimental.pallas import tpu as pltpu

D_MODEL = 1024
N_META = 16
RET_HEADS = 4
HEAD_DIM = 128
RET_WIDTH = RET_HEADS * HEAD_DIM
SSM_WIDTH = 512
SSM_GROUP = 16
SSM_GROUPS = SSM_WIDTH // SSM_GROUP
SSM_STATE = 64
RET_BLOCK = 256
D_FF = 2816
FFN_RES = 0.5
ROPE_BASE = 10000.0
EPS = 1e-6
IN_PROJ = 4 * RET_WIDTH + SSM_WIDTH

LANES = 128
SUBLANES = 8
HEAD_ROWS = SUBLANES
ENTER_ROW = HEAD_ROWS - 1
OCTETS = SSM_WIDTH // LANES
GROUPS_PER_OCTET = LANES // SSM_GROUP
SUB = N_META
OCT_STATE = GROUPS_PER_OCTET * SSM_STATE
GROUP_LANES = SUB * SSM_GROUP
PAIRS = SSM_GROUPS // 2
PAIR_STATE = 2 * SSM_STATE
S5_SLOTS = 16
TOK_PITCH = 24
FF_CHUNK = 256
WEIGHT_CHUNKS = 8
VMEM_LIMIT = 60 * 1024 * 1024

F32 = jnp.float32
BF16 = jnp.bfloat16


def _rms(x, w):
    return x * lax.rsqrt(jnp.mean(x * x, axis=-1, keepdims=True) + EPS) * w


def _swiglu_half_step(load_h, n, wg_ref, wu_ref, wd_ref, acc_ref):
    for lo in range(0, D_FF, FF_CHUNK):
        sl = slice(lo, min(lo + FF_CHUNK, D_FF))
        g = jnp.dot(n, wg_ref[:, sl], preferred_element_type=F32)
        u = jnp.dot(n, wu_ref[:, sl], preferred_element_type=F32)
        acc_ref[:, sl] = (g * jax.nn.sigmoid(g) * u).astype(BF16)
    return load_h() + FFN_RES * jnp.dot(acc_ref[...], wd_ref[...], preferred_element_type=F32)


def _rope(x, cos, sin_signed):
    return x * cos + pltpu.roll(x, HEAD_DIM // 2, axis=1) * sin_signed


def _lane_block(shape, width):
    return lax.broadcasted_iota(jnp.int32, shape, 1) // width


def _transpose_lane_blocks(v):
    n = LANES // SSM_GROUP
    assert len(v) == n == GROUPS_PER_OCTET
    blk = _lane_block(v[0].shape, SSM_GROUP)
    d = n // 2
    while d:
        low = (blk & d) == 0
        nxt = list(v)
        for a in range(n):
            if a & d == 0:
                nxt[a] = jnp.where(low, v[a], pltpu.roll(v[a + d], d * SSM_GROUP, axis=1))
                nxt[a + d] = jnp.where(low, pltpu.roll(v[a], LANES - d * SSM_GROUP, axis=1), v[a + d])
        v = nxt
        d //= 2
    return v


def _to_group_dense(tok_ref, out_ref, nsub, row0=0):
    per_half = LANES // SSM_GROUP
    for o in range(OCTETS):
        for hh in range(GROUP_LANES // LANES):
            slabs = [tok_ref[o, pl.ds(hh * per_half + jj, nsub, stride=TOK_PITCH), :]
                     for jj in range(per_half)]
            for g8, rows in enumerate(_transpose_lane_blocks(slabs)):
                out_ref[o * GROUPS_PER_OCTET + g8, row0:row0 + nsub,
                        hh * LANES:(hh + 1) * LANES] = rows


def _to_token_major(gd_ref, tok_ref, nsub, fn):
    per_half = LANES // SSM_GROUP
    for o in range(OCTETS):
        for hh in range(GROUP_LANES // LANES):
            rows = [gd_ref[o * GROUPS_PER_OCTET + g8, :, hh * LANES:(hh + 1) * LANES]
                    for g8 in range(GROUPS_PER_OCTET)]
            for jj, slab in enumerate(_transpose_lane_blocks(rows)):
                tok_ref[o, pl.ds(hh * per_half + jj, nsub, stride=TOK_PITCH), :] = fn(slab)


def _token_major_rows(tok_ref, nsub):
    return jnp.concatenate(
        [jnp.concatenate([tok_ref[o, c * TOK_PITCH:c * TOK_PITCH + SUB, :] for c in range(nsub)],
                         axis=0) for o in range(OCTETS)], axis=-1)


def _cast_weight_chunks(step, pairs):
    for src_ref, dst_ref in pairs:
        rows = src_ref.shape[0]
        r0 = pl.multiple_of(step * rows, rows)
        dst_ref[pl.ds(r0, rows), :] = src_ref[...].astype(BF16)


def _weight_chunk_spec(shape):
    rows = shape[0] // WEIGHT_CHUNKS
    assert rows * WEIGHT_CHUNKS == shape[0] and rows % 16 == 0, shape
    return pl.BlockSpec((rows, shape[1]), lambda i: (jnp.minimum(i, WEIGHT_CHUNKS - 1), 0))


def _tile_index(i):
    return jnp.maximum(i - WEIGHT_CHUNKS, 0)


def _ffn_inproj_kernel(x_ref, cos_ref, sin_ref, xm_ref, cosm_ref, sinm_ref, n1w_ref, wg32_ref,
                       wu32_ref, wd32_ref, n2w_ref, win32_ref,
                       h1_ref, q_ref, k_ref, v_ref, g_ref, u_ref, km_ref, vm_ref, um_ref,
                       wg_ref, wu_ref, wd_ref, win_ref, acc_ref, utok_ref, utokm_ref):
    step = pl.program_id(0)
    k_scale = HEAD_DIM ** -0.5

    def ffn_norm(h):
        return _rms(h, n1w_ref[...]).astype(BF16)

    def mix_norm(h1):
        return _rms(h1, n2w_ref[...]).astype(BF16)

    def in_proj(n, part):
        return jnp.dot(n, win_ref[:, part * RET_WIDTH:(part + 1) * RET_WIDTH],
                       preferred_element_type=F32)

    def rope_heads(p, cos, sin, scale, out_ref):
        for h in range(RET_HEADS):
            hs = slice(h * HEAD_DIM, (h + 1) * HEAD_DIM)
            y = _rope(p[:, hs], cos, sin)
            out_ref[:, hs] = (y if scale is None else y * scale).astype(BF16)

    def split_octets(p, tok_ref):
        for o in range(OCTETS):
            for c in range(p.shape[0] // SUB):
                tok_ref[o, c * TOK_PITCH:c * TOK_PITCH + SUB, :] = (
                    p[c * SUB:(c + 1) * SUB, o * LANES:(o + 1) * LANES])

    @pl.when(step < WEIGHT_CHUNKS)
    def _():
        _cast_weight_chunks(step, ((wg32_ref, wg_ref), (wu32_ref, wu_ref), (wd32_ref, wd_ref),
                                   (win32_ref, win_ref)))

    def token_tile(with_meta):
        tm = x_ref.shape[0]

        def load_x():
            if with_meta:
                return jnp.concatenate([x_ref[...], xm_ref[...]], axis=0)
            return x_ref[...]

        acc = acc_ref if with_meta else acc_ref.at[pl.ds(0, tm)]
        h1 = _swiglu_half_step(load_x, ffn_norm(load_x()), wg_ref, wu_ref, wd_ref, acc)
        h1_ref[...] = h1[0:tm]
        n = mix_norm(h1)
        pu = in_proj(n, 4)
        split_octets(pu[0:tm], utok_ref)
        _to_group_dense(utok_ref, u_ref, u_ref.shape[1])
        cos = cos_ref[...]
        sin = sin_ref[...]
        rope_heads(in_proj(n[0:tm], 0), cos, sin, None, q_ref)
        pk = in_proj(n, 1)
        rope_heads(pk[0:tm], cos, sin, k_scale, k_ref)
        pv = in_proj(n, 2)
        v_ref[...] = pv[0:tm].astype(BF16)
        g_ref[...] = in_proj(n[0:tm], 3)
        if with_meta:
            rope_heads(pk[tm:], cosm_ref[...], sinm_ref[...], k_scale, km_ref)
            vm_ref[...] = pv[tm:].astype(BF16)
            split_octets(pu[tm:], utokm_ref)
            um_ref[...] = jnp.zeros(um_ref.shape, F32)
            _to_group_dense(utokm_ref, um_ref, 1, row0=ENTER_ROW)

    @pl.when(step == WEIGHT_CHUNKS)
    def _():
        token_tile(with_meta=True)

    @pl.when(step > WEIGHT_CHUNKS)
    def _():
        token_tile(with_meta=False)


def _resident(shape):
    nd = len(shape)
    return pl.BlockSpec(shape, lambda *_: (0,) * nd, pipeline_mode=pl.Buffered(1))


def _ffn_inproj(x2, xm, cos, sin, cos_m, sin_m, n1w, wg, wu, wd, n2w, w_in, tm):
    rows = x2.shape[0]
    pos_blocks = cos.shape[0] // tm
    row_blk = lambda i: (_tile_index(i), 0)
    pos_blk = lambda i: (_tile_index(i) % pos_blocks, 0)
    out_shape = (
        jax.ShapeDtypeStruct((rows, D_MODEL), F32),
        jax.ShapeDtypeStruct((rows, RET_WIDTH), BF16),
        jax.ShapeDtypeStruct((rows, RET_WIDTH), BF16),
        jax.ShapeDtypeStruct((rows, RET_WIDTH), BF16),
        jax.ShapeDtypeStruct((rows, RET_WIDTH), F32),
        jax.ShapeDtypeStruct((SSM_GROUPS, rows // SUB, GROUP_LANES), F32),
        jax.ShapeDtypeStruct((N_META, RET_WIDTH), BF16),
        jax.ShapeDtypeStruct((N_META, RET_WIDTH), BF16),
        jax.ShapeDtypeStruct((SSM_GROUPS, HEAD_ROWS, GROUP_LANES), F32),
    )
    return pl.pallas_call(
        _ffn_inproj_kernel,
        grid=(WEIGHT_CHUNKS + rows // tm,),
        in_specs=[
            pl.BlockSpec((tm, D_MODEL), row_blk),
            pl.BlockSpec((tm, HEAD_DIM), pos_blk),
            pl.BlockSpec((tm, HEAD_DIM), pos_blk),
            _resident((N_META, D_MODEL)),
            _resident((N_META, HEAD_DIM)),
            _resident((N_META, HEAD_DIM)),
            _resident((1, D_MODEL)),
            _weight_chunk_spec((D_MODEL, D_FF)),
            _weight_chunk_spec((D_MODEL, D_FF)),
            _weight_chunk_spec((D_FF, D_MODEL)),
            _resident((1, D_MODEL)),
            _weight_chunk_spec((D_MODEL, IN_PROJ)),
        ],
        out_specs=(
            pl.BlockSpec((tm, D_MODEL), row_blk),
            pl.BlockSpec((tm, RET_WIDTH), row_blk),
            pl.BlockSpec((tm, RET_WIDTH), row_blk),
            pl.BlockSpec((tm, RET_WIDTH), row_blk),
            pl.BlockSpec((tm, RET_WIDTH), row_blk),
            pl.BlockSpec((SSM_GROUPS, tm // SUB, GROUP_LANES), lambda i: (0, _tile_index(i), 0)),
            pl.BlockSpec((N_META, RET_WIDTH), lambda i: (0, 0)),
            pl.BlockSpec((N_META, RET_WIDTH), lambda i: (0, 0)),
            pl.BlockSpec((SSM_GROUPS, HEAD_ROWS, GROUP_LANES), lambda i: (0, 0, 0)),
        ),
        out_shape=out_shape,
        scratch_shapes=[
            pltpu.VMEM((D_MODEL, D_FF), BF16),
            pltpu.VMEM((D_MODEL, D_FF), BF16),
            pltpu.VMEM((D_FF, D_MODEL), BF16),
            pltpu.VMEM((D_MODEL, IN_PROJ), BF16),
            pltpu.VMEM((tm + N_META, D_FF), BF16),
            pltpu.VMEM((OCTETS, tm // SUB * TOK_PITCH, LANES), F32),
            pltpu.VMEM((OCTETS, TOK_PITCH, LANES), F32),
        ],
        compiler_params=pltpu.CompilerParams(
            dimension_semantics=("arbitrary",), vmem_limit_bytes=VMEM_LIMIT),
        name="ffn_inproj",
    )(x2, cos, sin, xm, cos_m, sin_m, n1w, wg, wu, wd, n2w, w_in)


def _retention_kernel(q_ref, k_ref, v_ref, g_ref, km_ref, vm_ref, mask_ref, wq_ref, wk_ref, wm_ref,
                      gc_ref, nw_ref, o_ref, state_ref, *, rows):
    tn = (((0,), (0,)), ((), ()))
    nt = (((1,), (1,)), ((), ()))

    @pl.when(pl.program_id(1) == 0)
    def _():
        for h in range(RET_HEADS):
            hs = slice(h * HEAD_DIM, (h + 1) * HEAD_DIM)
            kw = (km_ref[:, hs].astype(F32) * wm_ref[h]).astype(BF16)
            state_ref[h] = lax.dot_general(kw, vm_ref[:, hs], tn, preferred_element_type=F32)

    def chunk(i, carry):
        r0 = pl.multiple_of(i * RET_BLOCK, RET_BLOCK)
        rs = pl.ds(r0, RET_BLOCK)
        for h in range(RET_HEADS):
            hs = slice(h * HEAD_DIM, (h + 1) * HEAD_DIM)
            qh = q_ref[rs, hs]
            kh = k_ref[rs, hs]
            vh = v_ref[rs, hs]
            st = state_ref[h]
            s = lax.dot_general(qh, kh, nt, preferred_element_type=F32) * mask_ref[h]
            o = (jnp.dot(s.astype(BF16), vh, preferred_element_type=F32)
                 + jnp.dot(qh, st.astype(BF16), preferred_element_type=F32) * wq_ref[h])
            kw = (kh.astype(F32) * wk_ref[h]).astype(BF16)
            state_ref[h] = gc_ref[h] * st + lax.dot_general(kw, vh, tn, preferred_element_type=F32)
            mu = jnp.mean(o, axis=-1, keepdims=True)
            d = o - mu
            var = jnp.mean(d * d, axis=-1, keepdims=True)
            y = d * lax.rsqrt(var + EPS) * nw_ref[:, hs]
            gate = g_ref[rs, hs]
            o_ref[rs, hs] = (gate * jax.nn.sigmoid(gate) * y).astype(BF16)
        return carry

    lax.fori_loop(0, rows // RET_BLOCK, chunk, 0, unroll=4)


def _retention_tables():
    log_g = np.log(1.0 - 2.0 ** (-5.0 - np.arange(RET_HEADS, dtype=np.float64)))
    i = np.arange(RET_BLOCK)
    diff = i[:, None] - i[None, :]
    mask = np.where(diff[None] >= 0, np.exp(log_g[:, None, None] * np.maximum(diff, 0)[None]), 0.0)
    pos = np.arange(RET_BLOCK, dtype=np.float64)
    full = lambda w: np.broadcast_to(w[:, :, None], w.shape + (HEAD_DIM,))
    w_q = full(np.exp(log_g[:, None] * (pos + 1.0)[None]))
    w_k = full(np.exp(log_g[:, None] * (RET_BLOCK - 1 - pos)[None]))
    w_m = full(np.exp(log_g[:, None] * (N_META - 1 - np.arange(N_META, dtype=np.float64))[None]))
    g_c = np.broadcast_to(np.exp(log_g * RET_BLOCK)[:, None, None], (RET_HEADS, HEAD_DIM, HEAD_DIM))
    return tuple(jnp.asarray(t, dtype=F32) for t in (mask, w_q, w_k, w_m, g_c))


def _retention(q, k, v, g, km, vm, ret_norm_w, bsz, seq, rows):
    mask, w_q, w_k, w_m, g_c = _retention_tables()
    steps = seq // rows
    blk = lambda b, c: (b * steps + c, 0)
    full3 = lambda a: pl.BlockSpec(a.shape, lambda b, c: (0, 0, 0))
    full2 = lambda a: pl.BlockSpec(a.shape, lambda b, c: (0, 0))
    return pl.pallas_call(
        functools.partial(_retention_kernel, rows=rows),
        grid=(bsz, steps),
        in_specs=[pl.BlockSpec((rows, RET_WIDTH), blk)] * 4
        + [full2(km), full2(vm), full3(mask), full3(w_q), full3(w_k), full3(w_m), full3(g_c),
           full2(ret_norm_w)],
        out_specs=pl.BlockSpec((rows, RET_WIDTH), blk),
        out_shape=jax.ShapeDtypeStruct((bsz * seq, RET_WIDTH), BF16),
        scratch_shapes=[pltpu.VMEM((RET_HEADS, HEAD_DIM, HEAD_DIM), F32)],
        compiler_params=pltpu.CompilerParams(
            dimension_semantics=("arbitrary", "arbitrary"), vmem_limit_bytes=VMEM_LIMIT),
        name="retention",
    )(q, k, v, g, km, vm, mask, w_q, w_k, w_m, g_c, ret_norm_w)


def _shift_lanes_zero_fill(halves, shift, lane):
    h0, h1 = halves
    whole, s = divmod(shift, LANES)
    r0 = h0 if s == 0 else pltpu.roll(h0, s, axis=1)
    zero = jnp.zeros_like(h0)
    if whole == 1:
        return zero, jnp.where(lane >= s, r0, zero)
    r1 = h1 if s == 0 else pltpu.roll(h1, s, axis=1)
    return jnp.where(lane >= s, r0, zero), jnp.where(lane >= s, r1, r0)


def _s5_assemble_octet(o, vec_ref, mat_ref,
                       toep_ref, win_ref, wout_ref, cpw_ref, pow_ref, xall_ref, wt_ref):
    vec = vec_ref[o]
    lre = vec[0:1]
    lim = vec[1:2]
    dt = jnp.exp(vec[2:3])
    n_pow = pow_ref.shape[1]
    ell = lax.broadcasted_iota(jnp.int32, (n_pow, OCT_STATE), 0).astype(F32)
    mag = jnp.exp(ell * (lre * dt))
    ang = ell * (lim * dt)
    pow_ref[0] = mag * jnp.cos(ang)
    pow_ref[1] = mag * jnp.sin(ang)
    ell = ((lax.broadcasted_iota(jnp.int32, (SUBLANES, OCT_STATE), 0) + 1) * SUB).astype(F32)
    mag = jnp.exp(ell * (lre * dt))
    ang = ell * (lim * dt)
    cp_re = mag * jnp.cos(ang)
    cp_im = mag * jnp.sin(ang)
    a_re = pow_ref[0, 1:2, :]
    a_im = pow_ref[1, 1:2, :]
    den = lre * lre + lim * lim
    num_re = a_re - 1.0
    coef_re = (num_re * lre + a_im * lim) / den
    coef_im = (a_im * lre - num_re * lim) / den
    btr = mat_ref[o, 0]
    bti = mat_ref[o, 1]
    bbar_re = coef_re * btr - coef_im * bti
    bbar_im = coef_re * bti + coef_im * btr
    ctr = mat_ref[o, 2]
    cti = mat_ref[o, 3]

    pairs_per_octet = GROUPS_PER_OCTET // 2
    pair_lane_group = _lane_block((SSM_GROUP, PAIR_STATE), SSM_STATE)
    for l in range(SUB):
        pr = pow_ref[0, l:l + 1, :]
        pi = pow_ref[1, l:l + 1, :]
        x_re = pr * bbar_re - pi * bbar_im
        x_im = pr * bbar_im + pi * bbar_re
        rows = slice(l * SSM_GROUP, (l + 1) * SSM_GROUP)
        xall_ref[0, rows, :] = x_re
        xall_ref[1, rows, :] = x_im
        pr1 = pow_ref[0, l + 1:l + 2, :]
        pi1 = pow_ref[1, l + 1:l + 2, :]
        w_re = pr1 * ctr - pi1 * cti
        w_im = pr1 * cti + pi1 * ctr
        j = SUB - 1 - l
        for pp in range(pairs_per_octet):
            sl = slice(pp * PAIR_STATE, (pp + 1) * PAIR_STATE)
            pair = o * pairs_per_octet + pp
            for gi in range(2):
                keep = pair_lane_group == gi
                in_rows = slice(gi * GROUP_LANES + j * SSM_GROUP, gi * GROUP_LANES + (j + 1) * SSM_GROUP)
                out_rows = slice(gi * GROUP_LANES + l * SSM_GROUP, gi * GROUP_LANES + (l + 1) * SSM_GROUP)
                for part, x, w in ((0, x_re, w_re), (1, x_im, -w_im)):
                    cols = slice(part * PAIR_STATE, (part + 1) * PAIR_STATE)
                    win_ref[pair, in_rows, cols] = jnp.where(keep, x[:, sl], 0.0).astype(BF16)
                    wt_ref[pp, out_rows, cols] = jnp.where(keep, w[:, sl], 0.0)
    for pp in range(pairs_per_octet):
        pair = o * pairs_per_octet + pp
        wout_ref[pair] = wt_ref[pp].T.astype(BF16)
        sl = slice(pp * PAIR_STATE, (pp + 1) * PAIR_STATE)
        cpw_ref[pair, 0] = cp_re[:, sl]
        cpw_ref[pair, 1] = cp_im[:, sl]

    row = lax.broadcasted_iota(jnp.int32, (LANES, OCT_STATE), 0)
    col = lax.broadcasted_iota(jnp.int32, (LANES, OCT_STATE), 1)
    same_group = (row // SSM_GROUP) == (col // SSM_STATE)

    def block_diag_t(x):
        return jnp.where(same_group, jnp.concatenate([x] * GROUPS_PER_OCTET, axis=0), 0.0).T

    k_all = (jnp.dot(xall_ref[0], block_diag_t(ctr), precision=lax.Precision.HIGHEST,
                     preferred_element_type=F32)
             - jnp.dot(xall_ref[1], block_diag_t(cti), precision=lax.Precision.HIGHEST,
                       preferred_element_type=F32))
    per_half = LANES // SSM_GROUP
    blk = _lane_block((SSM_GROUP, LANES), SSM_GROUP)
    lane = lax.broadcasted_iota(jnp.int32, (SSM_GROUP, LANES), 1)
    for g8 in range(GROUPS_PER_OCTET):
        strip = []
        for hh in range(GROUP_LANES // LANES):
            acc = jnp.zeros((SSM_GROUP, LANES), F32)
            for ll in range(per_half):
                lag = hh * per_half + ll
                k_lag = k_all[lag * SSM_GROUP:(lag + 1) * SSM_GROUP, :]
                shift = ((ll - g8) * SSM_GROUP) % LANES
                rolled = k_lag if shift == 0 else pltpu.roll(k_lag, shift, axis=1)
                acc = jnp.where(blk == ll, rolled, acc)
            strip.append(acc)
        for j in range(SUB):
            h0, h1 = _shift_lanes_zero_fill(strip, j * SSM_GROUP, lane)
            rows = slice(j * SSM_GROUP, (j + 1) * SSM_GROUP)
            toep_ref[o * GROUPS_PER_OCTET + g8, rows, 0:LANES] = h0.astype(BF16)
            toep_ref[o * GROUPS_PER_OCTET + g8, rows, LANES:2 * LANES] = h1.astype(BF16)


def _s5_state_scan(st_ref, zb_ref, cpow_ref, nsc):
    tiles = nsc // SUBLANES
    width = st_ref.shape[1] // 2
    re = slice(0, width)
    im = slice(width, 2 * width)
    body = slice(HEAD_ROWS, HEAD_ROWS + nsc)
    rowmod = lax.broadcasted_iota(jnp.int32, (SUBLANES, width), 0)
    src, dst = st_ref, zb_ref
    shift = 1
    while shift < SUBLANES:
        keep = rowmod >= shift
        power = slice(shift - 1, shift)
        c_re = jnp.concatenate([jnp.where(keep, cpow_ref[0, power, :], 0.0)] * tiles, axis=0)
        c_im = jnp.concatenate([jnp.where(keep, cpow_ref[1, power, :], 0.0)] * tiles, axis=0)
        sh_re = src[pl.ds(HEAD_ROWS - shift, nsc), re]
        sh_im = src[pl.ds(HEAD_ROWS - shift, nsc), im]
        dst[body, re] = src[body, re] + c_re * sh_re - c_im * sh_im
        dst[body, im] = src[body, im] + c_re * sh_im + c_im * sh_re
        src, dst = dst, src
        shift *= 2
    ca_re = cpow_ref[0]
    ca_im = cpow_ref[1]
    c_re = st_ref[ENTER_ROW:HEAD_ROWS, re]
    c_im = st_ref[ENTER_ROW:HEAD_ROWS, im]
    for t in range(tiles):
        rows = slice(HEAD_ROWS + SUBLANES * t, HEAD_ROWS + SUBLANES * (t + 1))
        cb_re = jnp.broadcast_to(c_re, (SUBLANES, width))
        cb_im = jnp.broadcast_to(c_im, (SUBLANES, width))
        p_re = src[rows, re] + ca_re * cb_re - ca_im * cb_im
        p_im = src[rows, im] + ca_re * cb_im + ca_im * cb_re
        st_ref[rows, re] = p_re
        st_ref[rows, im] = p_im
        c_re = p_re[SUBLANES - 1:SUBLANES]
        c_im = p_im[SUBLANES - 1:SUBLANES]


def _s5_kernel(u_ref, um_ref, vec_ref, mat_ref, d_ref,
               y_ref, toep_ref, win_ref, wout_ref, cpw_ref, carry_ref, meta_ref, *, nsc):
    first_call_step = (pl.program_id(0) == 0) & (pl.program_id(1) == 0)
    r = pl.program_id(1)
    n_pow = SUB + SUBLANES

    @pl.when(first_call_step)
    def _():
        def assemble(pow_ref, xall_ref, wt_ref):
            def octet(o, carry):
                _s5_assemble_octet(o, vec_ref, mat_ref,
                                   toep_ref, win_ref, wout_ref, cpw_ref, pow_ref, xall_ref, wt_ref)
                return carry
            lax.fori_loop(0, OCTETS, octet, 0)

        pl.run_scoped(
            assemble,
            pltpu.VMEM((2, n_pow, OCT_STATE), F32),
            pltpu.VMEM((2, SUB * SSM_GROUP, OCT_STATE), F32),
            pltpu.VMEM((GROUPS_PER_OCTET // 2, 2 * GROUP_LANES, 2 * PAIR_STATE), F32))

        def meta_state(pair, carry):
            meta = jnp.concatenate([um_ref[2 * pair], um_ref[2 * pair + 1]], axis=1).astype(BF16)
            meta_ref[pair] = jnp.dot(meta, win_ref[pair], preferred_element_type=F32)
            return carry
        lax.fori_loop(0, PAIRS, meta_state, 0)

    @pl.when(r == 0)
    def _():
        carry_ref[...] = meta_ref[...]

    lo = slice(0, GROUP_LANES)
    hi = slice(GROUP_LANES, 2 * GROUP_LANES)

    def pair_loop(lhs_ref, st_ref, zb_ref):
        zb_ref[:, 0:HEAD_ROWS, :] = jnp.zeros((S5_SLOTS, HEAD_ROWS, zb_ref.shape[2]), F32)

        def pair_block(k, carry):
            pairs = [S5_SLOTS * k + s for s in range(S5_SLOTS)]
            for s, pair in enumerate(pairs):
                lhs_ref[s, :, lo] = u_ref[2 * pair].astype(BF16)
                lhs_ref[s, :, hi] = u_ref[2 * pair + 1].astype(BF16)
                st_ref[s, 0:HEAD_ROWS, :] = carry_ref[pair]
                st_ref[s, HEAD_ROWS:HEAD_ROWS + nsc, :] = jnp.dot(lhs_ref[s], win_ref[pair],
                                                                  preferred_element_type=F32)
            for s, pair in enumerate(pairs):
                _s5_state_scan(st_ref.at[s], zb_ref.at[s], cpw_ref.at[pair], nsc)
                carry_ref[pair] = st_ref[s, nsc:nsc + HEAD_ROWS, :]
            for s, pair in enumerate(pairs):
                s_in = st_ref[s, pl.ds(ENTER_ROW, nsc), :].astype(BF16)
                y_state = jnp.dot(s_in, wout_ref[pair], preferred_element_type=F32)
                d = d_ref[pair]
                for gi, cols in ((0, lo), (1, hi)):
                    g = 2 * pair + gi
                    y = (jnp.dot(lhs_ref[s, :, cols], toep_ref[g], preferred_element_type=F32)
                         + y_state[:, cols] + d[:, cols] * u_ref[g])
                    y_ref[g] = y
            return carry

        lax.fori_loop(0, PAIRS // S5_SLOTS, pair_block, 0)

    pl.run_scoped(
        pair_loop,
        pltpu.VMEM((S5_SLOTS, nsc, 2 * GROUP_LANES), BF16),
        pltpu.VMEM((S5_SLOTS, HEAD_ROWS + nsc, 2 * PAIR_STATE), F32),
        pltpu.VMEM((S5_SLOTS, HEAD_ROWS + nsc, 2 * PAIR_STATE), F32))


def _s5_param_rows(lam_re, lam_im, log_dt, b_re, b_im, c_re, c_im, d):
    go = GROUPS_PER_OCTET
    lanes = lambda a: a.reshape(OCTETS, OCT_STATE)
    ldt = jnp.broadcast_to(log_dt[:, None], (SSM_GROUPS, SSM_STATE))
    bt = lambda a: jnp.transpose(a.reshape(OCTETS, go, SSM_STATE, SSM_GROUP),
                                 (0, 3, 1, 2)).reshape(OCTETS, SSM_GROUP, OCT_STATE)
    ct = lambda a: jnp.transpose(a.reshape(OCTETS, go, SSM_GROUP, SSM_STATE),
                                 (0, 2, 1, 3)).reshape(OCTETS, SSM_GROUP, OCT_STATE)
    vec = jnp.stack([lanes(lam_re), lanes(lam_im), lanes(ldt)], axis=1)
    mat = jnp.stack([bt(b_re), bt(b_im), ct(c_re), ct(c_im)], axis=1)
    d_pairs = jnp.broadcast_to(d.reshape(PAIRS, 2, 1, SSM_GROUP),
                               (PAIRS, 2, SUB, SSM_GROUP)).reshape(PAIRS, 1, 2 * GROUP_LANES)
    return vec, mat, d_pairs


def _s5(u, um, params, bsz, seq, rows):
    nsc = rows // SUB
    steps = seq // rows
    full = lambda a: pl.BlockSpec(a.shape, lambda b, r: (0,) * a.ndim)
    tok_blk = pl.BlockSpec((SSM_GROUPS, nsc, GROUP_LANES), lambda b, r: (0, b * steps + r, 0))
    return pl.pallas_call(
        functools.partial(_s5_kernel, nsc=nsc),
        grid=(bsz, steps),
        in_specs=[tok_blk, full(um)] + [full(p) for p in params],
        out_specs=tok_blk,
        out_shape=jax.ShapeDtypeStruct(u.shape, F32),
        scratch_shapes=[
            pltpu.VMEM((SSM_GROUPS, GROUP_LANES, GROUP_LANES), BF16),
            pltpu.VMEM((PAIRS, 2 * GROUP_LANES, 2 * PAIR_STATE), BF16),
            pltpu.VMEM((PAIRS, 2 * PAIR_STATE, 2 * GROUP_LANES), BF16),
            pltpu.VMEM((PAIRS, 2, SUBLANES, PAIR_STATE), F32),
            pltpu.VMEM((PAIRS, HEAD_ROWS, 2 * PAIR_STATE), F32),
            pltpu.VMEM((PAIRS, HEAD_ROWS, 2 * PAIR_STATE), F32),
        ],
        compiler_params=pltpu.CompilerParams(
            dimension_semantics=("arbitrary", "arbitrary"), vmem_limit_bytes=VMEM_LIMIT),
        name="s5_mixer",
    )(u, um, *params)


def _mix_ffn_kernel(h1_ref, ret_ref, ynext_ref, gw32_ref, gb_ref, snw_ref, wo32_ref, n3w_ref,
                    wg32_ref, wu32_ref, wd32_ref, fnw_ref, out_ref,
                    gw_ref, wo_ref, wg_ref, wu_ref, wd_ref, acc_ref, ytok_ref):
    step = pl.program_id(0)
    nsub = ynext_ref.shape[1]

    @pl.when(step < WEIGHT_CHUNKS)
    def _():
        _cast_weight_chunks(step, ((gw32_ref, gw_ref), (wo32_ref, wo_ref), (wg32_ref, wg_ref),
                                   (wu32_ref, wu_ref), (wd32_ref, wd_ref)))

    @pl.when(step == WEIGHT_CHUNKS - 1)
    def _():
        _to_token_major(ynext_ref, ytok_ref.at[0], nsub, jax.nn.gelu)

    @pl.when(step >= WEIGHT_CHUNKS)
    def _():
        slot = lax.rem(step - WEIGHT_CHUNKS, 2)
        y = _token_major_rows(ytok_ref.at[slot], nsub)
        _to_token_major(ynext_ref, ytok_ref.at[1 - slot], nsub, jax.nn.gelu)
        z = y * jax.nn.sigmoid(jnp.dot(y.astype(BF16), gw_ref[...], preferred_element_type=F32)
                               + gb_ref[...])
        ssm = _rms(z, snw_ref[...]).astype(BF16)
        mixed = jnp.dot(jnp.concatenate([ret_ref[...], ssm], axis=-1), wo_ref[...],
                        preferred_element_type=F32)
        h2 = h1_ref[...] + mixed
        out_ref[...] = h2
        h3 = _swiglu_half_step(lambda: out_ref[...], _rms(h2, n3w_ref[...]).astype(BF16), wg_ref,
                               wu_ref, wd_ref, acc_ref)
        out_ref[...] = _rms(h3, fnw_ref[...])


def _mix_ffn(h1, ret, y, glu_w, glu_b, ssm_norm_w, w_out, n3w, wg, wu, wd, fnw, tm):
    rows = h1.shape[0]
    tiles = rows // tm
    row_blk = lambda i: (_tile_index(i), 0)
    next_blk = lambda i: (0, jnp.minimum(_tile_index(i + 1), tiles - 1), 0)
    return pl.pallas_call(
        _mix_ffn_kernel,
        grid=(WEIGHT_CHUNKS + tiles,),
        in_specs=[
            pl.BlockSpec((tm, D_MODEL), row_blk),
            pl.BlockSpec((tm, RET_WIDTH), row_blk),
            pl.BlockSpec((SSM_GROUPS, tm // SUB, GROUP_LANES), next_blk),
            _weight_chunk_spec((SSM_WIDTH, SSM_WIDTH)),
            _resident((1, SSM_WIDTH)),
            _resident((1, SSM_WIDTH)),
            _weight_chunk_spec((D_MODEL, D_MODEL)),
            _resident((1, D_MODEL)),
            _weight_chunk_spec((D_MODEL, D_FF)),
            _weight_chunk_spec((D_MODEL, D_FF)),
            _weight_chunk_spec((D_FF, D_MODEL)),
            _resident((1, D_MODEL)),
        ],
        out_specs=pl.BlockSpec((tm, D_MODEL), row_blk),
        out_shape=jax.ShapeDtypeStruct((rows, D_MODEL), F32),
        scratch_shapes=[
            pltpu.VMEM((SSM_WIDTH, SSM_WIDTH), BF16),
            pltpu.VMEM((D_MODEL, D_MODEL), BF16),
            pltpu.VMEM((D_MODEL, D_FF), BF16),
            pltpu.VMEM((D_MODEL, D_FF), BF16),
            pltpu.VMEM((D_FF, D_MODEL), BF16),
            pltpu.VMEM((tm, D_FF), BF16),
            pltpu.VMEM((2, OCTETS, tm // SUB * TOK_PITCH, LANES), F32),
        ],
        compiler_params=pltpu.CompilerParams(
            dimension_semantics=("arbitrary",), vmem_limit_bytes=VMEM_LIMIT),
        name="mix_ffn",
    )(h1, ret, y, glu_w, glu_b, ssm_norm_w, w_out, n3w, wg, wu, wd, fnw)


def _rope_tables(n_pos):
    freqs = 1.0 / (ROPE_BASE ** (np.arange(0, HEAD_DIM, 2, dtype=np.float64) / HEAD_DIM))
    ang = np.arange(n_pos, dtype=np.float64)[:, None] * freqs[None, :]
    cos = np.cos(ang)
    sin = np.sin(ang)
    cos = np.concatenate([cos, cos], axis=-1).astype(np.float32)
    sin = np.concatenate([-sin, sin], axis=-1).astype(np.float32)
    split = lambda t: (jnp.asarray(t[:N_META]), jnp.asarray(t[N_META:]))
    return split(cos), split(sin)


def kernel(x, meta_tokens, ffn1_norm_w, ffn1_w_gate, ffn1_w_up, ffn1_w_down, mix_norm_w, w_in,
           ret_norm_w, ssm_lambda_re, ssm_lambda_im, ssm_log_dt, ssm_b_re, ssm_b_im, ssm_c_re,
           ssm_c_im, ssm_d, ssm_glu_w, ssm_glu_b, ssm_norm_w, w_out, ffn2_norm_w, ffn2_w_gate,
           ffn2_w_up, ffn2_w_down, final_norm_w):
    bsz, seq, _ = x.shape
    assert ffn1_norm_w.shape[0] == 1, "single layer only"
    tm = 512
    ret_rows = 2048
    s5_rows = 4096
    assert seq % tm == 0 and seq % ret_rows == 0 and seq % s5_rows == 0

    l = 0
    (cos_m, cos), (sin_m, sin) = _rope_tables(N_META + seq)

    h1, q, k, v, g, u, km, vm, um = _ffn_inproj(
        x.reshape(bsz * seq, D_MODEL), meta_tokens, cos, sin, cos_m, sin_m,
        ffn1_norm_w, ffn1_w_gate[l], ffn1_w_up[l], ffn1_w_down[l], mix_norm_w, w_in[l], tm=tm)

    ret = _retention(q, k, v, g, km, vm, ret_norm_w, bsz, seq, ret_rows)

    params = _s5_param_rows(ssm_lambda_re[l], ssm_lambda_im[l], ssm_log_dt[l], ssm_b_re[l],
                            ssm_b_im[l], ssm_c_re[l], ssm_c_im[l], ssm_d[l])
    y = _s5(u, um, params, bsz, seq, s5_rows)

    out = _mix_ffn(h1, ret, y, ssm_glu_w[l], ssm_glu_b, ssm_norm_w, w_out[l], ffn2_norm_w,
                   ffn2_w_gate[l], ffn2_w_up[l], ffn2_w_down[l], final_norm_w.reshape(1, D_MODEL),
                   tm=tm)
    return out.reshape(bsz, seq, D_MODEL)
```

```python
import functools

import jax
import jax.numpy as jnp
import numpy as np
from jax import lax
from jax.experimental import pallas as pl
from jax.experimental.pallas import tpu as pltpu

D_MODEL = 1024
N_META = 16
RET_HEADS = 4
HEAD_DIM = 128
RET_WIDTH = RET_HEADS * HEAD_DIM
SSM_WIDTH = 512
SSM_GROUP = 16
SSM_GROUPS = SSM_WIDTH // SSM_GROUP
SSM_STATE = 64
RET_BLOCK = 256
D_FF = 2816
FFN_RES = 0.5
ROPE_BASE = 10000.0
EPS = 1e-6
IN_PROJ = 4 * RET_WIDTH + SSM_WIDTH

LANES = 128
SUBLANES = 8
HEAD_ROWS = SUBLANES
ENTER_ROW = HEAD_ROWS - 1
OCTETS = SSM_WIDTH // LANES
GROUPS_PER_OCTET = LANES // SSM_GROUP
SUB = N_META
OCT_STATE = GROUPS_PER_OCTET * SSM_STATE
GROUP_LANES = SUB * SSM_GROUP
PAIRS = SSM_GROUPS // 2
PAIR_STATE = 2 * SSM_STATE
S5_SLOTS = 8
TOK_PITCH = 24
FF_CHUNK = 256
WEIGHT_CHUNKS = 8
VMEM_LIMIT = 60 * 1024 * 1024

F32 = jnp.float32
BF16 = jnp.bfloat16


def _rms(x, w):
    return x * lax.rsqrt(jnp.mean(x * x, axis=-1, keepdims=True) + EPS) * w


def _swiglu_half_step(load_h, n, wg_ref, wu_ref, wd_ref, acc_ref):
    for lo in range(0, D_FF, FF_CHUNK):
        sl = slice(lo, min(lo + FF_CHUNK, D_FF))
        g = jnp.dot(n, wg_ref[:, sl], preferred_element_type=F32)
        u = jnp.dot(n, wu_ref[:, sl], preferred_element_type=F32)
        acc_ref[:, sl] = (g * jax.nn.sigmoid(g) * u).astype(BF16)
    return load_h() + FFN_RES * jnp.dot(acc_ref[...], wd_ref[...], preferred_element_type=F32)


def _rope(x, cos, sin_signed):
    return x * cos + pltpu.roll(x, HEAD_DIM // 2, axis=1) * sin_signed


def _lane_block(shape, width):
    return lax.broadcasted_iota(jnp.int32, shape, 1) // width


def _transpose_lane_blocks(v):
    n = LANES // SSM_GROUP
    assert len(v) == n == GROUPS_PER_OCTET
    blk = _lane_block(v[0].shape, SSM_GROUP)
    d = n // 2
    while d:
        low = (blk & d) == 0
        nxt = list(v)
        for a in range(n):
            if a & d == 0:
                nxt[a] = jnp.where(low, v[a], pltpu.roll(v[a + d], d * SSM_GROUP, axis=1))
                nxt[a + d] = jnp.where(low, pltpu.roll(v[a], LANES - d * SSM_GROUP, axis=1), v[a + d])
        v = nxt
        d //= 2
    return v


def _to_group_dense(tok_ref, out_ref, nsub, row0=0):
    per_half = LANES // SSM_GROUP
    for o in range(OCTETS):
        for hh in range(GROUP_LANES // LANES):
            slabs = [tok_ref[o, pl.ds(hh * per_half + jj, nsub, stride=TOK_PITCH), :]
                     for jj in range(per_half)]
            for g8, rows in enumerate(_transpose_lane_blocks(slabs)):
                out_ref[o * GROUPS_PER_OCTET + g8, row0:row0 + nsub,
                        hh * LANES:(hh + 1) * LANES] = rows


def _to_token_major(gd_ref, tok_ref, nsub, fn):
    per_half = LANES // SSM_GROUP
    for o in range(OCTETS):
        for hh in range(GROUP_LANES // LANES):
            rows = [gd_ref[o * GROUPS_PER_OCTET + g8, :, hh * LANES:(hh + 1) * LANES]
                    for g8 in range(GROUPS_PER_OCTET)]
            for jj, slab in enumerate(_transpose_lane_blocks(rows)):
                tok_ref[o, pl.ds(hh * per_half + jj, nsub, stride=TOK_PITCH), :] = fn(slab)


def _token_major_rows(tok_ref, nsub):
    return jnp.concatenate(
        [jnp.concatenate([tok_ref[o, c * TOK_PITCH:c * TOK_PITCH + SUB, :] for c in range(nsub)],
                         axis=0) for o in range(OCTETS)], axis=-1)


def _cast_weight_chunks(step, pairs):
    for src_ref, dst_ref in pairs:
        rows = src_ref.shape[0]
        r0 = pl.multiple_of(step * rows, rows)
        dst_ref[pl.ds(r0, rows), :] = src_ref[...].astype(BF16)


def _weight_chunk_spec(shape):
    rows = shape[0] // WEIGHT_CHUNKS
    assert rows * WEIGHT_CHUNKS == shape[0] and rows % 16 == 0, shape
    return pl.BlockSpec((rows, shape[1]), lambda i: (jnp.minimum(i, WEIGHT_CHUNKS - 1), 0))


def _tile_index(i):
    return jnp.maximum(i - WEIGHT_CHUNKS, 0)


def _ffn_inproj_kernel(x_ref, cos_ref, sin_ref, xm_ref, cosm_ref, sinm_ref, n1w_ref, wg32_ref,
                       wu32_ref, wd32_ref, n2w_ref, win32_ref,
                       h1_ref, q_ref, k_ref, v_ref, g_ref, u_ref, km_ref, vm_ref, um_ref,
                       wg_ref, wu_ref, wd_ref, win_ref, acc_ref, utok_ref, utokm_ref):
    step = pl.program_id(0)
    k_scale = HEAD_DIM ** -0.5

    def ffn_norm(h):
        return _rms(h, n1w_ref[...]).astype(BF16)

    def mix_norm(h1):
        return _rms(h1, n2w_ref[...]).astype(BF16)

    def in_proj(n, part):
        return jnp.dot(n, win_ref[:, part * RET_WIDTH:(part + 1) * RET_WIDTH],
                       preferred_element_type=F32)

    def rope_heads(p, cos, sin, scale, out_ref):
        for h in range(RET_HEADS):
            hs = slice(h * HEAD_DIM, (h + 1) * HEAD_DIM)
            y = _rope(p[:, hs], cos, sin)
            out_ref[:, hs] = (y if scale is None else y * scale).astype(BF16)

    def split_octets(p, tok_ref):
        for o in range(OCTETS):
            for c in range(p.shape[0] // SUB):
                tok_ref[o, c * TOK_PITCH:c * TOK_PITCH + SUB, :] = (
                    p[c * SUB:(c + 1) * SUB, o * LANES:(o + 1) * LANES])

    @pl.when(step < WEIGHT_CHUNKS)
    def _():
        _cast_weight_chunks(step, ((wg32_ref, wg_ref), (wu32_ref, wu_ref), (wd32_ref, wd_ref),
                                   (win32_ref, win_ref)))

    def token_tile(with_meta):
        tm = x_ref.shape[0]

        def load_x():
            if with_meta:
                return jnp.concatenate([x_ref[...], xm_ref[...]], axis=0)
            return x_ref[...]

        acc = acc_ref if with_meta else acc_ref.at[pl.ds(0, tm)]
        h1 = _swiglu_half_step(load_x, ffn_norm(load_x()), wg_ref, wu_ref, wd_ref, acc)
        h1_ref[...] = h1[0:tm]
        n = mix_norm(h1)
        pu = in_proj(n, 4)
        split_octets(pu[0:tm], utok_ref)
        _to_group_dense(utok_ref, u_ref, u_ref.shape[1])
        cos = cos_ref[...]
        sin = sin_ref[...]
        rope_heads(in_proj(n[0:tm], 0), cos, sin, None, q_ref)
        pk = in_proj(n, 1)
        rope_heads(pk[0:tm], cos, sin, k_scale, k_ref)
        pv = in_proj(n, 2)
        v_ref[...] = pv[0:tm].astype(BF16)
        g_ref[...] = in_proj(n[0:tm], 3)
        if with_meta:
            rope_heads(pk[tm:], cosm_ref[...], sinm_ref[...], k_scale, km_ref)
            vm_ref[...] = pv[tm:].astype(BF16)
            split_octets(pu[tm:], utokm_ref)
            um_ref[...] = jnp.zeros(um_ref.shape, F32)
            _to_group_dense(utokm_ref, um_ref, 1, row0=ENTER_ROW)

    @pl.when(step == WEIGHT_CHUNKS)
    def _():
        token_tile(with_meta=True)

    @pl.when(step > WEIGHT_CHUNKS)
    def _():
        token_tile(with_meta=False)


def _resident(shape):
    nd = len(shape)
    return pl.BlockSpec(shape, lambda *_: (0,) * nd, pipeline_mode=pl.Buffered(1))


def _ffn_inproj(x2, xm, cos, sin, cos_m, sin_m, n1w, wg, wu, wd, n2w, w_in, tm):
    rows = x2.shape[0]
    pos_blocks = cos.shape[0] // tm
    row_blk = lambda i: (_tile_index(i), 0)
    pos_blk = lambda i: (_tile_index(i) % pos_blocks, 0)
    out_shape = (
        jax.ShapeDtypeStruct((rows, D_MODEL), F32),
        jax.ShapeDtypeStruct((rows, RET_WIDTH), BF16),
        jax.ShapeDtypeStruct((rows, RET_WIDTH), BF16),
        jax.ShapeDtypeStruct((rows, RET_WIDTH), BF16),
        jax.ShapeDtypeStruct((rows, RET_WIDTH), F32),
        jax.ShapeDtypeStruct((SSM_GROUPS, rows // SUB, GROUP_LANES), F32),
        jax.ShapeDtypeStruct((N_META, RET_WIDTH), BF16),
        jax.ShapeDtypeStruct((N_META, RET_WIDTH), BF16),
        jax.ShapeDtypeStruct((SSM_GROUPS, HEAD_ROWS, GROUP_LANES), F32),
    )
    return pl.pallas_call(
        _ffn_inproj_kernel,
        grid=(WEIGHT_CHUNKS + rows // tm,),
        in_specs=[
            pl.BlockSpec((tm, D_MODEL), row_blk),
            pl.BlockSpec((tm, HEAD_DIM), pos_blk),
            pl.BlockSpec((tm, HEAD_DIM), pos_blk),
            _resident((N_META, D_MODEL)),
            _resident((N_META, HEAD_DIM)),
            _resident((N_META, HEAD_DIM)),
            _resident((1, D_MODEL)),
            _weight_chunk_spec((D_MODEL, D_FF)),
            _weight_chunk_spec((D_MODEL, D_FF)),
            _weight_chunk_spec((D_FF, D_MODEL)),
            _resident((1, D_MODEL)),
            _weight_chunk_spec((D_MODEL, IN_PROJ)),
        ],
        out_specs=(
            pl.BlockSpec((tm, D_MODEL), row_blk),
            pl.BlockSpec((tm, RET_WIDTH), row_blk),
            pl.BlockSpec((tm, RET_WIDTH), row_blk),
            pl.BlockSpec((tm, RET_WIDTH), row_blk),
            pl.BlockSpec((tm, RET_WIDTH), row_blk),
            pl.BlockSpec((SSM_GROUPS, tm // SUB, GROUP_LANES), lambda i: (0, _tile_index(i), 0)),
            pl.BlockSpec((N_META, RET_WIDTH), lambda i: (0, 0)),
            pl.BlockSpec((N_META, RET_WIDTH), lambda i: (0, 0)),
            pl.BlockSpec((SSM_GROUPS, HEAD_ROWS, GROUP_LANES), lambda i: (0, 0, 0)),
        ),
        out_shape=out_shape,
        scratch_shapes=[
            pltpu.VMEM((D_MODEL, D_FF), BF16),
            pltpu.VMEM((D_MODEL, D_FF), BF16),
            pltpu.VMEM((D_FF, D_MODEL), BF16),
            pltpu.VMEM((D_MODEL, IN_PROJ), BF16),
            pltpu.VMEM((tm + N_META, D_FF), BF16),
            pltpu.VMEM((OCTETS, tm // SUB * TOK_PITCH, LANES), F32),
            pltpu.VMEM((OCTETS, TOK_PITCH, LANES), F32),
        ],
        compiler_params=pltpu.CompilerParams(
            dimension_semantics=("arbitrary",), vmem_limit_bytes=VMEM_LIMIT),
        name="ffn_inproj",
    )(x2, cos, sin, xm, cos_m, sin_m, n1w, wg, wu, wd, n2w, w_in)


def _retention_kernel(q_ref, k_ref, v_ref, g_ref, km_ref, vm_ref, mask_ref, wq_ref, wk_ref, wm_ref,
                      gc_ref, nw_ref, o_ref, state_ref, *, rows):
    tn = (((0,), (0,)), ((), ()))
    nt = (((1,), (1,)), ((), ()))

    @pl.when(pl.program_id(1) == 0)
    def _():
        for h in range(RET_HEADS):
            hs = slice(h * HEAD_DIM, (h + 1) * HEAD_DIM)
            kw = (km_ref[:, hs].astype(F32) * wm_ref[h]).astype(BF16)
            state_ref[h] = lax.dot_general(kw, vm_ref[:, hs], tn, preferred_element_type=F32)

    def chunk(i, carry):
        r0 = pl.multiple_of(i * RET_BLOCK, RET_BLOCK)
        rs = pl.ds(r0, RET_BLOCK)
        for h in range(RET_HEADS):
            hs = slice(h * HEAD_DIM, (h + 1) * HEAD_DIM)
            qh = q_ref[rs, hs]
            kh = k_ref[rs, hs]
            vh = v_ref[rs, hs]
            st = state_ref[h]
            s = lax.dot_general(qh, kh, nt, preferred_element_type=F32) * mask_ref[h]
            o = (jnp.dot(s.astype(BF16), vh, preferred_element_type=F32)
                 + jnp.dot(qh, st.astype(BF16), preferred_element_type=F32) * wq_ref[h])
            kw = (kh.astype(F32) * wk_ref[h]).astype(BF16)
            state_ref[h] = gc_ref[h] * st + lax.dot_general(kw, vh, tn, preferred_element_type=F32)
            mu = jnp.mean(o, axis=-1, keepdims=True)
            d = o - mu
            var = jnp.mean(d * d, axis=-1, keepdims=True)
            y = d * lax.rsqrt(var + EPS) * nw_ref[:, hs]
            gate = g_ref[rs, hs]
            o_ref[rs, hs] = (gate * jax.nn.sigmoid(gate) * y).astype(BF16)
        return carry

    lax.fori_loop(0, rows // RET_BLOCK, chunk, 0, unroll=4)


def _retention_tables():
    log_g = np.log(1.0 - 2.0 ** (-5.0 - np.arange(RET_HEADS, dtype=np.float64)))
    i = np.arange(RET_BLOCK)
    diff = i[:, None] - i[None, :]
    mask = np.where(diff[None] >= 0, np.exp(log_g[:, None, None] * np.maximum(diff, 0)[None]), 0.0)
    pos = np.arange(RET_BLOCK, dtype=np.float64)
    full = lambda w: np.broadcast_to(w[:, :, None], w.shape + (HEAD_DIM,))
    w_q = full(np.exp(log_g[:, None] * (pos + 1.0)[None]))
    w_k = full(np.exp(log_g[:, None] * (RET_BLOCK - 1 - pos)[None]))
    w_m = full(np.exp(log_g[:, None] * (N_META - 1 - np.arange(N_META, dtype=np.float64))[None]))
    g_c = np.broadcast_to(np.exp(log_g * RET_BLOCK)[:, None, None], (RET_HEADS, HEAD_DIM, HEAD_DIM))
    return tuple(jnp.asarray(t, dtype=F32) for t in (mask, w_q, w_k, w_m, g_c))


def _retention(q, k, v, g, km, vm, ret_norm_w, bsz, seq, rows):
    mask, w_q, w_k, w_m, g_c = _retention_tables()
    steps = seq // rows
    blk = lambda b, c: (b * steps + c, 0)
    full3 = lambda a: pl.BlockSpec(a.shape, lambda b, c: (0, 0, 0))
    full2 = lambda a: pl.BlockSpec(a.shape, lambda b, c: (0, 0))
    return pl.pallas_call(
        functools.partial(_retention_kernel, rows=rows),
        grid=(bsz, steps),
        in_specs=[pl.BlockSpec((rows, RET_WIDTH), blk)] * 4
        + [full2(km), full2(vm), full3(mask), full3(w_q), full3(w_k), full3(w_m), full3(g_c),
           full2(ret_norm_w)],
        out_specs=pl.BlockSpec((rows, RET_WIDTH), blk),
        out_shape=jax.ShapeDtypeStruct((bsz * seq, RET_WIDTH), BF16),
        scratch_shapes=[pltpu.VMEM((RET_HEADS, HEAD_DIM, HEAD_DIM), F32)],
        compiler_params=pltpu.CompilerParams(
            dimension_semantics=("arbitrary", "arbitrary"), vmem_limit_bytes=VMEM_LIMIT),
        name="retention",
    )(q, k, v, g, km, vm, mask, w_q, w_k, w_m, g_c, ret_norm_w)


def _shift_lanes_zero_fill(halves, shift, lane):
    h0, h1 = halves
    whole, s = divmod(shift, LANES)
    r0 = h0 if s == 0 else pltpu.roll(h0, s, axis=1)
    zero = jnp.zeros_like(h0)
    if whole == 1:
        return zero, jnp.where(lane >= s, r0, zero)
    r1 = h1 if s == 0 else pltpu.roll(h1, s, axis=1)
    return jnp.where(lane >= s, r0, zero), jnp.where(lane >= s, r1, r0)


def _s5_assemble_octet(o, vec_ref, mat_ref,
                       toep_ref, win_ref, wout_ref, cpw_ref, pow_ref, xall_ref, wt_ref):
    vec = vec_ref[o]
    lre = vec[0:1]
    lim = vec[1:2]
    dt = jnp.exp(vec[2:3])
    n_pow = pow_ref.shape[1]
    ell = lax.broadcasted_iota(jnp.int32, (n_pow, OCT_STATE), 0).astype(F32)
    mag = jnp.exp(ell * (lre * dt))
    ang = ell * (lim * dt)
    pow_ref[0] = mag * jnp.cos(ang)
    pow_ref[1] = mag * jnp.sin(ang)
    ell = ((lax.broadcasted_iota(jnp.int32, (SUBLANES, OCT_STATE), 0) + 1) * SUB).astype(F32)
    mag = jnp.exp(ell * (lre * dt))
    ang = ell * (lim * dt)
    cp_re = mag * jnp.cos(ang)
    cp_im = mag * jnp.sin(ang)
    a_re = pow_ref[0, 1:2, :]
    a_im = pow_ref[1, 1:2, :]
    den = lre * lre + lim * lim
    num_re = a_re - 1.0
    coef_re = (num_re * lre + a_im * lim) / den
    coef_im = (a_im * lre - num_re * lim) / den
    btr = mat_ref[o, 0]
    bti = mat_ref[o, 1]
    bbar_re = coef_re * btr - coef_im * bti
    bbar_im = coef_re * bti + coef_im * btr
    ctr = mat_ref[o, 2]
    cti = mat_ref[o, 3]

    pairs_per_octet = GROUPS_PER_OCTET // 2
    pair_lane_group = _lane_block((SSM_GROUP, PAIR_STATE), SSM_STATE)
    for l in range(SUB):
        pr = pow_ref[0, l:l + 1, :]
        pi = pow_ref[1, l:l + 1, :]
        x_re = pr * bbar_re - pi * bbar_im
        x_im = pr * bbar_im + pi * bbar_re
        rows = slice(l * SSM_GROUP, (l + 1) * SSM_GROUP)
        xall_ref[0, rows, :] = x_re
        xall_ref[1, rows, :] = x_im
        pr1 = pow_ref[0, l + 1:l + 2, :]
        pi1 = pow_ref[1, l + 1:l + 2, :]
        w_re = pr1 * ctr - pi1 * cti
        w_im = pr1 * cti + pi1 * ctr
        j = SUB - 1 - l
        for pp in range(pairs_per_octet):
            sl = slice(pp * PAIR_STATE, (pp + 1) * PAIR_STATE)
            pair = o * pairs_per_octet + pp
            for gi in range(2):
                keep = pair_lane_group == gi
                in_rows = slice(gi * GROUP_LANES + j * SSM_GROUP, gi * GROUP_LANES + (j + 1) * SSM_GROUP)
                out_rows = slice(gi * GROUP_LANES + l * SSM_GROUP, gi * GROUP_LANES + (l + 1) * SSM_GROUP)
                for part, x, w in ((0, x_re, w_re), (1, x_im, -w_im)):
                    cols = slice(part * PAIR_STATE, (part + 1) * PAIR_STATE)
                    win_ref[pair, in_rows, cols] = jnp.where(keep, x[:, sl], 0.0).astype(BF16)
                    wt_ref[pp, out_rows, cols] = jnp.where(keep, w[:, sl], 0.0)
    for pp in range(pairs_per_octet):
        pair = o * pairs_per_octet + pp
        wout_ref[pair] = wt_ref[pp].T.astype(BF16)
        sl = slice(pp * PAIR_STATE, (pp + 1) * PAIR_STATE)
        cpw_ref[pair, 0] = cp_re[:, sl]
        cpw_ref[pair, 1] = cp_im[:, sl]

    row = lax.broadcasted_iota(jnp.int32, (LANES, OCT_STATE), 0)
    col = lax.broadcasted_iota(jnp.int32, (LANES, OCT_STATE), 1)
    same_group = (row // SSM_GROUP) == (col // SSM_STATE)

    def block_diag_t(x):
        return jnp.where(same_group, jnp.concatenate([x] * GROUPS_PER_OCTET, axis=0), 0.0).T

    k_all = (jnp.dot(xall_ref[0], block_diag_t(ctr), precision=lax.Precision.HIGHEST,
                     preferred_element_type=F32)
             - jnp.dot(xall_ref[1], block_diag_t(cti), precision=lax.Precision.HIGHEST,
                       preferred_element_type=F32))
    per_half = LANES // SSM_GROUP
    blk = _lane_block((SSM_GROUP, LANES), SSM_GROUP)
    lane = lax.broadcasted_iota(jnp.int32, (SSM_GROUP, LANES), 1)
    for g8 in range(GROUPS_PER_OCTET):
        strip = []
        for hh in range(GROUP_LANES // LANES):
            acc = jnp.zeros((SSM_GROUP, LANES), F32)
            for ll in range(per_half):
                lag = hh * per_half + ll
                k_lag = k_all[lag * SSM_GROUP:(lag + 1) * SSM_GROUP, :]
                shift = ((ll - g8) * SSM_GROUP) % LANES
                rolled = k_lag if shift == 0 else pltpu.roll(k_lag, shift, axis=1)
                acc = jnp.where(blk == ll, rolled, acc)
            strip.append(acc)
        for j in range(SUB):
            h0, h1 = _shift_lanes_zero_fill(strip, j * SSM_GROUP, lane)
            rows = slice(j * SSM_GROUP, (j + 1) * SSM_GROUP)
            toep_ref[o * GROUPS_PER_OCTET + g8, rows, 0:LANES] = h0.astype(BF16)
            toep_ref[o * GROUPS_PER_OCTET + g8, rows, LANES:2 * LANES] = h1.astype(BF16)


def _s5_state_scan(st_ref, zb_ref, cpow_ref, nsc):
    tiles = nsc // SUBLANES
    width = st_ref.shape[1] // 2
    re = slice(0, width)
    im = slice(width, 2 * width)
    body = slice(HEAD_ROWS, HEAD_ROWS + nsc)
    rowmod = lax.broadcasted_iota(jnp.int32, (SUBLANES, width), 0)
    src, dst = st_ref, zb_ref
    shift = 1
    while shift < SUBLANES:
        keep = rowmod >= shift
        power = slice(shift - 1, shift)
        c_re = jnp.concatenate([jnp.where(keep, cpow_ref[0, power, :], 0.0)] * tiles, axis=0)
        c_im = jnp.concatenate([jnp.where(keep, cpow_ref[1, power, :], 0.0)] * tiles, axis=0)
        sh_re = src[pl.ds(HEAD_ROWS - shift, nsc), re]
        sh_im = src[pl.ds(HEAD_ROWS - shift, nsc), im]
        dst[body, re] = src[body, re] + c_re * sh_re - c_im * sh_im
        dst[body, im] = src[body, im] + c_re * sh_im + c_im * sh_re
        src, dst = dst, src
        shift *= 2
    ca_re = cpow_ref[0]
    ca_im = cpow_ref[1]
    c_re = st_ref[ENTER_ROW:HEAD_ROWS, re]
    c_im = st_ref[ENTER_ROW:HEAD_ROWS, im]
    for t in range(tiles):
        rows = slice(HEAD_ROWS + SUBLANES * t, HEAD_ROWS + SUBLANES * (t + 1))
        cb_re = jnp.broadcast_to(c_re, (SUBLANES, width))
        cb_im = jnp.broadcast_to(c_im, (SUBLANES, width))
        p_re = src[rows, re] + ca_re * cb_re - ca_im * cb_im
        p_im = src[rows, im] + ca_re * cb_im + ca_im * cb_re
        st_ref[rows, re] = p_re
        st_ref[rows, im] = p_im
        c_re = p_re[SUBLANES - 1:SUBLANES]
        c_im = p_im[SUBLANES - 1:SUBLANES]


def _s5_kernel(u_ref, um_ref, vec_ref, mat_ref, d_ref,
               y_ref, toep_ref, win_ref, wout_ref, cpw_ref, pow_ref, xall_ref, wt_ref, lhs_ref,
               st_ref, zb_ref, carry_ref, meta_ref, *, nsc):
    first_call_step = (pl.program_id(0) == 0) & (pl.program_id(1) == 0)
    r = pl.program_id(1)

    @pl.when(first_call_step)
    def _():
        zb_ref[:, 0:HEAD_ROWS, :] = jnp.zeros((zb_ref.shape[0], HEAD_ROWS, zb_ref.shape[2]), F32)

        def octet(o, carry):
            _s5_assemble_octet(o, vec_ref, mat_ref,
                               toep_ref, win_ref, wout_ref, cpw_ref, pow_ref, xall_ref, wt_ref)
            return carry
        lax.fori_loop(0, OCTETS, octet, 0)

        def meta_state(pair, carry):
            meta = jnp.concatenate([um_ref[2 * pair], um_ref[2 * pair + 1]], axis=1).astype(BF16)
            meta_ref[pair] = jnp.dot(meta, win_ref[pair], preferred_element_type=F32)
            return carry
        lax.fori_loop(0, PAIRS, meta_state, 0)

    @pl.when(r == 0)
    def _():
        carry_ref[...] = meta_ref[...]

    lo = slice(0, GROUP_LANES)
    hi = slice(GROUP_LANES, 2 * GROUP_LANES)
    slots = st_ref.shape[0]

    def pair_block(k, carry):
        pairs = [slots * k + s for s in range(slots)]
        for s, pair in enumerate(pairs):
            lhs_ref[s, :, lo] = u_ref[2 * pair].astype(BF16)
            lhs_ref[s, :, hi] = u_ref[2 * pair + 1].astype(BF16)
            st_ref[s, 0:HEAD_ROWS, :] = carry_ref[pair]
            st_ref[s, HEAD_ROWS:HEAD_ROWS + nsc, :] = jnp.dot(lhs_ref[s], win_ref[pair], preferred_element_type=F32)
        for s, pair in enumerate(pairs):
            _s5_state_scan(st_ref.at[s], zb_ref.at[s], cpw_ref.at[pair], nsc)
            carry_ref[pair] = st_ref[s, nsc:nsc + HEAD_ROWS, :]
        for s, pair in enumerate(pairs):
            s_in = st_ref[s, pl.ds(ENTER_ROW, nsc), :].astype(BF16)
            y_state = jnp.dot(s_in, wout_ref[pair], preferred_element_type=F32)
            d = d_ref[pair]
            for gi, cols in ((0, lo), (1, hi)):
                g = 2 * pair + gi
                y = (jnp.dot(lhs_ref[s, :, cols], toep_ref[g], preferred_element_type=F32)
                     + y_state[:, cols] + d[:, cols] * u_ref[g])
                y_ref[g] = y
        return carry

    lax.fori_loop(0, PAIRS // slots, pair_block, 0)


def _s5_param_rows(lam_re, lam_im, log_dt, b_re, b_im, c_re, c_im, d):
    go = GROUPS_PER_OCTET
    lanes = lambda a: a.reshape(OCTETS, OCT_STATE)
    ldt = jnp.broadcast_to(log_dt[:, None], (SSM_GROUPS, SSM_STATE))
    bt = lambda a: jnp.transpose(a.reshape(OCTETS, go, SSM_STATE, SSM_GROUP),
                                 (0, 3, 1, 2)).reshape(OCTETS, SSM_GROUP, OCT_STATE)
    ct = lambda a: jnp.transpose(a.reshape(OCTETS, go, SSM_GROUP, SSM_STATE),
                                 (0, 2, 1, 3)).reshape(OCTETS, SSM_GROUP, OCT_STATE)
    vec = jnp.stack([lanes(lam_re), lanes(lam_im), lanes(ldt)], axis=1)
    mat = jnp.stack([bt(b_re), bt(b_im), ct(c_re), ct(c_im)], axis=1)
    d_pairs = jnp.broadcast_to(d.reshape(PAIRS, 2, 1, SSM_GROUP),
                               (PAIRS, 2, SUB, SSM_GROUP)).reshape(PAIRS, 1, 2 * GROUP_LANES)
    return vec, mat, d_pairs


def _s5(u, um, params, bsz, seq, rows):
    nsc = rows // SUB
    steps = seq // rows
    n_pow = SUB + SUBLANES
    full = lambda a: pl.BlockSpec(a.shape, lambda b, r: (0,) * a.ndim)
    tok_blk = pl.BlockSpec((SSM_GROUPS, nsc, GROUP_LANES), lambda b, r: (0, b * steps + r, 0))
    return pl.pallas_call(
        functools.partial(_s5_kernel, nsc=nsc),
        grid=(bsz, steps),
        in_specs=[tok_blk, full(um)] + [full(p) for p in params],
        out_specs=tok_blk,
        out_shape=jax.ShapeDtypeStruct(u.shape, F32),
        scratch_shapes=[
            pltpu.VMEM((SSM_GROUPS, GROUP_LANES, GROUP_LANES), BF16),
            pltpu.VMEM((PAIRS, 2 * GROUP_LANES, 2 * PAIR_STATE), BF16),
            pltpu.VMEM((PAIRS, 2 * PAIR_STATE, 2 * GROUP_LANES), BF16),
            pltpu.VMEM((PAIRS, 2, SUBLANES, PAIR_STATE), F32),
            pltpu.VMEM((2, n_pow, OCT_STATE), F32),
            pltpu.VMEM((2, SUB * SSM_GROUP, OCT_STATE), F32),
            pltpu.VMEM((GROUPS_PER_OCTET // 2, 2 * GROUP_LANES, 2 * PAIR_STATE), F32),
            pltpu.VMEM((S5_SLOTS, nsc, 2 * GROUP_LANES), BF16),
            pltpu.VMEM((S5_SLOTS, HEAD_ROWS + nsc, 2 * PAIR_STATE), F32),
            pltpu.VMEM((S5_SLOTS, HEAD_ROWS + nsc, 2 * PAIR_STATE), F32),
            pltpu.VMEM((PAIRS, HEAD_ROWS, 2 * PAIR_STATE), F32),
            pltpu.VMEM((PAIRS, HEAD_ROWS, 2 * PAIR_STATE), F32),
        ],
        compiler_params=pltpu.CompilerParams(
            dimension_semantics=("arbitrary", "arbitrary"), vmem_limit_bytes=VMEM_LIMIT),
        name="s5_mixer",
    )(u, um, *params)


def _mix_ffn_kernel(h1_ref, ret_ref, ynext_ref, gw32_ref, gb_ref, snw_ref, wo32_ref, n3w_ref,
                    wg32_ref, wu32_ref, wd32_ref, fnw_ref, out_ref,
                    gw_ref, wo_ref, wg_ref, wu_ref, wd_ref, acc_ref, ytok_ref):
    step = pl.program_id(0)
    nsub = ynext_ref.shape[1]

    @pl.when(step < WEIGHT_CHUNKS)
    def _():
        _cast_weight_chunks(step, ((gw32_ref, gw_ref), (wo32_ref, wo_ref), (wg32_ref, wg_ref),
                                   (wu32_ref, wu_ref), (wd32_ref, wd_ref)))

    @pl.when(step == WEIGHT_CHUNKS - 1)
    def _():
        _to_token_major(ynext_ref, ytok_ref.at[0], nsub, jax.nn.gelu)

    @pl.when(step >= WEIGHT_CHUNKS)
    def _():
        slot = lax.rem(step - WEIGHT_CHUNKS, 2)
        y = _token_major_rows(ytok_ref.at[slot], nsub)
        _to_token_major(ynext_ref, ytok_ref.at[1 - slot], nsub, jax.nn.gelu)
        z = y * jax.nn.sigmoid(jnp.dot(y.astype(BF16), gw_ref[...], preferred_element_type=F32)
                               + gb_ref[...])
        ssm = _rms(z, snw_ref[...]).astype(BF16)
        mixed = jnp.dot(jnp.concatenate([ret_ref[...], ssm], axis=-1), wo_ref[...],
                        preferred_element_type=F32)
        h2 = h1_ref[...] + mixed
        out_ref[...] = h2
        h3 = _swiglu_half_step(lambda: out_ref[...], _rms(h2, n3w_ref[...]).astype(BF16), wg_ref,
                               wu_ref, wd_ref, acc_ref)
        out_ref[...] = _rms(h3, fnw_ref[...])


def _mix_ffn(h1, ret, y, glu_w, glu_b, ssm_norm_w, w_out, n3w, wg, wu, wd, fnw, tm):
    rows = h1.shape[0]
    tiles = rows // tm
    row_blk = lambda i: (_tile_index(i), 0)
    next_blk = lambda i: (0, jnp.minimum(_tile_index(i + 1), tiles - 1), 0)
    return pl.pallas_call(
        _mix_ffn_kernel,
        grid=(WEIGHT_CHUNKS + tiles,),
        in_specs=[
            pl.BlockSpec((tm, D_MODEL), row_blk),
            pl.BlockSpec((tm, RET_WIDTH), row_blk),
            pl.BlockSpec((SSM_GROUPS, tm // SUB, GROUP_LANES), next_blk),
            _weight_chunk_spec((SSM_WIDTH, SSM_WIDTH)),
            _resident((1, SSM_WIDTH)),
            _resident((1, SSM_WIDTH)),
            _weight_chunk_spec((D_MODEL, D_MODEL)),
            _resident((1, D_MODEL)),
            _weight_chunk_spec((D_MODEL, D_FF)),
            _weight_chunk_spec((D_MODEL, D_FF)),
            _weight_chunk_spec((D_FF, D_MODEL)),
            _resident((1, D_MODEL)),
        ],
        out_specs=pl.BlockSpec((tm, D_MODEL), row_blk),
        out_shape=jax.ShapeDtypeStruct((rows, D_MODEL), F32),
        scratch_shapes=[
            pltpu.VMEM((SSM_WIDTH, SSM_WIDTH), BF16),
            pltpu.VMEM((D_MODEL, D_MODEL), BF16),
            pltpu.VMEM((D_MODEL, D_FF), BF16),
            pltpu.VMEM((D_MODEL, D_FF), BF16),
            pltpu.VMEM((D_FF, D_MODEL), BF16),
            pltpu.VMEM((tm, D_FF), BF16),
            pltpu.VMEM((2, OCTETS, tm // SUB * TOK_PITCH, LANES), F32),
        ],
        compiler_params=pltpu.CompilerParams(
            dimension_semantics=("arbitrary",), vmem_limit_bytes=VMEM_LIMIT),
        name="mix_ffn",
    )(h1, ret, y, glu_w, glu_b, ssm_norm_w, w_out, n3w, wg, wu, wd, fnw)


def _rope_tables(n_pos):
    freqs = 1.0 / (ROPE_BASE ** (np.arange(0, HEAD_DIM, 2, dtype=np.float64) / HEAD_DIM))
    ang = np.arange(n_pos, dtype=np.float64)[:, None] * freqs[None, :]
    cos = np.cos(ang)
    sin = np.sin(ang)
    cos = np.concatenate([cos, cos], axis=-1).astype(np.float32)
    sin = np.concatenate([-sin, sin], axis=-1).astype(np.float32)
    split = lambda t: (jnp.asarray(t[:N_META]), jnp.asarray(t[N_META:]))
    return split(cos), split(sin)


def kernel(x, meta_tokens, ffn1_norm_w, ffn1_w_gate, ffn1_w_up, ffn1_w_down, mix_norm_w, w_in,
           ret_norm_w, ssm_lambda_re, ssm_lambda_im, ssm_log_dt, ssm_b_re, ssm_b_im, ssm_c_re,
           ssm_c_im, ssm_d, ssm_glu_w, ssm_glu_b, ssm_norm_w, w_out, ffn2_norm_w, ffn2_w_gate,
           ffn2_w_up, ffn2_w_down, final_norm_w):
    bsz, seq, _ = x.shape
    assert ffn1_norm_w.shape[0] == 1, "single layer only"
    tm = 512
    ret_rows = 2048
    s5_rows = 4096
    assert seq % tm == 0 and seq % ret_rows == 0 and seq % s5_rows == 0

    l = 0
    (cos_m, cos), (sin_m, sin) = _rope_tables(N_META + seq)

    h1, q, k, v, g, u, km, vm, um = _ffn_inproj(
        x.reshape(bsz * seq, D_MODEL), meta_tokens, cos, sin, cos_m, sin_m,
        ffn1_norm_w, ffn1_w_gate[l], ffn1_w_up[l], ffn1_w_down[l], mix_norm_w, w_in[l], tm=tm)

    ret = _retention(q, k, v, g, km, vm, ret_norm_w, bsz, seq, ret_rows)

    params = _s5_param_rows(ssm_lambda_re[l], ssm_lambda_im[l], ssm_log_dt[l], ssm_b_re[l],
                            ssm_b_im[l], ssm_c_re[l], ssm_c_im[l], ssm_d[l])
    y = _s5(u, um, params, bsz, seq, s5_rows)

    out = _mix_ffn(h1, ret, y, ssm_glu_w[l], ssm_glu_b, ssm_norm_w, w_out[l], ffn2_norm_w,
                   ffn2_w_gate[l], ffn2_w_up[l], ffn2_w_down[l], final_norm_w.reshape(1, D_MODEL),
                   tm=tm)
    return out.reshape(bsz, seq, D_MODEL)
```
